```python
import jax, jax.numpy as jnp
from jax import lax
import numpy as np

D_MODEL = 1024
BATCH = 8
SEQ = 4096
DEPTH = 1

N_HEADS = 16
HEAD_DIM = 64
ATTN_WIDTH = N_HEADS * HEAD_DIM
CONV_WIDTH = D_MODEL
CONV_K = 31
D_FF = 4 * D_MODEL
Q_BLOCK = 128
N_ADA = 6
NORM_EPS = 1e-6

IN_COLS = (ATTN_WIDTH, ATTN_WIDTH, ATTN_WIDTH, N_HEADS, 2 * CONV_WIDTH, 2 * D_MODEL)
IN_SPLITS = tuple(int(s) for s in np.cumsum(IN_COLS)[:-1])
D_IN = int(sum(IN_COLS))

kernel_name = "hybrid_fox_conformer_gated_block"


def rms_norm(x, g):
    xf = x.astype(jnp.float32)
    y = xf * lax.rsqrt(jnp.mean(xf * xf, axis=-1, keepdims=True) + NORM_EPS)
    return (y * g.astype(jnp.float32)).astype(x.dtype)


def layer_norm(x, g, b):
    xf = x.astype(jnp.float32)
    mu = jnp.mean(xf, axis=-1, keepdims=True)
    var = jnp.mean(jnp.square(xf - mu), axis=-1, keepdims=True)
    y = (xf - mu) * lax.rsqrt(var + NORM_EPS)
    return (y * g.astype(jnp.float32) + b.astype(jnp.float32)).astype(x.dtype)


def forgetting_attention(q, k, v, log_f):
    B, S, H, Dh = q.shape
    nb = S // Q_BLOCK
    scale = Dh ** -0.5
    f_cum = jnp.cumsum(log_f, axis=1)
    f_key = jnp.transpose(f_cum, (0, 2, 1))
    q_blocks = jnp.transpose(q.reshape(B, nb, Q_BLOCK, H, Dh), (1, 0, 2, 3, 4))
    f_blocks = jnp.transpose(f_cum.reshape(B, nb, Q_BLOCK, H), (1, 0, 3, 2))
    k_pos = jnp.arange(S)

    def one_block(args):
        q_i, f_i, i = args
        s = jnp.einsum('bqhd,bkhd->bhqk', q_i, k).astype(jnp.float32) * scale
        s = s + f_i[..., :, None] - f_key[:, :, None, :]
        q_pos = i * Q_BLOCK + jnp.arange(Q_BLOCK)
        causal = k_pos[None, :] <= q_pos[:, None]
        s = jnp.where(causal[None, None], s, -jnp.inf)
        p = jax.nn.softmax(s, axis=-1)
        return jnp.einsum('bhqk,bkhd->bqhd', p.astype(v.dtype), v)

    out = lax.map(one_block, (q_blocks, f_blocks, jnp.arange(nb)))
    return jnp.transpose(out, (1, 0, 2, 3, 4)).reshape(B, S, H * Dh)


def causal_depthwise_conv(u, w, b):
    K, C = w.shape
    u_pad = jnp.pad(u, ((0, 0), (K - 1, 0), (0, 0)))
    y = lax.conv_general_dilated(
        u_pad, w[:, None, :].astype(u.dtype), window_strides=(1,), padding='VALID',
        dimension_numbers=('NWC', 'WIO', 'NWC'), feature_group_count=C)
    return y + b.astype(u.dtype)


def _fwd_setup_inputs(seed: int = 0) -> dict:
    key = jax.random.key(seed)
    ks = jax.random.split(key, 20)
    n = jax.random.normal
    D, L = D_MODEL, DEPTH
    return {
        "x": n(ks[0], (BATCH, SEQ, D), jnp.float32),
        "c": n(ks[1], (BATCH, D), jnp.float32),
        "w_ada": n(ks[2], (L, D, N_ADA * D), jnp.float32) * (0.5 * D ** -0.5),
        "b_ada": n(ks[3], (L, N_ADA * D), jnp.float32) * 0.02,
        "norm1_g": 1.0 + 0.1 * n(ks[4], (L, D), jnp.float32),
        "w_in": n(ks[5], (L, D, D_IN), jnp.float32) * D ** -0.5,
        "b_forget": 3.0 + 0.5 * n(ks[6], (L, N_HEADS), jnp.float32),
        "q_norm_g": 1.0 + 0.1 * n(ks[7], (L, HEAD_DIM), jnp.float32),
        "k_norm_g": 1.0 + 0.1 * n(ks[8], (L, HEAD_DIM), jnp.float32),
        "w_attn_proj": n(ks[9], (L, ATTN_WIDTH, D), jnp.float32) * ATTN_WIDTH ** -0.5,
        "conv_w": n(ks[10], (L, CONV_K, CONV_WIDTH), jnp.float32) * CONV_K ** -0.5,
        "conv_b": 0.02 * n(ks[11], (L, CONV_WIDTH), jnp.float32),
        "conv_ln_g": 1.0 + 0.1 * n(ks[12], (L, CONV_WIDTH), jnp.float32),
        "conv_ln_b": 0.02 * n(ks[13], (L, CONV_WIDTH), jnp.float32),
        "w_conv_proj": n(ks[14], (L, CONV_WIDTH, D), jnp.float32) * CONV_WIDTH ** -0.5,
        "w_out": n(ks[15], (L, D, D), jnp.float32) * D ** -0.5,
        "norm2_g": 1.0 + 0.1 * n(ks[16], (L, D), jnp.float32),
        "w_mlp1": n(ks[17], (L, D, D_FF), jnp.float32) * D ** -0.5,
        "w_mlp2": n(ks[18], (L, D_FF, D), jnp.float32) * D_FF ** -0.5,
    }


def _fwd_reference(x, c, w_ada, b_ada, norm1_g, w_in, b_forget, q_norm_g, k_norm_g,
              w_attn_proj, conv_w, conv_b, conv_ln_g, conv_ln_b, w_conv_proj,
              w_out, norm2_g, w_mlp1, w_mlp2):
    B, S, D = x.shape
    c_act = jax.nn.silu(c)
    for l in range(DEPTH):
        mod = c_act @ w_ada[l] + b_ada[l]
        sh1, sc1, g1, sh2, sc2, g2 = [m[:, None, :] for m in jnp.split(mod, N_ADA, axis=-1)]

        h = rms_norm(x, norm1_g[l]) * (1.0 + sc1) + sh1
        proj = h @ w_in[l]
        q, k, v, f_logit, glu_in, gate_logit = jnp.split(proj, IN_SPLITS, axis=-1)

        q = rms_norm(q.reshape(B, S, N_HEADS, HEAD_DIM), q_norm_g[l])
        k = rms_norm(k.reshape(B, S, N_HEADS, HEAD_DIM), k_norm_g[l])
        v = v.reshape(B, S, N_HEADS, HEAD_DIM)
        log_f = jax.nn.log_sigmoid(f_logit.astype(jnp.float32) + b_forget[l].astype(jnp.float32))
        branch_a = forgetting_attention(q, k, v, log_f) @ w_attn_proj[l]

        u = glu_in[..., :CONV_WIDTH] * jax.nn.sigmoid(glu_in[..., CONV_WIDTH:])
        u = causal_depthwise_conv(u, conv_w[l], conv_b[l])
        u = jax.nn.silu(layer_norm(u, conv_ln_g[l], conv_ln_b[l]))
        branch_b = u @ w_conv_proj[l]

        gate_a, gate_b = jnp.split(gate_logit, 2, axis=-1)
        merged = jax.nn.sigmoid(gate_a) * branch_a + jax.nn.sigmoid(gate_b) * branch_b
        x = x + g1 * (merged @ w_out[l])

        h2 = rms_norm(x, norm2_g[l]) * (1.0 + sc2) + sh2
        x = x + g2 * (jnp.square(jax.nn.relu(h2 @ w_mlp1[l])) @ w_mlp2[l])
    return x


import jax as _jax
import jax.numpy as _jnp

TWIN_FORMAT = 'train_step'
FWD_PARAMS = ['x', 'c', 'w_ada', 'b_ada', 'norm1_g', 'w_in', 'b_forget', 'q_norm_g', 'k_norm_g', 'w_attn_proj', 'conv_w', 'conv_b', 'conv_ln_g', 'conv_ln_b', 'w_conv_proj', 'w_out', 'norm2_g', 'w_mlp1', 'w_mlp2']
TWIN_WEIGHTS = ['w_ada', 'b_ada', 'norm1_g', 'w_in', 'b_forget', 'q_norm_g', 'k_norm_g', 'w_attn_proj', 'conv_w', 'conv_b', 'conv_ln_g', 'conv_ln_b', 'w_conv_proj', 'w_out', 'norm2_g', 'w_mlp1', 'w_mlp2']
TWIN_DIFF_INPUT = 'x'
TWIN_INPUTS = ['x', 'c', 'w_ada', 'b_ada', 'norm1_g', 'w_in', 'b_forget', 'q_norm_g', 'k_norm_g', 'w_attn_proj', 'conv_w', 'conv_b', 'conv_ln_g', 'conv_ln_b', 'w_conv_proj', 'w_out', 'norm2_g', 'w_mlp1', 'w_mlp2', 'loss_target', 'm_w_ada', 'm_b_ada', 'm_norm1_g', 'm_w_in', 'm_b_forget', 'm_q_norm_g', 'm_k_norm_g', 'm_w_attn_proj', 'm_conv_w', 'm_conv_b', 'm_conv_ln_g', 'm_conv_ln_b', 'm_w_conv_proj', 'm_w_out', 'm_norm2_g', 'm_w_mlp1', 'm_w_mlp2', 'v_w_ada', 'v_b_ada', 'v_norm1_g', 'v_w_in', 'v_b_forget', 'v_q_norm_g', 'v_k_norm_g', 'v_w_attn_proj', 'v_conv_w', 'v_conv_b', 'v_conv_ln_g', 'v_conv_ln_b', 'v_w_conv_proj', 'v_w_out', 'v_norm2_g', 'v_w_mlp1', 'v_w_mlp2']
TWIN_OUTPUTS = ['loss', 'grad_x', 'grad_w_ada', 'grad_b_ada', 'grad_norm1_g', 'grad_w_in', 'grad_b_forget', 'grad_q_norm_g', 'grad_k_norm_g', 'grad_w_attn_proj', 'grad_conv_w', 'grad_conv_b', 'grad_conv_ln_g', 'grad_conv_ln_b', 'grad_w_conv_proj', 'grad_w_out', 'grad_norm2_g', 'grad_w_mlp1', 'grad_w_mlp2', 'delta_w_ada', 'delta_b_ada', 'delta_norm1_g', 'delta_w_in', 'delta_b_forget', 'delta_q_norm_g', 'delta_k_norm_g', 'delta_w_attn_proj', 'delta_conv_w', 'delta_conv_b', 'delta_conv_ln_g', 'delta_conv_ln_b', 'delta_w_conv_proj', 'delta_w_out', 'delta_norm2_g', 'delta_w_mlp1', 'delta_w_mlp2', 'new_m_w_ada', 'new_m_b_ada', 'new_m_norm1_g', 'new_m_w_in', 'new_m_b_forget', 'new_m_q_norm_g', 'new_m_k_norm_g', 'new_m_w_attn_proj', 'new_m_conv_w', 'new_m_conv_b', 'new_m_conv_ln_g', 'new_m_conv_ln_b', 'new_m_w_conv_proj', 'new_m_w_out', 'new_m_norm2_g', 'new_m_w_mlp1', 'new_m_w_mlp2', 'new_v_w_ada', 'new_v_b_ada', 'new_v_norm1_g', 'new_v_w_in', 'new_v_b_forget', 'new_v_q_norm_g', 'new_v_k_norm_g', 'new_v_w_attn_proj', 'new_v_conv_w', 'new_v_conv_b', 'new_v_conv_ln_g', 'new_v_conv_ln_b', 'new_v_w_conv_proj', 'new_v_w_out', 'new_v_norm2_g', 'new_v_w_mlp1', 'new_v_w_mlp2']
TWIN_LEAF_KINDS = {'loss': 'loss', 'grad_x': 'grad_x', 'grad_w_ada': 'grad_w', 'grad_b_ada': 'grad_w', 'grad_norm1_g': 'grad_w', 'grad_w_in': 'grad_w', 'grad_b_forget': 'grad_w', 'grad_q_norm_g': 'grad_w', 'grad_k_norm_g': 'grad_w', 'grad_w_attn_proj': 'grad_w', 'grad_conv_w': 'grad_w', 'grad_conv_b': 'grad_w', 'grad_conv_ln_g': 'grad_w', 'grad_conv_ln_b': 'grad_w', 'grad_w_conv_proj': 'grad_w', 'grad_w_out': 'grad_w', 'grad_norm2_g': 'grad_w', 'grad_w_mlp1': 'grad_w', 'grad_w_mlp2': 'grad_w', 'delta_w_ada': 'delta_w', 'delta_b_ada': 'delta_w', 'delta_norm1_g': 'delta_w', 'delta_w_in': 'delta_w', 'delta_b_forget': 'delta_w', 'delta_q_norm_g': 'delta_w', 'delta_k_norm_g': 'delta_w', 'delta_w_attn_proj': 'delta_w', 'delta_conv_w': 'delta_w', 'delta_conv_b': 'delta_w', 'delta_conv_ln_g': 'delta_w', 'delta_conv_ln_b': 'delta_w', 'delta_w_conv_proj': 'delta_w', 'delta_w_out': 'delta_w', 'delta_norm2_g': 'delta_w', 'delta_w_mlp1': 'delta_w', 'delta_w_mlp2': 'delta_w', 'new_m_w_ada': 'new_m', 'new_m_b_ada': 'new_m', 'new_m_norm1_g': 'new_m', 'new_m_w_in': 'new_m', 'new_m_b_forget': 'new_m', 'new_m_q_norm_g': 'new_m', 'new_m_k_norm_g': 'new_m', 'new_m_w_attn_proj': 'new_m', 'new_m_conv_w': 'new_m', 'new_m_conv_b': 'new_m', 'new_m_conv_ln_g': 'new_m', 'new_m_conv_ln_b': 'new_m', 'new_m_w_conv_proj': 'new_m', 'new_m_w_out': 'new_m', 'new_m_norm2_g': 'new_m', 'new_m_w_mlp1': 'new_m', 'new_m_w_mlp2': 'new_m', 'new_v_w_ada': 'new_v', 'new_v_b_ada': 'new_v', 'new_v_norm1_g': 'new_v', 'new_v_w_in': 'new_v', 'new_v_b_forget': 'new_v', 'new_v_q_norm_g': 'new_v', 'new_v_k_norm_g': 'new_v', 'new_v_w_attn_proj': 'new_v', 'new_v_conv_w': 'new_v', 'new_v_conv_b': 'new_v', 'new_v_conv_ln_g': 'new_v', 'new_v_conv_ln_b': 'new_v', 'new_v_w_conv_proj': 'new_v', 'new_v_w_out': 'new_v', 'new_v_norm2_g': 'new_v', 'new_v_w_mlp1': 'new_v', 'new_v_w_mlp2': 'new_v'}


def _forward(args):
    return _fwd_reference(*[args[k] for k in FWD_PARAMS])


def _output_shape():
    out = _jax.eval_shape(lambda: _forward(_fwd_setup_inputs(0)))
    return out.shape, out.dtype

N_MICROBATCH = 1
ADAM_LR = 0.001
ADAM_B1 = 0.9
ADAM_B2 = 0.999
ADAM_EPS = 1e-08
ADAM_WD = 0.01
ADAM_STEP = 10
PER_EXAMPLE_BATCH_AXIS = {'x': 0, 'c': 0, 'loss_target': 0}
SHARED_INPUTS = []
_WEIGHT_DTYPES = {'w_ada': _jnp.float32, 'b_ada': _jnp.float32, 'norm1_g': _jnp.float32, 'w_in': _jnp.float32, 'b_forget': _jnp.float32, 'q_norm_g': _jnp.float32, 'k_norm_g': _jnp.float32, 'w_attn_proj': _jnp.float32, 'conv_w': _jnp.float32, 'conv_b': _jnp.float32, 'conv_ln_g': _jnp.float32, 'conv_ln_b': _jnp.float32, 'w_conv_proj': _jnp.float32, 'w_out': _jnp.float32, 'norm2_g': _jnp.float32, 'w_mlp1': _jnp.float32, 'w_mlp2': _jnp.float32}
MOMENT_SCALE = {'w_ada': 3.423782e+00, 'b_ada': 7.438116e+00, 'norm1_g': 1.147387e-01, 'w_in': 8.684914e-02, 'b_forget': 1.779857e+00, 'q_norm_g': 5.024980e-01, 'k_norm_g': 5.043781e-01, 'w_attn_proj': 2.286594e-01, 'conv_w': 9.622585e-02, 'conv_b': 6.715225e-01, 'conv_ln_g': 5.276674e-01, 'conv_ln_b': 5.087345e-01, 'w_conv_proj': 1.837896e-01, 'w_out': 2.817036e-01, 'norm2_g': 1.262927e+01, 'w_mlp1': 3.749092e-01, 'w_mlp2': 1.413111e+00}


def _to_microbatches(a, axis):
    t = _jnp.moveaxis(a, axis, 0)
    t = t.reshape((N_MICROBATCH, t.shape[0] // N_MICROBATCH) + t.shape[1:])
    return _jnp.moveaxis(t, 1, axis + 1)


def setup_inputs(seed: int = 0) -> dict:
    inp = _fwd_setup_inputs(seed)
    key = _jax.random.fold_in(_jax.random.key(seed), 7919)
    shape, _ = _output_shape()
    out = dict(inp)
    out["loss_target"] = _jax.random.normal(_jax.random.fold_in(key, 0), shape, _jnp.float32)
    for i, name in enumerate(TWIN_WEIGHTS):
        w = inp[name].astype(_jnp.float32)
        if MOMENT_SCALE is None:
            s = _jnp.sqrt(_jnp.mean(_jnp.square(w)) + 1e-30)
        else:
            s = MOMENT_SCALE[name]
        km, kv = _jax.random.split(_jax.random.fold_in(key, i + 1))
        out[name] = w
        out["m_" + name] = s * _jax.random.normal(km, w.shape, _jnp.float32)
        out["v_" + name] = (s * s) * _jax.random.uniform(kv, w.shape, _jnp.float32, 0.5, 1.5)
    if N_MICROBATCH > 1:
        for name, axis in PER_EXAMPLE_BATCH_AXIS.items():
            out[name] = _to_microbatches(out[name], axis)
    return {'x': out['x'], 'c': out['c'], 'w_ada': out['w_ada'], 'b_ada': out['b_ada'], 'norm1_g': out['norm1_g'], 'w_in': out['w_in'], 'b_forget': out['b_forget'], 'q_norm_g': out['q_norm_g'], 'k_norm_g': out['k_norm_g'], 'w_attn_proj': out['w_attn_proj'], 'conv_w': out['conv_w'], 'conv_b': out['conv_b'], 'conv_ln_g': out['conv_ln_g'], 'conv_ln_b': out['conv_ln_b'], 'w_conv_proj': out['w_conv_proj'], 'w_out': out['w_out'], 'norm2_g': out['norm2_g'], 'w_mlp1': out['w_mlp1'], 'w_mlp2': out['w_mlp2'], 'loss_target': out['loss_target'], 'm_w_ada': out['m_w_ada'], 'm_b_ada': out['m_b_ada'], 'm_norm1_g': out['m_norm1_g'], 'm_w_in': out['m_w_in'], 'm_b_forget': out['m_b_forget'], 'm_q_norm_g': out['m_q_norm_g'], 'm_k_norm_g': out['m_k_norm_g'], 'm_w_attn_proj': out['m_w_attn_proj'], 'm_conv_w': out['m_conv_w'], 'm_conv_b': out['m_conv_b'], 'm_conv_ln_g': out['m_conv_ln_g'], 'm_conv_ln_b': out['m_conv_ln_b'], 'm_w_conv_proj': out['m_w_conv_proj'], 'm_w_out': out['m_w_out'], 'm_norm2_g': out['m_norm2_g'], 'm_w_mlp1': out['m_w_mlp1'], 'm_w_mlp2': out['m_w_mlp2'], 'v_w_ada': out['v_w_ada'], 'v_b_ada': out['v_b_ada'], 'v_norm1_g': out['v_norm1_g'], 'v_w_in': out['v_w_in'], 'v_b_forget': out['v_b_forget'], 'v_q_norm_g': out['v_q_norm_g'], 'v_k_norm_g': out['v_k_norm_g'], 'v_w_attn_proj': out['v_w_attn_proj'], 'v_conv_w': out['v_conv_w'], 'v_conv_b': out['v_conv_b'], 'v_conv_ln_g': out['v_conv_ln_g'], 'v_conv_ln_b': out['v_conv_ln_b'], 'v_w_conv_proj': out['v_w_conv_proj'], 'v_w_out': out['v_w_out'], 'v_norm2_g': out['v_norm2_g'], 'v_w_mlp1': out['v_w_mlp1'], 'v_w_mlp2': out['v_w_mlp2']}


def _loss(weights, diff, rest, loss_target):
    with _jax.named_scope("forward"):
        args = {**rest, TWIN_DIFF_INPUT: diff, **{k: w.astype(_WEIGHT_DTYPES[k]) for k, w in weights.items()}}
        y = _forward(args)
    with _jax.named_scope("loss_head"):
        err = _jnp.square(y.astype(_jnp.float32) - loss_target)
        return 0.5 * _jnp.sum(_jnp.mean(err, axis=-1)) if err.ndim else 0.5 * err


def _adamw(w, g, m, v):
    m = ADAM_B1 * m + (1.0 - ADAM_B1) * g
    v = ADAM_B2 * v + (1.0 - ADAM_B2) * _jnp.square(g)
    m_hat = m / (1.0 - ADAM_B1 ** ADAM_STEP)
    v_hat = v / (1.0 - ADAM_B2 ** ADAM_STEP)
    delta = -ADAM_LR * (m_hat / (_jnp.sqrt(v_hat) + ADAM_EPS) + ADAM_WD * w)
    return delta, m, v


def reference(x, c, w_ada, b_ada, norm1_g, w_in, b_forget, q_norm_g, k_norm_g, w_attn_proj, conv_w, conv_b, conv_ln_g, conv_ln_b, w_conv_proj, w_out, norm2_g, w_mlp1, w_mlp2, loss_target, m_w_ada, m_b_ada, m_norm1_g, m_w_in, m_b_forget, m_q_norm_g, m_k_norm_g, m_w_attn_proj, m_conv_w, m_conv_b, m_conv_ln_g, m_conv_ln_b, m_w_conv_proj, m_w_out, m_norm2_g, m_w_mlp1, m_w_mlp2, v_w_ada, v_b_ada, v_norm1_g, v_w_in, v_b_forget, v_q_norm_g, v_k_norm_g, v_w_attn_proj, v_conv_w, v_conv_b, v_conv_ln_g, v_conv_ln_b, v_w_conv_proj, v_w_out, v_norm2_g, v_w_mlp1, v_w_mlp2):
    given = dict(x=x, c=c, w_ada=w_ada, b_ada=b_ada, norm1_g=norm1_g, w_in=w_in, b_forget=b_forget, q_norm_g=q_norm_g, k_norm_g=k_norm_g, w_attn_proj=w_attn_proj, conv_w=conv_w, conv_b=conv_b, conv_ln_g=conv_ln_g, conv_ln_b=conv_ln_b, w_conv_proj=w_conv_proj, w_out=w_out, norm2_g=norm2_g, w_mlp1=w_mlp1, w_mlp2=w_mlp2, loss_target=loss_target, m_w_ada=m_w_ada, m_b_ada=m_b_ada, m_norm1_g=m_norm1_g, m_w_in=m_w_in, m_b_forget=m_b_forget, m_q_norm_g=m_q_norm_g, m_k_norm_g=m_k_norm_g, m_w_attn_proj=m_w_attn_proj, m_conv_w=m_conv_w, m_conv_b=m_conv_b, m_conv_ln_g=m_conv_ln_g, m_conv_ln_b=m_conv_ln_b, m_w_conv_proj=m_w_conv_proj, m_w_out=m_w_out, m_norm2_g=m_norm2_g, m_w_mlp1=m_w_mlp1, m_w_mlp2=m_w_mlp2, v_w_ada=v_w_ada, v_b_ada=v_b_ada, v_norm1_g=v_norm1_g, v_w_in=v_w_in, v_b_forget=v_b_forget, v_q_norm_g=v_q_norm_g, v_k_norm_g=v_k_norm_g, v_w_attn_proj=v_w_attn_proj, v_conv_w=v_conv_w, v_conv_b=v_conv_b, v_conv_ln_g=v_conv_ln_g, v_conv_ln_b=v_conv_ln_b, v_w_conv_proj=v_w_conv_proj, v_w_out=v_w_out, v_norm2_g=v_norm2_g, v_w_mlp1=v_w_mlp1, v_w_mlp2=v_w_mlp2)
    weights = {n: given[n] for n in TWIN_WEIGHTS}
    shared = {n: given[n] for n in SHARED_INPUTS}
    per_example = {n: given[n] for n in ['x', 'c']}
    grad_fn = _jax.value_and_grad(_loss, argnums=(0, 1))

    def one_microbatch(ex, loss_target):
        ex = dict(ex)
        diff = ex.pop(TWIN_DIFF_INPUT)
        return grad_fn(weights, diff, {**shared, **ex}, loss_target)

    if N_MICROBATCH == 1:
        loss, (grad_w, grad_x) = one_microbatch(per_example, given["loss_target"])
    else:
        def body(carry, xs):
            loss_sum, grad_sum = carry
            l_k, (gw_k, gx_k) = one_microbatch(xs[0], xs[1])
            with _jax.named_scope("update"):
                return (loss_sum + l_k, _jax.tree.map(_jnp.add, grad_sum, gw_k)), gx_k

        init = (_jnp.zeros((), _jnp.float32), _jax.tree.map(_jnp.zeros_like, weights))
        (loss, grad_w), grad_x = _jax.lax.scan(body, init, (per_example, given["loss_target"]))
    with _jax.named_scope("update"):
        delta_w, new_m, new_v = {}, {}, {}
        for n in TWIN_WEIGHTS:
            delta_w[n], new_m[n], new_v[n] = _adamw(weights[n], grad_w[n], given["m_" + n], given["v_" + n])
    return (loss, grad_x, *[grad_w[n] for n in TWIN_WEIGHTS], *[delta_w[n] for n in TWIN_WEIGHTS],
            *[new_m[n] for n in TWIN_WEIGHTS], *[new_v[n] for n in TWIN_WEIGHTS])
```

```python
import functools

import jax
import jax.numpy as jnp
from jax import lax
from jax.experimental import pallas as pl
from jax.experimental.pallas import tpu as pltpu

F32 = jnp.float32
BF = jnp.bfloat16

N_DEV = 8
D = 1024
N_HEADS = 16
HEAD_DIM = 64
LANES = 128
CONV_K = 31
CONV_KP = 32
HALO = 32
D_FF = 4 * D
N_ADA = 6
ADA_SHARD = N_ADA * D // N_DEV
EPS = 1e-6
QK_SCALE = HEAD_DIM ** -0.5
NEG = -1e30

ADAM_LR = 0.001
ADAM_B1 = 0.9
ADAM_B2 = 0.999
ADAM_EPS = 1e-08
ADAM_WD = 0.01
ADAM_STEP = 10

VMEM_LIMIT = 56 * 1024 * 1024
TM_ROWS = 256
TQ = 512

MESH = pl.DeviceIdType.MESH


def _cp(sem=None):
    return pltpu.CompilerParams(dimension_semantics=sem, vmem_limit_bytes=VMEM_LIMIT)


def _sds(shape, dtype):
    return jax.ShapeDtypeStruct(tuple(shape), dtype)


def _full(arr):
    nd = arr.ndim
    return pl.BlockSpec(arr.shape, lambda *_: (0,) * nd)


def _fullshape(shape):
    nd = len(shape)
    return pl.BlockSpec(tuple(shape), lambda *_: (0,) * nd)


def _split3(x):
    hi = x.astype(BF)
    r1 = x - hi.astype(F32)
    mid = r1.astype(BF)
    lo = (r1 - mid.astype(F32)).astype(BF)
    return hi, mid, lo


def _dot_exact(x, mat):
    hi, mid, lo = _split3(x)
    d = lambda t: jnp.dot(t, mat, preferred_element_type=F32)
    return d(hi) + d(mid) + d(lo)


def _dot_f32(a, b):
    a1, a2, a3 = _split3(a)
    b1, b2, b3 = _split3(b)
    d = lambda s, t: jnp.dot(s, t, preferred_element_type=F32)
    return (d(a1, b3) + d(a3, b1) + d(a2, b2)) + (d(a1, b2) + d(a2, b1)) + d(a1, b1)


def _sigmoid(x):
    return 1.0 / (1.0 + jnp.exp(-x))


def _colsum(x):
    return jnp.sum(x, axis=0, keepdims=True)


def _my_pos():
    x, y, c = lax.axis_index("x"), lax.axis_index("y"), lax.axis_index("c")
    return x, y, c, 4 * x + 2 * y + c


def _peer(x, y, c, d):
    px = (1 - x) if d & 4 else x
    py = (1 - y) if d & 2 else y
    pc = (1 - c) if d & 1 else c
    return (px, py, pc), 4 * px + 2 * py + pc


def _matmul(a, b, form, out_dtype, name, tm=512, tn=1024, tk=1024):
    if form == "nn":
        (M, K), N = a.shape, b.shape[1]
    elif form == "nt":
        (M, K), N = a.shape, b.shape[0]
    else:
        (K, M), N = a.shape, b.shape[1]
    tm, tn, tk = min(tm, M), min(tn, N), min(tk, K)
    assert M % tm == 0 and N % tn == 0 and K % tk == 0, (name, M, N, K)
    nk = K // tk
    if form == "tn":
        a_spec = pl.BlockSpec((tk, tm), lambda i, j, k: (k, i))
        dn = (((0,), (0,)), ((), ()))
    else:
        a_spec = pl.BlockSpec((tm, tk), lambda i, j, k: (i, k))
        dn = (((1,), (1 if form == "nt" else 0,)), ((), ()))
    if form == "nt":
        b_spec = pl.BlockSpec((tn, tk), lambda i, j, k: (j, k))
    else:
        b_spec = pl.BlockSpec((tk, tn), lambda i, j, k: (k, j))

    def body(a_ref, b_ref, o_ref, *scr):
        part = lax.dot_general(a_ref[...].astype(BF), b_ref[...].astype(BF), dn, preferred_element_type=F32)
        if nk == 1:
            o_ref[...] = part.astype(out_dtype)
        else:
            acc = scr[0]
            k = pl.program_id(2)

            @pl.when(k == 0)
            def _():
                acc[...] = part

            @pl.when(k > 0)
            def _():
                acc[...] += part

            @pl.when(k == nk - 1)
            def _():
                o_ref[...] = acc[...].astype(out_dtype)

    return pl.pallas_call(
        body, name=name, grid=(M // tm, N // tn, nk),
        in_specs=[a_spec, b_spec],
        out_specs=pl.BlockSpec((tm, tn), lambda i, j, k: (i, j)),
        out_shape=_sds((M, N), out_dtype),
        scratch_shapes=[] if nk == 1 else [pltpu.VMEM((tm, tn), F32)],
        compiler_params=_cp(("parallel", "parallel", "arbitrary")),
    )(a, b)


def _rows_call(body, name, n_tiles, ins, outs, scratch=()):
    res = pl.pallas_call(
        body, name=name, grid=(n_tiles,),
        in_specs=[s for _, s in ins],
        out_specs=[s for _, s in outs],
        out_shape=[o for o, _ in outs],
        scratch_shapes=list(scratch),
        compiler_params=_cp(("arbitrary",)),
    )(*[a for a, _ in ins])
    return res


def _rspec(tm, width, cb=0, rev_n=None):
    if rev_n is None:
        return pl.BlockSpec((tm, width), lambda i: (i, cb))
    return pl.BlockSpec((tm, width), lambda i: (rev_n - 1 - i, cb))


def _row_out(T, tm, width, dtype, rev_n=None):
    return (_sds((T, width), dtype), _rspec(tm, width, 0, rev_n))


def _acc_out(shape, dtype=F32):
    return (_sds(shape, dtype), _fullshape(shape))


def _mod_parts(mod):
    return [mod[:, i * D:(i + 1) * D] for i in range(N_ADA)]


def _pre_in(x, mod, n1g):
    T = x.shape[0]
    tm = TM_ROWS

    def body(x_ref, mod_ref, g_ref, h_ref):
        sh1, sc1 = mod_ref[:, 0:D], mod_ref[:, D:2 * D]
        xv = x_ref[...]
        r = lax.rsqrt(jnp.mean(xv * xv, axis=-1, keepdims=True) + EPS)
        h_ref[...] = ((xv * r) * g_ref[...] * (1.0 + sc1) + sh1).astype(BF)

    return _rows_call(body, "pre_in", T // tm,
                      [(x, _rspec(tm, D)), (mod, _full(mod)), (n1g, _full(n1g))],
                      [_row_out(T, tm, D, BF)])[0]


def _seg_mat():
    r = jnp.arange(LANES)[:, None] // HEAD_DIM
    c = jnp.arange(LANES)[None, :] // HEAD_DIM
    return jnp.where(r == c, 1.0 / HEAD_DIM, 0.0).astype(BF)


def _tri_mat(n, upper):
    r = jnp.arange(n)[:, None]
    c = jnp.arange(n)[None, :]
    return jnp.where((r <= c) if upper else (r >= c), 1.0, 0.0).astype(BF)


def _log_sigmoid(z):
    return jnp.minimum(z, 0.0) - jnp.log(1.0 + jnp.exp(-jnp.abs(z)))


def _qkv_post(proj, f, qg2, kg2, bf_pad):
    T = proj.shape[0]
    tm = TM_ROWS
    seg = _seg_mat()
    tri = _tri_mat(tm, True)

    def body(q_ref, k_ref, v_ref, f_ref, qg_ref, kg_ref, bf_ref, seg_ref, tri_ref,
             qo_ref, ko_ref, vo_ref, fc_ref, carry_ref):
        i = pl.program_id(0)

        @pl.when(i == 0)
        def _():
            carry_ref[...] = jnp.zeros_like(carry_ref)

        segm = seg_ref[...]
        for j in range(D // LANES):
            sl = slice(j * LANES, (j + 1) * LANES)
            qc = q_ref[:, sl]
            rq = lax.rsqrt(_dot_exact(qc * qc, segm) + EPS)
            qo_ref[:, sl] = ((qc * rq) * qg_ref[...] * QK_SCALE).astype(BF)
            kc = k_ref[:, sl]
            rk = lax.rsqrt(_dot_exact(kc * kc, segm) + EPS)
            ko_ref[:, sl] = ((kc * rk) * kg_ref[...]).astype(BF)
        vo_ref[...] = v_ref[...].astype(BF)
        lf = _log_sigmoid(f_ref[...] + bf_ref[...])
        lft = lf.T[0:N_HEADS, :]
        carry = carry_ref[:, 0:1]
        fc_ref[...] = _dot_exact(lft, tri_ref[...]) + carry
        carry_ref[...] = jnp.broadcast_to(carry + jnp.sum(lft, axis=1, keepdims=True), carry_ref.shape)

    outs = [_row_out(T, tm, D, BF), _row_out(T, tm, D, BF), _row_out(T, tm, D, BF),
            (_sds((N_HEADS, T), F32), pl.BlockSpec((N_HEADS, tm), lambda i: (0, i)))]
    ins = [(proj, _rspec(tm, D, 0)), (proj, _rspec(tm, D, 1)), (proj, _rspec(tm, D, 2)), (f, _rspec(tm, LANES)),
           (qg2, _full(qg2)), (kg2, _full(kg2)), (bf_pad, _full(bf_pad)), (seg, _full(seg)), (tri, _full(tri))]
    return _rows_call(body, "qkv_post", T // tm, ins, outs, [pltpu.VMEM((N_HEADS, LANES), F32)])


def _lane_lo():
    return lax.broadcasted_iota(jnp.int32, (1, LANES), 1) < HEAD_DIM


def _nt(a, b):
    return lax.dot_general(a, b, (((1,), (1,)), ((), ())), preferred_element_type=F32)


def _tn(a, b):
    return lax.dot_general(a, b, (((0,), (0,)), ((), ())), preferred_element_type=F32)


def _causal(qi, ki, tq, tk):
    row = qi * tq + lax.broadcasted_iota(jnp.int32, (tq, tk), 0)
    col = ki * tk + lax.broadcasted_iota(jnp.int32, (tq, tk), 1)
    return col <= row


def _attn_fwd(q, k, v, fc3):
    T = q.shape[0]
    tq = TQ
    nq = T // tq
    hp_n = N_HEADS // 2

    def body(q_ref, k_ref, v_ref, fk_ref, fq_ref, o_ref, la_ref, lb_ref, acc_ref, m_ref, l_ref):
        qi, ki = pl.program_id(1), pl.program_id(2)

        @pl.when(ki == 0)
        def _():
            acc_ref[...] = jnp.zeros_like(acc_ref)
            m_ref[...] = jnp.full_like(m_ref, NEG)
            l_ref[...] = jnp.zeros_like(l_ref)

        @pl.when(ki <= qi)
        def _():
            lo = _lane_lo()
            q2, k2, v2 = q_ref[...], k_ref[...], v_ref[...]
            zero = jnp.zeros_like(q2)
            bias = fq_ref[:, 0:1] - fk_ref[...]
            mask = _causal(qi, ki, tq, tq)
            pv = []
            alphas = []
            for hh in range(2):
                qh = jnp.where(lo, q2, zero) if hh == 0 else jnp.where(lo, zero, q2)
                vh = jnp.where(lo, v2, zero) if hh == 0 else jnp.where(lo, zero, v2)
                s = _nt(qh, k2) + bias[hh:hh + 1, :]
                s = jnp.where(mask, s, NEG)
                m_old = m_ref[hh]
                m_new = jnp.maximum(m_old, jnp.max(s, axis=-1, keepdims=True))
                alpha = jnp.exp(m_old - m_new)
                p = jnp.exp(s - m_new)
                l_ref[hh] = alpha * l_ref[hh] + jnp.sum(p, axis=-1, keepdims=True)
                m_ref[hh] = m_new
                pv.append(jnp.dot(p.astype(BF), vh, preferred_element_type=F32))
                alphas.append(alpha)
            acc_ref[...] = acc_ref[...] * jnp.where(lo, alphas[0], alphas[1]) + pv[0] + pv[1]

        @pl.when(ki == qi)
        def _():
            lo = _lane_lo()
            la, lb = l_ref[0], l_ref[1]
            o_ref[...] = acc_ref[...] * jnp.where(lo, 1.0 / la, 1.0 / lb)
            la_ref[...] = m_ref[0] + jnp.log(la)
            lb_ref[...] = m_ref[1] + jnp.log(lb)

    qspec = pl.BlockSpec((tq, LANES), lambda h, i, j: (i, h))
    kspec = pl.BlockSpec((tq, LANES), lambda h, i, j: (jnp.minimum(i, j), h))
    fkspec = pl.BlockSpec((None, 2, tq), lambda h, i, j: (h, 0, jnp.minimum(i, j)))
    fqspec = pl.BlockSpec((None, 2, tq), lambda h, i, j: (h, 0, i))
    lspec = pl.BlockSpec((None, tq, 1), lambda h, i, j: (h, i, 0))
    return pl.pallas_call(
        body, name="attn_fwd", grid=(hp_n, nq, nq),
        in_specs=[qspec, kspec, kspec, fkspec, fqspec],
        out_specs=[qspec, lspec, lspec],
        out_shape=[_sds((T, D), F32), _sds((hp_n, T, 1), F32), _sds((hp_n, T, 1), F32)],
        scratch_shapes=[pltpu.VMEM((tq, LANES), F32), pltpu.VMEM((2, tq, 1), F32), pltpu.VMEM((2, tq, 1), F32)],
        compiler_params=_cp(("parallel", "parallel", "arbitrary")),
    )(q, k, v, fc3, fc3)


def _attn_bwd_dq(q, k, v, do, o, la, lb, fc3):
    T = q.shape[0]
    tq = TQ
    nq = T // tq
    hp_n = N_HEADS // 2

    def body(q_ref, k_ref, v_ref, do_ref, o_ref, la_ref, lb_ref, fk_ref, fq_ref,
             dq_ref, da_ref, db_ref, ra_ref, rb_ref, acc_ref, dl_ref, rs_ref):
        qi, ki = pl.program_id(1), pl.program_id(2)

        @pl.when(ki == 0)
        def _():
            lo = _lane_lo()
            acc_ref[...] = jnp.zeros_like(acc_ref)
            rs_ref[...] = jnp.zeros_like(rs_ref)
            prod = do_ref[...].astype(BF).astype(F32) * o_ref[...]
            da = jnp.sum(jnp.where(lo, prod, 0.0), axis=-1, keepdims=True)
            db = jnp.sum(jnp.where(lo, 0.0, prod), axis=-1, keepdims=True)
            dl_ref[0] = da
            dl_ref[1] = db
            da_ref[...] = da
            db_ref[...] = db

        @pl.when(ki <= qi)
        def _():
            lo = _lane_lo()
            q2, k2, v2 = q_ref[...], k_ref[...], v_ref[...]
            do2 = do_ref[...].astype(BF)
            zero = jnp.zeros_like(q2)
            bias = fq_ref[:, 0:1] - fk_ref[...]
            mask = _causal(qi, ki, tq, tq)
            tot = None
            for hh in range(2):
                sel = (lambda t: jnp.where(lo, t, zero)) if hh == 0 else (lambda t: jnp.where(lo, zero, t))
                s = _nt(sel(q2), k2) + bias[hh:hh + 1, :]
                s = jnp.where(mask, s, NEG)
                lse = la_ref[...] if hh == 0 else lb_ref[...]
                p = jnp.exp(s - lse)
                dp = _nt(sel(do2), v2)
                ds = p * (dp - dl_ref[hh])
                rs_ref[hh] += jnp.sum(ds, axis=-1, keepdims=True)
                part = jnp.dot(ds.astype(BF), sel(k2), preferred_element_type=F32)
                tot = part if tot is None else tot + part
            acc_ref[...] += tot

        @pl.when(ki == qi)
        def _():
            dq_ref[...] = acc_ref[...] * QK_SCALE
            ra_ref[...] = rs_ref[0]
            rb_ref[...] = rs_ref[1]

    qspec = pl.BlockSpec((tq, LANES), lambda h, i, j: (i, h))
    kspec = pl.BlockSpec((tq, LANES), lambda h, i, j: (jnp.minimum(i, j), h))
    fkspec = pl.BlockSpec((None, 2, tq), lambda h, i, j: (h, 0, jnp.minimum(i, j)))
    fqspec = pl.BlockSpec((None, 2, tq), lambda h, i, j: (h, 0, i))
    lspec = pl.BlockSpec((None, tq, 1), lambda h, i, j: (h, i, 0))
    return pl.pallas_call(
        body, name="attn_bwd_dq", grid=(hp_n, nq, nq),
        in_specs=[qspec, kspec, kspec, qspec, qspec, lspec, lspec, fkspec, fqspec],
        out_specs=[qspec, lspec, lspec, lspec, lspec],
        out_shape=[_sds((T, D), F32)] + [_sds((hp_n, T, 1), F32)] * 4,
        scratch_shapes=[pltpu.VMEM((tq, LANES), F32), pltpu.VMEM((2, tq, 1), F32), pltpu.VMEM((2, tq, 1), F32)],
        compiler_params=_cp(("parallel", "parallel", "arbitrary")),
    )(q, k, v, do, o, la, lb, fc3, fc3)


def _attn_bwd_dkv(q, k, v, do, la, lb, da, db, fc3):
    T = q.shape[0]
    tq = TQ
    nq = T // tq
    hp_n = N_HEADS // 2

    def body(q_ref, k_ref, v_ref, do_ref, la_ref, lb_ref, da_ref, db_ref, fk_ref, fq_ref,
             dk_ref, dv_ref, dfc_ref, dk_acc, dv_acc, df_acc):
        ki, qi = pl.program_id(1), pl.program_id(2)

        @pl.when(qi == 0)
        def _():
            dk_acc[...] = jnp.zeros_like(dk_acc)
            dv_acc[...] = jnp.zeros_like(dv_acc)
            df_acc[...] = jnp.zeros_like(df_acc)

        @pl.when(qi >= ki)
        def _():
            lo = _lane_lo()
            q2, k2, v2 = q_ref[...], k_ref[...], v_ref[...]
            do2 = do_ref[...].astype(BF)
            zero = jnp.zeros_like(q2)
            bias = fq_ref[:, 0:1] - fk_ref[...]
            mask = _causal(qi, ki, tq, tq)
            dk_t = None
            dv_t = None
            dfs = []
            for hh in range(2):
                sel = (lambda t: jnp.where(lo, t, zero)) if hh == 0 else (lambda t: jnp.where(lo, zero, t))
                s = _nt(sel(q2), k2) + bias[hh:hh + 1, :]
                s = jnp.where(mask, s, NEG)
                lse = la_ref[...] if hh == 0 else lb_ref[...]
                dl = da_ref[...] if hh == 0 else db_ref[...]
                p = jnp.exp(s - lse)
                dp = _nt(sel(do2), v2)
                ds = p * (dp - dl)
                dvp = _tn(p.astype(BF), sel(do2))
                dkp = _tn(ds.astype(BF), sel(q2))
                dv_t = dvp if dv_t is None else dv_t + dvp
                dk_t = dkp if dk_t is None else dk_t + dkp
                dfs.append(_colsum(ds))
            dk_acc[...] += dk_t
            dv_acc[...] += dv_t
            df_acc[0:1, :] -= dfs[0]
            df_acc[1:2, :] -= dfs[1]

        @pl.when(qi == nq - 1)
        def _():
            dk_ref[...] = dk_acc[...]
            dv_ref[...] = dv_acc[...]
            dfc_ref[...] = df_acc[...]

    kspec = pl.BlockSpec((tq, LANES), lambda h, j, i: (j, h))
    qspec = pl.BlockSpec((tq, LANES), lambda h, j, i: (jnp.maximum(i, j), h))
    fkspec = pl.BlockSpec((None, 2, tq), lambda h, j, i: (h, 0, j))
    fqspec = pl.BlockSpec((None, 2, tq), lambda h, j, i: (h, 0, jnp.maximum(i, j)))
    lspec = pl.BlockSpec((None, tq, 1), lambda h, j, i: (h, jnp.maximum(i, j), 0))
    return pl.pallas_call(
        body, name="attn_bwd_dkv", grid=(hp_n, nq, nq),
        in_specs=[qspec, kspec, kspec, qspec, lspec, lspec, lspec, lspec, fkspec, fqspec],
        out_specs=[kspec, kspec, fkspec],
        out_shape=[_sds((T, D), F32), _sds((T, D), F32), _sds((hp_n, 2, T), F32)],
        scratch_shapes=[pltpu.VMEM((tq, LANES), F32), pltpu.VMEM((tq, LANES), F32), pltpu.VMEM((2, tq), F32)],
        compiler_params=_cp(("parallel", "parallel", "arbitrary")),
    )(q, k, v, do, la, lb, da, db, fc3, fc3)


def _layer_norm_stats(u1):
    mu = jnp.mean(u1, axis=-1, keepdims=True)
    xc = u1 - mu
    rstd = lax.rsqrt(jnp.mean(xc * xc, axis=-1, keepdims=True) + EPS)
    return xc * rstd, rstd


def _conv_fwd(proj, cw, cb, lng, lnb):
    T = proj.shape[0]
    tm = TM_ROWS

    def body(a_ref, b_ref, w_ref, cb_ref, g_ref, bb_ref, u0_ref, u1_ref, u3_ref, buf):
        i = pl.program_id(0)

        @pl.when(i == 0)
        def _():
            buf[0:HALO, :] = jnp.zeros((HALO, D), F32)

        u0 = a_ref[...] * _sigmoid(b_ref[...])
        u0_ref[...] = u0
        buf[HALO:HALO + tm, :] = u0
        for j in range(D // LANES):
            sl = slice(j * LANES, (j + 1) * LANES)
            acc = jnp.broadcast_to(cb_ref[:, sl], (tm, LANES))
            for kk in range(CONV_K):
                off = HALO - (CONV_K - 1) + kk
                acc = acc + w_ref[kk:kk + 1, sl] * buf[off:off + tm, sl]
            u1_ref[:, sl] = acc
        buf[0:HALO, :] = buf[tm:tm + HALO, :]
        xh, _ = _layer_norm_stats(u1_ref[...])
        u2 = xh * g_ref[...] + bb_ref[...]
        u3_ref[...] = (u2 * _sigmoid(u2)).astype(BF)

    ins = [(proj, _rspec(tm, D, 3)), (proj, _rspec(tm, D, 4)), (cw, _full(cw)), (cb, _full(cb)),
           (lng, _full(lng)), (lnb, _full(lnb))]
    outs = [_row_out(T, tm, D, F32), _row_out(T, tm, D, F32), _row_out(T, tm, D, BF)]
    return _rows_call(body, "conv_fwd", T // tm, ins, outs, [pltpu.VMEM((tm + HALO, D), F32)])


def _conv_bwd(du3, u1, u0, proj, cw, lng, lnb):
    T = du3.shape[0]
    tm = TM_ROWS
    n = T // tm
    per = tm // HALO

    def body(du3_ref, u1_ref, u0_ref, halo_ref, a_ref, b_ref, w_ref, g_ref, bb_ref,
             da_ref, db_ref, dg_ref, dbb_ref, dcb_ref, dw_ref, dbuf, ubuf, du0_buf):
        i = pl.program_id(0)
        r = n - 1 - i

        @pl.when(i == 0)
        def _():
            dbuf[tm:tm + HALO, :] = jnp.zeros((HALO, D), F32)
            dg_ref[...] = jnp.zeros_like(dg_ref)
            dbb_ref[...] = jnp.zeros_like(dbb_ref)
            dcb_ref[...] = jnp.zeros_like(dcb_ref)
            dw_ref[...] = jnp.zeros_like(dw_ref)

        xh, rstd = _layer_norm_stats(u1_ref[...])
        g = g_ref[...]
        u2 = xh * g + bb_ref[...]
        s2 = _sigmoid(u2)
        du2 = du3_ref[...] * (s2 * (1.0 + u2 * (1.0 - s2)))
        dg_ref[...] += _colsum(du2 * xh)
        dbb_ref[...] += _colsum(du2)
        dxh = du2 * g
        du1 = rstd * (dxh - jnp.mean(dxh, axis=-1, keepdims=True) - xh * jnp.mean(dxh * xh, axis=-1, keepdims=True))
        dcb_ref[...] += _colsum(du1)
        dbuf[0:tm, :] = du1
        ubuf[HALO:HALO + tm, :] = u0_ref[...]
        ubuf[0:HALO, :] = jnp.where(r > 0, halo_ref[...], 0.0)
        for j in range(D // LANES):
            sl = slice(j * LANES, (j + 1) * LANES)
            d1 = dbuf[0:tm, sl]
            acc = jnp.zeros((tm, LANES), F32)
            for kk in range(CONV_K):
                acc = acc + w_ref[kk:kk + 1, sl] * dbuf[CONV_K - 1 - kk:CONV_K - 1 - kk + tm, sl]
                off = HALO - (CONV_K - 1) + kk
                dw_ref[kk:kk + 1, sl] += _colsum(d1 * ubuf[off:off + tm, sl])
            du0_buf[:, sl] = acc
        dbuf[tm:tm + HALO, :] = dbuf[0:HALO, :]
        du0 = du0_buf[...]
        sb = _sigmoid(b_ref[...])
        da_ref[...] = (du0 * sb).astype(BF)
        db_ref[...] = (du0 * a_ref[...] * sb * (1.0 - sb)).astype(BF)

    rs = lambda cb: _rspec(tm, D, cb, n)
    halo_spec = pl.BlockSpec((HALO, D), lambda i: (jnp.maximum((n - 1 - i) * per - 1, 0), 0))
    ins = [(du3, rs(0)), (u1, rs(0)), (u0, rs(0)), (u0, halo_spec), (proj, rs(3)), (proj, rs(4)),
           (cw, _full(cw)), (lng, _full(lng)), (lnb, _full(lnb))]
    outs = [_row_out(T, tm, D, BF, n), _row_out(T, tm, D, BF, n),
            _acc_out((1, D)), _acc_out((1, D)), _acc_out((1, D)), _acc_out((CONV_KP, D))]
    return _rows_call(body, "conv_bwd", n, ins, outs,
                      [pltpu.VMEM((tm + HALO, D), F32), pltpu.VMEM((tm + HALO, D), F32), pltpu.VMEM((tm, D), F32)])


def _merge(ba, bb, proj):
    T = ba.shape[0]
    tm = TM_ROWS

    def body(ba_ref, bb_ref, ga_ref, gb_ref, o_ref):
        o_ref[...] = (_sigmoid(ga_ref[...]) * ba_ref[...] + _sigmoid(gb_ref[...]) * bb_ref[...]).astype(BF)

    ins = [(ba, _rspec(tm, D)), (bb, _rspec(tm, D)), (proj, _rspec(tm, D, 5)), (proj, _rspec(tm, D, 6))]
    return _rows_call(body, "merge", T // tm, ins, [_row_out(T, tm, D, BF)])[0]


def _post_out(x, mo, mod, n2g):
    T = x.shape[0]
    tm = TM_ROWS

    def body(x_ref, mo_ref, mod_ref, g_ref, x1_ref, h2_ref):
        g1 = mod_ref[:, 2 * D:3 * D]
        sh2, sc2 = mod_ref[:, 3 * D:4 * D], mod_ref[:, 4 * D:5 * D]
        x1 = x_ref[...] + g1 * mo_ref[...]
        x1_ref[...] = x1
        r = lax.rsqrt(jnp.mean(x1 * x1, axis=-1, keepdims=True) + EPS)
        h2_ref[...] = ((x1 * r) * g_ref[...] * (1.0 + sc2) + sh2).astype(BF)

    ins = [(x, _rspec(tm, D)), (mo, _rspec(tm, D)), (mod, _full(mod)), (n2g, _full(n2g))]
    return _rows_call(body, "post_out", T // tm, ins, [_row_out(T, tm, D, F32), _row_out(T, tm, D, BF)])


def _relu2(a):
    T = a.shape[0]
    tm = TM_ROWS

    def body(a_ref, o_ref):
        r = jnp.maximum(a_ref[...], 0.0)
        o_ref[...] = (r * r).astype(BF)

    return _rows_call(body, "relu2", T // tm, [(a, _rspec(tm, D_FF))], [_row_out(T, tm, D_FF, BF)])[0]


def _loss_head(x1, m2, tgt, mod):
    T = x1.shape[0]
    tm = TM_ROWS

    def body(x1_ref, m2_ref, t_ref, mod_ref, dy_ref, dm2_ref, dg2_ref, sq_ref):
        i = pl.program_id(0)

        @pl.when(i == 0)
        def _():
            dg2_ref[...] = jnp.zeros_like(dg2_ref)
            sq_ref[...] = jnp.zeros_like(sq_ref)

        g2 = mod_ref[:, 5 * D:6 * D]
        m2 = m2_ref[...]
        err = x1_ref[...] + g2 * m2 - t_ref[...]
        dy = err * (1.0 / D)
        dy_ref[...] = dy
        dm2_ref[...] = (g2 * dy).astype(BF)
        dg2_ref[...] += _colsum(dy * m2)
        sq_ref[...] += _colsum(err * err)

    ins = [(x1, _rspec(tm, D)), (m2, _rspec(tm, D)), (tgt, _rspec(tm, D)), (mod, _full(mod))]
    outs = [_row_out(T, tm, D, F32), _row_out(T, tm, D, BF), _acc_out((1, D)), _acc_out((1, D))]
    return _rows_call(body, "loss_head", T // tm, ins, outs)


def _relu2_bwd(drl, a):
    T = a.shape[0]
    tm = TM_ROWS

    def body(d_ref, a_ref, o_ref):
        o_ref[...] = (d_ref[...] * (2.0 * jnp.maximum(a_ref[...], 0.0))).astype(BF)

    ins = [(drl, _rspec(tm, D_FF)), (a, _rspec(tm, D_FF))]
    return _rows_call(body, "relu2_bwd", T // tm, ins, [_row_out(T, tm, D_FF, BF)])[0]


def _norm2_bwd(dh2, x1, dy, mo, mod, n2g):
    T = x1.shape[0]
    tm = TM_ROWS

    def body(dh_ref, x1_ref, dy_ref, mo_ref, mod_ref, g_ref, dx1_ref, dmo_ref, dsh_ref, dsc_ref, dg_ref, dg1_ref):
        i = pl.program_id(0)

        @pl.when(i == 0)
        def _():
            for r in (dsh_ref, dsc_ref, dg_ref, dg1_ref):
                r[...] = jnp.zeros_like(r)

        g1, sc2 = mod_ref[:, 2 * D:3 * D], mod_ref[:, 4 * D:5 * D]
        g = g_ref[...]
        x1 = x1_ref[...]
        dh = dh_ref[...]
        r = lax.rsqrt(jnp.mean(x1 * x1, axis=-1, keepdims=True) + EPS)
        xn = x1 * r
        dsh_ref[...] += _colsum(dh)
        dsc_ref[...] += _colsum(dh * xn * g)
        dg_ref[...] += _colsum(dh * xn * (1.0 + sc2))
        dxn = dh * g * (1.0 + sc2)
        dx1 = dy_ref[...] + r * (dxn - xn * jnp.mean(dxn * xn, axis=-1, keepdims=True))
        dx1_ref[...] = dx1
        dg1_ref[...] += _colsum(dx1 * mo_ref[...])
        dmo_ref[...] = (g1 * dx1).astype(BF)

    ins = [(dh2, _rspec(tm, D)), (x1, _rspec(tm, D)), (dy, _rspec(tm, D)), (mo, _rspec(tm, D)),
           (mod, _full(mod)), (n2g, _full(n2g))]
    outs = [_row_out(T, tm, D, F32), _row_out(T, tm, D, BF)] + [_acc_out((1, D)) for _ in range(4)]
    return _rows_call(body, "norm2_bwd", T // tm, ins, outs)


def _gate_bwd(dmerged, ba, bb, proj):
    T = ba.shape[0]
    tm = TM_ROWS

    def body(dm_ref, ba_ref, bb_ref, ga_ref, gb_ref, dba_ref, dbb_ref, dga_ref, dgb_ref):
        dm = dm_ref[...]
        sa, sb = _sigmoid(ga_ref[...]), _sigmoid(gb_ref[...])
        dba_ref[...] = (dm * sa).astype(BF)
        dbb_ref[...] = (dm * sb).astype(BF)
        dga_ref[...] = (dm * ba_ref[...] * sa * (1.0 - sa)).astype(BF)
        dgb_ref[...] = (dm * bb_ref[...] * sb * (1.0 - sb)).astype(BF)

    ins = [(dmerged, _rspec(tm, D)), (ba, _rspec(tm, D)), (bb, _rspec(tm, D)),
           (proj, _rspec(tm, D, 5)), (proj, _rspec(tm, D, 6))]
    return _rows_call(body, "gate_bwd", T // tm, ins, [_row_out(T, tm, D, BF) for _ in range(4)])


def _qkv_bwd(dq, dk, dv, proj, f, dfc, dfq, qg2, kg2, bf_pad):
    T = proj.shape[0]
    tm = TM_ROWS
    n = T // tm
    seg = _seg_mat()
    tri = _tri_mat(tm, False)

    def body(dq_ref, dk_ref, dv_ref, q_ref, k_ref, f_ref, dfc_ref, dfq_ref, qg_ref, kg_ref, bf_ref, seg_ref, tri_ref,
             dqo_ref, dko_ref, dvo_ref, dfo_ref, dqg_ref, dkg_ref, dbf_ref, carry_ref):
        i = pl.program_id(0)

        @pl.when(i == 0)
        def _():
            carry_ref[...] = jnp.zeros_like(carry_ref)
            dqg_ref[...] = jnp.zeros_like(dqg_ref)
            dkg_ref[...] = jnp.zeros_like(dkg_ref)
            dbf_ref[...] = jnp.zeros_like(dbf_ref)

        segm = seg_ref[...]
        dqg = jnp.zeros((1, LANES), F32)
        dkg = jnp.zeros((1, LANES), F32)
        for j in range(D // LANES):
            sl = slice(j * LANES, (j + 1) * LANES)
            for (raw_ref, d_ref, gn_ref, o_ref, which) in ((q_ref, dq_ref, qg_ref, dqo_ref, 0), (k_ref, dk_ref, kg_ref, dko_ref, 1)):
                xc = raw_ref[:, sl]
                rr = lax.rsqrt(_dot_exact(xc * xc, segm) + EPS)
                xn = xc * rr
                dc = d_ref[:, sl]
                if which == 0:
                    dqg = dqg + _colsum(dc * xn)
                else:
                    dkg = dkg + _colsum(dc * xn)
                dxn = dc * gn_ref[...]
                o_ref[:, sl] = (rr * (dxn - xn * _dot_exact(dxn * xn, segm))).astype(BF)
        dqg_ref[...] += dqg
        dkg_ref[...] += dkg
        dvo_ref[...] = dv_ref[...].astype(BF)
        z = f_ref[...] + bf_ref[...]
        sneg_t = _sigmoid(-z).T[0:N_HEADS, :]
        dfc = dfc_ref[...] + dfq_ref[...]
        carry = carry_ref[:, 0:1]
        dlf = _dot_exact(dfc, tri_ref[...]) + carry
        carry_ref[...] = jnp.broadcast_to(carry + jnp.sum(dfc, axis=1, keepdims=True), carry_ref.shape)
        dzt = dlf * sneg_t
        dz = jnp.concatenate([dzt, jnp.zeros((LANES - N_HEADS, tm), F32)], axis=0).T
        dbf_ref[...] += _colsum(dz)
        dfo_ref[...] = dz.astype(BF)

    rs = lambda w, cb=0: _rspec(tm, w, cb, n)
    ins = [(dq, rs(D)), (dk, rs(D)), (dv, rs(D)), (proj, rs(D, 0)), (proj, rs(D, 1)), (f, rs(LANES)),
           (dfc, pl.BlockSpec((N_HEADS, tm), lambda i: (0, n - 1 - i))),
           (dfq, pl.BlockSpec((N_HEADS, tm), lambda i: (0, n - 1 - i))),
           (qg2, _full(qg2)), (kg2, _full(kg2)), (bf_pad, _full(bf_pad)), (seg, _full(seg)), (tri, _full(tri))]
    outs = [_row_out(T, tm, D, BF, n), _row_out(T, tm, D, BF, n), _row_out(T, tm, D, BF, n), _row_out(T, tm, LANES, BF, n),
            _acc_out((1, LANES)), _acc_out((1, LANES)), _acc_out((1, LANES))]
    return _rows_call(body, "qkv_bwd", n, ins, outs, [pltpu.VMEM((N_HEADS, LANES), F32)])


def _norm1_bwd(dh, dhf, x, dx1, mod, n1g):
    T = x.shape[0]
    tm = TM_ROWS

    def body(dh_ref, dhf_ref, x_ref, dx1_ref, mod_ref, g_ref, dx_ref, dsh_ref, dsc_ref, dg_ref):
        i = pl.program_id(0)

        @pl.when(i == 0)
        def _():
            for r in (dsh_ref, dsc_ref, dg_ref):
                r[...] = jnp.zeros_like(r)

        sc1 = mod_ref[:, D:2 * D]
        g = g_ref[...]
        xv = x_ref[...]
        dh = dh_ref[...] + dhf_ref[...]
        r = lax.rsqrt(jnp.mean(xv * xv, axis=-1, keepdims=True) + EPS)
        xn = xv * r
        dsh_ref[...] += _colsum(dh)
        dsc_ref[...] += _colsum(dh * xn * g)
        dg_ref[...] += _colsum(dh * xn * (1.0 + sc1))
        dxn = dh * g * (1.0 + sc1)
        dx_ref[...] = dx1_ref[...] + r * (dxn - xn * jnp.mean(dxn * xn, axis=-1, keepdims=True))

    ins = [(dh, _rspec(tm, D)), (dhf, _rspec(tm, D)), (x, _rspec(tm, D)), (dx1, _rspec(tm, D)),
           (mod, _full(mod)), (n1g, _full(n1g))]
    outs = [_row_out(T, tm, D, F32)] + [_acc_out((1, D)) for _ in range(3)]
    return _rows_call(body, "norm1_bwd", T // tm, ins, outs)


def _adamw_math(w, g, m, v):
    m = ADAM_B1 * m + (1.0 - ADAM_B1) * g
    v = ADAM_B2 * v + (1.0 - ADAM_B2) * (g * g)
    m_hat = m / (1.0 - ADAM_B1 ** ADAM_STEP)
    v_hat = v / (1.0 - ADAM_B2 ** ADAM_STEP)
    delta = -ADAM_LR * (m_hat / (jnp.sqrt(v_hat) + ADAM_EPS) + ADAM_WD * w)
    return delta, m, v


def _adamw(parts, w, m, v, name):
    n, R, C = parts.shape
    tr = R if R <= 256 else 256
    assert R % tr == 0

    def body(p_ref, w_ref, m_ref, v_ref, g_ref, d_ref, mo_ref, vo_ref):
        g = p_ref[0].astype(F32)
        for s in range(1, n):
            g = g + p_ref[s].astype(F32)
        g_ref[...] = g
        d_ref[...], mo_ref[...], vo_ref[...] = _adamw_math(w_ref[...], g, m_ref[...], v_ref[...])

    spec = pl.BlockSpec((tr, C), lambda i: (i, 0))
    return pl.pallas_call(
        body, name=name, grid=(R // tr,),
        in_specs=[pl.BlockSpec((n, tr, C), lambda i: (0, i, 0)), spec, spec, spec],
        out_specs=[spec] * 4, out_shape=[_sds((R, C), F32)] * 4,
        compiler_params=_cp(("parallel",)),
    )(parts, w, m, v)


def _rcopy(src, dst, ssem, rsem, peer):
    return pltpu.make_async_remote_copy(src_ref=src, dst_ref=dst, send_sem=ssem, recv_sem=rsem,
                                        device_id=peer, device_id_type=MESH)


def _ada_fwd(c, w_ada, b_slice, cw_shard):
    def body(c_ref, w_ref, b_ref, cw_ref, mod_ref, ca_ref, cwf_ref, call, mp, ssem, rsem):
        x, y, cc, me = _my_pos()
        call[pl.ds(me, 1), :] = c_ref[...]
        cwf_ref[me] = cw_ref[...]
        first = []
        for d in range(1, N_DEV):
            peer, _ = _peer(x, y, cc, d)
            first.append(_rcopy(c_ref, call.at[pl.ds(me, 1), :], ssem.at[0, d - 1], rsem.at[0, d - 1], peer))
            first.append(_rcopy(cw_ref, cwf_ref.at[me], ssem.at[1, d - 1], rsem.at[1, d - 1], peer))
        for cp in first:
            cp.start()
        for d in range(1, N_DEV):
            peer, pid = _peer(x, y, cc, d)
            _rcopy(c_ref, call.at[pl.ds(pid, 1), :], ssem.at[0, d - 1], rsem.at[0, d - 1], peer).wait_recv()
            _rcopy(cw_ref, cwf_ref.at[pid], ssem.at[1, d - 1], rsem.at[1, d - 1], peer).wait_recv()
        cv = call[...]
        ca = cv * _sigmoid(cv)
        ca_ref[...] = ca
        mp[...] = _dot_f32(ca, w_ref[...]) + b_ref[...]
        mod_ref[pl.ds(me, 1), :] = mp[pl.ds(me, 1), :]
        second = []
        for d in range(1, N_DEV):
            peer, pid = _peer(x, y, cc, d)
            second.append(_rcopy(mp.at[pl.ds(pid, 1), :], mod_ref.at[pl.ds(me, 1), :], ssem.at[2, d - 1], rsem.at[2, d - 1], peer))
        for cp in second:
            cp.start()
        for d in range(1, N_DEV):
            peer, pid = _peer(x, y, cc, d)
            _rcopy(mp.at[pl.ds(pid, 1), :], mod_ref.at[pl.ds(pid, 1), :], ssem.at[2, d - 1], rsem.at[2, d - 1], peer).wait_recv()
        for cp in first + second:
            cp.wait_send()

    vm = pl.BlockSpec(memory_space=pltpu.VMEM)
    return pl.pallas_call(
        body, name="ada_fwd",
        in_specs=[vm, vm, vm, vm], out_specs=[vm, vm, vm],
        out_shape=[_sds((N_DEV, ADA_SHARD), F32), _sds((N_DEV, D), F32), _sds((N_DEV, CONV_KP, LANES), F32)],
        scratch_shapes=[pltpu.VMEM((N_DEV, D), F32), pltpu.VMEM((N_DEV, ADA_SHARD), F32),
                        pltpu.SemaphoreType.DMA((3, N_DEV - 1)), pltpu.SemaphoreType.DMA((3, N_DEV - 1))],
        compiler_params=pltpu.CompilerParams(vmem_limit_bytes=VMEM_LIMIT),
    )(c, w_ada, b_slice, cw_shard)


def _exchange(arrays, gather, name):
    n = len(arrays)

    def body(*refs):
        ins, outs = refs[:n], refs[n:2 * n]
        lsem, ssem, rsem = refs[2 * n:]
        x, y, cc, me = _my_pos()
        local = []
        sends = []
        for a in range(n):
            src_mine = ins[a] if gather else ins[a].at[me]
            cp = pltpu.make_async_copy(src_mine, outs[a].at[me], lsem.at[a])
            cp.start()
            local.append(cp)
            for d in range(1, N_DEV):
                peer, pid = _peer(x, y, cc, d)
                src = ins[a] if gather else ins[a].at[pid]
                cp = _rcopy(src, outs[a].at[me], ssem.at[a, d - 1], rsem.at[a, d - 1], peer)
                cp.start()
                sends.append(cp)
        for a in range(n):
            for d in range(1, N_DEV):
                peer, pid = _peer(x, y, cc, d)
                src = ins[a] if gather else ins[a].at[pid]
                _rcopy(src, outs[a].at[pid], ssem.at[a, d - 1], rsem.at[a, d - 1], peer).wait_recv()
        for cp in sends:
            cp.wait_send()
        for cp in local:
            cp.wait()

    anyspec = pl.BlockSpec(memory_space=pl.ANY)
    shapes = [_sds((N_DEV,) + tuple(a.shape[-2:]), a.dtype) for a in arrays]
    return pl.pallas_call(
        body, name=name,
        in_specs=[anyspec] * n, out_specs=[anyspec] * n, out_shape=shapes,
        scratch_shapes=[pltpu.SemaphoreType.DMA((n,)), pltpu.SemaphoreType.DMA((n, N_DEV - 1)),
                        pltpu.SemaphoreType.DMA((n, N_DEV - 1))],
    )(*arrays)


PACK_ROWS = 16
ROW_MISC = 5
ROW_LOSS = 6
ROW_DMOD = 8


def _small_bwd(pack, dmodb, dcw, cat, wp, mp_, vp, cw_w, cw_m, cw_v):
    def body(pack_ref, dmodb_ref, dcw_ref, cat_ref, wp_ref, mp_ref, vp_ref, cww_ref, cwm_ref, cwv_ref,
             g_ref, d_ref, mo_ref, vo_ref, cg_ref, cd_ref, cm_ref, cv_ref, gwa_ref, loss_ref,
             allp, dmc, cwg, ssem, rsem):
        x, y, cc, me = _my_pos()
        allp[me] = pack_ref[...]
        dmc[pl.ds(me, 1), :] = dmodb_ref[pl.ds(me, 1), :]
        cwg[me] = dcw_ref[me]
        sends = []
        for d in range(1, N_DEV):
            peer, pid = _peer(x, y, cc, d)
            sends.append(_rcopy(pack_ref, allp.at[me], ssem.at[0, d - 1], rsem.at[0, d - 1], peer))
            sends.append(_rcopy(dmodb_ref.at[pl.ds(pid, 1), :], dmc.at[pl.ds(me, 1), :], ssem.at[1, d - 1], rsem.at[1, d - 1], peer))
            sends.append(_rcopy(dcw_ref.at[pid], cwg.at[me], ssem.at[2, d - 1], rsem.at[2, d - 1], peer))
        for cp in sends:
            cp.start()
        for d in range(1, N_DEV):
            peer, pid = _peer(x, y, cc, d)
            _rcopy(pack_ref, allp.at[pid], ssem.at[0, d - 1], rsem.at[0, d - 1], peer).wait_recv()
            _rcopy(dmodb_ref.at[pl.ds(pid, 1), :], dmc.at[pl.ds(pid, 1), :], ssem.at[1, d - 1], rsem.at[1, d - 1], peer).wait_recv()
            _rcopy(dcw_ref.at[pid], cwg.at[pid], ssem.at[2, d - 1], rsem.at[2, d - 1], peer).wait_recv()
        for cp in sends:
            cp.wait_send()

        tot = allp[0]
        cg = cwg[0]
        for s in range(1, N_DEV):
            tot = tot + allp[s]
            cg = cg + cwg[s]
        lane = lax.broadcasted_iota(jnp.int32, (PACK_ROWS, D), 1)
        row = lax.broadcasted_iota(jnp.int32, (PACK_ROWS, D), 0)
        gains = (row == ROW_MISC) & (lane >= LANES) & (lane < 3 * LANES)
        folded = tot + pltpu.roll(tot, D - HEAD_DIM, axis=1)
        keep = (lane % LANES) < HEAD_DIM
        g = jnp.where(gains, jnp.where(keep, folded, 0.0), tot)
        loss_ref[...] = jnp.broadcast_to(
            (0.5 / D) * jnp.sum(jnp.where(row == ROW_LOSS, tot, 0.0), keepdims=True).reshape(1, 1), loss_ref.shape)
        g = jnp.where(row == ROW_LOSS, 0.0, g)
        g_ref[...] = g
        d_ref[...], mo_ref[...], vo_ref[...] = _adamw_math(wp_ref[...], g, mp_ref[...], vp_ref[...])
        cg_ref[...] = cg
        cd_ref[...], cm_ref[...], cv_ref[...] = _adamw_math(cww_ref[...], cg, cwm_ref[...], cwv_ref[...])
        dm_pad = jnp.concatenate([dmc[...], jnp.zeros((LANES - N_DEV, ADA_SHARD), F32)], axis=0)
        gwa_ref[...] = _dot_f32(cat_ref[...], dm_pad)

    vm = pl.BlockSpec(memory_space=pltpu.VMEM)
    p16 = _sds((PACK_ROWS, D), F32)
    c32 = _sds((CONV_KP, LANES), F32)
    return pl.pallas_call(
        body, name="small_bwd",
        in_specs=[vm] * 10, out_specs=[vm] * 10,
        out_shape=[p16, p16, p16, p16, c32, c32, c32, c32, _sds((D, ADA_SHARD), F32), _sds((8, LANES), F32)],
        scratch_shapes=[pltpu.VMEM((N_DEV, PACK_ROWS, D), F32), pltpu.VMEM((N_DEV, ADA_SHARD), F32),
                        pltpu.VMEM((N_DEV, CONV_KP, LANES), F32),
                        pltpu.SemaphoreType.DMA((3, N_DEV - 1)), pltpu.SemaphoreType.DMA((3, N_DEV - 1))],
        compiler_params=pltpu.CompilerParams(vmem_limit_bytes=VMEM_LIMIT),
    )(pack, dmodb, dcw, cat, wp, mp_, vp, cw_w, cw_m, cw_v)


def _lanes(vec, start, total=D):
    n = vec.shape[1]
    return jnp.pad(vec, ((0, 0), (start, total - start - n)))


def _pack_small(rows5, misc, loss_row, six):
    z = jnp.zeros((1, D), F32)
    return jnp.concatenate(rows5 + [misc, loss_row, z] + [six.reshape(N_ADA, D), z, z], axis=0)


def kernel(x, c, w_ada, b_ada, norm1_g, w_in, b_forget, q_norm_g, k_norm_g, w_attn_proj, conv_w, conv_b, conv_ln_g, conv_ln_b, w_conv_proj, w_out, norm2_g, w_mlp1, w_mlp2, loss_target, m_w_ada, m_b_ada, m_norm1_g, m_w_in, m_b_forget, m_q_norm_g, m_k_norm_g, m_w_attn_proj, m_conv_w, m_conv_b, m_conv_ln_g, m_conv_ln_b, m_w_conv_proj, m_w_out, m_norm2_g, m_w_mlp1, m_w_mlp2, v_w_ada, v_b_ada, v_norm1_g, v_w_in, v_b_forget, v_q_norm_g, v_k_norm_g, v_w_attn_proj, v_conv_w, v_conv_b, v_conv_ln_g, v_conv_ln_b, v_w_conv_proj, v_w_out, v_norm2_g, v_w_mlp1, v_w_mlp2):
    me = 4 * lax.axis_index("x") + 2 * lax.axis_index("y") + lax.axis_index("c")
    xs, tgt = x[0], loss_target[0]
    T = xs.shape[0]
    sq = lambda a: a[0]
    pad_taps = lambda a: jnp.pad(a[0], ((0, CONV_KP - CONV_K), (0, 0)))

    b_slice = lax.dynamic_slice(b_ada, (0, me * ADA_SHARD), (1, ADA_SHARD))
    modb, ca_all, cwf = _ada_fwd(c, sq(w_ada), b_slice, pad_taps(conv_w))
    mod = modb.reshape(1, N_ADA * D)
    cw = jnp.transpose(cwf, (1, 0, 2)).reshape(CONV_KP, D)

    shards = [sq(w_in).astype(BF), sq(w_attn_proj).astype(BF), sq(w_conv_proj).astype(BF), sq(w_out).astype(BF),
              sq(w_mlp1).astype(BF), sq(w_mlp2).astype(BF)]
    g_in, g_ap, g_cp, g_out, g_1, g_2 = _exchange(shards, True, "weight_gather")
    d_in = g_in.shape[2] * N_DEV
    w_in_f = jnp.transpose(g_in, (1, 0, 2)).reshape(D, d_in)
    w_main = jnp.concatenate([w_in_f[:, :3 * D], w_in_f[:, 3 * D + N_HEADS:]], axis=1)
    w_f = jnp.pad(w_in_f[:, 3 * D:3 * D + N_HEADS], ((0, 0), (0, LANES - N_HEADS)))
    w_ap, w_cp, w_o = g_ap.reshape(D, D), g_cp.reshape(D, D), g_out.reshape(D, D)
    w_1 = jnp.transpose(g_1, (1, 0, 2)).reshape(D, D_FF)
    w_2 = g_2.reshape(D_FF, D)

    qg2 = jnp.tile(q_norm_g, (1, 2))
    kg2 = jnp.tile(k_norm_g, (1, 2))
    bf_pad = _lanes(b_forget, 0, LANES)

    h = _pre_in(xs, mod, norm1_g)
    proj = _matmul(h, w_main, "nn", F32, "mm_proj")
    f = _matmul(h, w_f, "nn", F32, "mm_f")
    q, k, v, fc = _qkv_post(proj, f, qg2, kg2, bf_pad)
    fc3 = fc.reshape(N_HEADS // 2, 2, T)
    o, lse_a, lse_b = _attn_fwd(q, k, v, fc3)
    ba = _matmul(o, w_ap, "nn", F32, "mm_ba")
    u0, u1, u3 = _conv_fwd(proj, cw, conv_b, conv_ln_g, conv_ln_b)
    bb = _matmul(u3, w_cp, "nn", F32, "mm_bb")
    merged = _merge(ba, bb, proj)
    mo = _matmul(merged, w_o, "nn", F32, "mm_out")
    x1, h2 = _post_out(xs, mo, mod, norm2_g)
    a = _matmul(h2, w_1, "nn", F32, "mm_mlp1")
    rl = _relu2(a)
    m2 = _matmul(rl, w_2, "nn", F32, "mm_mlp2")
    dy, dm2, dg2, sqcols = _loss_head(x1, m2, tgt, mod)

    drl = _matmul(dm2, w_2, "nt", F32, "mm_drl")
    dw_2 = _matmul(rl, dm2, "tn", BF, "mm_dw2")
    da = _relu2_bwd(drl, a)
    dh2 = _matmul(da, w_1, "nt", F32, "mm_dh2")
    dw_1 = _matmul(h2, da, "tn", BF, "mm_dw1")
    dx1, dmo, dsh2, dsc2, dn2g, dg1 = _norm2_bwd(dh2, x1, dy, mo, mod, norm2_g)
    dmerged = _matmul(dmo, w_o, "nt", F32, "mm_dmerged")
    dw_o = _matmul(merged, dmo, "tn", BF, "mm_dwout")
    dba, dbb, dga, dgb = _gate_bwd(dmerged, ba, bb, proj)
    du3 = _matmul(dbb, w_cp, "nt", F32, "mm_du3")
    dw_cp = _matmul(u3, dbb, "tn", BF, "mm_dwcp")
    do = _matmul(dba, w_ap, "nt", F32, "mm_do")
    dw_ap = _matmul(o, dba, "tn", BF, "mm_dwap")
    dglu_a, dglu_b, dlng, dlnb, dcb, dcw_full = _conv_bwd(du3, u1, u0, proj, cw, conv_ln_g, conv_ln_b)
    dq, dl_a, dl_b, rs_a, rs_b = _attn_bwd_dq(q, k, v, do, o, lse_a, lse_b, fc3)
    dk, dv, dfc3 = _attn_bwd_dkv(q, k, v, do, lse_a, lse_b, dl_a, dl_b, fc3)
    dfq = jnp.stack([rs_a, rs_b], axis=1).reshape(N_HEADS, T)
    dq_raw, dk_raw, dv_b, df, dqg, dkg, dbf = _qkv_bwd(dq, dk, dv, proj, f, dfc3.reshape(N_HEADS, T), dfq, qg2, kg2, bf_pad)
    dproj = jnp.concatenate([dq_raw, dk_raw, dv_b, dglu_a, dglu_b, dga, dgb], axis=1)
    dh = _matmul(dproj, w_main, "nt", F32, "mm_dh")
    dhf = _matmul(df, w_f, "nt", F32, "mm_dhf")
    dw_main = _matmul(h, dproj, "tn", BF, "mm_dwmain")
    dw_f = _matmul(h, df, "tn", BF, "mm_dwf")
    grad_x, dsh1, dsc1, dn1g = _norm1_bwd(dh, dhf, xs, dx1, mod, norm1_g)

    dmod = jnp.concatenate([dsh1, dsc1, dg1, dsh2, dsc2, dg2], axis=1)
    misc = jnp.concatenate([dbf, dqg, dkg, jnp.zeros((1, D - 3 * LANES), F32)], axis=1)
    pack = _pack_small([dn1g, dcb, dlng, dlnb, dn2g], misc, sqcols, dmod)
    dcw_blocks = jnp.transpose(dcw_full.reshape(CONV_KP, N_DEV, LANES), (1, 0, 2))

    def small_params(b_a, n1, bfg, qn, kn, cvb, lg, lb, n2):
        misc_p = jnp.concatenate([_lanes(bfg, 0, LANES), _lanes(qn, 0, LANES), _lanes(kn, 0, LANES),
                                  jnp.zeros((1, D - 3 * LANES), F32)], axis=1)
        return _pack_small([n1, cvb, lg, lb, n2], misc_p, jnp.zeros((1, D), F32), b_a)

    wp = small_params(b_ada, norm1_g, b_forget, q_norm_g, k_norm_g, conv_b, conv_ln_g, conv_ln_b, norm2_g)
    mp_ = small_params(m_b_ada, m_norm1_g, m_b_forget, m_q_norm_g, m_k_norm_g, m_conv_b, m_conv_ln_g, m_conv_ln_b, m_norm2_g)
    vp = small_params(v_b_ada, v_norm1_g, v_b_forget, v_q_norm_g, v_k_norm_g, v_conv_b, v_conv_ln_g, v_conv_ln_b, v_norm2_g)
    cat = jnp.pad(jnp.transpose(ca_all), ((0, 0), (0, LANES - N_DEV)))
    small = _small_bwd(pack, dmod.reshape(N_DEV, ADA_SHARD), dcw_blocks, cat,
                       wp, mp_, vp, pad_taps(conv_w), pad_taps(m_conv_w), pad_taps(v_conv_w))
    sp = small[0:4]
    scw = small[4:8]
    gw_ada, loss_t = small[8], small[9]
    loss = loss_t[0, 0]

    def unpack(p):
        misc_r = p[ROW_MISC:ROW_MISC + 1]
        return dict(
            b_ada=p[ROW_DMOD:ROW_DMOD + N_ADA].reshape(1, N_ADA * D), norm1_g=p[0:1], conv_b=p[1:2], conv_ln_g=p[2:3],
            conv_ln_b=p[3:4], norm2_g=p[4:5], b_forget=misc_r[:, 0:N_HEADS],
            q_norm_g=misc_r[:, LANES:LANES + HEAD_DIM], k_norm_g=misc_r[:, 2 * LANES:2 * LANES + HEAD_DIM])

    dw_in_f = jnp.concatenate([dw_main[:, :3 * D], dw_f[:, :N_HEADS], dw_main[:, 3 * D:]], axis=1)
    parts = [jnp.transpose(dw_in_f.reshape(D, N_DEV, d_in // N_DEV), (1, 0, 2)),
             dw_ap.reshape(N_DEV, D // N_DEV, D), dw_cp.reshape(N_DEV, D // N_DEV, D), dw_o.reshape(N_DEV, D // N_DEV, D),
             jnp.transpose(dw_1.reshape(D, N_DEV, D_FF // N_DEV), (1, 0, 2)), dw_2.reshape(N_DEV, D_FF // N_DEV, D)]
    r_in, r_ap, r_cp, r_out, r_1, r_2 = _exchange(parts, False, "grad_scatter")

    res = {}
    res["w_ada"] = _adamw(gw_ada[None], sq(w_ada), sq(m_w_ada), sq(v_w_ada), "adamw_w_ada")
    res["w_in"] = _adamw(r_in, sq(w_in), sq(m_w_in), sq(v_w_in), "adamw_w_in")
    res["w_attn_proj"] = _adamw(r_ap, sq(w_attn_proj), sq(m_w_attn_proj), sq(v_w_attn_proj), "adamw_w_ap")
    res["w_conv_proj"] = _adamw(r_cp, sq(w_conv_proj), sq(m_w_conv_proj), sq(v_w_conv_proj), "adamw_w_cp")
    res["w_out"] = _adamw(r_out, sq(w_out), sq(m_w_out), sq(v_w_out), "adamw_w_out")
    res["w_mlp1"] = _adamw(r_1, sq(w_mlp1), sq(m_w_mlp1), sq(v_w_mlp1), "adamw_w_mlp1")
    res["w_mlp2"] = _adamw(r_2, sq(w_mlp2), sq(m_w_mlp2), sq(v_w_mlp2), "adamw_w_mlp2")

    names = ["w_ada", "b_ada", "norm1_g", "w_in", "b_forget", "q_norm_g", "k_norm_g", "w_attn_proj", "conv_w", "conv_b",
             "conv_ln_g", "conv_ln_b", "w_conv_proj", "w_out", "norm2_g", "w_mlp1", "w_mlp2"]
    outs = [loss, grad_x[None]]
    for kind in range(4):
        small_d = unpack(sp[kind])
        for nm in names:
            if nm in res:
                outs.append(res[nm][kind][None])
            elif nm == "conv_w":
                outs.append(scw[kind][:CONV_K][None])
            else:
                outs.append(small_d[nm])
    return tuple(outs)
```

```python
import functools

import jax
import jax.numpy as jnp
from jax import lax
from jax.experimental import pallas as pl
from jax.experimental.pallas import tpu as pltpu

F32 = jnp.float32
BF = jnp.bfloat16

N_DEV = 8
D = 1024
N_HEADS = 16
HEAD_DIM = 64
LANES = 128
CONV_K = 31
CONV_KP = 32
HALO = 32
D_FF = 4 * D
N_ADA = 6
ADA_SHARD = N_ADA * D // N_DEV
EPS = 1e-6
QK_SCALE = HEAD_DIM ** -0.5
NEG = -1e30

ADAM_LR = 0.001
ADAM_B1 = 0.9
ADAM_B2 = 0.999
ADAM_EPS = 1e-08
ADAM_WD = 0.01
ADAM_STEP = 10

VMEM_LIMIT = 56 * 1024 * 1024
TM_ROWS = 256
TQ = 512

MESH = pl.DeviceIdType.MESH


def _cp(sem=None):
    return pltpu.CompilerParams(dimension_semantics=sem, vmem_limit_bytes=VMEM_LIMIT)


def _sds(shape, dtype):
    return jax.ShapeDtypeStruct(tuple(shape), dtype)


def _full(arr):
    nd = arr.ndim
    return pl.BlockSpec(arr.shape, lambda *_: (0,) * nd)


def _fullshape(shape):
    nd = len(shape)
    return pl.BlockSpec(tuple(shape), lambda *_: (0,) * nd)


def _split3(x):
    hi = x.astype(BF)
    r1 = x - hi.astype(F32)
    mid = r1.astype(BF)
    lo = (r1 - mid.astype(F32)).astype(BF)
    return hi, mid, lo


def _dot_exact(x, mat):
    hi, mid, lo = _split3(x)
    d = lambda t: jnp.dot(t, mat, preferred_element_type=F32)
    return d(hi) + d(mid) + d(lo)


def _dot_f32(a, b):
    a1, a2, a3 = _split3(a)
    b1, b2, b3 = _split3(b)
    d = lambda s, t: jnp.dot(s, t, preferred_element_type=F32)
    return (d(a1, b3) + d(a3, b1) + d(a2, b2)) + (d(a1, b2) + d(a2, b1)) + d(a1, b1)


def _sigmoid(x):
    return 1.0 / (1.0 + jnp.exp(-x))


def _colsum(x):
    return jnp.sum(x, axis=0, keepdims=True)


def _my_pos():
    x, y, c = lax.axis_index("x"), lax.axis_index("y"), lax.axis_index("c")
    return x, y, c, 4 * x + 2 * y + c


def _peer(x, y, c, d):
    px = (1 - x) if d & 4 else x
    py = (1 - y) if d & 2 else y
    pc = (1 - c) if d & 1 else c
    return (px, py, pc), 4 * px + 2 * py + pc


def _matmul(a, b, form, out_dtype, name, tm=512, tn=1024, tk=1024, scatter=()):
    if form == "nn":
        (M, K), N = a.shape, b.shape[1]
    elif form == "nt":
        (M, K), N = a.shape, b.shape[0]
    else:
        (K, M), N = a.shape, b.shape[1]
    tm, tn, tk = min(tm, M), min(tn, N), min(tk, K)
    assert M % tm == 0 and N % tn == 0 and K % tk == 0, (name, M, N, K)
    nk = K // tk
    if form == "tn":
        a_spec = pl.BlockSpec((tk, tm), lambda i, j, k: (k, i))
        dn = (((0,), (0,)), ((), ()))
    else:
        a_spec = pl.BlockSpec((tm, tk), lambda i, j, k: (i, k))
        dn = (((1,), (1 if form == "nt" else 0,)), ((), ()))
    if form == "nt":
        b_spec = pl.BlockSpec((tn, tk), lambda i, j, k: (j, k))
    else:
        b_spec = pl.BlockSpec((tk, tn), lambda i, j, k: (k, j))

    nx = len(scatter)
    grid = (M // tm, N // tn, nk)

    def body(a_ref, b_ref, *rest):
        x_in, o_ref, x_out = rest[:nx], rest[nx], rest[nx + 1:2 * nx + 1]
        scr = rest[2 * nx + 1:]
        if nx:
            first, last = _first_last(grid)

            @pl.when(first)
            def _():
                _xchg(x_in, x_out, scr[-3:], False, wait=False)

        part = lax.dot_general(a_ref[...].astype(BF), b_ref[...].astype(BF), dn, preferred_element_type=F32)
        if nk == 1:
            o_ref[...] = part.astype(out_dtype)
        else:
            acc = scr[0]
            k = pl.program_id(2)

            @pl.when(k == 0)
            def _():
                acc[...] = part

            @pl.when(k > 0)
            def _():
                acc[...] += part

            @pl.when(k == nk - 1)
            def _():
                o_ref[...] = acc[...].astype(out_dtype)

        if nx:
            @pl.when(last)
            def _():
                _xchg(x_in, x_out, scr[-3:], False, wait=True)

    x_specs, x_shapes, x_scratch = _xchg_parts(scatter) if nx else ([], [], [])
    sem = ("arbitrary",) * 3 if nx else ("parallel", "parallel", "arbitrary")
    res = pl.pallas_call(
        body, name=name, grid=grid,
        in_specs=[a_spec, b_spec] + x_specs,
        out_specs=[pl.BlockSpec((tm, tn), lambda i, j, k: (i, j))] + x_specs,
        out_shape=[_sds((M, N), out_dtype)] + x_shapes,
        scratch_shapes=([] if nk == 1 else [pltpu.VMEM((tm, tn), F32)]) + x_scratch,
        compiler_params=_cp(sem),
    )(a, b, *scatter)
    return res if nx else res[0]


def _rows_call(body, name, n_tiles, ins, outs, scratch=()):
    res = pl.pallas_call(
        body, name=name, grid=(n_tiles,),
        in_specs=[s for _, s in ins],
        out_specs=[s for _, s in outs],
        out_shape=[o for o, _ in outs],
        scratch_shapes=list(scratch),
        compiler_params=_cp(("arbitrary",)),
    )(*[a for a, _ in ins])
    return res


def _rspec(tm, width, cb=0, rev_n=None):
    if rev_n is None:
        return pl.BlockSpec((tm, width), lambda i: (i, cb))
    return pl.BlockSpec((tm, width), lambda i: (rev_n - 1 - i, cb))


def _row_out(T, tm, width, dtype, rev_n=None):
    return (_sds((T, width), dtype), _rspec(tm, width, 0, rev_n))


def _acc_out(shape, dtype=F32):
    return (_sds(shape, dtype), _fullshape(shape))


def _mod_parts(mod):
    return [mod[:, i * D:(i + 1) * D] for i in range(N_ADA)]


def _pre_in(x, mod, n1g):
    T = x.shape[0]
    tm = TM_ROWS

    def body(x_ref, mod_ref, g_ref, h_ref):
        sh1, sc1 = mod_ref[:, 0:D], mod_ref[:, D:2 * D]
        xv = x_ref[...]
        r = lax.rsqrt(jnp.mean(xv * xv, axis=-1, keepdims=True) + EPS)
        h_ref[...] = ((xv * r) * g_ref[...] * (1.0 + sc1) + sh1).astype(BF)

    return _rows_call(body, "pre_in", T // tm,
                      [(x, _rspec(tm, D)), (mod, _full(mod)), (n1g, _full(n1g))],
                      [_row_out(T, tm, D, BF)])[0]


def _seg_mat():
    r = jnp.arange(LANES)[:, None] // HEAD_DIM
    c = jnp.arange(LANES)[None, :] // HEAD_DIM
    return jnp.where(r == c, 1.0 / HEAD_DIM, 0.0).astype(BF)


def _tri_mat(n, upper):
    r = jnp.arange(n)[:, None]
    c = jnp.arange(n)[None, :]
    return jnp.where((r <= c) if upper else (r >= c), 1.0, 0.0).astype(BF)


def _log_sigmoid(z):
    return jnp.minimum(z, 0.0) - jnp.log(1.0 + jnp.exp(-jnp.abs(z)))


def _qkv_post(proj, f, qg2, kg2, bf_pad):
    T = proj.shape[0]
    tm = TM_ROWS
    seg = _seg_mat()
    tri = _tri_mat(tm, True)

    def body(q_ref, k_ref, v_ref, f_ref, qg_ref, kg_ref, bf_ref, seg_ref, tri_ref,
             qo_ref, ko_ref, vo_ref, fc_ref, carry_ref):
        i = pl.program_id(0)

        @pl.when(i == 0)
        def _():
            carry_ref[...] = jnp.zeros_like(carry_ref)

        segm = seg_ref[...]
        for j in range(D // LANES):
            sl = slice(j * LANES, (j + 1) * LANES)
            qc = q_ref[:, sl]
            rq = lax.rsqrt(_dot_exact(qc * qc, segm) + EPS)
            qo_ref[:, sl] = ((qc * rq) * qg_ref[...] * QK_SCALE).astype(BF)
            kc = k_ref[:, sl]
            rk = lax.rsqrt(_dot_exact(kc * kc, segm) + EPS)
            ko_ref[:, sl] = ((kc * rk) * kg_ref[...]).astype(BF)
        vo_ref[...] = v_ref[...].astype(BF)
        lf = _log_sigmoid(f_ref[...] + bf_ref[...])
        lft = lf.T[0:N_HEADS, :]
        carry = carry_ref[:, 0:1]
        fc_ref[...] = _dot_exact(lft, tri_ref[...]) + carry
        carry_ref[...] = jnp.broadcast_to(carry + jnp.sum(lft, axis=1, keepdims=True), carry_ref.shape)

    outs = [_row_out(T, tm, D, BF), _row_out(T, tm, D, BF), _row_out(T, tm, D, BF),
            (_sds((N_HEADS, T), F32), pl.BlockSpec((N_HEADS, tm), lambda i: (0, i)))]
    ins = [(proj, _rspec(tm, D, 0)), (proj, _rspec(tm, D, 1)), (proj, _rspec(tm, D, 2)), (f, _rspec(tm, LANES)),
           (qg2, _full(qg2)), (kg2, _full(kg2)), (bf_pad, _full(bf_pad)), (seg, _full(seg)), (tri, _full(tri))]
    return _rows_call(body, "qkv_post", T // tm, ins, outs, [pltpu.VMEM((N_HEADS, LANES), F32)])


def _lane_lo():
    return lax.broadcasted_iota(jnp.int32, (1, LANES), 1) < HEAD_DIM


def _nt(a, b):
    return lax.dot_general(a, b, (((1,), (1,)), ((), ())), preferred_element_type=F32)


def _tn(a, b):
    return lax.dot_general(a, b, (((0,), (0,)), ((), ())), preferred_element_type=F32)


def _head_rep(x, lo):
    rolled = pltpu.roll(x, HEAD_DIM, axis=1)
    return jnp.where(lo, x, rolled), jnp.where(lo, rolled, x)


def _diag_mask(t):
    return lax.broadcasted_iota(jnp.int32, (t, t), 1) <= lax.broadcasted_iota(jnp.int32, (t, t), 0)


def _first_last(grid):
    ids = [pl.program_id(a) for a in range(len(grid))]
    first = functools.reduce(jnp.logical_and, [i == 0 for i in ids])
    last = functools.reduce(jnp.logical_and, [i == g - 1 for i, g in zip(ids, grid)])
    return first, last


def _flash_fwd(q, k, v, fc3, shards):
    T = q.shape[0]
    tq = TQ
    nq = T // tq
    hp_n = N_HEADS // 2
    rep = tq // LANES
    nx = len(shards)
    grid = (hp_n, nq, nq)

    def body(q_ref, k_ref, v_ref, fk_ref, fq_ref, *rest):
        x_in, (o_ref, lse_ref), x_out = rest[:nx], rest[nx:nx + 2], rest[nx + 2:2 * nx + 2]
        acc_ref, m_ref, l_ref = rest[2 * nx + 2:2 * nx + 5]
        sems = rest[2 * nx + 5:]
        qi, ki = pl.program_id(1), pl.program_id(2)
        first, last = _first_last(grid)

        @pl.when(first)
        def _():
            _xchg(x_in, x_out, sems, True, wait=False)

        @pl.when(ki == 0)
        def _():
            acc_ref[...] = jnp.zeros_like(acc_ref)
            m_ref[...] = jnp.full_like(m_ref, NEG)
            l_ref[...] = jnp.zeros_like(l_ref)

        def step(diag):
            lo = _lane_lo()
            q2, k2, v2 = q_ref[...], k_ref[...], v_ref[...]
            zero = jnp.zeros_like(k2)
            bias = fq_ref[:, 0:1] - fk_ref[...]
            alphas, pvs = [], []
            for hh in range(2):
                sel = (lambda t: jnp.where(lo, t, zero)) if hh == 0 else (lambda t: jnp.where(lo, zero, t))
                s = _nt(sel(q2), k2) + bias[hh:hh + 1, :]
                if diag:
                    s = jnp.where(_diag_mask(tq), s, NEG)
                m_old = m_ref[hh]
                m_new = jnp.maximum(m_old, jnp.max(s, axis=-1, keepdims=True))
                alpha = jnp.exp(m_old - m_new)
                p = jnp.exp(s - jnp.tile(m_new, (1, rep)))
                l_ref[hh] = alpha * l_ref[hh] + jnp.sum(p, axis=-1, keepdims=True)
                m_ref[hh] = m_new
                pvs.append(jnp.dot(p.astype(BF), sel(v2), preferred_element_type=F32))
                alphas.append(alpha)
            acc_ref[...] = acc_ref[...] * jnp.where(lo, alphas[0], alphas[1]) + pvs[0] + pvs[1]

        @pl.when(ki < qi)
        def _():
            step(False)

        @pl.when(ki == qi)
        def _():
            step(True)
            lo = _lane_lo()
            la, lb = l_ref[0], l_ref[1]
            o_ref[...] = acc_ref[...] * jnp.where(lo, 1.0 / la, 1.0 / lb)
            lse_ref[...] = jnp.where(lo, m_ref[0] + jnp.log(la), m_ref[1] + jnp.log(lb))

        @pl.when(last)
        def _():
            _xchg(x_in, x_out, sems, True, wait=True)

    qspec = pl.BlockSpec((tq, LANES), lambda h, i, j: (i, h))
    kspec = pl.BlockSpec((tq, LANES), lambda h, i, j: (jnp.minimum(i, j), h))
    fkspec = pl.BlockSpec((None, 2, tq), lambda h, i, j: (h, 0, jnp.minimum(i, j)))
    fqspec = pl.BlockSpec((None, 2, tq), lambda h, i, j: (h, 0, i))
    x_specs, x_shapes, x_scratch = _xchg_parts(shards)
    return pl.pallas_call(
        body, name="attn_fwd", grid=grid,
        in_specs=[qspec, kspec, kspec, fkspec, fqspec] + x_specs,
        out_specs=[qspec, qspec] + x_specs,
        out_shape=[_sds((T, D), F32), _sds((T, D), F32)] + x_shapes,
        scratch_shapes=[pltpu.VMEM((tq, LANES), F32), pltpu.VMEM((2, tq, LANES), F32),
                        pltpu.VMEM((2, tq, LANES), F32)] + x_scratch,
        compiler_params=_cp(("arbitrary", "arbitrary", "arbitrary")),
    )(q, k, v, fc3, fc3, *shards)


def _flash_bwd_dq(q, k, v, do, o, lse, fc3):
    T = q.shape[0]
    tq = TQ
    nq = T // tq
    hp_n = N_HEADS // 2
    rep = tq // LANES

    def body(q_ref, k_ref, v_ref, do_ref, o_ref, lse_ref, fk_ref, fq_ref,
             dq_ref, dl_out_ref, ra_ref, rb_ref, acc_ref, dl_ref, ls_ref, rs_ref):
        qi, ki = pl.program_id(1), pl.program_id(2)

        @pl.when(ki == 0)
        def _():
            lo = _lane_lo()
            acc_ref[...] = jnp.zeros_like(acc_ref)
            rs_ref[...] = jnp.zeros_like(rs_ref)
            prod = do_ref[...].astype(BF).astype(F32) * o_ref[...]
            da = jnp.broadcast_to(jnp.sum(jnp.where(lo, prod, 0.0), axis=-1, keepdims=True), (tq, LANES))
            db = jnp.broadcast_to(jnp.sum(jnp.where(lo, 0.0, prod), axis=-1, keepdims=True), (tq, LANES))
            dl_ref[0] = da
            dl_ref[1] = db
            dl_out_ref[...] = jnp.where(lo, da, db)
            la, lb = _head_rep(lse_ref[...], lo)
            ls_ref[0] = la
            ls_ref[1] = lb

        def step(diag):
            lo = _lane_lo()
            q2, k2, v2 = q_ref[...], k_ref[...], v_ref[...]
            do2 = do_ref[...].astype(BF)
            zero = jnp.zeros_like(q2)
            bias = fq_ref[:, 0:1] - fk_ref[...]
            tot = None
            for hh in range(2):
                sel = (lambda t: jnp.where(lo, t, zero)) if hh == 0 else (lambda t: jnp.where(lo, zero, t))
                s = _nt(sel(q2), k2) + bias[hh:hh + 1, :]
                if diag:
                    s = jnp.where(_diag_mask(tq), s, NEG)
                p = jnp.exp(s - jnp.tile(ls_ref[hh], (1, rep)))
                dp = _nt(sel(do2), v2)
                ds = p * (dp - jnp.tile(dl_ref[hh], (1, rep)))
                rs_ref[hh] += jnp.sum(ds, axis=-1, keepdims=True)
                part = jnp.dot(ds.astype(BF), sel(k2), preferred_element_type=F32)
                tot = part if tot is None else tot + part
            acc_ref[...] += tot

        @pl.when(ki < qi)
        def _():
            step(False)

        @pl.when(ki == qi)
        def _():
            step(True)
            dq_ref[...] = acc_ref[...] * QK_SCALE
            ra_ref[...] = rs_ref[0][:, 0:1]
            rb_ref[...] = rs_ref[1][:, 0:1]

    qspec = pl.BlockSpec((tq, LANES), lambda h, i, j: (i, h))
    kspec = pl.BlockSpec((tq, LANES), lambda h, i, j: (jnp.minimum(i, j), h))
    fkspec = pl.BlockSpec((None, 2, tq), lambda h, i, j: (h, 0, jnp.minimum(i, j)))
    fqspec = pl.BlockSpec((None, 2, tq), lambda h, i, j: (h, 0, i))
    lspec = pl.BlockSpec((None, tq, 1), lambda h, i, j: (h, i, 0))
    return pl.pallas_call(
        body, name="attn_bwd_dq", grid=(hp_n, nq, nq),
        in_specs=[qspec, kspec, kspec, qspec, qspec, qspec, fkspec, fqspec],
        out_specs=[qspec, qspec, lspec, lspec],
        out_shape=[_sds((T, D), F32), _sds((T, D), F32), _sds((hp_n, T, 1), F32), _sds((hp_n, T, 1), F32)],
        scratch_shapes=[pltpu.VMEM((tq, LANES), F32)] + [pltpu.VMEM((2, tq, LANES), F32)] * 3,
        compiler_params=_cp(("parallel", "parallel", "arbitrary")),
    )(q, k, v, do, o, lse, fc3, fc3)


def _flash_bwd_dkv(q, k, v, do, lse, delta, fc3, parts):
    T = q.shape[0]
    tq = TQ
    nq = T // tq
    hp_n = N_HEADS // 2
    rep = tq // LANES
    nx = len(parts)
    grid = (hp_n, nq, nq)

    def body(q_ref, k_ref, v_ref, do_ref, lse_ref, dl_ref, fk_ref, fq_ref, *rest):
        x_in, (dk_ref, dv_ref, dfc_ref), x_out = rest[:nx], rest[nx:nx + 3], rest[nx + 3:2 * nx + 3]
        dk_acc, dv_acc, df_acc = rest[2 * nx + 3:2 * nx + 6]
        sems = rest[2 * nx + 6:]
        ki, qi = pl.program_id(1), pl.program_id(2)
        first, last = _first_last(grid)

        @pl.when(first)
        def _():
            _xchg(x_in, x_out, sems, False, wait=False)

        @pl.when(qi == 0)
        def _():
            dk_acc[...] = jnp.zeros_like(dk_acc)
            dv_acc[...] = jnp.zeros_like(dv_acc)
            df_acc[...] = jnp.zeros_like(df_acc)

        def step(diag):
            lo = _lane_lo()
            q2, k2, v2 = q_ref[...], k_ref[...], v_ref[...]
            do2 = do_ref[...].astype(BF)
            zero = jnp.zeros_like(q2)
            bias = fq_ref[:, 0:1] - fk_ref[...]
            lses = _head_rep(lse_ref[...], lo)
            dls = _head_rep(dl_ref[...], lo)
            dk_t = None
            dv_t = None
            for hh in range(2):
                sel = (lambda t: jnp.where(lo, t, zero)) if hh == 0 else (lambda t: jnp.where(lo, zero, t))
                s = _nt(sel(q2), k2) + bias[hh:hh + 1, :]
                if diag:
                    s = jnp.where(_diag_mask(tq), s, NEG)
                p = jnp.exp(s - jnp.tile(lses[hh], (1, rep)))
                dp = _nt(sel(do2), v2)
                ds = p * (dp - jnp.tile(dls[hh], (1, rep)))
                dvp = _tn(p.astype(BF), sel(do2))
                dkp = _tn(ds.astype(BF), sel(q2))
                dv_t = dvp if dv_t is None else dv_t + dvp
                dk_t = dkp if dk_t is None else dk_t + dkp
                df_acc[hh:hh + 1, :] -= _colsum(ds)
            dk_acc[...] += dk_t
            dv_acc[...] += dv_t

        @pl.when(qi > ki)
        def _():
            step(False)

        @pl.when(qi == ki)
        def _():
            step(True)

        @pl.when(qi == nq - 1)
        def _():
            dk_ref[...] = dk_acc[...]
            dv_ref[...] = dv_acc[...]
            dfc_ref[...] = df_acc[...]

        @pl.when(last)
        def _():
            _xchg(x_in, x_out, sems, False, wait=True)

    kspec = pl.BlockSpec((tq, LANES), lambda h, j, i: (j, h))
    qspec = pl.BlockSpec((tq, LANES), lambda h, j, i: (jnp.maximum(i, j), h))
    fkspec = pl.BlockSpec((None, 2, tq), lambda h, j, i: (h, 0, j))
    fqspec = pl.BlockSpec((None, 2, tq), lambda h, j, i: (h, 0, jnp.maximum(i, j)))
    x_specs, x_shapes, x_scratch = _xchg_parts(parts)
    return pl.pallas_call(
        body, name="attn_bwd_dkv", grid=grid,
        in_specs=[qspec, kspec, kspec, qspec, qspec, qspec, fkspec, fqspec] + x_specs,
        out_specs=[kspec, kspec, fkspec] + x_specs,
        out_shape=[_sds((T, D), F32), _sds((T, D), F32), _sds((hp_n, 2, T), F32)] + x_shapes,
        scratch_shapes=[pltpu.VMEM((tq, LANES), F32), pltpu.VMEM((tq, LANES), F32), pltpu.VMEM((2, tq), F32)] + x_scratch,
        compiler_params=_cp(("arbitrary", "arbitrary", "arbitrary")),
    )(q, k, v, do, lse, delta, fc3, fc3, *parts)


def _layer_norm_stats(u1):
    mu = jnp.mean(u1, axis=-1, keepdims=True)
    xc = u1 - mu
    rstd = lax.rsqrt(jnp.mean(xc * xc, axis=-1, keepdims=True) + EPS)
    return xc * rstd, rstd


def _conv_fwd(proj, cw, cb, lng, lnb):
    T = proj.shape[0]
    tm = TM_ROWS

    def body(a_ref, b_ref, w_ref, cb_ref, g_ref, bb_ref, u0_ref, u1_ref, u3_ref, buf):
        i = pl.program_id(0)

        @pl.when(i == 0)
        def _():
            buf[0:HALO, :] = jnp.zeros((HALO, D), F32)

        u0 = a_ref[...] * _sigmoid(b_ref[...])
        u0_ref[...] = u0
        buf[HALO:HALO + tm, :] = u0
        for j in range(D // LANES):
            sl = slice(j * LANES, (j + 1) * LANES)
            acc = jnp.broadcast_to(cb_ref[:, sl], (tm, LANES))
            for kk in range(CONV_K):
                off = HALO - (CONV_K - 1) + kk
                acc = acc + w_ref[kk:kk + 1, sl] * buf[off:off + tm, sl]
            u1_ref[:, sl] = acc
        buf[0:HALO, :] = buf[tm:tm + HALO, :]
        xh, _ = _layer_norm_stats(u1_ref[...])
        u2 = xh * g_ref[...] + bb_ref[...]
        u3_ref[...] = (u2 * _sigmoid(u2)).astype(BF)

    ins = [(proj, _rspec(tm, D, 3)), (proj, _rspec(tm, D, 4)), (cw, _full(cw)), (cb, _full(cb)),
           (lng, _full(lng)), (lnb, _full(lnb))]
    outs = [_row_out(T, tm, D, F32), _row_out(T, tm, D, F32), _row_out(T, tm, D, BF)]
    return _rows_call(body, "conv_fwd", T // tm, ins, outs, [pltpu.VMEM((tm + HALO, D), F32)])


def _conv_bwd(du3, u1, u0, proj, cw, lng, lnb):
    T = du3.shape[0]
    tm = TM_ROWS
    n = T // tm
    per = tm // HALO

    def body(du3_ref, u1_ref, u0_ref, halo_ref, a_ref, b_ref, w_ref, g_ref, bb_ref,
             da_ref, db_ref, dg_ref, dbb_ref, dcb_ref, dw_ref, dbuf, ubuf, du0_buf):
        i = pl.program_id(0)
        r = n - 1 - i

        @pl.when(i == 0)
        def _():
            dbuf[tm:tm + HALO, :] = jnp.zeros((HALO, D), F32)
            dg_ref[...] = jnp.zeros_like(dg_ref)
            dbb_ref[...] = jnp.zeros_like(dbb_ref)
            dcb_ref[...] = jnp.zeros_like(dcb_ref)
            dw_ref[...] = jnp.zeros_like(dw_ref)

        xh, rstd = _layer_norm_stats(u1_ref[...])
        g = g_ref[...]
        u2 = xh * g + bb_ref[...]
        s2 = _sigmoid(u2)
        du2 = du3_ref[...] * (s2 * (1.0 + u2 * (1.0 - s2)))
        dg_ref[...] += _colsum(du2 * xh)
        dbb_ref[...] += _colsum(du2)
        dxh = du2 * g
        du1 = rstd * (dxh - jnp.mean(dxh, axis=-1, keepdims=True) - xh * jnp.mean(dxh * xh, axis=-1, keepdims=True))
        dcb_ref[...] += _colsum(du1)
        dbuf[0:tm, :] = du1
        ubuf[HALO:HALO + tm, :] = u0_ref[...]
        ubuf[0:HALO, :] = jnp.where(r > 0, halo_ref[...], 0.0)
        for j in range(D // LANES):
            sl = slice(j * LANES, (j + 1) * LANES)
            d1 = dbuf[0:tm, sl]
            acc = jnp.zeros((tm, LANES), F32)
            for kk in range(CONV_K):
                acc = acc + w_ref[kk:kk + 1, sl] * dbuf[CONV_K - 1 - kk:CONV_K - 1 - kk + tm, sl]
                off = HALO - (CONV_K - 1) + kk
                dw_ref[kk:kk + 1, sl] += _colsum(d1 * ubuf[off:off + tm, sl])
            du0_buf[:, sl] = acc
        dbuf[tm:tm + HALO, :] = dbuf[0:HALO, :]
        du0 = du0_buf[...]
        sb = _sigmoid(b_ref[...])
        da_ref[...] = (du0 * sb).astype(BF)
        db_ref[...] = (du0 * a_ref[...] * sb * (1.0 - sb)).astype(BF)

    rs = lambda cb: _rspec(tm, D, cb, n)
    halo_spec = pl.BlockSpec((HALO, D), lambda i: (jnp.maximum((n - 1 - i) * per - 1, 0), 0))
    ins = [(du3, rs(0)), (u1, rs(0)), (u0, rs(0)), (u0, halo_spec), (proj, rs(3)), (proj, rs(4)),
           (cw, _full(cw)), (lng, _full(lng)), (lnb, _full(lnb))]
    outs = [_row_out(T, tm, D, BF, n), _row_out(T, tm, D, BF, n),
            _acc_out((1, D)), _acc_out((1, D)), _acc_out((1, D)), _acc_out((CONV_KP, D))]
    return _rows_call(body, "conv_bwd", n, ins, outs,
                      [pltpu.VMEM((tm + HALO, D), F32), pltpu.VMEM((tm + HALO, D), F32), pltpu.VMEM((tm, D), F32)])


def _merge(ba, bb, proj):
    T = ba.shape[0]
    tm = TM_ROWS

    def body(ba_ref, bb_ref, ga_ref, gb_ref, o_ref):
        o_ref[...] = (_sigmoid(ga_ref[...]) * ba_ref[...] + _sigmoid(gb_ref[...]) * bb_ref[...]).astype(BF)

    ins = [(ba, _rspec(tm, D)), (bb, _rspec(tm, D)), (proj, _rspec(tm, D, 5)), (proj, _rspec(tm, D, 6))]
    return _rows_call(body, "merge", T // tm, ins, [_row_out(T, tm, D, BF)])[0]


def _post_out(x, mo, mod, n2g):
    T = x.shape[0]
    tm = TM_ROWS

    def body(x_ref, mo_ref, mod_ref, g_ref, x1_ref, h2_ref):
        g1 = mod_ref[:, 2 * D:3 * D]
        sh2, sc2 = mod_ref[:, 3 * D:4 * D], mod_ref[:, 4 * D:5 * D]
        x1 = x_ref[...] + g1 * mo_ref[...]
        x1_ref[...] = x1
        r = lax.rsqrt(jnp.mean(x1 * x1, axis=-1, keepdims=True) + EPS)
        h2_ref[...] = ((x1 * r) * g_ref[...] * (1.0 + sc2) + sh2).astype(BF)

    ins = [(x, _rspec(tm, D)), (mo, _rspec(tm, D)), (mod, _full(mod)), (n2g, _full(n2g))]
    return _rows_call(body, "post_out", T // tm, ins, [_row_out(T, tm, D, F32), _row_out(T, tm, D, BF)])


def _relu2(a):
    T = a.shape[0]
    tm = TM_ROWS

    def body(a_ref, o_ref):
        r = jnp.maximum(a_ref[...], 0.0)
        o_ref[...] = (r * r).astype(BF)

    return _rows_call(body, "relu2", T // tm, [(a, _rspec(tm, D_FF))], [_row_out(T, tm, D_FF, BF)])[0]


def _loss_head(x1, m2, tgt, mod):
    T = x1.shape[0]
    tm = TM_ROWS

    def body(x1_ref, m2_ref, t_ref, mod_ref, dy_ref, dm2_ref, dg2_ref, sq_ref):
        i = pl.program_id(0)

        @pl.when(i == 0)
        def _():
            dg2_ref[...] = jnp.zeros_like(dg2_ref)
            sq_ref[...] = jnp.zeros_like(sq_ref)

        g2 = mod_ref[:, 5 * D:6 * D]
        m2 = m2_ref[...]
        err = x1_ref[...] + g2 * m2 - t_ref[...]
        dy = err * (1.0 / D)
        dy_ref[...] = dy
        dm2_ref[...] = (g2 * dy).astype(BF)
        dg2_ref[...] += _colsum(dy * m2)
        sq_ref[...] += _colsum(err * err)

    ins = [(x1, _rspec(tm, D)), (m2, _rspec(tm, D)), (tgt, _rspec(tm, D)), (mod, _full(mod))]
    outs = [_row_out(T, tm, D, F32), _row_out(T, tm, D, BF), _acc_out((1, D)), _acc_out((1, D))]
    return _rows_call(body, "loss_head", T // tm, ins, outs)


def _relu2_bwd(drl, a):
    T = a.shape[0]
    tm = TM_ROWS

    def body(d_ref, a_ref, o_ref):
        o_ref[...] = (d_ref[...] * (2.0 * jnp.maximum(a_ref[...], 0.0))).astype(BF)

    ins = [(drl, _rspec(tm, D_FF)), (a, _rspec(tm, D_FF))]
    return _rows_call(body, "relu2_bwd", T // tm, ins, [_row_out(T, tm, D_FF, BF)])[0]


def _norm2_bwd(dh2, x1, dy, mo, mod, n2g):
    T = x1.shape[0]
    tm = TM_ROWS

    def body(dh_ref, x1_ref, dy_ref, mo_ref, mod_ref, g_ref, dx1_ref, dmo_ref, dsh_ref, dsc_ref, dg_ref, dg1_ref):
        i = pl.program_id(0)

        @pl.when(i == 0)
        def _():
            for r in (dsh_ref, dsc_ref, dg_ref, dg1_ref):
                r[...] = jnp.zeros_like(r)

        g1, sc2 = mod_ref[:, 2 * D:3 * D], mod_ref[:, 4 * D:5 * D]
        g = g_ref[...]
        x1 = x1_ref[...]
        dh = dh_ref[...]
        r = lax.rsqrt(jnp.mean(x1 * x1, axis=-1, keepdims=True) + EPS)
        xn = x1 * r
        dsh_ref[...] += _colsum(dh)
        dsc_ref[...] += _colsum(dh * xn * g)
        dg_ref[...] += _colsum(dh * xn * (1.0 + sc2))
        dxn = dh * g * (1.0 + sc2)
        dx1 = dy_ref[...] + r * (dxn - xn * jnp.mean(dxn * xn, axis=-1, keepdims=True))
        dx1_ref[...] = dx1
        dg1_ref[...] += _colsum(dx1 * mo_ref[...])
        dmo_ref[...] = (g1 * dx1).astype(BF)

    ins = [(dh2, _rspec(tm, D)), (x1, _rspec(tm, D)), (dy, _rspec(tm, D)), (mo, _rspec(tm, D)),
           (mod, _full(mod)), (n2g, _full(n2g))]
    outs = [_row_out(T, tm, D, F32), _row_out(T, tm, D, BF)] + [_acc_out((1, D)) for _ in range(4)]
    return _rows_call(body, "norm2_bwd", T // tm, ins, outs)


def _gate_bwd(dmerged, ba, bb, proj):
    T = ba.shape[0]
    tm = TM_ROWS

    def body(dm_ref, ba_ref, bb_ref, ga_ref, gb_ref, dba_ref, dbb_ref, dga_ref, dgb_ref):
        dm = dm_ref[...]
        sa, sb = _sigmoid(ga_ref[...]), _sigmoid(gb_ref[...])
        dba_ref[...] = (dm * sa).astype(BF)
        dbb_ref[...] = (dm * sb).astype(BF)
        dga_ref[...] = (dm * ba_ref[...] * sa * (1.0 - sa)).astype(BF)
        dgb_ref[...] = (dm * bb_ref[...] * sb * (1.0 - sb)).astype(BF)

    ins = [(dmerged, _rspec(tm, D)), (ba, _rspec(tm, D)), (bb, _rspec(tm, D)),
           (proj, _rspec(tm, D, 5)), (proj, _rspec(tm, D, 6))]
    return _rows_call(body, "gate_bwd", T // tm, ins, [_row_out(T, tm, D, BF) for _ in range(4)])


def _qkv_bwd(dq, dk, dv, proj, f, dfc, dfq, qg2, kg2, bf_pad):
    T = proj.shape[0]
    tm = TM_ROWS
    n = T // tm
    seg = _seg_mat()
    tri = _tri_mat(tm, False)

    def body(dq_ref, dk_ref, dv_ref, q_ref, k_ref, f_ref, dfc_ref, dfq_ref, qg_ref, kg_ref, bf_ref, seg_ref, tri_ref,
             dqo_ref, dko_ref, dvo_ref, dfo_ref, dqg_ref, dkg_ref, dbf_ref, carry_ref):
        i = pl.program_id(0)

        @pl.when(i == 0)
        def _():
            carry_ref[...] = jnp.zeros_like(carry_ref)
            dqg_ref[...] = jnp.zeros_like(dqg_ref)
            dkg_ref[...] = jnp.zeros_like(dkg_ref)
            dbf_ref[...] = jnp.zeros_like(dbf_ref)

        segm = seg_ref[...]
        dqg = jnp.zeros((1, LANES), F32)
        dkg = jnp.zeros((1, LANES), F32)
        for j in range(D // LANES):
            sl = slice(j * LANES, (j + 1) * LANES)
            for (raw_ref, d_ref, gn_ref, o_ref, which) in ((q_ref, dq_ref, qg_ref, dqo_ref, 0), (k_ref, dk_ref, kg_ref, dko_ref, 1)):
                xc = raw_ref[:, sl]
                rr = lax.rsqrt(_dot_exact(xc * xc, segm) + EPS)
                xn = xc * rr
                dc = d_ref[:, sl]
                if which == 0:
                    dqg = dqg + _colsum(dc * xn)
                else:
                    dkg = dkg + _colsum(dc * xn)
                dxn = dc * gn_ref[...]
                o_ref[:, sl] = (rr * (dxn - xn * _dot_exact(dxn * xn, segm))).astype(BF)
        dqg_ref[...] += dqg
        dkg_ref[...] += dkg
        dvo_ref[...] = dv_ref[...].astype(BF)
        z = f_ref[...] + bf_ref[...]
        sneg_t = _sigmoid(-z).T[0:N_HEADS, :]
        dfc = dfc_ref[...] + dfq_ref[...]
        carry = carry_ref[:, 0:1]
        dlf = _dot_exact(dfc, tri_ref[...]) + carry
        carry_ref[...] = jnp.broadcast_to(carry + jnp.sum(dfc, axis=1, keepdims=True), carry_ref.shape)
        dzt = dlf * sneg_t
        dz = jnp.concatenate([dzt, jnp.zeros((LANES - N_HEADS, tm), F32)], axis=0).T
        dbf_ref[...] += _colsum(dz)
        dfo_ref[...] = dz.astype(BF)

    rs = lambda w, cb=0: _rspec(tm, w, cb, n)
    ins = [(dq, rs(D)), (dk, rs(D)), (dv, rs(D)), (proj, rs(D, 0)), (proj, rs(D, 1)), (f, rs(LANES)),
           (dfc, pl.BlockSpec((N_HEADS, tm), lambda i: (0, n - 1 - i))),
           (dfq, pl.BlockSpec((N_HEADS, tm), lambda i: (0, n - 1 - i))),
           (qg2, _full(qg2)), (kg2, _full(kg2)), (bf_pad, _full(bf_pad)), (seg, _full(seg)), (tri, _full(tri))]
    outs = [_row_out(T, tm, D, BF, n), _row_out(T, tm, D, BF, n), _row_out(T, tm, D, BF, n), _row_out(T, tm, LANES, BF, n),
            _acc_out((1, LANES)), _acc_out((1, LANES)), _acc_out((1, LANES))]
    return _rows_call(body, "qkv_bwd", n, ins, outs, [pltpu.VMEM((N_HEADS, LANES), F32)])


def _norm1_bwd(dh, dhf, x, dx1, mod, n1g):
    T = x.shape[0]
    tm = TM_ROWS

    def body(dh_ref, dhf_ref, x_ref, dx1_ref, mod_ref, g_ref, dx_ref, dsh_ref, dsc_ref, dg_ref):
        i = pl.program_id(0)

        @pl.when(i == 0)
        def _():
            for r in (dsh_ref, dsc_ref, dg_ref):
                r[...] = jnp.zeros_like(r)

        sc1 = mod_ref[:, D:2 * D]
        g = g_ref[...]
        xv = x_ref[...]
        dh = dh_ref[...] + dhf_ref[...]
        r = lax.rsqrt(jnp.mean(xv * xv, axis=-1, keepdims=True) + EPS)
        xn = xv * r
        dsh_ref[...] += _colsum(dh)
        dsc_ref[...] += _colsum(dh * xn * g)
        dg_ref[...] += _colsum(dh * xn * (1.0 + sc1))
        dxn = dh * g * (1.0 + sc1)
        dx_ref[...] = dx1_ref[...] + r * (dxn - xn * jnp.mean(dxn * xn, axis=-1, keepdims=True))

    ins = [(dh, _rspec(tm, D)), (dhf, _rspec(tm, D)), (x, _rspec(tm, D)), (dx1, _rspec(tm, D)),
           (mod, _full(mod)), (n1g, _full(n1g))]
    outs = [_row_out(T, tm, D, F32)] + [_acc_out((1, D)) for _ in range(3)]
    return _rows_call(body, "norm1_bwd", T // tm, ins, outs)


def _adamw_math(w, g, m, v):
    m = ADAM_B1 * m + (1.0 - ADAM_B1) * g
    v = ADAM_B2 * v + (1.0 - ADAM_B2) * (g * g)
    m_hat = m / (1.0 - ADAM_B1 ** ADAM_STEP)
    v_hat = v / (1.0 - ADAM_B2 ** ADAM_STEP)
    delta = -ADAM_LR * (m_hat / (jnp.sqrt(v_hat) + ADAM_EPS) + ADAM_WD * w)
    return delta, m, v


def _adamw(parts, w, m, v, name):
    n, R, C = parts.shape
    tr = R if R <= 256 else 256
    assert R % tr == 0

    def body(p_ref, w_ref, m_ref, v_ref, g_ref, d_ref, mo_ref, vo_ref):
        g = p_ref[0].astype(F32)
        for s in range(1, n):
            g = g + p_ref[s].astype(F32)
        g_ref[...] = g
        d_ref[...], mo_ref[...], vo_ref[...] = _adamw_math(w_ref[...], g, m_ref[...], v_ref[...])

    spec = pl.BlockSpec((tr, C), lambda i: (i, 0))
    return pl.pallas_call(
        body, name=name, grid=(R // tr,),
        in_specs=[pl.BlockSpec((n, tr, C), lambda i: (0, i, 0)), spec, spec, spec],
        out_specs=[spec] * 4, out_shape=[_sds((R, C), F32)] * 4,
        compiler_params=_cp(("parallel",)),
    )(parts, w, m, v)


def _rcopy(src, dst, ssem, rsem, peer):
    return pltpu.make_async_remote_copy(src_ref=src, dst_ref=dst, send_sem=ssem, recv_sem=rsem,
                                        device_id=peer, device_id_type=MESH)


def _ada_fwd(c, w_ada, b_slice, cw_shard):
    def body(c_ref, w_ref, b_ref, cw_ref, mod_ref, ca_ref, cwf_ref, call, mp, ssem, rsem):
        x, y, cc, me = _my_pos()
        call[pl.ds(me, 1), :] = c_ref[...]
        cwf_ref[me] = cw_ref[...]
        first = []
        for d in range(1, N_DEV):
            peer, _ = _peer(x, y, cc, d)
            first.append(_rcopy(c_ref, call.at[pl.ds(me, 1), :], ssem.at[0, d - 1], rsem.at[0, d - 1], peer))
            first.append(_rcopy(cw_ref, cwf_ref.at[me], ssem.at[1, d - 1], rsem.at[1, d - 1], peer))
        for cp in first:
            cp.start()
        for d in range(1, N_DEV):
            peer, pid = _peer(x, y, cc, d)
            _rcopy(c_ref, call.at[pl.ds(pid, 1), :], ssem.at[0, d - 1], rsem.at[0, d - 1], peer).wait_recv()
            _rcopy(cw_ref, cwf_ref.at[pid], ssem.at[1, d - 1], rsem.at[1, d - 1], peer).wait_recv()
        cv = call[...]
        ca = cv * _sigmoid(cv)
        ca_ref[...] = ca
        mp[...] = _dot_f32(ca, w_ref[...]) + b_ref[...]
        mod_ref[pl.ds(me, 1), :] = mp[pl.ds(me, 1), :]
        second = []
        for d in range(1, N_DEV):
            peer, pid = _peer(x, y, cc, d)
            second.append(_rcopy(mp.at[pl.ds(pid, 1), :], mod_ref.at[pl.ds(me, 1), :], ssem.at[2, d - 1], rsem.at[2, d - 1], peer))
        for cp in second:
            cp.start()
        for d in range(1, N_DEV):
            peer, pid = _peer(x, y, cc, d)
            _rcopy(mp.at[pl.ds(pid, 1), :], mod_ref.at[pl.ds(pid, 1), :], ssem.at[2, d - 1], rsem.at[2, d - 1], peer).wait_recv()
        for cp in first + second:
            cp.wait_send()

    vm = pl.BlockSpec(memory_space=pltpu.VMEM)
    return pl.pallas_call(
        body, name="ada_fwd",
        in_specs=[vm, vm, vm, vm], out_specs=[vm, vm, vm],
        out_shape=[_sds((N_DEV, ADA_SHARD), F32), _sds((N_DEV, D), F32), _sds((N_DEV, CONV_KP, LANES), F32)],
        scratch_shapes=[pltpu.VMEM((N_DEV, D), F32), pltpu.VMEM((N_DEV, ADA_SHARD), F32),
                        pltpu.SemaphoreType.DMA((3, N_DEV - 1)), pltpu.SemaphoreType.DMA((3, N_DEV - 1))],
        compiler_params=pltpu.CompilerParams(vmem_limit_bytes=VMEM_LIMIT),
    )(c, w_ada, b_slice, cw_shard)


def _xchg_parts(arrays):
    n = len(arrays)
    anyspec = pl.BlockSpec(memory_space=pl.ANY)
    shapes = [_sds((N_DEV,) + tuple(a.shape[-2:]), a.dtype) for a in arrays]
    scratch = [pltpu.SemaphoreType.DMA((n,)), pltpu.SemaphoreType.DMA((n, N_DEV - 1)),
               pltpu.SemaphoreType.DMA((n, N_DEV - 1))]
    return [anyspec] * n, shapes, scratch


def _xchg(ins, outs, sems, gather, wait):
    lsem, ssem, rsem = sems
    x, y, cc, me = _my_pos()
    for a in range(len(ins)):
        local = pltpu.make_async_copy(ins[a] if gather else ins[a].at[me], outs[a].at[me], lsem.at[a])
        if not wait:
            local.start()
        for d in range(1, N_DEV):
            peer, pid = _peer(x, y, cc, d)
            src = ins[a] if gather else ins[a].at[pid]
            if not wait:
                _rcopy(src, outs[a].at[me], ssem.at[a, d - 1], rsem.at[a, d - 1], peer).start()
            else:
                cp = _rcopy(src, outs[a].at[pid], ssem.at[a, d - 1], rsem.at[a, d - 1], peer)
                cp.wait_recv()
                cp.wait_send()
        if wait:
            local.wait()


def _gather_two_level(shard, name):
    def body(x_ref, out_ref, ssem, rsem, lsem):
        x, y, c, me = _my_pos()
        sibling = (x, y, 1 - c)
        chips = [(1 - x, y), (x, 1 - y), (1 - x, 1 - y)]
        slot = lambda px, py, pc: out_ref.at[4 * px + 2 * py + pc]

        def copy(kk, block, to, src=None):
            return _rcopy(slot(*block) if src is None else src, slot(*block), ssem.at[kk], rsem.at[kk], to)

        mine = pltpu.make_async_copy(x_ref, slot(x, y, c), lsem)
        mine.start()
        first = [copy(0, (x, y, c), sibling, src=x_ref)]
        first += [copy(1 + j, (x, y, c), (*chip, c), src=x_ref) for j, chip in enumerate(chips)]
        for cp in first:
            cp.start()
        passed = [copy(4 + j, (*chip, c), sibling) for j, chip in enumerate(chips)]
        for j, chip in enumerate(chips):
            copy(1 + j, (*chip, c), (x, y, c)).wait_recv()
            passed[j].start()
        copy(0, sibling, (x, y, c)).wait_recv()
        for j, chip in enumerate(chips):
            copy(4 + j, (*chip, 1 - c), (x, y, c)).wait_recv()
        for cp in first + passed:
            cp.wait_send()
        mine.wait()

    anyspec = pl.BlockSpec(memory_space=pl.ANY)
    return pl.pallas_call(
        body, name=name, in_specs=[anyspec], out_specs=anyspec,
        out_shape=_sds((N_DEV,) + tuple(shard.shape), shard.dtype),
        scratch_shapes=[pltpu.SemaphoreType.DMA((N_DEV - 1,)), pltpu.SemaphoreType.DMA((N_DEV - 1,)),
                        pltpu.SemaphoreType.DMA(())],
    )(shard)


PACK_ROWS = 16
ROW_MISC = 5
ROW_LOSS = 6
ROW_DMOD = 8


def _small_bwd(pack, dmodb, dcw, cat, wp, mp_, vp, cw_w, cw_m, cw_v):
    def body(pack_ref, dmodb_ref, dcw_ref, cat_ref, wp_ref, mp_ref, vp_ref, cww_ref, cwm_ref, cwv_ref,
             g_ref, d_ref, mo_ref, vo_ref, cg_ref, cd_ref, cm_ref, cv_ref, gwa_ref, loss_ref,
             allp, dmc, cwg, ssem, rsem):
        x, y, cc, me = _my_pos()
        allp[me] = pack_ref[...]
        dmc[pl.ds(me, 1), :] = dmodb_ref[pl.ds(me, 1), :]
        cwg[me] = dcw_ref[me]
        sends = []
        for d in range(1, N_DEV):
            peer, pid = _peer(x, y, cc, d)
            sends.append(_rcopy(pack_ref, allp.at[me], ssem.at[0, d - 1], rsem.at[0, d - 1], peer))
            sends.append(_rcopy(dmodb_ref.at[pl.ds(pid, 1), :], dmc.at[pl.ds(me, 1), :], ssem.at[1, d - 1], rsem.at[1, d - 1], peer))
            sends.append(_rcopy(dcw_ref.at[pid], cwg.at[me], ssem.at[2, d - 1], rsem.at[2, d - 1], peer))
        for cp in sends:
            cp.start()
        for d in range(1, N_DEV):
            peer, pid = _peer(x, y, cc, d)
            _rcopy(pack_ref, allp.at[pid], ssem.at[0, d - 1], rsem.at[0, d - 1], peer).wait_recv()
            _rcopy(dmodb_ref.at[pl.ds(pid, 1), :], dmc.at[pl.ds(pid, 1), :], ssem.at[1, d - 1], rsem.at[1, d - 1], peer).wait_recv()
            _rcopy(dcw_ref.at[pid], cwg.at[pid], ssem.at[2, d - 1], rsem.at[2, d - 1], peer).wait_recv()
        for cp in sends:
            cp.wait_send()

        tot = allp[0]
        cg = cwg[0]
        for s in range(1, N_DEV):
            tot = tot + allp[s]
            cg = cg + cwg[s]
        lane = lax.broadcasted_iota(jnp.int32, (PACK_ROWS, D), 1)
        row = lax.broadcasted_iota(jnp.int32, (PACK_ROWS, D), 0)
        gains = (row == ROW_MISC) & (lane >= LANES) & (lane < 3 * LANES)
        folded = tot + pltpu.roll(tot, D - HEAD_DIM, axis=1)
        keep = (lane % LANES) < HEAD_DIM
        g = jnp.where(gains, jnp.where(keep, folded, 0.0), tot)
        loss_ref[...] = jnp.broadcast_to(
            (0.5 / D) * jnp.sum(jnp.where(row == ROW_LOSS, tot, 0.0), keepdims=True).reshape(1, 1), loss_ref.shape)
        g = jnp.where(row == ROW_LOSS, 0.0, g)
        g_ref[...] = g
        d_ref[...], mo_ref[...], vo_ref[...] = _adamw_math(wp_ref[...], g, mp_ref[...], vp_ref[...])
        cg_ref[...] = cg
        cd_ref[...], cm_ref[...], cv_ref[...] = _adamw_math(cww_ref[...], cg, cwm_ref[...], cwv_ref[...])
        dm_pad = jnp.concatenate([dmc[...], jnp.zeros((LANES - N_DEV, ADA_SHARD), F32)], axis=0)
        gwa_ref[...] = _dot_f32(cat_ref[...], dm_pad)

    vm = pl.BlockSpec(memory_space=pltpu.VMEM)
    p16 = _sds((PACK_ROWS, D), F32)
    c32 = _sds((CONV_KP, LANES), F32)
    return pl.pallas_call(
        body, name="small_bwd",
        in_specs=[vm] * 10, out_specs=[vm] * 10,
        out_shape=[p16, p16, p16, p16, c32, c32, c32, c32, _sds((D, ADA_SHARD), F32), _sds((8, LANES), F32)],
        scratch_shapes=[pltpu.VMEM((N_DEV, PACK_ROWS, D), F32), pltpu.VMEM((N_DEV, ADA_SHARD), F32),
                        pltpu.VMEM((N_DEV, CONV_KP, LANES), F32),
                        pltpu.SemaphoreType.DMA((3, N_DEV - 1)), pltpu.SemaphoreType.DMA((3, N_DEV - 1))],
        compiler_params=pltpu.CompilerParams(vmem_limit_bytes=VMEM_LIMIT),
    )(pack, dmodb, dcw, cat, wp, mp_, vp, cw_w, cw_m, cw_v)


def _lanes(vec, start, total=D):
    n = vec.shape[1]
    return jnp.pad(vec, ((0, 0), (start, total - start - n)))


def _pack_small(rows5, misc, loss_row, six):
    z = jnp.zeros((1, D), F32)
    return jnp.concatenate(rows5 + [misc, loss_row, z] + [six.reshape(N_ADA, D), z, z], axis=0)


def kernel(x, c, w_ada, b_ada, norm1_g, w_in, b_forget, q_norm_g, k_norm_g, w_attn_proj, conv_w, conv_b, conv_ln_g, conv_ln_b, w_conv_proj, w_out, norm2_g, w_mlp1, w_mlp2, loss_target, m_w_ada, m_b_ada, m_norm1_g, m_w_in, m_b_forget, m_q_norm_g, m_k_norm_g, m_w_attn_proj, m_conv_w, m_conv_b, m_conv_ln_g, m_conv_ln_b, m_w_conv_proj, m_w_out, m_norm2_g, m_w_mlp1, m_w_mlp2, v_w_ada, v_b_ada, v_norm1_g, v_w_in, v_b_forget, v_q_norm_g, v_k_norm_g, v_w_attn_proj, v_conv_w, v_conv_b, v_conv_ln_g, v_conv_ln_b, v_w_conv_proj, v_w_out, v_norm2_g, v_w_mlp1, v_w_mlp2):
    me = 4 * lax.axis_index("x") + 2 * lax.axis_index("y") + lax.axis_index("c")
    xs, tgt = x[0], loss_target[0]
    T = xs.shape[0]
    sq = lambda a: a[0]
    pad_taps = lambda a: jnp.pad(a[0], ((0, CONV_KP - CONV_K), (0, 0)))

    b_slice = lax.dynamic_slice(b_ada, (0, me * ADA_SHARD), (1, ADA_SHARD))
    modb, ca_all, cwf = _ada_fwd(c, sq(w_ada), b_slice, pad_taps(conv_w))
    mod = modb.reshape(1, N_ADA * D)
    cw = jnp.transpose(cwf, (1, 0, 2)).reshape(CONV_KP, D)

    g_in = _gather_two_level(sq(w_in).astype(BF), "w_in_gather")
    d_in = g_in.shape[2] * N_DEV
    w_in_f = jnp.transpose(g_in, (1, 0, 2)).reshape(D, d_in)
    w_main = jnp.concatenate([w_in_f[:, :3 * D], w_in_f[:, 3 * D + N_HEADS:]], axis=1)
    w_f = jnp.pad(w_in_f[:, 3 * D:3 * D + N_HEADS], ((0, 0), (0, LANES - N_HEADS)))
    shards = [sq(w_attn_proj).astype(BF), sq(w_conv_proj).astype(BF), sq(w_out).astype(BF),
              sq(w_mlp1).astype(BF), sq(w_mlp2).astype(BF)]

    qg2 = jnp.tile(q_norm_g, (1, 2))
    kg2 = jnp.tile(k_norm_g, (1, 2))
    bf_pad = _lanes(b_forget, 0, LANES)

    h = _pre_in(xs, mod, norm1_g)
    proj = _matmul(h, w_main, "nn", F32, "mm_proj")
    f = _matmul(h, w_f, "nn", F32, "mm_f")
    q, k, v, fc = _qkv_post(proj, f, qg2, kg2, bf_pad)
    fc3 = fc.reshape(N_HEADS // 2, 2, T)
    o, lse, g_ap, g_cp, g_out, g_1, g_2 = _flash_fwd(q, k, v, fc3, shards)
    w_ap, w_cp, w_o = g_ap.reshape(D, D), g_cp.reshape(D, D), g_out.reshape(D, D)
    w_1 = jnp.transpose(g_1, (1, 0, 2)).reshape(D, D_FF)
    w_2 = g_2.reshape(D_FF, D)
    ba = _matmul(o, w_ap, "nn", F32, "mm_ba")
    u0, u1, u3 = _conv_fwd(proj, cw, conv_b, conv_ln_g, conv_ln_b)
    bb = _matmul(u3, w_cp, "nn", F32, "mm_bb")
    merged = _merge(ba, bb, proj)
    mo = _matmul(merged, w_o, "nn", F32, "mm_out")
    x1, h2 = _post_out(xs, mo, mod, norm2_g)
    a = _matmul(h2, w_1, "nn", F32, "mm_mlp1")
    rl = _relu2(a)
    m2 = _matmul(rl, w_2, "nn", F32, "mm_mlp2")
    dy, dm2, dg2, sqcols = _loss_head(x1, m2, tgt, mod)

    drl = _matmul(dm2, w_2, "nt", F32, "mm_drl")
    dw_2 = _matmul(rl, dm2, "tn", BF, "mm_dw2")
    da = _relu2_bwd(drl, a)
    dh2 = _matmul(da, w_1, "nt", F32, "mm_dh2")
    dw_1 = _matmul(h2, da, "tn", BF, "mm_dw1")
    dx1, dmo, dsh2, dsc2, dn2g, dg1 = _norm2_bwd(dh2, x1, dy, mo, mod, norm2_g)
    dmerged = _matmul(dmo, w_o, "nt", F32, "mm_dmerged")
    dw_o = _matmul(merged, dmo, "tn", BF, "mm_dwout")
    dba, dbb, dga, dgb = _gate_bwd(dmerged, ba, bb, proj)
    du3 = _matmul(dbb, w_cp, "nt", F32, "mm_du3")
    dw_cp = _matmul(u3, dbb, "tn", BF, "mm_dwcp")
    do = _matmul(dba, w_ap, "nt", F32, "mm_do")
    dw_ap = _matmul(o, dba, "tn", BF, "mm_dwap")
    dglu_a, dglu_b, dlng, dlnb, dcb, dcw_full = _conv_bwd(du3, u1, u0, proj, cw, conv_ln_g, conv_ln_b)
    dq, delta, rs_a, rs_b = _flash_bwd_dq(q, k, v, do, o, lse, fc3)
    parts = [dw_ap.reshape(N_DEV, D // N_DEV, D), dw_cp.reshape(N_DEV, D // N_DEV, D), dw_o.reshape(N_DEV, D // N_DEV, D),
             jnp.transpose(dw_1.reshape(D, N_DEV, D_FF // N_DEV), (1, 0, 2)), dw_2.reshape(N_DEV, D_FF // N_DEV, D)]
    dk, dv, dfc3, r_ap, r_cp, r_out, r_1, r_2 = _flash_bwd_dkv(q, k, v, do, lse, delta, fc3, parts)
    dfq = jnp.stack([rs_a, rs_b], axis=1).reshape(N_HEADS, T)
    dq_raw, dk_raw, dv_b, df, dqg, dkg, dbf = _qkv_bwd(dq, dk, dv, proj, f, dfc3.reshape(N_HEADS, T), dfq, qg2, kg2, bf_pad)
    dproj = jnp.concatenate([dq_raw, dk_raw, dv_b, dglu_a, dglu_b, dga, dgb], axis=1)
    dw_main = _matmul(h, dproj, "tn", BF, "mm_dwmain")
    dw_f = _matmul(h, df, "tn", BF, "mm_dwf")
    dw_in_f = jnp.concatenate([dw_main[:, :3 * D], dw_f[:, :N_HEADS], dw_main[:, 3 * D:]], axis=1)
    part_in = jnp.transpose(dw_in_f.reshape(D, N_DEV, d_in // N_DEV), (1, 0, 2))
    dh, r_in = _matmul(dproj, w_main, "nt", F32, "mm_dh", scatter=(part_in,))
    dhf = _matmul(df, w_f, "nt", F32, "mm_dhf")
    grad_x, dsh1, dsc1, dn1g = _norm1_bwd(dh, dhf, xs, dx1, mod, norm1_g)

    dmod = jnp.concatenate([dsh1, dsc1, dg1, dsh2, dsc2, dg2], axis=1)
    misc = jnp.concatenate([dbf, dqg, dkg, jnp.zeros((1, D - 3 * LANES), F32)], axis=1)
    pack = _pack_small([dn1g, dcb, dlng, dlnb, dn2g], misc, sqcols, dmod)
    dcw_blocks = jnp.transpose(dcw_full.reshape(CONV_KP, N_DEV, LANES), (1, 0, 2))

    def small_params(b_a, n1, bfg, qn, kn, cvb, lg, lb, n2):
        misc_p = jnp.concatenate([_lanes(bfg, 0, LANES), _lanes(qn, 0, LANES), _lanes(kn, 0, LANES),
                                  jnp.zeros((1, D - 3 * LANES), F32)], axis=1)
        return _pack_small([n1, cvb, lg, lb, n2], misc_p, jnp.zeros((1, D), F32), b_a)

    wp = small_params(b_ada, norm1_g, b_forget, q_norm_g, k_norm_g, conv_b, conv_ln_g, conv_ln_b, norm2_g)
    mp_ = small_params(m_b_ada, m_norm1_g, m_b_forget, m_q_norm_g, m_k_norm_g, m_conv_b, m_conv_ln_g, m_conv_ln_b, m_norm2_g)
    vp = small_params(v_b_ada, v_norm1_g, v_b_forget, v_q_norm_g, v_k_norm_g, v_conv_b, v_conv_ln_g, v_conv_ln_b, v_norm2_g)
    cat = jnp.pad(jnp.transpose(ca_all), ((0, 0), (0, LANES - N_DEV)))
    small = _small_bwd(pack, dmod.reshape(N_DEV, ADA_SHARD), dcw_blocks, cat,
                       wp, mp_, vp, pad_taps(conv_w), pad_taps(m_conv_w), pad_taps(v_conv_w))
    sp = small[0:4]
    scw = small[4:8]
    gw_ada, loss_t = small[8], small[9]
    loss = loss_t[0, 0]

    def unpack(p):
        misc_r = p[ROW_MISC:ROW_MISC + 1]
        return dict(
            b_ada=p[ROW_DMOD:ROW_DMOD + N_ADA].reshape(1, N_ADA * D), norm1_g=p[0:1], conv_b=p[1:2], conv_ln_g=p[2:3],
            conv_ln_b=p[3:4], norm2_g=p[4:5], b_forget=misc_r[:, 0:N_HEADS],
            q_norm_g=misc_r[:, LANES:LANES + HEAD_DIM], k_norm_g=misc_r[:, 2 * LANES:2 * LANES + HEAD_DIM])

    res = {}
    res["w_ada"] = _adamw(gw_ada[None], sq(w_ada), sq(m_w_ada), sq(v_w_ada), "adamw_w_ada")
    res["w_in"] = _adamw(r_in, sq(w_in), sq(m_w_in), sq(v_w_in), "adamw_w_in")
    res["w_attn_proj"] = _adamw(r_ap, sq(w_attn_proj), sq(m_w_attn_proj), sq(v_w_attn_proj), "adamw_w_ap")
    res["w_conv_proj"] = _adamw(r_cp, sq(w_conv_proj), sq(m_w_conv_proj), sq(v_w_conv_proj), "adamw_w_cp")
    res["w_out"] = _adamw(r_out, sq(w_out), sq(m_w_out), sq(v_w_out), "adamw_w_out")
    res["w_mlp1"] = _adamw(r_1, sq(w_mlp1), sq(m_w_mlp1), sq(v_w_mlp1), "adamw_w_mlp1")
    res["w_mlp2"] = _adamw(r_2, sq(w_mlp2), sq(m_w_mlp2), sq(v_w_mlp2), "adamw_w_mlp2")

    names = ["w_ada", "b_ada", "norm1_g", "w_in", "b_forget", "q_norm_g", "k_norm_g", "w_attn_proj", "conv_w", "conv_b",
             "conv_ln_g", "conv_ln_b", "w_conv_proj", "w_out", "norm2_g", "w_mlp1", "w_mlp2"]
    outs = [loss, grad_x[None]]
    for kind in range(4):
        small_d = unpack(sp[kind])
        for nm in names:
            if nm in res:
                outs.append(res[nm][kind][None])
            elif nm == "conv_w":
                outs.append(scw[kind][:CONV_K][None])
            else:
                outs.append(small_d[nm])
    return tuple(outs)
```

```python
import functools

import jax
import jax.numpy as jnp
from jax import lax
from jax.experimental import pallas as pl
from jax.experimental.pallas import tpu as pltpu

F32 = jnp.float32
BF = jnp.bfloat16

N_DEV = 8
D = 1024
N_HEADS = 16
HEAD_DIM = 64
LANES = 128
SUBLANES = 8
CONV_K = 31
CONV_KP = 32
HALO = 32
CONV_ROWS = 64
D_FF = 4 * D
N_ADA = 6
ADA_SHARD = N_ADA * D // N_DEV
EPS = 1e-6
QK_SCALE = HEAD_DIM ** -0.5
LOG2E = 1.4426950408889634
LN2 = 0.6931471805599453
NEG = -1e30

ADAM_LR = 0.001
ADAM_B1 = 0.9
ADAM_B2 = 0.999
ADAM_EPS = 1e-08
ADAM_WD = 0.01
ADAM_STEP = 10

VMEM_LIMIT = 56 * 1024 * 1024
TM_ROWS = 256
TQ = 512

MESH = pl.DeviceIdType.MESH


def _cp(sem=None):
    return pltpu.CompilerParams(dimension_semantics=sem, vmem_limit_bytes=VMEM_LIMIT)


def _sds(shape, dtype):
    return jax.ShapeDtypeStruct(tuple(shape), dtype)


def _full(arr):
    nd = arr.ndim
    return pl.BlockSpec(arr.shape, lambda *_: (0,) * nd)


def _fullshape(shape):
    nd = len(shape)
    return pl.BlockSpec(tuple(shape), lambda *_: (0,) * nd)


def _split3(x):
    hi = x.astype(BF)
    r1 = x - hi.astype(F32)
    mid = r1.astype(BF)
    lo = (r1 - mid.astype(F32)).astype(BF)
    return hi, mid, lo


def _dot_exact(x, mat):
    hi, mid, lo = _split3(x)
    d = lambda t: jnp.dot(t, mat, preferred_element_type=F32)
    return d(hi) + d(mid) + d(lo)


def _dot_f32(a, b):
    a1, a2, a3 = _split3(a)
    b1, b2, b3 = _split3(b)
    d = lambda s, t: jnp.dot(s, t, preferred_element_type=F32)
    return (d(a1, b3) + d(a3, b1) + d(a2, b2)) + (d(a1, b2) + d(a2, b1)) + d(a1, b1)


def _sigmoid(x):
    return 1.0 / (1.0 + jnp.exp(-x))


def _colsum(x):
    return jnp.sum(x, axis=0, keepdims=True)


def _my_pos():
    x, y, c = lax.axis_index("x"), lax.axis_index("y"), lax.axis_index("c")
    return x, y, c, 4 * x + 2 * y + c


def _peer(x, y, c, d):
    px = (1 - x) if d & 4 else x
    py = (1 - y) if d & 2 else y
    pc = (1 - c) if d & 1 else c
    return (px, py, pc), 4 * px + 2 * py + pc


def _matmul(a, b, form, out_dtype, name, tm=1024, tn=1024, tk=1024, scatter=(), relu2=False, relu_of=None):
    if form == "nn":
        (M, K), N = a.shape, b.shape[1]
    elif form == "nt":
        (M, K), N = a.shape, b.shape[0]
    else:
        (K, M), N = a.shape, b.shape[1]
    tm, tn, tk = min(tm, M), min(tn, N), min(tk, K)
    assert M % tm == 0 and N % tn == 0 and K % tk == 0, (name, M, N, K)
    nk = K // tk
    if form == "tn":
        a_spec = pl.BlockSpec((tk, tm), lambda i, j, k: (k, i))
        dn = (((0,), (0,)), ((), ()))
    else:
        a_spec = pl.BlockSpec((tm, tk), lambda i, j, k: (i, k))
        dn = (((1,), (1 if form == "nt" else 0,)), ((), ()))
    if form == "nt":
        b_spec = pl.BlockSpec((tn, tk), lambda i, j, k: (j, k))
    else:
        b_spec = pl.BlockSpec((tk, tn), lambda i, j, k: (k, j))

    nx = len(scatter)
    ne = 0 if relu_of is None else 1
    no = 2 if relu2 else 1
    grid = (M // tm, N // tn, nk)

    def body(a_ref, b_ref, *rest):
        e_ref = rest[0] if ne else None
        x_in = rest[ne:ne + nx]
        o_refs = rest[ne + nx:ne + nx + no]
        x_out = rest[ne + nx + no:ne + 2 * nx + no]
        scr = rest[ne + 2 * nx + no:]
        if nx:
            first, last = _first_last(grid)

            @pl.when(first)
            def _():
                _xchg(x_in, x_out, scr[-3:], False, wait=False)

        def finish(val):
            if relu2:
                o_refs[0][...] = val.astype(out_dtype)
                r = jnp.maximum(val, 0.0)
                o_refs[1][...] = (r * r).astype(out_dtype)
            elif ne:
                o_refs[0][...] = (val * (2.0 * jnp.maximum(e_ref[...].astype(F32), 0.0))).astype(out_dtype)
            else:
                o_refs[0][...] = val.astype(out_dtype)

        part = lax.dot_general(a_ref[...].astype(BF), b_ref[...].astype(BF), dn, preferred_element_type=F32)
        if nk == 1:
            finish(part)
        else:
            acc = scr[0]
            k = pl.program_id(2)

            @pl.when(k == 0)
            def _():
                acc[...] = part

            @pl.when(k > 0)
            def _():
                acc[...] += part

            @pl.when(k == nk - 1)
            def _():
                finish(acc[...])

        if nx:
            @pl.when(last)
            def _():
                _xchg(x_in, x_out, scr[-3:], False, wait=True)

    x_specs, x_shapes, x_scratch = _xchg_parts(scatter) if nx else ([], [], [])
    sem = ("arbitrary",) * 3 if nx else ("parallel", "parallel", "arbitrary")
    o_spec = pl.BlockSpec((tm, tn), lambda i, j, k: (i, j))
    res = pl.pallas_call(
        body, name=name, grid=grid,
        in_specs=[a_spec, b_spec] + [o_spec] * ne + x_specs,
        out_specs=[o_spec] * no + x_specs,
        out_shape=[_sds((M, N), out_dtype)] * no + x_shapes,
        scratch_shapes=([] if nk == 1 else [pltpu.VMEM((tm, tn), F32)]) + x_scratch,
        compiler_params=_cp(sem),
    )(a, b, *([relu_of] if ne else []), *scatter)
    return res if (nx or relu2) else res[0]


def _rows_call(body, name, n_tiles, ins, outs, scratch=()):
    res = pl.pallas_call(
        body, name=name, grid=(n_tiles,),
        in_specs=[s for _, s in ins],
        out_specs=[s for _, s in outs],
        out_shape=[o for o, _ in outs],
        scratch_shapes=list(scratch),
        compiler_params=_cp(("arbitrary",)),
    )(*[a for a, _ in ins])
    return res


def _rspec(tm, width, cb=0, rev_n=None):
    if rev_n is None:
        return pl.BlockSpec((tm, width), lambda i: (i, cb))
    return pl.BlockSpec((tm, width), lambda i: (rev_n - 1 - i, cb))


def _row_out(T, tm, width, dtype, rev_n=None):
    return (_sds((T, width), dtype), _rspec(tm, width, 0, rev_n))


def _acc_out(shape, dtype=F32):
    return (_sds(shape, dtype), _fullshape(shape))


def _mod_parts(mod):
    return [mod[:, i * D:(i + 1) * D] for i in range(N_ADA)]


def _pre_in(x, mod, n1g):
    T = x.shape[0]
    tm = TM_ROWS

    def body(x_ref, mod_ref, g_ref, h_ref):
        sh1, sc1 = mod_ref[:, 0:D], mod_ref[:, D:2 * D]
        xv = x_ref[...]
        r = lax.rsqrt(jnp.mean(xv * xv, axis=-1, keepdims=True) + EPS)
        h_ref[...] = ((xv * r) * g_ref[...] * (1.0 + sc1) + sh1).astype(BF)

    return _rows_call(body, "pre_in", T // tm,
                      [(x, _rspec(tm, D)), (mod, _full(mod)), (n1g, _full(n1g))],
                      [_row_out(T, tm, D, BF)])[0]


def _seg_mat():
    r = jnp.arange(LANES)[:, None] // HEAD_DIM
    c = jnp.arange(LANES)[None, :] // HEAD_DIM
    return jnp.where(r == c, 1.0 / HEAD_DIM, 0.0).astype(BF)


def _tri_mat(n, upper):
    r = jnp.arange(n)[:, None]
    c = jnp.arange(n)[None, :]
    return jnp.where((r <= c) if upper else (r >= c), 1.0, 0.0).astype(BF)


def _log_sigmoid(z):
    return jnp.minimum(z, 0.0) - jnp.log(1.0 + jnp.exp(-jnp.abs(z)))


def _qkv_post(proj, f, qg2, kg2, bf_pad):
    T = proj.shape[0]
    tm = TM_ROWS
    seg = _seg_mat()
    tri = _tri_mat(tm, True)

    def body(q_ref, k_ref, v_ref, f_ref, qg_ref, kg_ref, bf_ref, seg_ref, tri_ref,
             qo_ref, ko_ref, vo_ref, fc_ref, carry_ref):
        i = pl.program_id(0)

        @pl.when(i == 0)
        def _():
            carry_ref[...] = jnp.zeros_like(carry_ref)

        segm = seg_ref[...]
        for j in range(D // LANES):
            sl = slice(j * LANES, (j + 1) * LANES)
            qc = q_ref[:, sl].astype(F32)
            rq = lax.rsqrt(_dot_exact(qc * qc, segm) + EPS)
            qo_ref[:, sl] = ((qc * rq) * qg_ref[...] * (QK_SCALE * LOG2E)).astype(BF)
            kc = k_ref[:, sl].astype(F32)
            rk = lax.rsqrt(_dot_exact(kc * kc, segm) + EPS)
            ko_ref[:, sl] = ((kc * rk) * kg_ref[...]).astype(BF)
        vo_ref[...] = v_ref[...].astype(BF)
        lf = _log_sigmoid(f_ref[...] + bf_ref[...])
        lft = lf.T[0:N_HEADS, :]
        carry = carry_ref[:, 0:1]
        fc_ref[...] = _dot_exact(lft, tri_ref[...]) + carry
        carry_ref[...] = jnp.broadcast_to(carry + jnp.sum(lft, axis=1, keepdims=True), carry_ref.shape)

    outs = [_row_out(T, tm, D, BF), _row_out(T, tm, D, BF), _row_out(T, tm, D, BF),
            (_sds((N_HEADS, T), F32), pl.BlockSpec((N_HEADS, tm), lambda i: (0, i)))]
    ins = [(proj, _rspec(tm, D, 0)), (proj, _rspec(tm, D, 1)), (proj, _rspec(tm, D, 2)), (f, _rspec(tm, LANES)),
           (qg2, _full(qg2)), (kg2, _full(kg2)), (bf_pad, _full(bf_pad)), (seg, _full(seg)), (tri, _full(tri))]
    return _rows_call(body, "qkv_post", T // tm, ins, outs, [pltpu.VMEM((N_HEADS, LANES), F32)])


def _lane_lo():
    return lax.broadcasted_iota(jnp.int32, (1, LANES), 1) < HEAD_DIM


def _nt(a, b):
    return lax.dot_general(a, b, (((1,), (1,)), ((), ())), preferred_element_type=F32)


def _tn(a, b):
    return lax.dot_general(a, b, (((0,), (0,)), ((), ())), preferred_element_type=F32)


def _head_rep(x, lo):
    rolled = pltpu.roll(x, HEAD_DIM, axis=1)
    return jnp.where(lo, x, rolled), jnp.where(lo, rolled, x)


def _diag_mask(t):
    return lax.broadcasted_iota(jnp.int32, (t, t), 1) <= lax.broadcasted_iota(jnp.int32, (t, t), 0)


def _first_last(grid):
    ids = [pl.program_id(a) for a in range(len(grid))]
    first = functools.reduce(jnp.logical_and, [i == 0 for i in ids])
    last = functools.reduce(jnp.logical_and, [i == g - 1 for i, g in zip(ids, grid)])
    return first, last


def _flash_fwd(q, k, v, fc3, shards):
    T = q.shape[0]
    tq = TQ
    nq = T // tq
    hp_n = N_HEADS // 2
    rep = tq // LANES
    nx = len(shards)
    grid = (hp_n, nq, nq)

    def body(q_ref, k_ref, v_ref, fk_ref, fq_ref, *rest):
        x_in, (o_ref, lse_ref), x_out = rest[:nx], rest[nx:nx + 2], rest[nx + 2:2 * nx + 2]
        acc_ref, m_ref, l_ref = rest[2 * nx + 2:2 * nx + 5]
        sems = rest[2 * nx + 5:]
        qi, ki = pl.program_id(1), pl.program_id(2)
        first, last = _first_last(grid)

        @pl.when(first)
        def _():
            _xchg(x_in, x_out, sems, True, wait=False)

        @pl.when(ki == 0)
        def _():
            acc_ref[...] = jnp.zeros_like(acc_ref)
            m_ref[...] = jnp.full_like(m_ref, NEG)
            l_ref[...] = jnp.zeros_like(l_ref)

        def step(diag):
            lo = _lane_lo()
            q2, k2, v2 = q_ref[...], k_ref[...], v_ref[...]
            zero = jnp.zeros_like(k2)
            bias = (fq_ref[:, 0:1] - fk_ref[...]) * LOG2E
            alphas, pvs = [], []
            for hh in range(2):
                sel = (lambda t: jnp.where(lo, t, zero)) if hh == 0 else (lambda t: jnp.where(lo, zero, t))
                s = _nt(sel(q2), k2) + bias[hh:hh + 1, :]
                if diag:
                    s = jnp.where(_diag_mask(tq), s, NEG)
                m_old = m_ref[hh]
                m_new = jnp.maximum(m_old, jnp.max(s, axis=-1, keepdims=True))
                alpha = jnp.exp2(m_old - m_new)
                p = jnp.exp2(s - jnp.tile(m_new, (1, rep)))
                l_ref[hh] = alpha * l_ref[hh] + jnp.sum(p, axis=-1, keepdims=True)
                m_ref[hh] = m_new
                pvs.append(jnp.dot(p.astype(BF), sel(v2), preferred_element_type=F32))
                alphas.append(alpha)
            acc_ref[...] = acc_ref[...] * jnp.where(lo, alphas[0], alphas[1]) + pvs[0] + pvs[1]

        @pl.when(ki < qi)
        def _():
            step(False)

        @pl.when(ki == qi)
        def _():
            step(True)
            lo = _lane_lo()
            la, lb = l_ref[0], l_ref[1]
            o_ref[...] = acc_ref[...] * jnp.where(lo, 1.0 / la, 1.0 / lb)
            lse_ref[...] = jnp.where(lo, m_ref[0] + jnp.log(la) * LOG2E, m_ref[1] + jnp.log(lb) * LOG2E)

        @pl.when(last)
        def _():
            _xchg(x_in, x_out, sems, True, wait=True)

    qspec = pl.BlockSpec((tq, LANES), lambda h, i, j: (i, h))
    kspec = pl.BlockSpec((tq, LANES), lambda h, i, j: (jnp.minimum(i, j), h))
    fkspec = pl.BlockSpec((None, 2, tq), lambda h, i, j: (h, 0, jnp.minimum(i, j)))
    fqspec = pl.BlockSpec((None, 2, tq), lambda h, i, j: (h, 0, i))
    x_specs, x_shapes, x_scratch = _xchg_parts(shards)
    return pl.pallas_call(
        body, name="attn_fwd", grid=grid,
        in_specs=[qspec, kspec, kspec, fkspec, fqspec] + x_specs,
        out_specs=[qspec, qspec] + x_specs,
        out_shape=[_sds((T, D), F32), _sds((T, D), F32)] + x_shapes,
        scratch_shapes=[pltpu.VMEM((tq, LANES), F32), pltpu.VMEM((2, tq, LANES), F32),
                        pltpu.VMEM((2, tq, LANES), F32)] + x_scratch,
        compiler_params=_cp(("arbitrary", "arbitrary", "arbitrary")),
    )(q, k, v, fc3, fc3, *shards)


def _flash_bwd_dq(q, k, v, do, o, lse, fc3):
    T = q.shape[0]
    tq = TQ
    nq = T // tq
    hp_n = N_HEADS // 2
    rep = tq // LANES

    def body(q_ref, k_ref, v_ref, do_ref, o_ref, lse_ref, fk_ref, fq_ref,
             dq_ref, dl_out_ref, ra_ref, rb_ref, acc_ref, dl_ref, ls_ref, rs_ref):
        qi, ki = pl.program_id(1), pl.program_id(2)

        @pl.when(ki == 0)
        def _():
            lo = _lane_lo()
            acc_ref[...] = jnp.zeros_like(acc_ref)
            rs_ref[...] = jnp.zeros_like(rs_ref)
            prod = do_ref[...].astype(BF).astype(F32) * o_ref[...]
            da = jnp.broadcast_to(jnp.sum(jnp.where(lo, prod, 0.0), axis=-1, keepdims=True), (tq, LANES))
            db = jnp.broadcast_to(jnp.sum(jnp.where(lo, 0.0, prod), axis=-1, keepdims=True), (tq, LANES))
            dl_ref[0] = da
            dl_ref[1] = db
            dl_out_ref[...] = jnp.where(lo, da, db)
            la, lb = _head_rep(lse_ref[...], lo)
            ls_ref[0] = la
            ls_ref[1] = lb

        def step(diag):
            lo = _lane_lo()
            q2, k2, v2 = q_ref[...], k_ref[...], v_ref[...]
            do2 = do_ref[...].astype(BF)
            zero = jnp.zeros_like(q2)
            bias = (fq_ref[:, 0:1] - fk_ref[...]) * LOG2E
            tot = None
            for hh in range(2):
                sel = (lambda t: jnp.where(lo, t, zero)) if hh == 0 else (lambda t: jnp.where(lo, zero, t))
                s = _nt(sel(q2), k2) + bias[hh:hh + 1, :]
                if diag:
                    s = jnp.where(_diag_mask(tq), s, NEG)
                p = jnp.exp2(s - jnp.tile(ls_ref[hh], (1, rep)))
                dp = _nt(sel(do2), v2)
                ds = p * (dp - jnp.tile(dl_ref[hh], (1, rep)))
                rs_ref[hh] += jnp.sum(ds, axis=-1, keepdims=True)
                part = jnp.dot(ds.astype(BF), sel(k2), preferred_element_type=F32)
                tot = part if tot is None else tot + part
            acc_ref[...] += tot

        @pl.when(ki < qi)
        def _():
            step(False)

        @pl.when(ki == qi)
        def _():
            step(True)
            dq_ref[...] = acc_ref[...] * QK_SCALE
            ra_ref[...] = rs_ref[0][:, 0:1]
            rb_ref[...] = rs_ref[1][:, 0:1]

    qspec = pl.BlockSpec((tq, LANES), lambda h, i, j: (i, h))
    kspec = pl.BlockSpec((tq, LANES), lambda h, i, j: (jnp.minimum(i, j), h))
    fkspec = pl.BlockSpec((None, 2, tq), lambda h, i, j: (h, 0, jnp.minimum(i, j)))
    fqspec = pl.BlockSpec((None, 2, tq), lambda h, i, j: (h, 0, i))
    lspec = pl.BlockSpec((None, tq, 1), lambda h, i, j: (h, i, 0))
    return pl.pallas_call(
        body, name="attn_bwd_dq", grid=(hp_n, nq, nq),
        in_specs=[qspec, kspec, kspec, qspec, qspec, qspec, fkspec, fqspec],
        out_specs=[qspec, qspec, lspec, lspec],
        out_shape=[_sds((T, D), F32), _sds((T, D), F32), _sds((hp_n, T, 1), F32), _sds((hp_n, T, 1), F32)],
        scratch_shapes=[pltpu.VMEM((tq, LANES), F32)] + [pltpu.VMEM((2, tq, LANES), F32)] * 3,
        compiler_params=_cp(("parallel", "parallel", "arbitrary")),
    )(q, k, v, do, o, lse, fc3, fc3)


def _flash_bwd_dkv(q, k, v, do, lse, delta, fc3, parts):
    T = q.shape[0]
    tq = TQ
    nq = T // tq
    hp_n = N_HEADS // 2
    rep = tq // LANES
    nx = len(parts)
    grid = (hp_n, nq, nq)

    def body(q_ref, k_ref, v_ref, do_ref, lse_ref, dl_ref, fk_ref, fq_ref, *rest):
        x_in, (dk_ref, dv_ref, dfc_ref), x_out = rest[:nx], rest[nx:nx + 3], rest[nx + 3:2 * nx + 3]
        dk_acc, dv_acc, df_acc = rest[2 * nx + 3:2 * nx + 6]
        sems = rest[2 * nx + 6:]
        ki, qi = pl.program_id(1), pl.program_id(2)
        first, last = _first_last(grid)

        @pl.when(first)
        def _():
            _xchg(x_in, x_out, sems, False, wait=False)

        @pl.when(qi == 0)
        def _():
            dk_acc[...] = jnp.zeros_like(dk_acc)
            dv_acc[...] = jnp.zeros_like(dv_acc)
            df_acc[...] = jnp.zeros_like(df_acc)

        def step(diag):
            lo = _lane_lo()
            q2, k2, v2 = q_ref[...], k_ref[...], v_ref[...]
            do2 = do_ref[...].astype(BF)
            zero = jnp.zeros_like(q2)
            bias = (fq_ref[:, 0:1] - fk_ref[...]) * LOG2E
            lses = _head_rep(lse_ref[...], lo)
            dls = _head_rep(dl_ref[...], lo)
            dk_t = None
            dv_t = None
            for hh in range(2):
                sel = (lambda t: jnp.where(lo, t, zero)) if hh == 0 else (lambda t: jnp.where(lo, zero, t))
                s = _nt(sel(q2), k2) + bias[hh:hh + 1, :]
                if diag:
                    s = jnp.where(_diag_mask(tq), s, NEG)
                p = jnp.exp2(s - jnp.tile(lses[hh], (1, rep)))
                dp = _nt(sel(do2), v2)
                ds = p * (dp - jnp.tile(dls[hh], (1, rep)))
                dvp = _tn(p.astype(BF), sel(do2))
                dkp = _tn(ds.astype(BF), sel(q2))
                dv_t = dvp if dv_t is None else dv_t + dvp
                dk_t = dkp if dk_t is None else dk_t + dkp
                df_acc[hh:hh + 1, :] -= _colsum(ds)
            dk_acc[...] += dk_t
            dv_acc[...] += dv_t

        @pl.when(qi > ki)
        def _():
            step(False)

        @pl.when(qi == ki)
        def _():
            step(True)

        @pl.when(qi == nq - 1)
        def _():
            dk_ref[...] = dk_acc[...] * LN2
            dv_ref[...] = dv_acc[...]
            dfc_ref[...] = df_acc[...]

        @pl.when(last)
        def _():
            _xchg(x_in, x_out, sems, False, wait=True)

    kspec = pl.BlockSpec((tq, LANES), lambda h, j, i: (j, h))
    qspec = pl.BlockSpec((tq, LANES), lambda h, j, i: (jnp.maximum(i, j), h))
    fkspec = pl.BlockSpec((None, 2, tq), lambda h, j, i: (h, 0, j))
    fqspec = pl.BlockSpec((None, 2, tq), lambda h, j, i: (h, 0, jnp.maximum(i, j)))
    x_specs, x_shapes, x_scratch = _xchg_parts(parts)
    return pl.pallas_call(
        body, name="attn_bwd_dkv", grid=grid,
        in_specs=[qspec, kspec, kspec, qspec, qspec, qspec, fkspec, fqspec] + x_specs,
        out_specs=[kspec, kspec, fkspec] + x_specs,
        out_shape=[_sds((T, D), F32), _sds((T, D), F32), _sds((hp_n, 2, T), F32)] + x_shapes,
        scratch_shapes=[pltpu.VMEM((tq, LANES), F32), pltpu.VMEM((tq, LANES), F32), pltpu.VMEM((2, tq), F32)] + x_scratch,
        compiler_params=_cp(("arbitrary", "arbitrary", "arbitrary")),
    )(q, k, v, do, lse, delta, fc3, fc3, *parts)


def _layer_norm_stats(u1):
    mu = jnp.mean(u1, axis=-1, keepdims=True)
    xc = u1 - mu
    rstd = lax.rsqrt(jnp.mean(xc * xc, axis=-1, keepdims=True) + EPS)
    return xc * rstd, rstd


def _shifted_copies(buf, sh, tm):
    rows = tm + HALO - SUBLANES
    for b in range(1, SUBLANES):
        sh[b - 1, 0:rows, :] = buf[b:b + rows, :]


def _window(buf, sh, off, rows, sl):
    a8, b = off // SUBLANES * SUBLANES, off % SUBLANES
    return buf[a8:a8 + rows, sl] if b == 0 else sh[b - 1, a8:a8 + rows, sl]


def _conv_fwd(proj, cw, cb, lng, lnb):
    T = proj.shape[0]
    tm = TM_ROWS

    def body(a_ref, b_ref, w_ref, cb_ref, g_ref, bb_ref, u0_ref, u1_ref, u3_ref, buf, sh):
        i = pl.program_id(0)

        @pl.when(i == 0)
        def _():
            buf[0:HALO, :] = jnp.zeros((HALO, D), F32)

        u0 = a_ref[...].astype(F32) * _sigmoid(b_ref[...].astype(F32))
        u0_ref[...] = u0
        buf[HALO:HALO + tm, :] = u0
        _shifted_copies(buf, sh, tm)
        for j in range(D // LANES):
            sl = slice(j * LANES, (j + 1) * LANES)
            for r0 in range(0, tm, CONV_ROWS):
                acc = jnp.broadcast_to(cb_ref[:, sl], (CONV_ROWS, LANES))
                for kk in range(CONV_K):
                    acc = acc + w_ref[kk:kk + 1, sl] * _window(buf, sh, r0 + HALO - (CONV_K - 1) + kk, CONV_ROWS, sl)
                u1_ref[r0:r0 + CONV_ROWS, sl] = acc
        buf[0:HALO, :] = buf[tm:tm + HALO, :]
        xh, _ = _layer_norm_stats(u1_ref[...])
        u2 = xh * g_ref[...] + bb_ref[...]
        u3_ref[...] = (u2 * _sigmoid(u2)).astype(BF)

    ins = [(proj, _rspec(tm, D, 3)), (proj, _rspec(tm, D, 4)), (cw, _full(cw)), (cb, _full(cb)),
           (lng, _full(lng)), (lnb, _full(lnb))]
    outs = [_row_out(T, tm, D, F32), _row_out(T, tm, D, F32), _row_out(T, tm, D, BF)]
    return _rows_call(body, "conv_fwd", T // tm, ins, outs,
                      [pltpu.VMEM((tm + HALO, D), F32), pltpu.VMEM((SUBLANES - 1, tm + HALO, D), F32)])


def _conv_bwd(du3, u1, u0, proj, cw, lng, lnb):
    T = du3.shape[0]
    tm = TM_ROWS
    n = T // tm
    per = tm // HALO

    def body(du3_ref, u1_ref, u0_ref, halo_ref, a_ref, b_ref, w_ref, g_ref, bb_ref,
             da_ref, db_ref, dg_ref, dbb_ref, dcb_ref, dw_ref, dbuf, ubuf, du0_buf, dsh, ush, dw8):
        i = pl.program_id(0)
        r = n - 1 - i

        @pl.when(i == 0)
        def _():
            dbuf[tm:tm + HALO, :] = jnp.zeros((HALO, D), F32)
            dg_ref[...] = jnp.zeros_like(dg_ref)
            dbb_ref[...] = jnp.zeros_like(dbb_ref)
            dcb_ref[...] = jnp.zeros_like(dcb_ref)
            dw8[...] = jnp.zeros_like(dw8)

        xh, rstd = _layer_norm_stats(u1_ref[...])
        g = g_ref[...]
        u2 = xh * g + bb_ref[...]
        s2 = _sigmoid(u2)
        du2 = du3_ref[...] * (s2 * (1.0 + u2 * (1.0 - s2)))
        dg_ref[...] += _colsum(du2 * xh)
        dbb_ref[...] += _colsum(du2)
        dxh = du2 * g
        du1 = rstd * (dxh - jnp.mean(dxh, axis=-1, keepdims=True) - xh * jnp.mean(dxh * xh, axis=-1, keepdims=True))
        dcb_ref[...] += _colsum(du1)
        dbuf[0:tm, :] = du1
        ubuf[HALO:HALO + tm, :] = u0_ref[...]
        ubuf[0:HALO, :] = jnp.where(r > 0, halo_ref[...], 0.0)
        _shifted_copies(dbuf, dsh, tm)
        _shifted_copies(ubuf, ush, tm)
        for j in range(D // LANES):
            sl = slice(j * LANES, (j + 1) * LANES)
            for r0 in range(0, tm, CONV_ROWS):
                d1 = dbuf[r0:r0 + CONV_ROWS, sl]
                acc = jnp.zeros((CONV_ROWS, LANES), F32)
                for kk in range(CONV_K):
                    acc = acc + w_ref[kk:kk + 1, sl] * _window(dbuf, dsh, r0 + CONV_K - 1 - kk, CONV_ROWS, sl)
                    prod = d1 * _window(ubuf, ush, r0 + HALO - (CONV_K - 1) + kk, CONV_ROWS, sl)
                    dw8[kk * SUBLANES:(kk + 1) * SUBLANES, sl] += prod.reshape(
                        CONV_ROWS // SUBLANES, SUBLANES, LANES).sum(axis=0)
                du0_buf[r0:r0 + CONV_ROWS, sl] = acc
        dbuf[tm:tm + HALO, :] = dbuf[0:HALO, :]
        du0 = du0_buf[...]
        af, bfl = a_ref[...].astype(F32), b_ref[...].astype(F32)
        sb = _sigmoid(bfl)
        da_ref[...] = (du0 * sb).astype(BF)
        db_ref[...] = (du0 * af * sb * (1.0 - sb)).astype(BF)

        @pl.when(i == n - 1)
        def _():
            for kk in range(CONV_KP):
                dw_ref[kk:kk + 1, :] = _colsum(dw8[kk * SUBLANES:(kk + 1) * SUBLANES, :])

    rs = lambda cb: _rspec(tm, D, cb, n)
    halo_spec = pl.BlockSpec((HALO, D), lambda i: (jnp.maximum((n - 1 - i) * per - 1, 0), 0))
    ins = [(du3, rs(0)), (u1, rs(0)), (u0, rs(0)), (u0, halo_spec), (proj, rs(3)), (proj, rs(4)),
           (cw, _full(cw)), (lng, _full(lng)), (lnb, _full(lnb))]
    outs = [_row_out(T, tm, D, BF, n), _row_out(T, tm, D, BF, n),
            _acc_out((1, D)), _acc_out((1, D)), _acc_out((1, D)), _acc_out((CONV_KP, D))]
    shifted = pltpu.VMEM((SUBLANES - 1, tm + HALO, D), F32)
    return _rows_call(body, "conv_bwd", n, ins, outs,
                      [pltpu.VMEM((tm + HALO, D), F32), pltpu.VMEM((tm + HALO, D), F32), pltpu.VMEM((tm, D), F32),
                       shifted, shifted, pltpu.VMEM((CONV_KP * SUBLANES, D), F32)])


def _merge(ba, bb, proj):
    T = ba.shape[0]
    tm = TM_ROWS

    def body(ba_ref, bb_ref, ga_ref, gb_ref, o_ref):
        sa, sb = _sigmoid(ga_ref[...].astype(F32)), _sigmoid(gb_ref[...].astype(F32))
        o_ref[...] = (sa * ba_ref[...] + sb * bb_ref[...]).astype(BF)

    ins = [(ba, _rspec(tm, D)), (bb, _rspec(tm, D)), (proj, _rspec(tm, D, 5)), (proj, _rspec(tm, D, 6))]
    return _rows_call(body, "merge", T // tm, ins, [_row_out(T, tm, D, BF)])[0]


def _post_out(x, mo, mod, n2g):
    T = x.shape[0]
    tm = TM_ROWS

    def body(x_ref, mo_ref, mod_ref, g_ref, x1_ref, h2_ref):
        g1 = mod_ref[:, 2 * D:3 * D]
        sh2, sc2 = mod_ref[:, 3 * D:4 * D], mod_ref[:, 4 * D:5 * D]
        x1 = x_ref[...] + g1 * mo_ref[...]
        x1_ref[...] = x1
        r = lax.rsqrt(jnp.mean(x1 * x1, axis=-1, keepdims=True) + EPS)
        h2_ref[...] = ((x1 * r) * g_ref[...] * (1.0 + sc2) + sh2).astype(BF)

    ins = [(x, _rspec(tm, D)), (mo, _rspec(tm, D)), (mod, _full(mod)), (n2g, _full(n2g))]
    return _rows_call(body, "post_out", T // tm, ins, [_row_out(T, tm, D, F32), _row_out(T, tm, D, BF)])


def _loss_head(x1, m2, tgt, mod):
    T = x1.shape[0]
    tm = TM_ROWS

    def body(x1_ref, m2_ref, t_ref, mod_ref, dy_ref, dm2_ref, dg2_ref, sq_ref):
        i = pl.program_id(0)

        @pl.when(i == 0)
        def _():
            dg2_ref[...] = jnp.zeros_like(dg2_ref)
            sq_ref[...] = jnp.zeros_like(sq_ref)

        g2 = mod_ref[:, 5 * D:6 * D]
        m2 = m2_ref[...]
        err = x1_ref[...] + g2 * m2 - t_ref[...]
        dy = err * (1.0 / D)
        dy_ref[...] = dy
        dm2_ref[...] = (g2 * dy).astype(BF)
        dg2_ref[...] += _colsum(dy * m2)
        sq_ref[...] += _colsum(err * err)

    ins = [(x1, _rspec(tm, D)), (m2, _rspec(tm, D)), (tgt, _rspec(tm, D)), (mod, _full(mod))]
    outs = [_row_out(T, tm, D, F32), _row_out(T, tm, D, BF), _acc_out((1, D)), _acc_out((1, D))]
    return _rows_call(body, "loss_head", T // tm, ins, outs)


def _norm2_bwd(dh2, x1, dy, mo, mod, n2g):
    T = x1.shape[0]
    tm = TM_ROWS

    def body(dh_ref, x1_ref, dy_ref, mo_ref, mod_ref, g_ref, dx1_ref, dmo_ref, dsh_ref, dsc_ref, dg_ref, dg1_ref):
        i = pl.program_id(0)

        @pl.when(i == 0)
        def _():
            for r in (dsh_ref, dsc_ref, dg_ref, dg1_ref):
                r[...] = jnp.zeros_like(r)

        g1, sc2 = mod_ref[:, 2 * D:3 * D], mod_ref[:, 4 * D:5 * D]
        g = g_ref[...]
        x1 = x1_ref[...]
        dh = dh_ref[...]
        r = lax.rsqrt(jnp.mean(x1 * x1, axis=-1, keepdims=True) + EPS)
        xn = x1 * r
        dsh_ref[...] += _colsum(dh)
        dsc_ref[...] += _colsum(dh * xn * g)
        dg_ref[...] += _colsum(dh * xn * (1.0 + sc2))
        dxn = dh * g * (1.0 + sc2)
        dx1 = dy_ref[...] + r * (dxn - xn * jnp.mean(dxn * xn, axis=-1, keepdims=True))
        dx1_ref[...] = dx1
        dg1_ref[...] += _colsum(dx1 * mo_ref[...])
        dmo_ref[...] = (g1 * dx1).astype(BF)

    ins = [(dh2, _rspec(tm, D)), (x1, _rspec(tm, D)), (dy, _rspec(tm, D)), (mo, _rspec(tm, D)),
           (mod, _full(mod)), (n2g, _full(n2g))]
    outs = [_row_out(T, tm, D, F32), _row_out(T, tm, D, BF)] + [_acc_out((1, D)) for _ in range(4)]
    return _rows_call(body, "norm2_bwd", T // tm, ins, outs)


def _gate_bwd(dmerged, ba, bb, proj):
    T = ba.shape[0]
    tm = TM_ROWS

    def body(dm_ref, ba_ref, bb_ref, ga_ref, gb_ref, dba_ref, dbb_ref, dga_ref, dgb_ref):
        dm = dm_ref[...]
        sa, sb = _sigmoid(ga_ref[...].astype(F32)), _sigmoid(gb_ref[...].astype(F32))
        dba_ref[...] = (dm * sa).astype(BF)
        dbb_ref[...] = (dm * sb).astype(BF)
        dga_ref[...] = (dm * ba_ref[...] * sa * (1.0 - sa)).astype(BF)
        dgb_ref[...] = (dm * bb_ref[...] * sb * (1.0 - sb)).astype(BF)

    ins = [(dmerged, _rspec(tm, D)), (ba, _rspec(tm, D)), (bb, _rspec(tm, D)),
           (proj, _rspec(tm, D, 5)), (proj, _rspec(tm, D, 6))]
    return _rows_call(body, "gate_bwd", T // tm, ins, [_row_out(T, tm, D, BF) for _ in range(4)])


def _qkv_bwd(dq, dk, dv, proj, f, dfc, dfq, qg2, kg2, bf_pad):
    T = proj.shape[0]
    tm = TM_ROWS
    n = T // tm
    seg = _seg_mat()
    tri = _tri_mat(tm, False)

    def body(dq_ref, dk_ref, dv_ref, q_ref, k_ref, f_ref, dfc_ref, dfq_ref, qg_ref, kg_ref, bf_ref, seg_ref, tri_ref,
             dqo_ref, dko_ref, dvo_ref, dfo_ref, dqg_ref, dkg_ref, dbf_ref, carry_ref):
        i = pl.program_id(0)

        @pl.when(i == 0)
        def _():
            carry_ref[...] = jnp.zeros_like(carry_ref)
            dqg_ref[...] = jnp.zeros_like(dqg_ref)
            dkg_ref[...] = jnp.zeros_like(dkg_ref)
            dbf_ref[...] = jnp.zeros_like(dbf_ref)

        segm = seg_ref[...]
        dqg = jnp.zeros((1, LANES), F32)
        dkg = jnp.zeros((1, LANES), F32)
        for j in range(D // LANES):
            sl = slice(j * LANES, (j + 1) * LANES)
            for (raw_ref, d_ref, gn_ref, o_ref, which) in ((q_ref, dq_ref, qg_ref, dqo_ref, 0), (k_ref, dk_ref, kg_ref, dko_ref, 1)):
                xc = raw_ref[:, sl].astype(F32)
                rr = lax.rsqrt(_dot_exact(xc * xc, segm) + EPS)
                xn = xc * rr
                dc = d_ref[:, sl]
                if which == 0:
                    dqg = dqg + _colsum(dc * xn)
                else:
                    dkg = dkg + _colsum(dc * xn)
                dxn = dc * gn_ref[...]
                o_ref[:, sl] = (rr * (dxn - xn * _dot_exact(dxn * xn, segm))).astype(BF)
        dqg_ref[...] += dqg
        dkg_ref[...] += dkg
        dvo_ref[...] = dv_ref[...].astype(BF)
        z = f_ref[...] + bf_ref[...]
        sneg_t = _sigmoid(-z).T[0:N_HEADS, :]
        dfc = dfc_ref[...] + dfq_ref[...]
        carry = carry_ref[:, 0:1]
        dlf = _dot_exact(dfc, tri_ref[...]) + carry
        carry_ref[...] = jnp.broadcast_to(carry + jnp.sum(dfc, axis=1, keepdims=True), carry_ref.shape)
        dzt = dlf * sneg_t
        dz = jnp.concatenate([dzt, jnp.zeros((LANES - N_HEADS, tm), F32)], axis=0).T
        dbf_ref[...] += _colsum(dz)
        dfo_ref[...] = dz.astype(BF)

    rs = lambda w, cb=0: _rspec(tm, w, cb, n)
    ins = [(dq, rs(D)), (dk, rs(D)), (dv, rs(D)), (proj, rs(D, 0)), (proj, rs(D, 1)), (f, rs(LANES)),
           (dfc, pl.BlockSpec((N_HEADS, tm), lambda i: (0, n - 1 - i))),
           (dfq, pl.BlockSpec((N_HEADS, tm), lambda i: (0, n - 1 - i))),
           (qg2, _full(qg2)), (kg2, _full(kg2)), (bf_pad, _full(bf_pad)), (seg, _full(seg)), (tri, _full(tri))]
    outs = [_row_out(T, tm, D, BF, n), _row_out(T, tm, D, BF, n), _row_out(T, tm, D, BF, n), _row_out(T, tm, LANES, BF, n),
            _acc_out((1, LANES)), _acc_out((1, LANES)), _acc_out((1, LANES))]
    return _rows_call(body, "qkv_bwd", n, ins, outs, [pltpu.VMEM((N_HEADS, LANES), F32)])


def _norm1_bwd(dh, dhf, x, dx1, mod, n1g):
    T = x.shape[0]
    tm = TM_ROWS

    def body(dh_ref, dhf_ref, x_ref, dx1_ref, mod_ref, g_ref, dx_ref, dsh_ref, dsc_ref, dg_ref):
        i = pl.program_id(0)

        @pl.when(i == 0)
        def _():
            for r in (dsh_ref, dsc_ref, dg_ref):
                r[...] = jnp.zeros_like(r)

        sc1 = mod_ref[:, D:2 * D]
        g = g_ref[...]
        xv = x_ref[...]
        dh = dh_ref[...] + dhf_ref[...]
        r = lax.rsqrt(jnp.mean(xv * xv, axis=-1, keepdims=True) + EPS)
        xn = xv * r
        dsh_ref[...] += _colsum(dh)
        dsc_ref[...] += _colsum(dh * xn * g)
        dg_ref[...] += _colsum(dh * xn * (1.0 + sc1))
        dxn = dh * g * (1.0 + sc1)
        dx_ref[...] = dx1_ref[...] + r * (dxn - xn * jnp.mean(dxn * xn, axis=-1, keepdims=True))

    ins = [(dh, _rspec(tm, D)), (dhf, _rspec(tm, D)), (x, _rspec(tm, D)), (dx1, _rspec(tm, D)),
           (mod, _full(mod)), (n1g, _full(n1g))]
    outs = [_row_out(T, tm, D, F32)] + [_acc_out((1, D)) for _ in range(3)]
    return _rows_call(body, "norm1_bwd", T // tm, ins, outs)


def _adamw_math(w, g, m, v):
    m = ADAM_B1 * m + (1.0 - ADAM_B1) * g
    v = ADAM_B2 * v + (1.0 - ADAM_B2) * (g * g)
    m_hat = m / (1.0 - ADAM_B1 ** ADAM_STEP)
    v_hat = v / (1.0 - ADAM_B2 ** ADAM_STEP)
    delta = -ADAM_LR * (m_hat / (jnp.sqrt(v_hat) + ADAM_EPS) + ADAM_WD * w)
    return delta, m, v


def _adamw(parts, w, m, v, name):
    n, R, C = parts.shape
    tr = R if R <= 256 else 256
    assert R % tr == 0

    def body(p_ref, w_ref, m_ref, v_ref, g_ref, d_ref, mo_ref, vo_ref):
        g = p_ref[0].astype(F32)
        for s in range(1, n):
            g = g + p_ref[s].astype(F32)
        g_ref[...] = g
        d_ref[...], mo_ref[...], vo_ref[...] = _adamw_math(w_ref[...], g, m_ref[...], v_ref[...])

    spec = pl.BlockSpec((None, tr, C), lambda i: (0, i, 0))
    return pl.pallas_call(
        body, name=name, grid=(R // tr,),
        in_specs=[pl.BlockSpec((n, tr, C), lambda i: (0, i, 0)), spec, spec, spec],
        out_specs=[spec] * 4, out_shape=[_sds((1, R, C), F32)] * 4,
        compiler_params=_cp(("parallel",)),
    )(parts, w, m, v)


def _rcopy(src, dst, ssem, rsem, peer):
    return pltpu.make_async_remote_copy(src_ref=src, dst_ref=dst, send_sem=ssem, recv_sem=rsem,
                                        device_id=peer, device_id_type=MESH)


def _ada_fwd(c, w_ada, b_slice, cw_shard):
    def body(c_ref, w_ref, b_ref, cw_ref, mod_ref, ca_ref, cwf_ref, call, mp, ssem, rsem):
        x, y, cc, me = _my_pos()
        call[pl.ds(me, 1), :] = c_ref[...]
        cwf_ref[me] = cw_ref[...]
        first = []
        for d in range(1, N_DEV):
            peer, _ = _peer(x, y, cc, d)
            first.append(_rcopy(c_ref, call.at[pl.ds(me, 1), :], ssem.at[0, d - 1], rsem.at[0, d - 1], peer))
            first.append(_rcopy(cw_ref, cwf_ref.at[me], ssem.at[1, d - 1], rsem.at[1, d - 1], peer))
        for cp in first:
            cp.start()
        for d in range(1, N_DEV):
            peer, pid = _peer(x, y, cc, d)
            _rcopy(c_ref, call.at[pl.ds(pid, 1), :], ssem.at[0, d - 1], rsem.at[0, d - 1], peer).wait_recv()
            _rcopy(cw_ref, cwf_ref.at[pid], ssem.at[1, d - 1], rsem.at[1, d - 1], peer).wait_recv()
        cv = call[...]
        ca = cv * _sigmoid(cv)
        ca_ref[...] = ca
        mp[...] = _dot_f32(ca, w_ref[...]) + b_ref[...]
        mod_ref[pl.ds(me, 1), :] = mp[pl.ds(me, 1), :]
        second = []
        for d in range(1, N_DEV):
            peer, pid = _peer(x, y, cc, d)
            second.append(_rcopy(mp.at[pl.ds(pid, 1), :], mod_ref.at[pl.ds(me, 1), :], ssem.at[2, d - 1], rsem.at[2, d - 1], peer))
        for cp in second:
            cp.start()
        for d in range(1, N_DEV):
            peer, pid = _peer(x, y, cc, d)
            _rcopy(mp.at[pl.ds(pid, 1), :], mod_ref.at[pl.ds(pid, 1), :], ssem.at[2, d - 1], rsem.at[2, d - 1], peer).wait_recv()
        for cp in first + second:
            cp.wait_send()

    vm = pl.BlockSpec(memory_space=pltpu.VMEM)
    return pl.pallas_call(
        body, name="ada_fwd",
        in_specs=[vm, vm, vm, vm], out_specs=[vm, vm, vm],
        out_shape=[_sds((N_DEV, ADA_SHARD), F32), _sds((N_DEV, D), F32), _sds((N_DEV, CONV_KP, LANES), F32)],
        scratch_shapes=[pltpu.VMEM((N_DEV, D), F32), pltpu.VMEM((N_DEV, ADA_SHARD), F32),
                        pltpu.SemaphoreType.DMA((3, N_DEV - 1)), pltpu.SemaphoreType.DMA((3, N_DEV - 1))],
        compiler_params=pltpu.CompilerParams(vmem_limit_bytes=VMEM_LIMIT),
    )(c, w_ada, b_slice, cw_shard)


def _xchg_parts(arrays):
    n = len(arrays)
    anyspec = pl.BlockSpec(memory_space=pl.ANY)
    shapes = [_sds((N_DEV,) + tuple(a.shape[-2:]), a.dtype) for a in arrays]
    scratch = [pltpu.SemaphoreType.DMA((n,)), pltpu.SemaphoreType.DMA((n, N_DEV - 1)),
               pltpu.SemaphoreType.DMA((n, N_DEV - 1))]
    return [anyspec] * n, shapes, scratch


def _xchg(ins, outs, sems, gather, wait):
    lsem, ssem, rsem = sems
    x, y, cc, me = _my_pos()
    for a in range(len(ins)):
        local = pltpu.make_async_copy(ins[a] if gather else ins[a].at[me], outs[a].at[me], lsem.at[a])
        if not wait:
            local.start()
        for d in range(1, N_DEV):
            peer, pid = _peer(x, y, cc, d)
            src = ins[a] if gather else ins[a].at[pid]
            if not wait:
                _rcopy(src, outs[a].at[me], ssem.at[a, d - 1], rsem.at[a, d - 1], peer).start()
            else:
                cp = _rcopy(src, outs[a].at[pid], ssem.at[a, d - 1], rsem.at[a, d - 1], peer)
                cp.wait_recv()
                cp.wait_send()
        if wait:
            local.wait()


def _gather_two_level(shard, name):
    def body(x_ref, out_ref, ssem, rsem, lsem):
        x, y, c, me = _my_pos()
        sibling = (x, y, 1 - c)
        chips = [(1 - x, y), (x, 1 - y), (1 - x, 1 - y)]
        slot = lambda px, py, pc: out_ref.at[4 * px + 2 * py + pc]

        def copy(kk, block, to, src=None):
            return _rcopy(slot(*block) if src is None else src, slot(*block), ssem.at[kk], rsem.at[kk], to)

        mine = pltpu.make_async_copy(x_ref, slot(x, y, c), lsem)
        mine.start()
        first = [copy(0, (x, y, c), sibling, src=x_ref)]
        first += [copy(1 + j, (x, y, c), (*chip, c), src=x_ref) for j, chip in enumerate(chips)]
        for cp in first:
            cp.start()
        passed = [copy(4 + j, (*chip, c), sibling) for j, chip in enumerate(chips)]
        for j, chip in enumerate(chips):
            copy(1 + j, (*chip, c), (x, y, c)).wait_recv()
            passed[j].start()
        copy(0, sibling, (x, y, c)).wait_recv()
        for j, chip in enumerate(chips):
            copy(4 + j, (*chip, 1 - c), (x, y, c)).wait_recv()
        for cp in first + passed:
            cp.wait_send()
        mine.wait()

    anyspec = pl.BlockSpec(memory_space=pl.ANY)
    return pl.pallas_call(
        body, name=name, in_specs=[anyspec], out_specs=anyspec,
        out_shape=_sds((N_DEV,) + tuple(shard.shape), shard.dtype),
        scratch_shapes=[pltpu.SemaphoreType.DMA((N_DEV - 1,)), pltpu.SemaphoreType.DMA((N_DEV - 1,)),
                        pltpu.SemaphoreType.DMA(())],
    )(shard)


PACK_ROWS = 16
ROW_MISC = 5
ROW_LOSS = 6
ROW_DMOD = 8


def _small_bwd(pack, dmodb, dcw, cat, wp, mp_, vp, cw_w, cw_m, cw_v):
    def body(pack_ref, dmodb_ref, dcw_ref, cat_ref, wp_ref, mp_ref, vp_ref, cww_ref, cwm_ref, cwv_ref,
             g_ref, d_ref, mo_ref, vo_ref, cg_ref, cd_ref, cm_ref, cv_ref, gwa_ref, loss_ref,
             allp, dmc, cwg, ssem, rsem):
        x, y, cc, me = _my_pos()
        allp[me] = pack_ref[...]
        dmc[pl.ds(me, 1), :] = dmodb_ref[pl.ds(me, 1), :]
        cwg[me] = dcw_ref[me]
        sends = []
        for d in range(1, N_DEV):
            peer, pid = _peer(x, y, cc, d)
            sends.append(_rcopy(pack_ref, allp.at[me], ssem.at[0, d - 1], rsem.at[0, d - 1], peer))
            sends.append(_rcopy(dmodb_ref.at[pl.ds(pid, 1), :], dmc.at[pl.ds(me, 1), :], ssem.at[1, d - 1], rsem.at[1, d - 1], peer))
            sends.append(_rcopy(dcw_ref.at[pid], cwg.at[me], ssem.at[2, d - 1], rsem.at[2, d - 1], peer))
        for cp in sends:
            cp.start()
        for d in range(1, N_DEV):
            peer, pid = _peer(x, y, cc, d)
            _rcopy(pack_ref, allp.at[pid], ssem.at[0, d - 1], rsem.at[0, d - 1], peer).wait_recv()
            _rcopy(dmodb_ref.at[pl.ds(pid, 1), :], dmc.at[pl.ds(pid, 1), :], ssem.at[1, d - 1], rsem.at[1, d - 1], peer).wait_recv()
            _rcopy(dcw_ref.at[pid], cwg.at[pid], ssem.at[2, d - 1], rsem.at[2, d - 1], peer).wait_recv()
        for cp in sends:
            cp.wait_send()

        tot = allp[0]
        cg = cwg[0]
        for s in range(1, N_DEV):
            tot = tot + allp[s]
            cg = cg + cwg[s]
        lane = lax.broadcasted_iota(jnp.int32, (PACK_ROWS, D), 1)
        row = lax.broadcasted_iota(jnp.int32, (PACK_ROWS, D), 0)
        gains = (row == ROW_MISC) & (lane >= LANES) & (lane < 3 * LANES)
        folded = tot + pltpu.roll(tot, D - HEAD_DIM, axis=1)
        keep = (lane % LANES) < HEAD_DIM
        g = jnp.where(gains, jnp.where(keep, folded, 0.0), tot)
        loss_ref[...] = jnp.broadcast_to(
            (0.5 / D) * jnp.sum(jnp.where(row == ROW_LOSS, tot, 0.0), keepdims=True).reshape(1, 1), loss_ref.shape)
        g = jnp.where(row == ROW_LOSS, 0.0, g)
        g_ref[...] = g
        d_ref[...], mo_ref[...], vo_ref[...] = _adamw_math(wp_ref[...], g, mp_ref[...], vp_ref[...])
        cg_ref[...] = cg
        cd_ref[...], cm_ref[...], cv_ref[...] = _adamw_math(cww_ref[...], cg, cwm_ref[...], cwv_ref[...])
        dm_pad = jnp.concatenate([dmc[...], jnp.zeros((LANES - N_DEV, ADA_SHARD), F32)], axis=0)
        gwa_ref[...] = _dot_f32(cat_ref[...], dm_pad)

    vm = pl.BlockSpec(memory_space=pltpu.VMEM)
    p16 = _sds((PACK_ROWS, D), F32)
    c32 = _sds((CONV_KP, LANES), F32)
    return pl.pallas_call(
        body, name="small_bwd",
        in_specs=[vm] * 10, out_specs=[vm] * 10,
        out_shape=[p16, p16, p16, p16, c32, c32, c32, c32, _sds((D, ADA_SHARD), F32), _sds((8, LANES), F32)],
        scratch_shapes=[pltpu.VMEM((N_DEV, PACK_ROWS, D), F32), pltpu.VMEM((N_DEV, ADA_SHARD), F32),
                        pltpu.VMEM((N_DEV, CONV_KP, LANES), F32),
                        pltpu.SemaphoreType.DMA((3, N_DEV - 1)), pltpu.SemaphoreType.DMA((3, N_DEV - 1))],
        compiler_params=pltpu.CompilerParams(vmem_limit_bytes=VMEM_LIMIT),
    )(pack, dmodb, dcw, cat, wp, mp_, vp, cw_w, cw_m, cw_v)


def _lanes(vec, start, total=D):
    n = vec.shape[1]
    return jnp.pad(vec, ((0, 0), (start, total - start - n)))


def _pack_small(rows5, misc, loss_row, six):
    z = jnp.zeros((1, D), F32)
    return jnp.concatenate(rows5 + [misc, loss_row, z] + [six.reshape(N_ADA, D), z, z], axis=0)


def kernel(x, c, w_ada, b_ada, norm1_g, w_in, b_forget, q_norm_g, k_norm_g, w_attn_proj, conv_w, conv_b, conv_ln_g, conv_ln_b, w_conv_proj, w_out, norm2_g, w_mlp1, w_mlp2, loss_target, m_w_ada, m_b_ada, m_norm1_g, m_w_in, m_b_forget, m_q_norm_g, m_k_norm_g, m_w_attn_proj, m_conv_w, m_conv_b, m_conv_ln_g, m_conv_ln_b, m_w_conv_proj, m_w_out, m_norm2_g, m_w_mlp1, m_w_mlp2, v_w_ada, v_b_ada, v_norm1_g, v_w_in, v_b_forget, v_q_norm_g, v_k_norm_g, v_w_attn_proj, v_conv_w, v_conv_b, v_conv_ln_g, v_conv_ln_b, v_w_conv_proj, v_w_out, v_norm2_g, v_w_mlp1, v_w_mlp2):
    me = 4 * lax.axis_index("x") + 2 * lax.axis_index("y") + lax.axis_index("c")
    xs, tgt = x[0], loss_target[0]
    T = xs.shape[0]
    sq = lambda a: a[0]
    pad_taps = lambda a: jnp.pad(a[0], ((0, CONV_KP - CONV_K), (0, 0)))

    b_slice = lax.dynamic_slice(b_ada, (0, me * ADA_SHARD), (1, ADA_SHARD))
    modb, ca_all, cwf = _ada_fwd(c, sq(w_ada), b_slice, pad_taps(conv_w))
    mod = modb.reshape(1, N_ADA * D)
    cw = jnp.transpose(cwf, (1, 0, 2)).reshape(CONV_KP, D)

    g_in = _gather_two_level(sq(w_in).astype(BF), "w_in_gather")
    d_in = g_in.shape[2] * N_DEV
    w_in_f = jnp.transpose(g_in, (1, 0, 2)).reshape(D, d_in)
    w_main = jnp.concatenate([w_in_f[:, :3 * D], w_in_f[:, 3 * D + N_HEADS:]], axis=1)
    w_f = jnp.pad(w_in_f[:, 3 * D:3 * D + N_HEADS], ((0, 0), (0, LANES - N_HEADS)))
    shards = [sq(w_attn_proj).astype(BF), sq(w_conv_proj).astype(BF), sq(w_out).astype(BF),
              sq(w_mlp1).astype(BF), sq(w_mlp2).astype(BF)]

    qg2 = jnp.tile(q_norm_g, (1, 2))
    kg2 = jnp.tile(k_norm_g, (1, 2))
    bf_pad = _lanes(b_forget, 0, LANES)

    h = _pre_in(xs, mod, norm1_g)
    proj = _matmul(h, w_main, "nn", BF, "mm_proj")
    f = _matmul(h, w_f, "nn", F32, "mm_f")
    q, k, v, fc = _qkv_post(proj, f, qg2, kg2, bf_pad)
    fc3 = fc.reshape(N_HEADS // 2, 2, T)
    o, lse, g_ap, g_cp, g_out, g_1, g_2 = _flash_fwd(q, k, v, fc3, shards)
    w_ap, w_cp, w_o = g_ap.reshape(D, D), g_cp.reshape(D, D), g_out.reshape(D, D)
    w_1 = jnp.transpose(g_1, (1, 0, 2)).reshape(D, D_FF)
    w_2 = g_2.reshape(D_FF, D)
    ba = _matmul(o, w_ap, "nn", F32, "mm_ba")
    u0, u1, u3 = _conv_fwd(proj, cw, conv_b, conv_ln_g, conv_ln_b)
    bb = _matmul(u3, w_cp, "nn", F32, "mm_bb")
    merged = _merge(ba, bb, proj)
    mo = _matmul(merged, w_o, "nn", F32, "mm_out")
    x1, h2 = _post_out(xs, mo, mod, norm2_g)
    a, rl = _matmul(h2, w_1, "nn", BF, "mm_mlp1", relu2=True)
    m2 = _matmul(rl, w_2, "nn", F32, "mm_mlp2")
    dy, dm2, dg2, sqcols = _loss_head(x1, m2, tgt, mod)

    da = _matmul(dm2, w_2, "nt", BF, "mm_drl", relu_of=a)
    dw_2 = _matmul(rl, dm2, "tn", BF, "mm_dw2")
    dh2 = _matmul(da, w_1, "nt", F32, "mm_dh2")
    dw_1 = _matmul(h2, da, "tn", BF, "mm_dw1")
    dx1, dmo, dsh2, dsc2, dn2g, dg1 = _norm2_bwd(dh2, x1, dy, mo, mod, norm2_g)
    dmerged = _matmul(dmo, w_o, "nt", F32, "mm_dmerged")
    dw_o = _matmul(merged, dmo, "tn", BF, "mm_dwout")
    dba, dbb, dga, dgb = _gate_bwd(dmerged, ba, bb, proj)
    du3 = _matmul(dbb, w_cp, "nt", F32, "mm_du3")
    dw_cp = _matmul(u3, dbb, "tn", BF, "mm_dwcp")
    do = _matmul(dba, w_ap, "nt", F32, "mm_do")
    dw_ap = _matmul(o, dba, "tn", BF, "mm_dwap")
    dglu_a, dglu_b, dlng, dlnb, dcb, dcw_full = _conv_bwd(du3, u1, u0, proj, cw, conv_ln_g, conv_ln_b)
    dq, delta, rs_a, rs_b = _flash_bwd_dq(q, k, v, do, o, lse, fc3)
    parts = [dw_ap.reshape(N_DEV, D // N_DEV, D), dw_cp.reshape(N_DEV, D // N_DEV, D), dw_o.reshape(N_DEV, D // N_DEV, D),
             jnp.transpose(dw_1.reshape(D, N_DEV, D_FF // N_DEV), (1, 0, 2)), dw_2.reshape(N_DEV, D_FF // N_DEV, D)]
    dk, dv, dfc3, r_ap, r_cp, r_out, r_1, r_2 = _flash_bwd_dkv(q, k, v, do, lse, delta, fc3, parts)
    dfq = jnp.stack([rs_a, rs_b], axis=1).reshape(N_HEADS, T)
    dq_raw, dk_raw, dv_b, df, dqg, dkg, dbf = _qkv_bwd(dq, dk, dv, proj, f, dfc3.reshape(N_HEADS, T), dfq, qg2, kg2, bf_pad)
    dproj = jnp.concatenate([dq_raw, dk_raw, dv_b, dglu_a, dglu_b, dga, dgb], axis=1)
    dw_main = _matmul(h, dproj, "tn", BF, "mm_dwmain")
    dw_f = _matmul(h, df, "tn", BF, "mm_dwf")
    dw_in_f = jnp.concatenate([dw_main[:, :3 * D], dw_f[:, :N_HEADS], dw_main[:, 3 * D:]], axis=1)
    part_in = jnp.transpose(dw_in_f.reshape(D, N_DEV, d_in // N_DEV), (1, 0, 2))
    dh, r_in = _matmul(dproj, w_main, "nt", F32, "mm_dh", scatter=(part_in,))
    dhf = _matmul(df, w_f, "nt", F32, "mm_dhf")
    grad_x, dsh1, dsc1, dn1g = _norm1_bwd(dh, dhf, xs, dx1, mod, norm1_g)

    dmod = jnp.concatenate([dsh1, dsc1, dg1, dsh2, dsc2, dg2], axis=1)
    misc = jnp.concatenate([dbf, dqg, dkg, jnp.zeros((1, D - 3 * LANES), F32)], axis=1)
    pack = _pack_small([dn1g, dcb, dlng, dlnb, dn2g], misc, sqcols, dmod)
    dcw_blocks = jnp.transpose(dcw_full.reshape(CONV_KP, N_DEV, LANES), (1, 0, 2))

    def small_params(b_a, n1, bfg, qn, kn, cvb, lg, lb, n2):
        misc_p = jnp.concatenate([_lanes(bfg, 0, LANES), _lanes(qn, 0, LANES), _lanes(kn, 0, LANES),
                                  jnp.zeros((1, D - 3 * LANES), F32)], axis=1)
        return _pack_small([n1, cvb, lg, lb, n2], misc_p, jnp.zeros((1, D), F32), b_a)

    wp = small_params(b_ada, norm1_g, b_forget, q_norm_g, k_norm_g, conv_b, conv_ln_g, conv_ln_b, norm2_g)
    mp_ = small_params(m_b_ada, m_norm1_g, m_b_forget, m_q_norm_g, m_k_norm_g, m_conv_b, m_conv_ln_g, m_conv_ln_b, m_norm2_g)
    vp = small_params(v_b_ada, v_norm1_g, v_b_forget, v_q_norm_g, v_k_norm_g, v_conv_b, v_conv_ln_g, v_conv_ln_b, v_norm2_g)
    cat = jnp.pad(jnp.transpose(ca_all), ((0, 0), (0, LANES - N_DEV)))
    small = _small_bwd(pack, dmod.reshape(N_DEV, ADA_SHARD), dcw_blocks, cat,
                       wp, mp_, vp, pad_taps(conv_w), pad_taps(m_conv_w), pad_taps(v_conv_w))
    sp = small[0:4]
    scw = small[4:8]
    gw_ada, loss_t = small[8], small[9]
    loss = loss_t[0, 0]

    def unpack(p):
        misc_r = p[ROW_MISC:ROW_MISC + 1]
        return dict(
            b_ada=p[ROW_DMOD:ROW_DMOD + N_ADA].reshape(1, N_ADA * D), norm1_g=p[0:1], conv_b=p[1:2], conv_ln_g=p[2:3],
            conv_ln_b=p[3:4], norm2_g=p[4:5], b_forget=misc_r[:, 0:N_HEADS],
            q_norm_g=misc_r[:, LANES:LANES + HEAD_DIM], k_norm_g=misc_r[:, 2 * LANES:2 * LANES + HEAD_DIM])

    res = {}
    res["w_ada"] = _adamw(gw_ada[None], w_ada, m_w_ada, v_w_ada, "adamw_w_ada")
    res["w_in"] = _adamw(r_in, w_in, m_w_in, v_w_in, "adamw_w_in")
    res["w_attn_proj"] = _adamw(r_ap, w_attn_proj, m_w_attn_proj, v_w_attn_proj, "adamw_w_ap")
    res["w_conv_proj"] = _adamw(r_cp, w_conv_proj, m_w_conv_proj, v_w_conv_proj, "adamw_w_cp")
    res["w_out"] = _adamw(r_out, w_out, m_w_out, v_w_out, "adamw_w_out")
    res["w_mlp1"] = _adamw(r_1, w_mlp1, m_w_mlp1, v_w_mlp1, "adamw_w_mlp1")
    res["w_mlp2"] = _adamw(r_2, w_mlp2, m_w_mlp2, v_w_mlp2, "adamw_w_mlp2")

    names = ["w_ada", "b_ada", "norm1_g", "w_in", "b_forget", "q_norm_g", "k_norm_g", "w_attn_proj", "conv_w", "conv_b",
             "conv_ln_g", "conv_ln_b", "w_conv_proj", "w_out", "norm2_g", "w_mlp1", "w_mlp2"]
    outs = [loss, grad_x[None]]
    for kind in range(4):
        small_d = unpack(sp[kind])
        for nm in names:
            if nm in res:
                outs.append(res[nm][kind])
            elif nm == "conv_w":
                outs.append(scw[kind][:CONV_K][None])
            else:
                outs.append(small_d[nm])
    return tuple(outs)
```

```python
import functools

import jax
import jax.numpy as jnp
from jax import lax
from jax.experimental import pallas as pl
from jax.experimental.pallas import tpu as pltpu

F32 = jnp.float32
BF = jnp.bfloat16

N_DEV = 8
D = 1024
N_HEADS = 16
HEAD_DIM = 64
LANES = 128
SUBLANES = 8
CONV_K = 31
CONV_KP = 32
HALO = 32
CONV_ROWS = 64
D_FF = 4 * D
N_ADA = 6
ADA_SHARD = N_ADA * D // N_DEV
EPS = 1e-6
QK_SCALE = HEAD_DIM ** -0.5
LOG2E = 1.4426950408889634
LN2 = 0.6931471805599453
NEG = -1e30

ADAM_LR = 0.001
ADAM_B1 = 0.9
ADAM_B2 = 0.999
ADAM_EPS = 1e-08
ADAM_WD = 0.01
ADAM_STEP = 10

VMEM_LIMIT = 56 * 1024 * 1024
TM_ROWS = 256
TQ = 512

MESH = pl.DeviceIdType.MESH


def _cp(sem=None):
    return pltpu.CompilerParams(dimension_semantics=sem, vmem_limit_bytes=VMEM_LIMIT)


def _sds(shape, dtype):
    return jax.ShapeDtypeStruct(tuple(shape), dtype)


def _full(arr):
    nd = arr.ndim
    return pl.BlockSpec(arr.shape, lambda *_: (0,) * nd)


def _fullshape(shape):
    nd = len(shape)
    return pl.BlockSpec(tuple(shape), lambda *_: (0,) * nd)


def _split3(x):
    hi = x.astype(BF)
    r1 = x - hi.astype(F32)
    mid = r1.astype(BF)
    lo = (r1 - mid.astype(F32)).astype(BF)
    return hi, mid, lo


def _dot_exact(x, mat):
    hi, mid, lo = _split3(x)
    d = lambda t: jnp.dot(t, mat, preferred_element_type=F32)
    return d(hi) + d(mid) + d(lo)


def _dot_f32(a, b):
    a1, a2, a3 = _split3(a)
    b1, b2, b3 = _split3(b)
    d = lambda s, t: jnp.dot(s, t, preferred_element_type=F32)
    return (d(a1, b3) + d(a3, b1) + d(a2, b2)) + (d(a1, b2) + d(a2, b1)) + d(a1, b1)


def _sigmoid(x):
    return 1.0 / (1.0 + jnp.exp(-x))


def _colsum(x):
    return jnp.sum(x, axis=0, keepdims=True)


def _my_pos():
    x, y, c = lax.axis_index("x"), lax.axis_index("y"), lax.axis_index("c")
    return x, y, c, 4 * x + 2 * y + c


def _peer(x, y, c, d):
    px = (1 - x) if d & 4 else x
    py = (1 - y) if d & 2 else y
    pc = (1 - c) if d & 1 else c
    return (px, py, pc), 4 * px + 2 * py + pc


def _matmul(a, b, form, out_dtype, name, tm=1024, tn=1024, tk=1024, scatter=(), relu2=False, relu_of=None,
            b_slots=False, out_slots=False):
    if b_slots:
        assert form in ("nn", "nt") and b.shape[0] == N_DEV
        width = b.shape[2]
        if form == "nn":
            (M, K), N, tn = a.shape, N_DEV * width, width
        else:
            (M, K), N, tk = a.shape, b.shape[1], width
    elif form == "nn":
        (M, K), N = a.shape, b.shape[1]
    elif form == "nt":
        (M, K), N = a.shape, b.shape[0]
    else:
        (K, M), N = a.shape, b.shape[1]
    if out_slots:
        tn = N // N_DEV
    tm, tn, tk = min(tm, M), min(tn, N), min(tk, K)
    assert M % tm == 0 and N % tn == 0 and K % tk == 0, (name, M, N, K)
    nk = K // tk
    if form == "tn":
        a_spec = pl.BlockSpec((tk, tm), lambda i, j, k: (k, i))
        dn = (((0,), (0,)), ((), ()))
    else:
        a_spec = pl.BlockSpec((tm, tk), lambda i, j, k: (i, k))
        dn = (((1,), (1 if form == "nt" else 0,)), ((), ()))
    if b_slots and form == "nn":
        b_spec = pl.BlockSpec((None, tk, tn), lambda i, j, k: (j, k, 0))
    elif b_slots:
        b_spec = pl.BlockSpec((None, tn, tk), lambda i, j, k: (k, j, 0))
    elif form == "nt":
        b_spec = pl.BlockSpec((tn, tk), lambda i, j, k: (j, k))
    else:
        b_spec = pl.BlockSpec((tk, tn), lambda i, j, k: (k, j))

    nx = len(scatter)
    ne = 0 if relu_of is None else 1
    no = 2 if relu2 else 1
    grid = (M // tm, N // tn, nk)

    def body(a_ref, b_ref, *rest):
        e_ref = rest[0] if ne else None
        x_in = rest[ne:ne + nx]
        o_refs = rest[ne + nx:ne + nx + no]
        x_out = rest[ne + nx + no:ne + 2 * nx + no]
        scr = rest[ne + 2 * nx + no:]
        if nx:
            first, last = _first_last(grid)

            @pl.when(first)
            def _():
                _xchg(x_in, x_out, scr[-3:], False, wait=False)

        def finish(val):
            if relu2:
                o_refs[0][...] = val.astype(out_dtype)
                r = jnp.maximum(val, 0.0)
                o_refs[1][...] = (r * r).astype(out_dtype)
            elif ne:
                o_refs[0][...] = (val * (2.0 * jnp.maximum(e_ref[...].astype(F32), 0.0))).astype(out_dtype)
            else:
                o_refs[0][...] = val.astype(out_dtype)

        part = lax.dot_general(a_ref[...].astype(BF), b_ref[...].astype(BF), dn, preferred_element_type=F32)
        if nk == 1:
            finish(part)
        else:
            acc = scr[0]
            k = pl.program_id(2)

            @pl.when(k == 0)
            def _():
                acc[...] = part

            @pl.when(k > 0)
            def _():
                acc[...] += part

            @pl.when(k == nk - 1)
            def _():
                finish(acc[...])

        if nx:
            @pl.when(last)
            def _():
                _xchg(x_in, x_out, scr[-3:], False, wait=True)

    x_specs, x_shapes, x_scratch = _xchg_parts(scatter) if nx else ([], [], [])
    sem = ("arbitrary",) * 3 if nx else ("parallel", "parallel", "arbitrary")
    o_spec = pl.BlockSpec((tm, tn), lambda i, j, k: (i, j))
    o_shape = _sds((M, N), out_dtype)
    if out_slots:
        o_spec = pl.BlockSpec((None, tm, tn), lambda i, j, k: (j, i, 0))
        o_shape = _sds((N_DEV, M, tn), out_dtype)
    res = pl.pallas_call(
        body, name=name, grid=grid,
        in_specs=[a_spec, b_spec] + [o_spec] * ne + x_specs,
        out_specs=[o_spec] * no + x_specs,
        out_shape=[o_shape] * no + x_shapes,
        scratch_shapes=([] if nk == 1 else [pltpu.VMEM((tm, tn), F32)]) + x_scratch,
        compiler_params=_cp(sem),
    )(a, b, *([relu_of] if ne else []), *scatter)
    return res if (nx or relu2) else res[0]


def _rows_call(body, name, n_tiles, ins, outs, scratch=()):
    res = pl.pallas_call(
        body, name=name, grid=(n_tiles,),
        in_specs=[s for _, s in ins],
        out_specs=[s for _, s in outs],
        out_shape=[o for o, _ in outs],
        scratch_shapes=list(scratch),
        compiler_params=_cp(("arbitrary",)),
    )(*[a for a, _ in ins])
    return res


def _rspec(tm, width, cb=0, rev_n=None):
    if rev_n is None:
        return pl.BlockSpec((tm, width), lambda i: (i, cb))
    return pl.BlockSpec((tm, width), lambda i: (rev_n - 1 - i, cb))


def _row_out(T, tm, width, dtype, rev_n=None):
    return (_sds((T, width), dtype), _rspec(tm, width, 0, rev_n))


def _acc_out(shape, dtype=F32):
    return (_sds(shape, dtype), _fullshape(shape))


def _mod_parts(mod):
    return [mod[:, i * D:(i + 1) * D] for i in range(N_ADA)]


def _pre_in(x, mod, n1g):
    T = x.shape[0]
    tm = TM_ROWS

    def body(x_ref, mod_ref, g_ref, h_ref):
        sh1, sc1 = mod_ref[:, 0:D], mod_ref[:, D:2 * D]
        xv = x_ref[...]
        r = lax.rsqrt(jnp.mean(xv * xv, axis=-1, keepdims=True) + EPS)
        h_ref[...] = ((xv * r) * g_ref[...] * (1.0 + sc1) + sh1).astype(BF)

    return _rows_call(body, "pre_in", T // tm,
                      [(x, _rspec(tm, D)), (mod, _full(mod)), (n1g, _full(n1g))],
                      [_row_out(T, tm, D, BF)])[0]


def _seg_mat():
    r = jnp.arange(LANES)[:, None] // HEAD_DIM
    c = jnp.arange(LANES)[None, :] // HEAD_DIM
    return jnp.where(r == c, 1.0 / HEAD_DIM, 0.0).astype(BF)


def _tri_mat(n, upper):
    r = jnp.arange(n)[:, None]
    c = jnp.arange(n)[None, :]
    return jnp.where((r <= c) if upper else (r >= c), 1.0, 0.0).astype(BF)


def _log_sigmoid(z):
    return jnp.minimum(z, 0.0) - jnp.log(1.0 + jnp.exp(-jnp.abs(z)))


def _qkv_post(proj, f, qg2, kg2, bf_pad):
    T = proj.shape[0]
    tm = TM_ROWS
    seg = _seg_mat()
    tri = _tri_mat(tm, True)

    def body(q_ref, k_ref, v_ref, f_ref, qg_ref, kg_ref, bf_ref, seg_ref, tri_ref,
             qo_ref, ko_ref, vo_ref, fc_ref, carry_ref):
        i = pl.program_id(0)

        @pl.when(i == 0)
        def _():
            carry_ref[...] = jnp.zeros_like(carry_ref)

        segm = seg_ref[...]
        for j in range(D // LANES):
            sl = slice(j * LANES, (j + 1) * LANES)
            qc = q_ref[:, sl].astype(F32)
            rq = lax.rsqrt(_dot_exact(qc * qc, segm) + EPS)
            qo_ref[:, sl] = ((qc * rq) * qg_ref[...] * (QK_SCALE * LOG2E)).astype(BF)
            kc = k_ref[:, sl].astype(F32)
            rk = lax.rsqrt(_dot_exact(kc * kc, segm) + EPS)
            ko_ref[:, sl] = ((kc * rk) * kg_ref[...]).astype(BF)
        vo_ref[...] = v_ref[...].astype(BF)
        lf = _log_sigmoid(f_ref[...] + bf_ref[...])
        lft = lf.T[0:N_HEADS, :]
        carry = carry_ref[:, 0:1]
        fc_ref[...] = _dot_exact(lft, tri_ref[...]) + carry
        carry_ref[...] = jnp.broadcast_to(carry + jnp.sum(lft, axis=1, keepdims=True), carry_ref.shape)

    outs = [_row_out(T, tm, D, BF), _row_out(T, tm, D, BF), _row_out(T, tm, D, BF),
            (_sds((N_HEADS, T), F32), pl.BlockSpec((N_HEADS, tm), lambda i: (0, i)))]
    ins = [(proj, _rspec(tm, D, 0)), (proj, _rspec(tm, D, 1)), (proj, _rspec(tm, D, 2)), (f, _rspec(tm, LANES)),
           (qg2, _full(qg2)), (kg2, _full(kg2)), (bf_pad, _full(bf_pad)), (seg, _full(seg)), (tri, _full(tri))]
    return _rows_call(body, "qkv_post", T // tm, ins, outs, [pltpu.VMEM((N_HEADS, LANES), F32)])


def _lane_lo():
    return lax.broadcasted_iota(jnp.int32, (1, LANES), 1) < HEAD_DIM


def _nt(a, b):
    return lax.dot_general(a, b, (((1,), (1,)), ((), ())), preferred_element_type=F32)


def _tn(a, b):
    return lax.dot_general(a, b, (((0,), (0,)), ((), ())), preferred_element_type=F32)


def _head_rep(x, lo):
    rolled = pltpu.roll(x, HEAD_DIM, axis=1)
    return jnp.where(lo, x, rolled), jnp.where(lo, rolled, x)


def _diag_mask(t):
    return lax.broadcasted_iota(jnp.int32, (t, t), 1) <= lax.broadcasted_iota(jnp.int32, (t, t), 0)


def _first_last(grid):
    ids = [pl.program_id(a) for a in range(len(grid))]
    first = functools.reduce(jnp.logical_and, [i == 0 for i in ids])
    last = functools.reduce(jnp.logical_and, [i == g - 1 for i, g in zip(ids, grid)])
    return first, last


def _flash_fwd(q, k, v, fc3, shards):
    T = q.shape[0]
    tq = TQ
    nq = T // tq
    hp_n = N_HEADS // 2
    rep = tq // LANES
    nx = len(shards)
    grid = (hp_n, nq, nq)

    def body(q_ref, k_ref, v_ref, fk_ref, fq_ref, *rest):
        x_in, (o_ref, lse_ref), x_out = rest[:nx], rest[nx:nx + 2], rest[nx + 2:2 * nx + 2]
        acc_ref, m_ref, l_ref = rest[2 * nx + 2:2 * nx + 5]
        sems = rest[2 * nx + 5:]
        qi, ki = pl.program_id(1), pl.program_id(2)
        first, last = _first_last(grid)

        @pl.when(first)
        def _():
            _xchg(x_in, x_out, sems, True, wait=False)

        @pl.when(ki == 0)
        def _():
            acc_ref[...] = jnp.zeros_like(acc_ref)
            m_ref[...] = jnp.full_like(m_ref, NEG)
            l_ref[...] = jnp.zeros_like(l_ref)

        def step(diag):
            lo = _lane_lo()
            q2, k2, v2 = q_ref[...], k_ref[...], v_ref[...]
            zero = jnp.zeros_like(k2)
            bias = (fq_ref[:, 0:1] - fk_ref[...]) * LOG2E
            alphas, pvs = [], []
            for hh in range(2):
                sel = (lambda t: jnp.where(lo, t, zero)) if hh == 0 else (lambda t: jnp.where(lo, zero, t))
                s = _nt(sel(q2), k2) + bias[hh:hh + 1, :]
                if diag:
                    s = jnp.where(_diag_mask(tq), s, NEG)
                m_old = m_ref[hh]
                m_new = jnp.maximum(m_old, jnp.max(s, axis=-1, keepdims=True))
                alpha = jnp.exp2(m_old - m_new)
                p = jnp.exp2(s - jnp.tile(m_new, (1, rep)))
                l_ref[hh] = alpha * l_ref[hh] + jnp.sum(p, axis=-1, keepdims=True)
                m_ref[hh] = m_new
                pvs.append(jnp.dot(p.astype(BF), sel(v2), preferred_element_type=F32))
                alphas.append(alpha)
            acc_ref[...] = acc_ref[...] * jnp.where(lo, alphas[0], alphas[1]) + pvs[0] + pvs[1]

        @pl.when(ki < qi)
        def _():
            step(False)

        @pl.when(ki == qi)
        def _():
            step(True)
            lo = _lane_lo()
            la, lb = l_ref[0], l_ref[1]
            o_ref[...] = acc_ref[...] * jnp.where(lo, 1.0 / la, 1.0 / lb)
            lse_ref[...] = jnp.where(lo, m_ref[0] + jnp.log(la) * LOG2E, m_ref[1] + jnp.log(lb) * LOG2E)

        @pl.when(last)
        def _():
            _xchg(x_in, x_out, sems, True, wait=True)

    qspec = pl.BlockSpec((tq, LANES), lambda h, i, j: (i, h))
    kspec = pl.BlockSpec((tq, LANES), lambda h, i, j: (jnp.minimum(i, j), h))
    fkspec = pl.BlockSpec((None, 2, tq), lambda h, i, j: (h, 0, jnp.minimum(i, j)))
    fqspec = pl.BlockSpec((None, 2, tq), lambda h, i, j: (h, 0, i))
    x_specs, x_shapes, x_scratch = _xchg_parts(shards)
    return pl.pallas_call(
        body, name="attn_fwd", grid=grid,
        in_specs=[qspec, kspec, kspec, fkspec, fqspec] + x_specs,
        out_specs=[qspec, qspec] + x_specs,
        out_shape=[_sds((T, D), F32), _sds((T, D), F32)] + x_shapes,
        scratch_shapes=[pltpu.VMEM((tq, LANES), F32), pltpu.VMEM((2, tq, LANES), F32),
                        pltpu.VMEM((2, tq, LANES), F32)] + x_scratch,
        compiler_params=_cp(("arbitrary", "arbitrary", "arbitrary")),
    )(q, k, v, fc3, fc3, *shards)


def _attn_delta(do, o):
    T = o.shape[0]
    tm = TM_ROWS
    ones = (_seg_mat().astype(F32) * HEAD_DIM).astype(BF)

    def body(do_ref, o_ref, seg_ref, dl_ref):
        segm = seg_ref[...]
        for j in range(D // LANES):
            sl = slice(j * LANES, (j + 1) * LANES)
            dl_ref[:, sl] = _dot_exact(do_ref[:, sl].astype(BF).astype(F32) * o_ref[:, sl], segm)

    ins = [(do, _rspec(tm, D)), (o, _rspec(tm, D)), (ones, _full(ones))]
    return _rows_call(body, "attn_delta", T // tm, ins, [_row_out(T, tm, D, F32)])[0]


def _flash_bwd(q, k, v, do, lse, delta, fc3, parts):
    T = q.shape[0]
    tq = TQ
    nq = T // tq
    hp_n = N_HEADS // 2
    rep = tq // LANES
    nx = len(parts)
    grid = (hp_n, nq, nq)

    def body(q_ref, k_ref, v_ref, do_ref, lse_ref, dl_ref, fk_ref, fq_ref, *rest):
        x_in, x_out = rest[:nx], rest[nx + 6:2 * nx + 6]
        dq_ref, ra_ref, rb_ref, dk_ref, dv_ref, dfc_ref = rest[nx:nx + 6]
        dk_acc, dv_acc, df_acc = rest[2 * nx + 6:2 * nx + 9]
        sems = rest[2 * nx + 9:]
        ki, qi = pl.program_id(1), pl.program_id(2)
        first, last = _first_last(grid)
        qrows = pl.ds(pl.multiple_of(qi * tq, tq), tq)

        @pl.when(first)
        def _():
            _xchg(x_in, x_out, sems, False, wait=False)

        @pl.when((ki == 0) & (qi == 0))
        def _():
            dq_ref[...] = jnp.zeros_like(dq_ref)
            ra_ref[...] = jnp.zeros_like(ra_ref)
            rb_ref[...] = jnp.zeros_like(rb_ref)

        @pl.when(qi == 0)
        def _():
            dk_acc[...] = jnp.zeros_like(dk_acc)
            dv_acc[...] = jnp.zeros_like(dv_acc)
            df_acc[...] = jnp.zeros_like(df_acc)

        def step(diag):
            lo = _lane_lo()
            q2, k2, v2 = q_ref[...], k_ref[...], v_ref[...]
            do2 = do_ref[...].astype(BF)
            zero = jnp.zeros_like(q2)
            bias = (fq_ref[:, 0:1] - fk_ref[...]) * LOG2E
            lses = _head_rep(lse_ref[...], lo)
            dls = _head_rep(dl_ref[...], lo)
            dk_t = None
            dv_t = None
            dq_t = None
            for hh in range(2):
                sel = (lambda t: jnp.where(lo, t, zero)) if hh == 0 else (lambda t: jnp.where(lo, zero, t))
                s = _nt(sel(q2), k2) + bias[hh:hh + 1, :]
                if diag:
                    s = jnp.where(_diag_mask(tq), s, NEG)
                p = jnp.exp2(s - jnp.tile(lses[hh], (1, rep)))
                dp = _nt(sel(do2), v2)
                ds = p * (dp - jnp.tile(dls[hh], (1, rep)))
                ds_b = ds.astype(BF)
                dvp = _tn(p.astype(BF), sel(do2))
                dkp = _tn(ds_b, sel(q2))
                dqp = jnp.dot(ds_b, sel(k2), preferred_element_type=F32)
                dv_t = dvp if dv_t is None else dv_t + dvp
                dk_t = dkp if dk_t is None else dk_t + dkp
                dq_t = dqp if dq_t is None else dq_t + dqp
                df_acc[hh:hh + 1, :] -= _colsum(ds)
                r_ref = ra_ref if hh == 0 else rb_ref
                r_ref[qrows, :] += jnp.sum(ds, axis=-1, keepdims=True)
            dk_acc[...] += dk_t
            dv_acc[...] += dv_t
            dq_ref[qrows, :] += dq_t * QK_SCALE

        @pl.when(qi > ki)
        def _():
            step(False)

        @pl.when(qi == ki)
        def _():
            step(True)

        @pl.when(qi == nq - 1)
        def _():
            dk_ref[...] = dk_acc[...] * LN2
            dv_ref[...] = dv_acc[...]
            dfc_ref[...] = df_acc[...]

        @pl.when(last)
        def _():
            _xchg(x_in, x_out, sems, False, wait=True)

    kspec = pl.BlockSpec((tq, LANES), lambda h, j, i: (j, h))
    qspec = pl.BlockSpec((tq, LANES), lambda h, j, i: (jnp.maximum(i, j), h))
    fkspec = pl.BlockSpec((None, 2, tq), lambda h, j, i: (h, 0, j))
    fqspec = pl.BlockSpec((None, 2, tq), lambda h, j, i: (h, 0, jnp.maximum(i, j)))
    x_specs, x_shapes, x_scratch = _xchg_parts(parts)
    dqspec = pl.BlockSpec((T, LANES), lambda h, j, i: (0, h))
    rspec = pl.BlockSpec((None, T, 1), lambda h, j, i: (h, 0, 0))
    return pl.pallas_call(
        body, name="attn_bwd", grid=grid,
        in_specs=[qspec, kspec, kspec, qspec, qspec, qspec, fkspec, fqspec] + x_specs,
        out_specs=[dqspec, rspec, rspec, kspec, kspec, fkspec] + x_specs,
        out_shape=[_sds((T, D), F32), _sds((hp_n, T, 1), F32), _sds((hp_n, T, 1), F32),
                   _sds((T, D), F32), _sds((T, D), F32), _sds((hp_n, 2, T), F32)] + x_shapes,
        scratch_shapes=[pltpu.VMEM((tq, LANES), F32), pltpu.VMEM((tq, LANES), F32), pltpu.VMEM((2, tq), F32)] + x_scratch,
        compiler_params=_cp(("arbitrary", "arbitrary", "arbitrary")),
    )(q, k, v, do, lse, delta, fc3, fc3, *parts)


def _layer_norm_stats(u1):
    mu = jnp.mean(u1, axis=-1, keepdims=True)
    xc = u1 - mu
    rstd = lax.rsqrt(jnp.mean(xc * xc, axis=-1, keepdims=True) + EPS)
    return xc * rstd, rstd


def _shifted_copies(buf, sh, tm):
    rows = tm + HALO - SUBLANES
    for b in range(1, SUBLANES):
        sh[b - 1, 0:rows, :] = buf[b:b + rows, :]


def _window(buf, sh, off, rows, sl):
    a8, b = off // SUBLANES * SUBLANES, off % SUBLANES
    return buf[a8:a8 + rows, sl] if b == 0 else sh[b - 1, a8:a8 + rows, sl]


def _conv_fwd(proj, cw, cb, lng, lnb):
    T = proj.shape[0]
    tm = TM_ROWS

    def body(a_ref, b_ref, w_ref, cb_ref, g_ref, bb_ref, u0_ref, u1_ref, u3_ref, buf, sh):
        i = pl.program_id(0)

        @pl.when(i == 0)
        def _():
            buf[0:HALO, :] = jnp.zeros((HALO, D), F32)

        u0 = a_ref[...].astype(F32) * _sigmoid(b_ref[...].astype(F32))
        u0_ref[...] = u0
        buf[HALO:HALO + tm, :] = u0
        _shifted_copies(buf, sh, tm)
        for j in range(D // LANES):
            sl = slice(j * LANES, (j + 1) * LANES)
            for r0 in range(0, tm, CONV_ROWS):
                acc = jnp.broadcast_to(cb_ref[:, sl], (CONV_ROWS, LANES))
                for kk in range(CONV_K):
                    acc = acc + w_ref[kk:kk + 1, sl] * _window(buf, sh, r0 + HALO - (CONV_K - 1) + kk, CONV_ROWS, sl)
                u1_ref[r0:r0 + CONV_ROWS, sl] = acc
        buf[0:HALO, :] = buf[tm:tm + HALO, :]
        xh, _ = _layer_norm_stats(u1_ref[...])
        u2 = xh * g_ref[...] + bb_ref[...]
        u3_ref[...] = (u2 * _sigmoid(u2)).astype(BF)

    ins = [(proj, _rspec(tm, D, 3)), (proj, _rspec(tm, D, 4)), (cw, _full(cw)), (cb, _full(cb)),
           (lng, _full(lng)), (lnb, _full(lnb))]
    outs = [_row_out(T, tm, D, F32), _row_out(T, tm, D, F32), _row_out(T, tm, D, BF)]
    return _rows_call(body, "conv_fwd", T // tm, ins, outs,
                      [pltpu.VMEM((tm + HALO, D), F32), pltpu.VMEM((SUBLANES - 1, tm + HALO, D), F32)])


def _conv_bwd(du3, u1, u0, proj, cw, lng, lnb):
    T = du3.shape[0]
    tm = TM_ROWS
    n = T // tm
    per = tm // HALO

    def body(du3_ref, u1_ref, u0_ref, halo_ref, a_ref, b_ref, w_ref, g_ref, bb_ref,
             da_ref, db_ref, dg_ref, dbb_ref, dcb_ref, dw_ref, dbuf, ubuf, du0_buf, dsh, ush, dw8):
        i = pl.program_id(0)
        r = n - 1 - i

        @pl.when(i == 0)
        def _():
            dbuf[tm:tm + HALO, :] = jnp.zeros((HALO, D), F32)
            dg_ref[...] = jnp.zeros_like(dg_ref)
            dbb_ref[...] = jnp.zeros_like(dbb_ref)
            dcb_ref[...] = jnp.zeros_like(dcb_ref)
            dw8[...] = jnp.zeros_like(dw8)

        xh, rstd = _layer_norm_stats(u1_ref[...])
        g = g_ref[...]
        u2 = xh * g + bb_ref[...]
        s2 = _sigmoid(u2)
        du2 = du3_ref[...] * (s2 * (1.0 + u2 * (1.0 - s2)))
        dg_ref[...] += _colsum(du2 * xh)
        dbb_ref[...] += _colsum(du2)
        dxh = du2 * g
        du1 = rstd * (dxh - jnp.mean(dxh, axis=-1, keepdims=True) - xh * jnp.mean(dxh * xh, axis=-1, keepdims=True))
        dcb_ref[...] += _colsum(du1)
        dbuf[0:tm, :] = du1
        ubuf[HALO:HALO + tm, :] = u0_ref[...]
        ubuf[0:HALO, :] = jnp.where(r > 0, halo_ref[...], 0.0)
        _shifted_copies(dbuf, dsh, tm)
        _shifted_copies(ubuf, ush, tm)
        for j in range(D // LANES):
            sl = slice(j * LANES, (j + 1) * LANES)
            for r0 in range(0, tm, CONV_ROWS):
                d1 = dbuf[r0:r0 + CONV_ROWS, sl]
                acc = jnp.zeros((CONV_ROWS, LANES), F32)
                for kk in range(CONV_K):
                    acc = acc + w_ref[kk:kk + 1, sl] * _window(dbuf, dsh, r0 + CONV_K - 1 - kk, CONV_ROWS, sl)
                    prod = d1 * _window(ubuf, ush, r0 + HALO - (CONV_K - 1) + kk, CONV_ROWS, sl)
                    dw8[kk * SUBLANES:(kk + 1) * SUBLANES, sl] += prod.reshape(
                        CONV_ROWS // SUBLANES, SUBLANES, LANES).sum(axis=0)
                du0_buf[r0:r0 + CONV_ROWS, sl] = acc
        dbuf[tm:tm + HALO, :] = dbuf[0:HALO, :]
        du0 = du0_buf[...]
        af, bfl = a_ref[...].astype(F32), b_ref[...].astype(F32)
        sb = _sigmoid(bfl)
        da_ref[...] = (du0 * sb).astype(BF)
        db_ref[...] = (du0 * af * sb * (1.0 - sb)).astype(BF)

        @pl.when(i == n - 1)
        def _():
            for kk in range(CONV_KP):
                dw_ref[kk:kk + 1, :] = _colsum(dw8[kk * SUBLANES:(kk + 1) * SUBLANES, :])

    rs = lambda cb: _rspec(tm, D, cb, n)
    halo_spec = pl.BlockSpec((HALO, D), lambda i: (jnp.maximum((n - 1 - i) * per - 1, 0), 0))
    ins = [(du3, rs(0)), (u1, rs(0)), (u0, rs(0)), (u0, halo_spec), (proj, rs(3)), (proj, rs(4)),
           (cw, _full(cw)), (lng, _full(lng)), (lnb, _full(lnb))]
    outs = [_row_out(T, tm, D, BF, n), _row_out(T, tm, D, BF, n),
            _acc_out((1, D)), _acc_out((1, D)), _acc_out((1, D)), _acc_out((CONV_KP, D))]
    shifted = pltpu.VMEM((SUBLANES - 1, tm + HALO, D), F32)
    return _rows_call(body, "conv_bwd", n, ins, outs,
                      [pltpu.VMEM((tm + HALO, D), F32), pltpu.VMEM((tm + HALO, D), F32), pltpu.VMEM((tm, D), F32),
                       shifted, shifted, pltpu.VMEM((CONV_KP * SUBLANES, D), F32)])


def _merge(ba, bb, proj):
    T = ba.shape[0]
    tm = TM_ROWS

    def body(ba_ref, bb_ref, ga_ref, gb_ref, o_ref):
        sa, sb = _sigmoid(ga_ref[...].astype(F32)), _sigmoid(gb_ref[...].astype(F32))
        o_ref[...] = (sa * ba_ref[...] + sb * bb_ref[...]).astype(BF)

    ins = [(ba, _rspec(tm, D)), (bb, _rspec(tm, D)), (proj, _rspec(tm, D, 5)), (proj, _rspec(tm, D, 6))]
    return _rows_call(body, "merge", T // tm, ins, [_row_out(T, tm, D, BF)])[0]


def _post_out(x, mo, mod, n2g):
    T = x.shape[0]
    tm = TM_ROWS

    def body(x_ref, mo_ref, mod_ref, g_ref, x1_ref, h2_ref):
        g1 = mod_ref[:, 2 * D:3 * D]
        sh2, sc2 = mod_ref[:, 3 * D:4 * D], mod_ref[:, 4 * D:5 * D]
        x1 = x_ref[...] + g1 * mo_ref[...]
        x1_ref[...] = x1
        r = lax.rsqrt(jnp.mean(x1 * x1, axis=-1, keepdims=True) + EPS)
        h2_ref[...] = ((x1 * r) * g_ref[...] * (1.0 + sc2) + sh2).astype(BF)

    ins = [(x, _rspec(tm, D)), (mo, _rspec(tm, D)), (mod, _full(mod)), (n2g, _full(n2g))]
    return _rows_call(body, "post_out", T // tm, ins, [_row_out(T, tm, D, F32), _row_out(T, tm, D, BF)])


def _loss_head(x1, m2, tgt, mod):
    T = x1.shape[0]
    tm = TM_ROWS

    def body(x1_ref, m2_ref, t_ref, mod_ref, dy_ref, dm2_ref, dg2_ref, sq_ref):
        i = pl.program_id(0)

        @pl.when(i == 0)
        def _():
            dg2_ref[...] = jnp.zeros_like(dg2_ref)
            sq_ref[...] = jnp.zeros_like(sq_ref)

        g2 = mod_ref[:, 5 * D:6 * D]
        m2 = m2_ref[...]
        err = x1_ref[...] + g2 * m2 - t_ref[...]
        dy = err * (1.0 / D)
        dy_ref[...] = dy
        dm2_ref[...] = (g2 * dy).astype(BF)
        dg2_ref[...] += _colsum(dy * m2)
        sq_ref[...] += _colsum(err * err)

    ins = [(x1, _rspec(tm, D)), (m2, _rspec(tm, D)), (tgt, _rspec(tm, D)), (mod, _full(mod))]
    outs = [_row_out(T, tm, D, F32), _row_out(T, tm, D, BF), _acc_out((1, D)), _acc_out((1, D))]
    return _rows_call(body, "loss_head", T // tm, ins, outs)


def _norm2_bwd(dh2, x1, dy, mo, mod, n2g):
    T = x1.shape[0]
    tm = TM_ROWS

    def body(dh_ref, x1_ref, dy_ref, mo_ref, mod_ref, g_ref, dx1_ref, dmo_ref, dsh_ref, dsc_ref, dg_ref, dg1_ref):
        i = pl.program_id(0)

        @pl.when(i == 0)
        def _():
            for r in (dsh_ref, dsc_ref, dg_ref, dg1_ref):
                r[...] = jnp.zeros_like(r)

        g1, sc2 = mod_ref[:, 2 * D:3 * D], mod_ref[:, 4 * D:5 * D]
        g = g_ref[...]
        x1 = x1_ref[...]
        dh = dh_ref[...]
        r = lax.rsqrt(jnp.mean(x1 * x1, axis=-1, keepdims=True) + EPS)
        xn = x1 * r
        dsh_ref[...] += _colsum(dh)
        dsc_ref[...] += _colsum(dh * xn * g)
        dg_ref[...] += _colsum(dh * xn * (1.0 + sc2))
        dxn = dh * g * (1.0 + sc2)
        dx1 = dy_ref[...] + r * (dxn - xn * jnp.mean(dxn * xn, axis=-1, keepdims=True))
        dx1_ref[...] = dx1
        dg1_ref[...] += _colsum(dx1 * mo_ref[...])
        dmo_ref[...] = (g1 * dx1).astype(BF)

    ins = [(dh2, _rspec(tm, D)), (x1, _rspec(tm, D)), (dy, _rspec(tm, D)), (mo, _rspec(tm, D)),
           (mod, _full(mod)), (n2g, _full(n2g))]
    outs = [_row_out(T, tm, D, F32), _row_out(T, tm, D, BF)] + [_acc_out((1, D)) for _ in range(4)]
    return _rows_call(body, "norm2_bwd", T // tm, ins, outs)


def _gate_bwd(dmerged, ba, bb, proj):
    T = ba.shape[0]
    tm = TM_ROWS

    def body(dm_ref, ba_ref, bb_ref, ga_ref, gb_ref, dba_ref, dbb_ref, dga_ref, dgb_ref):
        dm = dm_ref[...]
        sa, sb = _sigmoid(ga_ref[...].astype(F32)), _sigmoid(gb_ref[...].astype(F32))
        dba_ref[...] = (dm * sa).astype(BF)
        dbb_ref[...] = (dm * sb).astype(BF)
        dga_ref[...] = (dm * ba_ref[...] * sa * (1.0 - sa)).astype(BF)
        dgb_ref[...] = (dm * bb_ref[...] * sb * (1.0 - sb)).astype(BF)

    ins = [(dmerged, _rspec(tm, D)), (ba, _rspec(tm, D)), (bb, _rspec(tm, D)),
           (proj, _rspec(tm, D, 5)), (proj, _rspec(tm, D, 6))]
    return _rows_call(body, "gate_bwd", T // tm, ins, [_row_out(T, tm, D, BF) for _ in range(4)])


def _qkv_bwd(dq, dk, dv, proj, f, dfc, dfq, qg2, kg2, bf_pad):
    T = proj.shape[0]
    tm = TM_ROWS
    n = T // tm
    seg = _seg_mat()
    tri = _tri_mat(tm, False)

    def body(dq_ref, dk_ref, dv_ref, q_ref, k_ref, f_ref, dfc_ref, dfq_ref, qg_ref, kg_ref, bf_ref, seg_ref, tri_ref,
             dqo_ref, dko_ref, dvo_ref, dfo_ref, dqg_ref, dkg_ref, dbf_ref, carry_ref):
        i = pl.program_id(0)

        @pl.when(i == 0)
        def _():
            carry_ref[...] = jnp.zeros_like(carry_ref)
            dqg_ref[...] = jnp.zeros_like(dqg_ref)
            dkg_ref[...] = jnp.zeros_like(dkg_ref)
            dbf_ref[...] = jnp.zeros_like(dbf_ref)

        segm = seg_ref[...]
        dqg = jnp.zeros((1, LANES), F32)
        dkg = jnp.zeros((1, LANES), F32)
        for j in range(D // LANES):
            sl = slice(j * LANES, (j + 1) * LANES)
            for (raw_ref, d_ref, gn_ref, o_ref, which) in ((q_ref, dq_ref, qg_ref, dqo_ref, 0), (k_ref, dk_ref, kg_ref, dko_ref, 1)):
                xc = raw_ref[:, sl].astype(F32)
                rr = lax.rsqrt(_dot_exact(xc * xc, segm) + EPS)
                xn = xc * rr
                dc = d_ref[:, sl]
                if which == 0:
                    dqg = dqg + _colsum(dc * xn)
                else:
                    dkg = dkg + _colsum(dc * xn)
                dxn = dc * gn_ref[...]
                o_ref[:, sl] = (rr * (dxn - xn * _dot_exact(dxn * xn, segm))).astype(BF)
        dqg_ref[...] += dqg
        dkg_ref[...] += dkg
        dvo_ref[...] = dv_ref[...].astype(BF)
        z = f_ref[...] + bf_ref[...]
        sneg_t = _sigmoid(-z).T[0:N_HEADS, :]
        dfc = dfc_ref[...] + dfq_ref[...]
        carry = carry_ref[:, 0:1]
        dlf = _dot_exact(dfc, tri_ref[...]) + carry
        carry_ref[...] = jnp.broadcast_to(carry + jnp.sum(dfc, axis=1, keepdims=True), carry_ref.shape)
        dzt = dlf * sneg_t
        dz = jnp.concatenate([dzt, jnp.zeros((LANES - N_HEADS, tm), F32)], axis=0).T
        dbf_ref[...] += _colsum(dz)
        dfo_ref[...] = dz.astype(BF)

    rs = lambda w, cb=0: _rspec(tm, w, cb, n)
    ins = [(dq, rs(D)), (dk, rs(D)), (dv, rs(D)), (proj, rs(D, 0)), (proj, rs(D, 1)), (f, rs(LANES)),
           (dfc, pl.BlockSpec((N_HEADS, tm), lambda i: (0, n - 1 - i))),
           (dfq, pl.BlockSpec((N_HEADS, tm), lambda i: (0, n - 1 - i))),
           (qg2, _full(qg2)), (kg2, _full(kg2)), (bf_pad, _full(bf_pad)), (seg, _full(seg)), (tri, _full(tri))]
    outs = [_row_out(T, tm, D, BF, n), _row_out(T, tm, D, BF, n), _row_out(T, tm, D, BF, n), _row_out(T, tm, LANES, BF, n),
            _acc_out((1, LANES)), _acc_out((1, LANES)), _acc_out((1, LANES))]
    return _rows_call(body, "qkv_bwd", n, ins, outs, [pltpu.VMEM((N_HEADS, LANES), F32)])


def _norm1_bwd(dh, dhf, x, dx1, mod, n1g):
    T = x.shape[0]
    tm = TM_ROWS

    def body(dh_ref, dhf_ref, x_ref, dx1_ref, mod_ref, g_ref, dx_ref, dsh_ref, dsc_ref, dg_ref):
        i = pl.program_id(0)

        @pl.when(i == 0)
        def _():
            for r in (dsh_ref, dsc_ref, dg_ref):
                r[...] = jnp.zeros_like(r)

        sc1 = mod_ref[:, D:2 * D]
        g = g_ref[...]
        xv = x_ref[...]
        dh = dh_ref[...] + dhf_ref[...]
        r = lax.rsqrt(jnp.mean(xv * xv, axis=-1, keepdims=True) + EPS)
        xn = xv * r
        dsh_ref[...] += _colsum(dh)
        dsc_ref[...] += _colsum(dh * xn * g)
        dg_ref[...] += _colsum(dh * xn * (1.0 + sc1))
        dxn = dh * g * (1.0 + sc1)
        dx_ref[...] = dx1_ref[...] + r * (dxn - xn * jnp.mean(dxn * xn, axis=-1, keepdims=True))

    ins = [(dh, _rspec(tm, D)), (dhf, _rspec(tm, D)), (x, _rspec(tm, D)), (dx1, _rspec(tm, D)),
           (mod, _full(mod)), (n1g, _full(n1g))]
    outs = [_row_out(T, tm, D, F32)] + [_acc_out((1, D)) for _ in range(3)]
    return _rows_call(body, "norm1_bwd", T // tm, ins, outs)


def _adamw_math(w, g, m, v):
    m = ADAM_B1 * m + (1.0 - ADAM_B1) * g
    v = ADAM_B2 * v + (1.0 - ADAM_B2) * (g * g)
    m_hat = m / (1.0 - ADAM_B1 ** ADAM_STEP)
    v_hat = v / (1.0 - ADAM_B2 ** ADAM_STEP)
    delta = -ADAM_LR * (m_hat / (jnp.sqrt(v_hat) + ADAM_EPS) + ADAM_WD * w)
    return delta, m, v


def _adamw(parts, w, m, v, name):
    n, R, C = parts.shape
    tr = R if R <= 256 else 256
    assert R % tr == 0

    def body(p_ref, w_ref, m_ref, v_ref, g_ref, d_ref, mo_ref, vo_ref):
        g = p_ref[0].astype(F32)
        for s in range(1, n):
            g = g + p_ref[s].astype(F32)
        g_ref[...] = g
        d_ref[...], mo_ref[...], vo_ref[...] = _adamw_math(w_ref[...], g, m_ref[...], v_ref[...])

    spec = pl.BlockSpec((None, tr, C), lambda i: (0, i, 0))
    return pl.pallas_call(
        body, name=name, grid=(R // tr,),
        in_specs=[pl.BlockSpec((n, tr, C), lambda i: (0, i, 0)), spec, spec, spec],
        out_specs=[spec] * 4, out_shape=[_sds((1, R, C), F32)] * 4,
        compiler_params=_cp(("parallel",)),
    )(parts, w, m, v)


def _rcopy(src, dst, ssem, rsem, peer):
    return pltpu.make_async_remote_copy(src_ref=src, dst_ref=dst, send_sem=ssem, recv_sem=rsem,
                                        device_id=peer, device_id_type=MESH)


def _ada_fwd(c, w_ada, b_slice, cw_shard):
    def body(c_ref, w_ref, b_ref, cw_ref, mod_ref, ca_ref, cwf_ref, call, mp, ssem, rsem):
        x, y, cc, me = _my_pos()
        call[pl.ds(me, 1), :] = c_ref[...]
        cwf_ref[me] = cw_ref[...]
        first = []
        for d in range(1, N_DEV):
            peer, _ = _peer(x, y, cc, d)
            first.append(_rcopy(c_ref, call.at[pl.ds(me, 1), :], ssem.at[0, d - 1], rsem.at[0, d - 1], peer))
            first.append(_rcopy(cw_ref, cwf_ref.at[me], ssem.at[1, d - 1], rsem.at[1, d - 1], peer))
        for cp in first:
            cp.start()
        for d in range(1, N_DEV):
            peer, pid = _peer(x, y, cc, d)
            _rcopy(c_ref, call.at[pl.ds(pid, 1), :], ssem.at[0, d - 1], rsem.at[0, d - 1], peer).wait_recv()
            _rcopy(cw_ref, cwf_ref.at[pid], ssem.at[1, d - 1], rsem.at[1, d - 1], peer).wait_recv()
        cv = call[...]
        ca = cv * _sigmoid(cv)
        ca_ref[...] = ca
        mp[...] = _dot_f32(ca, w_ref[...]) + b_ref[...]
        mod_ref[pl.ds(me, 1), :] = mp[pl.ds(me, 1), :]
        second = []
        for d in range(1, N_DEV):
            peer, pid = _peer(x, y, cc, d)
            second.append(_rcopy(mp.at[pl.ds(pid, 1), :], mod_ref.at[pl.ds(me, 1), :], ssem.at[2, d - 1], rsem.at[2, d - 1], peer))
        for cp in second:
            cp.start()
        for d in range(1, N_DEV):
            peer, pid = _peer(x, y, cc, d)
            _rcopy(mp.at[pl.ds(pid, 1), :], mod_ref.at[pl.ds(pid, 1), :], ssem.at[2, d - 1], rsem.at[2, d - 1], peer).wait_recv()
        for cp in first + second:
            cp.wait_send()

    vm = pl.BlockSpec(memory_space=pltpu.VMEM)
    return pl.pallas_call(
        body, name="ada_fwd",
        in_specs=[vm, vm, vm, vm], out_specs=[vm, vm, vm],
        out_shape=[_sds((N_DEV, ADA_SHARD), F32), _sds((N_DEV, D), F32), _sds((N_DEV, CONV_KP, LANES), F32)],
        scratch_shapes=[pltpu.VMEM((N_DEV, D), F32), pltpu.VMEM((N_DEV, ADA_SHARD), F32),
                        pltpu.SemaphoreType.DMA((3, N_DEV - 1)), pltpu.SemaphoreType.DMA((3, N_DEV - 1))],
        compiler_params=pltpu.CompilerParams(vmem_limit_bytes=VMEM_LIMIT),
    )(c, w_ada, b_slice, cw_shard)


def _xchg_parts(arrays):
    n = len(arrays)
    anyspec = pl.BlockSpec(memory_space=pl.ANY)
    shapes = [_sds((N_DEV,) + tuple(a.shape[-2:]), a.dtype) for a in arrays]
    scratch = [pltpu.SemaphoreType.DMA((n,)), pltpu.SemaphoreType.DMA((n, N_DEV - 1)),
               pltpu.SemaphoreType.DMA((n, N_DEV - 1))]
    return [anyspec] * n, shapes, scratch


def _xchg(ins, outs, sems, gather, wait):
    lsem, ssem, rsem = sems
    x, y, cc, me = _my_pos()
    for a in range(len(ins)):
        local = pltpu.make_async_copy(ins[a] if gather else ins[a].at[me], outs[a].at[me], lsem.at[a])
        if not wait:
            local.start()
        for d in range(1, N_DEV):
            peer, pid = _peer(x, y, cc, d)
            src = ins[a] if gather else ins[a].at[pid]
            if not wait:
                _rcopy(src, outs[a].at[me], ssem.at[a, d - 1], rsem.at[a, d - 1], peer).start()
            else:
                cp = _rcopy(src, outs[a].at[pid], ssem.at[a, d - 1], rsem.at[a, d - 1], peer)
                cp.wait_recv()
                cp.wait_send()
        if wait:
            local.wait()


def _gather_two_level(shard, name):
    def body(x_ref, out_ref, ssem, rsem, lsem):
        x, y, c, me = _my_pos()
        sibling = (x, y, 1 - c)
        chips = [(1 - x, y), (x, 1 - y), (1 - x, 1 - y)]
        slot = lambda px, py, pc: out_ref.at[4 * px + 2 * py + pc]

        def copy(kk, block, to, src=None):
            return _rcopy(slot(*block) if src is None else src, slot(*block), ssem.at[kk], rsem.at[kk], to)

        mine = pltpu.make_async_copy(x_ref, slot(x, y, c), lsem)
        mine.start()
        first = [copy(0, (x, y, c), sibling, src=x_ref)]
        first += [copy(1 + j, (x, y, c), (*chip, c), src=x_ref) for j, chip in enumerate(chips)]
        for cp in first:
            cp.start()
        passed = [copy(4 + j, (*chip, c), sibling) for j, chip in enumerate(chips)]
        for j, chip in enumerate(chips):
            copy(1 + j, (*chip, c), (x, y, c)).wait_recv()
            passed[j].start()
        copy(0, sibling, (x, y, c)).wait_recv()
        for j, chip in enumerate(chips):
            copy(4 + j, (*chip, 1 - c), (x, y, c)).wait_recv()
        for cp in first + passed:
            cp.wait_send()
        mine.wait()

    anyspec = pl.BlockSpec(memory_space=pl.ANY)
    return pl.pallas_call(
        body, name=name, in_specs=[anyspec], out_specs=anyspec,
        out_shape=_sds((N_DEV,) + tuple(shard.shape), shard.dtype),
        scratch_shapes=[pltpu.SemaphoreType.DMA((N_DEV - 1,)), pltpu.SemaphoreType.DMA((N_DEV - 1,)),
                        pltpu.SemaphoreType.DMA(())],
    )(shard)


PACK_ROWS = 16
ROW_MISC = 5
ROW_LOSS = 6
ROW_DMOD = 8


def _small_bwd(pack, dmodb, dcw, cat, wp, mp_, vp, cw_w, cw_m, cw_v):
    def body(pack_ref, dmodb_ref, dcw_ref, cat_ref, wp_ref, mp_ref, vp_ref, cww_ref, cwm_ref, cwv_ref,
             g_ref, d_ref, mo_ref, vo_ref, cg_ref, cd_ref, cm_ref, cv_ref, gwa_ref, loss_ref,
             allp, dmc, cwg, ssem, rsem):
        x, y, cc, me = _my_pos()
        allp[me] = pack_ref[...]
        dmc[pl.ds(me, 1), :] = dmodb_ref[pl.ds(me, 1), :]
        cwg[me] = dcw_ref[me]
        sends = []
        for d in range(1, N_DEV):
            peer, pid = _peer(x, y, cc, d)
            sends.append(_rcopy(pack_ref, allp.at[me], ssem.at[0, d - 1], rsem.at[0, d - 1], peer))
            sends.append(_rcopy(dmodb_ref.at[pl.ds(pid, 1), :], dmc.at[pl.ds(me, 1), :], ssem.at[1, d - 1], rsem.at[1, d - 1], peer))
            sends.append(_rcopy(dcw_ref.at[pid], cwg.at[me], ssem.at[2, d - 1], rsem.at[2, d - 1], peer))
        for cp in sends:
            cp.start()
        for d in range(1, N_DEV):
            peer, pid = _peer(x, y, cc, d)
            _rcopy(pack_ref, allp.at[pid], ssem.at[0, d - 1], rsem.at[0, d - 1], peer).wait_recv()
            _rcopy(dmodb_ref.at[pl.ds(pid, 1), :], dmc.at[pl.ds(pid, 1), :], ssem.at[1, d - 1], rsem.at[1, d - 1], peer).wait_recv()
            _rcopy(dcw_ref.at[pid], cwg.at[pid], ssem.at[2, d - 1], rsem.at[2, d - 1], peer).wait_recv()
        for cp in sends:
            cp.wait_send()

        tot = allp[0]
        cg = cwg[0]
        for s in range(1, N_DEV):
            tot = tot + allp[s]
            cg = cg + cwg[s]
        lane = lax.broadcasted_iota(jnp.int32, (PACK_ROWS, D), 1)
        row = lax.broadcasted_iota(jnp.int32, (PACK_ROWS, D), 0)
        gains = (row == ROW_MISC) & (lane >= LANES) & (lane < 3 * LANES)
        folded = tot + pltpu.roll(tot, D - HEAD_DIM, axis=1)
        keep = (lane % LANES) < HEAD_DIM
        g = jnp.where(gains, jnp.where(keep, folded, 0.0), tot)
        loss_ref[...] = jnp.broadcast_to(
            (0.5 / D) * jnp.sum(jnp.where(row == ROW_LOSS, tot, 0.0), keepdims=True).reshape(1, 1), loss_ref.shape)
        g = jnp.where(row == ROW_LOSS, 0.0, g)
        g_ref[...] = g
        d_ref[...], mo_ref[...], vo_ref[...] = _adamw_math(wp_ref[...], g, mp_ref[...], vp_ref[...])
        cg_ref[...] = cg
        cd_ref[...], cm_ref[...], cv_ref[...] = _adamw_math(cww_ref[...], cg, cwm_ref[...], cwv_ref[...])
        dm_pad = jnp.concatenate([dmc[...], jnp.zeros((LANES - N_DEV, ADA_SHARD), F32)], axis=0)
        gwa_ref[...] = _dot_f32(cat_ref[...], dm_pad)

    vm = pl.BlockSpec(memory_space=pltpu.VMEM)
    p16 = _sds((PACK_ROWS, D), F32)
    c32 = _sds((CONV_KP, LANES), F32)
    return pl.pallas_call(
        body, name="small_bwd",
        in_specs=[vm] * 10, out_specs=[vm] * 10,
        out_shape=[p16, p16, p16, p16, c32, c32, c32, c32, _sds((D, ADA_SHARD), F32), _sds((8, LANES), F32)],
        scratch_shapes=[pltpu.VMEM((N_DEV, PACK_ROWS, D), F32), pltpu.VMEM((N_DEV, ADA_SHARD), F32),
                        pltpu.VMEM((N_DEV, CONV_KP, LANES), F32),
                        pltpu.SemaphoreType.DMA((3, N_DEV - 1)), pltpu.SemaphoreType.DMA((3, N_DEV - 1))],
        compiler_params=pltpu.CompilerParams(vmem_limit_bytes=VMEM_LIMIT),
    )(pack, dmodb, dcw, cat, wp, mp_, vp, cw_w, cw_m, cw_v)


def _lanes(vec, start, total=D):
    n = vec.shape[1]
    return jnp.pad(vec, ((0, 0), (start, total - start - n)))


def _pack_small(rows5, misc, loss_row, six):
    z = jnp.zeros((1, D), F32)
    return jnp.concatenate(rows5 + [misc, loss_row, z] + [six.reshape(N_ADA, D), z, z], axis=0)


def kernel(x, c, w_ada, b_ada, norm1_g, w_in, b_forget, q_norm_g, k_norm_g, w_attn_proj, conv_w, conv_b, conv_ln_g, conv_ln_b, w_conv_proj, w_out, norm2_g, w_mlp1, w_mlp2, loss_target, m_w_ada, m_b_ada, m_norm1_g, m_w_in, m_b_forget, m_q_norm_g, m_k_norm_g, m_w_attn_proj, m_conv_w, m_conv_b, m_conv_ln_g, m_conv_ln_b, m_w_conv_proj, m_w_out, m_norm2_g, m_w_mlp1, m_w_mlp2, v_w_ada, v_b_ada, v_norm1_g, v_w_in, v_b_forget, v_q_norm_g, v_k_norm_g, v_w_attn_proj, v_conv_w, v_conv_b, v_conv_ln_g, v_conv_ln_b, v_w_conv_proj, v_w_out, v_norm2_g, v_w_mlp1, v_w_mlp2):
    me = 4 * lax.axis_index("x") + 2 * lax.axis_index("y") + lax.axis_index("c")
    xs, tgt = x[0], loss_target[0]
    T = xs.shape[0]
    sq = lambda a: a[0]
    pad_taps = lambda a: jnp.pad(a[0], ((0, CONV_KP - CONV_K), (0, 0)))

    b_slice = lax.dynamic_slice(b_ada, (0, me * ADA_SHARD), (1, ADA_SHARD))
    modb, ca_all, cwf = _ada_fwd(c, sq(w_ada), b_slice, pad_taps(conv_w))
    mod = modb.reshape(1, N_ADA * D)
    cw = jnp.transpose(cwf, (1, 0, 2)).reshape(CONV_KP, D)

    g_in = _gather_two_level(sq(w_in).astype(BF), "w_in_gather")
    d_in = g_in.shape[2] * N_DEV
    w_in_f = jnp.transpose(g_in, (1, 0, 2)).reshape(D, d_in)
    w_main = jnp.concatenate([w_in_f[:, :3 * D], w_in_f[:, 3 * D + N_HEADS:]], axis=1)
    w_f = jnp.pad(w_in_f[:, 3 * D:3 * D + N_HEADS], ((0, 0), (0, LANES - N_HEADS)))
    shards = [sq(w_attn_proj).astype(BF), sq(w_conv_proj).astype(BF), sq(w_out).astype(BF),
              sq(w_mlp1).astype(BF), sq(w_mlp2).astype(BF)]

    qg2 = jnp.tile(q_norm_g, (1, 2))
    kg2 = jnp.tile(k_norm_g, (1, 2))
    bf_pad = _lanes(b_forget, 0, LANES)

    h = _pre_in(xs, mod, norm1_g)
    proj = _matmul(h, w_main, "nn", BF, "mm_proj")
    f = _matmul(h, w_f, "nn", F32, "mm_f")
    q, k, v, fc = _qkv_post(proj, f, qg2, kg2, bf_pad)
    fc3 = fc.reshape(N_HEADS // 2, 2, T)
    o, lse, g_ap, g_cp, g_out, g_1, g_2 = _flash_fwd(q, k, v, fc3, shards)
    w_ap, w_cp, w_o = g_ap.reshape(D, D), g_cp.reshape(D, D), g_out.reshape(D, D)
    w_2 = g_2.reshape(D_FF, D)
    ba = _matmul(o, w_ap, "nn", F32, "mm_ba")
    u0, u1, u3 = _conv_fwd(proj, cw, conv_b, conv_ln_g, conv_ln_b)
    bb = _matmul(u3, w_cp, "nn", F32, "mm_bb")
    merged = _merge(ba, bb, proj)
    mo = _matmul(merged, w_o, "nn", F32, "mm_out")
    x1, h2 = _post_out(xs, mo, mod, norm2_g)
    a, rl = _matmul(h2, g_1, "nn", BF, "mm_mlp1", relu2=True, b_slots=True)
    m2 = _matmul(rl, w_2, "nn", F32, "mm_mlp2")
    dy, dm2, dg2, sqcols = _loss_head(x1, m2, tgt, mod)

    da = _matmul(dm2, w_2, "nt", BF, "mm_drl", relu_of=a)
    dw_2 = _matmul(rl, dm2, "tn", BF, "mm_dw2")
    dh2 = _matmul(da, g_1, "nt", F32, "mm_dh2", b_slots=True)
    dw_1 = _matmul(h2, da, "tn", BF, "mm_dw1", out_slots=True)
    dx1, dmo, dsh2, dsc2, dn2g, dg1 = _norm2_bwd(dh2, x1, dy, mo, mod, norm2_g)
    dmerged = _matmul(dmo, w_o, "nt", F32, "mm_dmerged")
    dw_o = _matmul(merged, dmo, "tn", BF, "mm_dwout")
    dba, dbb, dga, dgb = _gate_bwd(dmerged, ba, bb, proj)
    du3 = _matmul(dbb, w_cp, "nt", F32, "mm_du3")
    dw_cp = _matmul(u3, dbb, "tn", BF, "mm_dwcp")
    do = _matmul(dba, w_ap, "nt", F32, "mm_do")
    dw_ap = _matmul(o, dba, "tn", BF, "mm_dwap")
    dglu_a, dglu_b, dlng, dlnb, dcb, dcw_full = _conv_bwd(du3, u1, u0, proj, cw, conv_ln_g, conv_ln_b)
    delta = _attn_delta(do, o)
    parts = [dw_ap.reshape(N_DEV, D // N_DEV, D), dw_cp.reshape(N_DEV, D // N_DEV, D), dw_o.reshape(N_DEV, D // N_DEV, D),
             dw_1, dw_2.reshape(N_DEV, D_FF // N_DEV, D)]
    dq, rs_a, rs_b, dk, dv, dfc3, r_ap, r_cp, r_out, r_1, r_2 = _flash_bwd(q, k, v, do, lse, delta, fc3, parts)
    dfq = jnp.stack([rs_a, rs_b], axis=1).reshape(N_HEADS, T)
    dq_raw, dk_raw, dv_b, df, dqg, dkg, dbf = _qkv_bwd(dq, dk, dv, proj, f, dfc3.reshape(N_HEADS, T), dfq, qg2, kg2, bf_pad)
    dproj = jnp.concatenate([dq_raw, dk_raw, dv_b, dglu_a, dglu_b, dga, dgb], axis=1)
    dw_main = _matmul(h, dproj, "tn", BF, "mm_dwmain")
    dw_f = _matmul(h, df, "tn", BF, "mm_dwf")
    dw_in_f = jnp.concatenate([dw_main[:, :3 * D], dw_f[:, :N_HEADS], dw_main[:, 3 * D:]], axis=1)
    part_in = jnp.transpose(dw_in_f.reshape(D, N_DEV, d_in // N_DEV), (1, 0, 2))
    dh, r_in = _matmul(dproj, w_main, "nt", F32, "mm_dh", scatter=(part_in,))
    dhf = _matmul(df, w_f, "nt", F32, "mm_dhf")
    grad_x, dsh1, dsc1, dn1g = _norm1_bwd(dh, dhf, xs, dx1, mod, norm1_g)

    dmod = jnp.concatenate([dsh1, dsc1, dg1, dsh2, dsc2, dg2], axis=1)
    misc = jnp.concatenate([dbf, dqg, dkg, jnp.zeros((1, D - 3 * LANES), F32)], axis=1)
    pack = _pack_small([dn1g, dcb, dlng, dlnb, dn2g], misc, sqcols, dmod)
    dcw_blocks = jnp.transpose(dcw_full.reshape(CONV_KP, N_DEV, LANES), (1, 0, 2))

    def small_params(b_a, n1, bfg, qn, kn, cvb, lg, lb, n2):
        misc_p = jnp.concatenate([_lanes(bfg, 0, LANES), _lanes(qn, 0, LANES), _lanes(kn, 0, LANES),
                                  jnp.zeros((1, D - 3 * LANES), F32)], axis=1)
        return _pack_small([n1, cvb, lg, lb, n2], misc_p, jnp.zeros((1, D), F32), b_a)

    wp = small_params(b_ada, norm1_g, b_forget, q_norm_g, k_norm_g, conv_b, conv_ln_g, conv_ln_b, norm2_g)
    mp_ = small_params(m_b_ada, m_norm1_g, m_b_forget, m_q_norm_g, m_k_norm_g, m_conv_b, m_conv_ln_g, m_conv_ln_b, m_norm2_g)
    vp = small_params(v_b_ada, v_norm1_g, v_b_forget, v_q_norm_g, v_k_norm_g, v_conv_b, v_conv_ln_g, v_conv_ln_b, v_norm2_g)
    cat = jnp.pad(jnp.transpose(ca_all), ((0, 0), (0, LANES - N_DEV)))
    small = _small_bwd(pack, dmod.reshape(N_DEV, ADA_SHARD), dcw_blocks, cat,
                       wp, mp_, vp, pad_taps(conv_w), pad_taps(m_conv_w), pad_taps(v_conv_w))
    sp = small[0:4]
    scw = small[4:8]
    gw_ada, loss_t = small[8], small[9]
    loss = loss_t[0, 0]

    def unpack(p):
        misc_r = p[ROW_MISC:ROW_MISC + 1]
        return dict(
            b_ada=p[ROW_DMOD:ROW_DMOD + N_ADA].reshape(1, N_ADA * D), norm1_g=p[0:1], conv_b=p[1:2], conv_ln_g=p[2:3],
            conv_ln_b=p[3:4], norm2_g=p[4:5], b_forget=misc_r[:, 0:N_HEADS],
            q_norm_g=misc_r[:, LANES:LANES + HEAD_DIM], k_norm_g=misc_r[:, 2 * LANES:2 * LANES + HEAD_DIM])

    res = {}
    res["w_ada"] = _adamw(gw_ada[None], w_ada, m_w_ada, v_w_ada, "adamw_w_ada")
    res["w_in"] = _adamw(r_in, w_in, m_w_in, v_w_in, "adamw_w_in")
    res["w_attn_proj"] = _adamw(r_ap, w_attn_proj, m_w_attn_proj, v_w_attn_proj, "adamw_w_ap")
    res["w_conv_proj"] = _adamw(r_cp, w_conv_proj, m_w_conv_proj, v_w_conv_proj, "adamw_w_cp")
    res["w_out"] = _adamw(r_out, w_out, m_w_out, v_w_out, "adamw_w_out")
    res["w_mlp1"] = _adamw(r_1, w_mlp1, m_w_mlp1, v_w_mlp1, "adamw_w_mlp1")
    res["w_mlp2"] = _adamw(r_2, w_mlp2, m_w_mlp2, v_w_mlp2, "adamw_w_mlp2")

    names = ["w_ada", "b_ada", "norm1_g", "w_in", "b_forget", "q_norm_g", "k_norm_g", "w_attn_proj", "conv_w", "conv_b",
             "conv_ln_g", "conv_ln_b", "w_conv_proj", "w_out", "norm2_g", "w_mlp1", "w_mlp2"]
    outs = [loss, grad_x[None]]
    for kind in range(4):
        small_d = unpack(sp[kind])
        for nm in names:
            if nm in res:
                outs.append(res[nm][kind])
            elif nm == "conv_w":
                outs.append(scw[kind][:CONV_K][None])
            else:
                outs.append(small_d[nm])
    return tuple(outs)
```

```python
import functools

import jax
import jax.numpy as jnp
from jax import lax
from jax.experimental import pallas as pl
from jax.experimental.pallas import tpu as pltpu

F32 = jnp.float32
BF = jnp.bfloat16

N_DEV = 8
D = 1024
N_HEADS = 16
HEAD_DIM = 64
LANES = 128
SUBLANES = 8
CONV_K = 31
CONV_KP = 32
HALO = 32
CONV_ROWS = 64
D_FF = 4 * D
N_ADA = 6
ADA_SHARD = N_ADA * D // N_DEV
EPS = 1e-6
QK_SCALE = HEAD_DIM ** -0.5
LOG2E = 1.4426950408889634
LN2 = 0.6931471805599453
NEG = -1e30

ADAM_LR = 0.001
ADAM_B1 = 0.9
ADAM_B2 = 0.999
ADAM_EPS = 1e-08
ADAM_WD = 0.01
ADAM_STEP = 10

VMEM_LIMIT = 56 * 1024 * 1024
TM_ROWS = 256
TQ = 512

MESH = pl.DeviceIdType.MESH


def _cp(sem=None):
    return pltpu.CompilerParams(dimension_semantics=sem, vmem_limit_bytes=VMEM_LIMIT)


def _sds(shape, dtype):
    return jax.ShapeDtypeStruct(tuple(shape), dtype)


def _full(arr):
    nd = arr.ndim
    return pl.BlockSpec(arr.shape, lambda *_: (0,) * nd)


def _fullshape(shape):
    nd = len(shape)
    return pl.BlockSpec(tuple(shape), lambda *_: (0,) * nd)


def _split3(x):
    hi = x.astype(BF)
    r1 = x - hi.astype(F32)
    mid = r1.astype(BF)
    lo = (r1 - mid.astype(F32)).astype(BF)
    return hi, mid, lo


def _dot_exact(x, mat):
    hi, mid, lo = _split3(x)
    d = lambda t: jnp.dot(t, mat, preferred_element_type=F32)
    return d(hi) + d(mid) + d(lo)


def _dot_f32(a, b):
    a1, a2, a3 = _split3(a)
    b1, b2, b3 = _split3(b)
    d = lambda s, t: jnp.dot(s, t, preferred_element_type=F32)
    return (d(a1, b3) + d(a3, b1) + d(a2, b2)) + (d(a1, b2) + d(a2, b1)) + d(a1, b1)


def _sigmoid(x):
    return 1.0 / (1.0 + jnp.exp(-x))


def _colsum(x):
    return jnp.sum(x, axis=0, keepdims=True)


def _my_pos():
    x, y, c = lax.axis_index("x"), lax.axis_index("y"), lax.axis_index("c")
    return x, y, c, 4 * x + 2 * y + c


def _peer(x, y, c, d):
    px = (1 - x) if d & 4 else x
    py = (1 - y) if d & 2 else y
    pc = (1 - c) if d & 1 else c
    return (px, py, pc), 4 * px + 2 * py + pc


def _matmul(a, b, form, out_dtype, name, tm=1024, tn=1024, tk=1024, scatter=(), relu2=False, relu_of=None,
            b_slots=False, out_slots=False, more=(), halves=None):
    if b_slots:
        assert form in ("nn", "nt") and b.shape[0] == N_DEV
        width = b.shape[2]
        if form == "nn":
            (M, K), N, tn = a.shape, N_DEV * width, width
        else:
            (M, K), N, tk = a.shape, b.shape[1], width
    elif form == "nn":
        (M, K), N = a.shape, b.shape[1]
    elif form == "nt":
        (M, K), N = a.shape, b.shape[0]
    else:
        (K, M), N = a.shape, b.shape[1]
    if out_slots:
        tn = N // N_DEV
    tm, tn, tk = min(tm, M), min(tn, N), min(tk, K)
    assert M % tm == 0 and N % tn == 0 and K % tk == 0, (name, M, N, K)
    nk = K // tk
    if form == "tn":
        a_spec = pl.BlockSpec((tk, tm), lambda i, j, k: (k, i))
        dn = (((0,), (0,)), ((), ()))
    else:
        a_spec = pl.BlockSpec((tm, tk), lambda i, j, k: (i, k))
        dn = (((1,), (1 if form == "nt" else 0,)), ((), ()))
    if b_slots and form == "nn":
        b_spec = pl.BlockSpec((None, tk, tn), lambda i, j, k: (j, k, 0))
    elif b_slots:
        b_spec = pl.BlockSpec((None, tn, tk), lambda i, j, k: (k, j, 0))
    elif form == "nt":
        b_spec = pl.BlockSpec((tn, tk), lambda i, j, k: (j, k))
    else:
        b_spec = pl.BlockSpec((tk, tn), lambda i, j, k: (k, j))

    pairs = [(a, b)] + list(more)
    seg = [0]
    for a_s, _ in pairs:
        k_s = K if len(pairs) == 1 else a_s.shape[1]
        assert k_s % tk == 0 and (len(pairs) == 1 or (form == "nt" and not b_slots))
        seg.append(seg[-1] + k_s // tk)
    nk = seg[-1]
    specs_more = []
    for s in range(1, len(pairs)):
        lo_k, n_k = seg[s], seg[s + 1] - seg[s]
        kk = lambda k, lo_k=lo_k, n_k=n_k: jnp.clip(k - lo_k, 0, n_k - 1)
        specs_more += [pl.BlockSpec((tm, tk), lambda i, j, k, kk=kk: (i, kk(k))),
                       pl.BlockSpec((tn, tk), lambda i, j, k, kk=kk: (j, kk(k)))]
    if len(pairs) > 1:
        n0 = seg[1]
        a_spec = pl.BlockSpec((tm, tk), lambda i, j, k: (i, jnp.minimum(k, n0 - 1)))
        b_spec = pl.BlockSpec((tn, tk), lambda i, j, k: (j, jnp.minimum(k, n0 - 1)))

    nx = len(scatter)
    ne = 0 if relu_of is None else 1
    no = 2 if relu2 else 1
    nm = 2 * (len(pairs) - 1)
    grid = (M // tm, N // tn, nk)

    def body(a_ref, b_ref, *rest):
        ab_refs = [(a_ref, b_ref)] + [(rest[2 * s], rest[2 * s + 1]) for s in range(len(pairs) - 1)]
        rest = rest[nm:]
        e_ref = rest[0] if ne else None
        x_in = rest[ne:ne + nx]
        o_refs = rest[ne + nx:ne + nx + no]
        x_out = rest[ne + nx + no:ne + 2 * nx + no]
        scr = rest[ne + 2 * nx + no:]
        k = pl.program_id(2)
        if nx:
            first, last = _first_last(grid)

            @pl.when(first)
            def _():
                _xchg(x_in, x_out, scr[-3:], False, wait=False, halves=halves)

        def finish(val):
            if relu2:
                o_refs[0][...] = val.astype(out_dtype)
                r = jnp.maximum(val, 0.0)
                o_refs[1][...] = (r * r).astype(out_dtype)
            elif ne:
                o_refs[0][...] = (val * (2.0 * jnp.maximum(e_ref[...].astype(F32), 0.0))).astype(out_dtype)
            else:
                o_refs[0][...] = val.astype(out_dtype)

        def accumulate(ar, br):
            part = lax.dot_general(ar[...].astype(BF), br[...].astype(BF), dn, preferred_element_type=F32)
            if nk == 1:
                finish(part)
            else:
                acc = scr[0]

                @pl.when(k == 0)
                def _():
                    acc[...] = part

                @pl.when(k > 0)
                def _():
                    acc[...] += part

        if len(pairs) == 1:
            accumulate(a_ref, b_ref)
        else:
            for s, (ar, br) in enumerate(ab_refs):
                @pl.when((k >= seg[s]) & (k < seg[s + 1]))
                def _(ar=ar, br=br):
                    accumulate(ar, br)

        if nk > 1:
            @pl.when(k == nk - 1)
            def _():
                finish(scr[0][...])

        if nx:
            @pl.when(last)
            def _():
                _xchg(x_in, x_out, scr[-3:], False, wait=True, halves=halves)

    x_specs, x_shapes, x_scratch = _xchg_parts(scatter) if nx else ([], [], [])
    sem = ("arbitrary",) * 3 if nx else ("parallel", "parallel", "arbitrary")
    o_spec = pl.BlockSpec((tm, tn), lambda i, j, k: (i, j))
    o_shape = _sds((M, N), out_dtype)
    if out_slots:
        o_spec = pl.BlockSpec((None, tm, tn), lambda i, j, k: (j, i, 0))
        o_shape = _sds((N_DEV, M, tn), out_dtype)
    res = pl.pallas_call(
        body, name=name, grid=grid,
        in_specs=[a_spec, b_spec] + specs_more + [o_spec] * ne + x_specs,
        out_specs=[o_spec] * no + x_specs,
        out_shape=[o_shape] * no + x_shapes,
        scratch_shapes=([] if nk == 1 else [pltpu.VMEM((tm, tn), F32)]) + x_scratch,
        compiler_params=_cp(sem),
    )(a, b, *[t for p in more for t in p], *([relu_of] if ne else []), *scatter)
    return res if (nx or relu2) else res[0]


def _rows_call(body, name, n_tiles, ins, outs, scratch=(), aliases=None):
    res = pl.pallas_call(
        body, name=name, grid=(n_tiles,),
        in_specs=[s for _, s in ins],
        out_specs=[s for _, s in outs],
        out_shape=[o for o, _ in outs],
        scratch_shapes=list(scratch),
        input_output_aliases=aliases or {},
        compiler_params=_cp(("arbitrary",)),
    )(*[a for a, _ in ins])
    return res


def _rspec(tm, width, cb=0, rev_n=None):
    if rev_n is None:
        return pl.BlockSpec((tm, width), lambda i: (i, cb))
    return pl.BlockSpec((tm, width), lambda i: (rev_n - 1 - i, cb))


def _row_out(T, tm, width, dtype, rev_n=None):
    return (_sds((T, width), dtype), _rspec(tm, width, 0, rev_n))


def _acc_out(shape, dtype=F32):
    return (_sds(shape, dtype), _fullshape(shape))


def _mod_parts(mod):
    return [mod[:, i * D:(i + 1) * D] for i in range(N_ADA)]


def _pre_in(x, mod, n1g):
    T = x.shape[0]
    tm = TM_ROWS

    def body(x_ref, mod_ref, g_ref, h_ref):
        sh1, sc1 = mod_ref[:, 0:D], mod_ref[:, D:2 * D]
        xv = x_ref[...]
        r = lax.rsqrt(jnp.mean(xv * xv, axis=-1, keepdims=True) + EPS)
        h_ref[...] = ((xv * r) * g_ref[...] * (1.0 + sc1) + sh1).astype(BF)

    return _rows_call(body, "pre_in", T // tm,
                      [(x, _rspec(tm, D)), (mod, _full(mod)), (n1g, _full(n1g))],
                      [_row_out(T, tm, D, BF)])[0]


def _seg_mat():
    r = jnp.arange(LANES)[:, None] // HEAD_DIM
    c = jnp.arange(LANES)[None, :] // HEAD_DIM
    return jnp.where(r == c, 1.0 / HEAD_DIM, 0.0).astype(BF)


def _tri_mat(n, upper):
    r = jnp.arange(n)[:, None]
    c = jnp.arange(n)[None, :]
    return jnp.where((r <= c) if upper else (r >= c), 1.0, 0.0).astype(BF)


def _log_sigmoid(z):
    return jnp.minimum(z, 0.0) - jnp.log(1.0 + jnp.exp(-jnp.abs(z)))


def _qkv_post(proj, f, qg2, kg2, bf_pad):
    T = proj.shape[0]
    tm = TM_ROWS
    seg = _seg_mat()
    tri = _tri_mat(tm, True)

    def body(q_ref, k_ref, v_ref, f_ref, qg_ref, kg_ref, bf_ref, seg_ref, tri_ref,
             qo_ref, ko_ref, vo_ref, fc_ref, carry_ref):
        i = pl.program_id(0)

        @pl.when(i == 0)
        def _():
            carry_ref[...] = jnp.zeros_like(carry_ref)

        segm = seg_ref[...]
        for j in range(D // LANES):
            sl = slice(j * LANES, (j + 1) * LANES)
            qc = q_ref[:, sl].astype(F32)
            rq = lax.rsqrt(_dot_exact(qc * qc, segm) + EPS)
            qo_ref[:, sl] = ((qc * rq) * qg_ref[...] * (QK_SCALE * LOG2E)).astype(BF)
            kc = k_ref[:, sl].astype(F32)
            rk = lax.rsqrt(_dot_exact(kc * kc, segm) + EPS)
            ko_ref[:, sl] = ((kc * rk) * kg_ref[...]).astype(BF)
        vo_ref[...] = v_ref[...].astype(BF)
        lf = _log_sigmoid(f_ref[...] + bf_ref[...])
        lft = lf.T[0:N_HEADS, :]
        carry = carry_ref[:, 0:1]
        fc_ref[...] = _dot_exact(lft, tri_ref[...]) + carry
        carry_ref[...] = jnp.broadcast_to(carry + jnp.sum(lft, axis=1, keepdims=True), carry_ref.shape)

    outs = [_row_out(T, tm, D, BF), _row_out(T, tm, D, BF), _row_out(T, tm, D, BF),
            (_sds((N_HEADS, T), F32), pl.BlockSpec((N_HEADS, tm), lambda i: (0, i)))]
    ins = [(proj, _rspec(tm, D, 0)), (proj, _rspec(tm, D, 1)), (proj, _rspec(tm, D, 2)), (f, _rspec(tm, LANES)),
           (qg2, _full(qg2)), (kg2, _full(kg2)), (bf_pad, _full(bf_pad)), (seg, _full(seg)), (tri, _full(tri))]
    return _rows_call(body, "qkv_post", T // tm, ins, outs, [pltpu.VMEM((N_HEADS, LANES), F32)])


def _lane_lo():
    return lax.broadcasted_iota(jnp.int32, (1, LANES), 1) < HEAD_DIM


def _nt(a, b):
    return lax.dot_general(a, b, (((1,), (1,)), ((), ())), preferred_element_type=F32)


def _tn(a, b):
    return lax.dot_general(a, b, (((0,), (0,)), ((), ())), preferred_element_type=F32)


def _head_rep(x, lo):
    rolled = pltpu.roll(x, HEAD_DIM, axis=1)
    return jnp.where(lo, x, rolled), jnp.where(lo, rolled, x)


def _diag_mask(t):
    return lax.broadcasted_iota(jnp.int32, (t, t), 1) <= lax.broadcasted_iota(jnp.int32, (t, t), 0)


def _first_last(grid):
    ids = [pl.program_id(a) for a in range(len(grid))]
    first = functools.reduce(jnp.logical_and, [i == 0 for i in ids])
    last = functools.reduce(jnp.logical_and, [i == g - 1 for i, g in zip(ids, grid)])
    return first, last


def _flash_fwd(q, k, v, fc3, shards):
    T = q.shape[0]
    tq = TQ
    nq = T // tq
    hp_n = N_HEADS // 2
    rep = tq // LANES
    nx = len(shards)
    grid = (hp_n, nq, nq)

    def body(q_ref, k_ref, v_ref, fk_ref, fq_ref, *rest):
        x_in, (o_ref, lse_ref), x_out = rest[:nx], rest[nx:nx + 2], rest[nx + 2:2 * nx + 2]
        acc_ref, m_ref = rest[2 * nx + 2:2 * nx + 4]
        sems = rest[2 * nx + 4:]
        qi, ki = pl.program_id(1), pl.program_id(2)
        first, last = _first_last(grid)

        @pl.when(first)
        def _():
            _xchg(x_in, x_out, sems, True, wait=False)

        @pl.when(ki == 0)
        def _():
            acc_ref[...] = jnp.zeros_like(acc_ref)
            m_ref[...] = jnp.full_like(m_ref, NEG)

        lane = lax.broadcasted_iota(jnp.int32, (1, LANES), 1)
        sum_lane = (HEAD_DIM, 0)

        def step(diag):
            lo = _lane_lo()
            q2, k2, v2 = q_ref[...], k_ref[...], v_ref[...]
            zero = jnp.zeros_like(k2)
            bias = (fq_ref[:, 0:1] - fk_ref[...]) * LOG2E
            for hh in range(2):
                sel = (lambda t: jnp.where(lo, t, zero)) if hh == 0 else (lambda t: jnp.where(lo, zero, t))
                ones = jnp.where(lane == sum_lane[hh], 1.0, 0.0).astype(BF)
                v_aug = jnp.where(lo, v2, ones) if hh == 0 else jnp.where(lo, ones, v2)
                s = _nt(sel(q2), k2) + bias[hh:hh + 1, :]
                if diag:
                    s = jnp.where(_diag_mask(tq), s, NEG)
                m_old = m_ref[hh]
                m_new = jnp.maximum(m_old, jnp.max(s, axis=-1, keepdims=True))
                alpha = jnp.exp2(m_old - m_new)
                p = jnp.exp2(s - jnp.tile(m_new, (1, rep)))
                m_ref[hh] = m_new
                acc_ref[hh] = acc_ref[hh] * alpha + jnp.dot(p.astype(BF), v_aug, preferred_element_type=F32)

        @pl.when(ki < qi)
        def _():
            step(False)

        @pl.when(ki == qi)
        def _():
            step(True)
            lo = _lane_lo()
            acc_a, acc_b = acc_ref[0], acc_ref[1]
            la = jnp.broadcast_to(acc_a[:, sum_lane[0]:sum_lane[0] + 1], (tq, LANES))
            lb = jnp.broadcast_to(acc_b[:, sum_lane[1]:sum_lane[1] + 1], (tq, LANES))
            o_ref[...] = jnp.where(lo, acc_a / la, acc_b / lb)
            lse_ref[...] = jnp.where(lo, m_ref[0] + jnp.log(la) * LOG2E, m_ref[1] + jnp.log(lb) * LOG2E)

        @pl.when(last)
        def _():
            _xchg(x_in, x_out, sems, True, wait=True)

    qspec = pl.BlockSpec((tq, LANES), lambda h, i, j: (i, h))
    kspec = pl.BlockSpec((tq, LANES), lambda h, i, j: (jnp.minimum(i, j), h))
    fkspec = pl.BlockSpec((None, 2, tq), lambda h, i, j: (h, 0, jnp.minimum(i, j)))
    fqspec = pl.BlockSpec((None, 2, tq), lambda h, i, j: (h, 0, i))
    x_specs, x_shapes, x_scratch = _xchg_parts(shards)
    return pl.pallas_call(
        body, name="attn_fwd", grid=grid,
        in_specs=[qspec, kspec, kspec, fkspec, fqspec] + x_specs,
        out_specs=[qspec, qspec] + x_specs,
        out_shape=[_sds((T, D), F32), _sds((T, D), F32)] + x_shapes,
        scratch_shapes=[pltpu.VMEM((2, tq, LANES), F32), pltpu.VMEM((2, tq, LANES), F32)] + x_scratch,
        compiler_params=_cp(("arbitrary", "arbitrary", "arbitrary")),
    )(q, k, v, fc3, fc3, *shards)


def _attn_delta(do, o):
    T = o.shape[0]
    tm = TM_ROWS
    ones = (_seg_mat().astype(F32) * HEAD_DIM).astype(BF)

    def body(do_ref, o_ref, seg_ref, dl_ref):
        segm = seg_ref[...]
        for j in range(D // LANES):
            sl = slice(j * LANES, (j + 1) * LANES)
            dl_ref[:, sl] = _dot_exact(do_ref[:, sl].astype(BF).astype(F32) * o_ref[:, sl], segm)

    ins = [(do, _rspec(tm, D)), (o, _rspec(tm, D)), (ones, _full(ones))]
    return _rows_call(body, "attn_delta", T // tm, ins, [_row_out(T, tm, D, F32)])[0]


def _flash_bwd(q, k, v, do, lse, delta, fc3, parts, halves):
    T = q.shape[0]
    tq = TQ
    nq = T // tq
    hp_n = N_HEADS // 2
    rep = tq // LANES
    nx = len(parts)
    grid = (hp_n, nq, nq)

    def body(q_ref, k_ref, v_ref, do_ref, lse_ref, dl_ref, fk_ref, fq_ref, *rest):
        x_in, x_out = rest[:nx], rest[nx + 6:2 * nx + 6]
        dq_ref, ra_ref, rb_ref, dk_ref, dv_ref, dfc_ref = rest[nx:nx + 6]
        dk_acc, dv_acc, df_acc = rest[2 * nx + 6:2 * nx + 9]
        sems = rest[2 * nx + 9:]
        ki, qi = pl.program_id(1), pl.program_id(2)
        first, last = _first_last(grid)
        qrows = pl.ds(pl.multiple_of(qi * tq, tq), tq)

        @pl.when(first)
        def _():
            _xchg(x_in, x_out, sems, False, wait=False, halves=halves)

        @pl.when((ki == 0) & (qi == 0))
        def _():
            dq_ref[...] = jnp.zeros_like(dq_ref)
            ra_ref[...] = jnp.zeros_like(ra_ref)
            rb_ref[...] = jnp.zeros_like(rb_ref)

        @pl.when(qi == 0)
        def _():
            dk_acc[...] = jnp.zeros_like(dk_acc)
            dv_acc[...] = jnp.zeros_like(dv_acc)
            df_acc[...] = jnp.zeros_like(df_acc)

        def step(diag):
            lo = _lane_lo()
            q2, k2, v2 = q_ref[...], k_ref[...], v_ref[...]
            do2 = do_ref[...].astype(BF)
            zero = jnp.zeros_like(q2)
            bias = (fq_ref[:, 0:1] - fk_ref[...]) * LOG2E
            lses = _head_rep(lse_ref[...], lo)
            dls = _head_rep(dl_ref[...], lo)
            dk_t = None
            dv_t = None
            dq_t = None
            for hh in range(2):
                sel = (lambda t: jnp.where(lo, t, zero)) if hh == 0 else (lambda t: jnp.where(lo, zero, t))
                s = _nt(sel(q2), k2) + bias[hh:hh + 1, :]
                if diag:
                    s = jnp.where(_diag_mask(tq), s, NEG)
                p = jnp.exp2(s - jnp.tile(lses[hh], (1, rep)))
                dp = _nt(sel(do2), v2)
                ds = p * (dp - jnp.tile(dls[hh], (1, rep)))
                ds_b = ds.astype(BF)
                dvp = _tn(p.astype(BF), sel(do2))
                dkp = _tn(ds_b, sel(q2))
                dqp = jnp.dot(ds_b, sel(k2), preferred_element_type=F32)
                dv_t = dvp if dv_t is None else dv_t + dvp
                dk_t = dkp if dk_t is None else dk_t + dkp
                dq_t = dqp if dq_t is None else dq_t + dqp
                df_acc[hh:hh + 1, :] -= _colsum(ds)
                r_ref = ra_ref if hh == 0 else rb_ref
                r_ref[qrows, :] += jnp.sum(ds, axis=-1, keepdims=True)
            dk_acc[...] += dk_t
            dv_acc[...] += dv_t
            dq_ref[qrows, :] += dq_t * QK_SCALE

        @pl.when(qi > ki)
        def _():
            step(False)

        @pl.when(qi == ki)
        def _():
            step(True)

        @pl.when(qi == nq - 1)
        def _():
            dk_ref[...] = dk_acc[...] * LN2
            dv_ref[...] = dv_acc[...]
            dfc_ref[...] = df_acc[...]

        @pl.when(last)
        def _():
            _xchg(x_in, x_out, sems, False, wait=True, halves=halves)

    kspec = pl.BlockSpec((tq, LANES), lambda h, j, i: (j, h))
    qspec = pl.BlockSpec((tq, LANES), lambda h, j, i: (jnp.maximum(i, j), h))
    fkspec = pl.BlockSpec((None, 2, tq), lambda h, j, i: (h, 0, j))
    fqspec = pl.BlockSpec((None, 2, tq), lambda h, j, i: (h, 0, jnp.maximum(i, j)))
    x_specs, x_shapes, x_scratch = _xchg_parts(parts)
    dqspec = pl.BlockSpec((T, LANES), lambda h, j, i: (0, h))
    rspec = pl.BlockSpec((None, T, 1), lambda h, j, i: (h, 0, 0))
    return pl.pallas_call(
        body, name="attn_bwd", grid=grid,
        in_specs=[qspec, kspec, kspec, qspec, qspec, qspec, fkspec, fqspec] + x_specs,
        out_specs=[dqspec, rspec, rspec, kspec, kspec, fkspec] + x_specs,
        out_shape=[_sds((T, D), F32), _sds((hp_n, T, 1), F32), _sds((hp_n, T, 1), F32),
                   _sds((T, D), F32), _sds((T, D), F32), _sds((hp_n, 2, T), F32)] + x_shapes,
        scratch_shapes=[pltpu.VMEM((tq, LANES), F32), pltpu.VMEM((tq, LANES), F32), pltpu.VMEM((2, tq), F32)] + x_scratch,
        compiler_params=_cp(("arbitrary", "arbitrary", "arbitrary")),
    )(q, k, v, do, lse, delta, fc3, fc3, *parts)


def _layer_norm_stats(u1):
    mu = jnp.mean(u1, axis=-1, keepdims=True)
    xc = u1 - mu
    rstd = lax.rsqrt(jnp.mean(xc * xc, axis=-1, keepdims=True) + EPS)
    return xc * rstd, rstd


def _shifted_copies(buf, sh, tm):
    rows = tm + HALO - SUBLANES
    for b in range(1, SUBLANES):
        sh[b - 1, 0:rows, :] = buf[b:b + rows, :]


def _window(buf, sh, off, rows, sl):
    a8, b = off // SUBLANES * SUBLANES, off % SUBLANES
    return buf[a8:a8 + rows, sl] if b == 0 else sh[b - 1, a8:a8 + rows, sl]


def _conv_fwd(proj, cw, cb, lng, lnb):
    T = proj.shape[0]
    tm = TM_ROWS

    def body(a_ref, b_ref, w_ref, cb_ref, g_ref, bb_ref, u0_ref, u1_ref, u3_ref, buf, sh):
        i = pl.program_id(0)

        @pl.when(i == 0)
        def _():
            buf[0:HALO, :] = jnp.zeros((HALO, D), F32)

        u0 = a_ref[...].astype(F32) * _sigmoid(b_ref[...].astype(F32))
        u0_ref[...] = u0
        buf[HALO:HALO + tm, :] = u0
        _shifted_copies(buf, sh, tm)
        for j in range(D // LANES):
            sl = slice(j * LANES, (j + 1) * LANES)
            for r0 in range(0, tm, CONV_ROWS):
                acc = jnp.broadcast_to(cb_ref[:, sl], (CONV_ROWS, LANES))
                for kk in range(CONV_K):
                    acc = acc + w_ref[kk:kk + 1, sl] * _window(buf, sh, r0 + HALO - (CONV_K - 1) + kk, CONV_ROWS, sl)
                u1_ref[r0:r0 + CONV_ROWS, sl] = acc
        buf[0:HALO, :] = buf[tm:tm + HALO, :]
        xh, _ = _layer_norm_stats(u1_ref[...])
        u2 = xh * g_ref[...] + bb_ref[...]
        u3_ref[...] = (u2 * _sigmoid(u2)).astype(BF)

    ins = [(proj, _rspec(tm, D, 0)), (proj, _rspec(tm, D, 1)), (cw, _full(cw)), (cb, _full(cb)),
           (lng, _full(lng)), (lnb, _full(lnb))]
    outs = [_row_out(T, tm, D, F32), _row_out(T, tm, D, F32), _row_out(T, tm, D, BF)]
    return _rows_call(body, "conv_fwd", T // tm, ins, outs,
                      [pltpu.VMEM((tm + HALO, D), F32), pltpu.VMEM((SUBLANES - 1, tm + HALO, D), F32)])


def _conv_bwd(du3, u1, u0, proj, cw, lng, lnb, dgg):
    T = du3.shape[0]
    tm = TM_ROWS
    n = T // tm
    per = tm // HALO

    def body(du3_ref, u1_ref, u0_ref, halo_ref, a_ref, b_ref, w_ref, g_ref, bb_ref, dgg_in_ref,
             dgl_ref, dg_ref, dbb_ref, dcb_ref, dw_ref, dbuf, ubuf, du0_buf, dsh, ush, dw8):
        i = pl.program_id(0)
        r = n - 1 - i

        @pl.when(i == 0)
        def _():
            dbuf[tm:tm + HALO, :] = jnp.zeros((HALO, D), F32)
            dg_ref[...] = jnp.zeros_like(dg_ref)
            dbb_ref[...] = jnp.zeros_like(dbb_ref)
            dcb_ref[...] = jnp.zeros_like(dcb_ref)
            dw8[...] = jnp.zeros_like(dw8)

        xh, rstd = _layer_norm_stats(u1_ref[...])
        g = g_ref[...]
        u2 = xh * g + bb_ref[...]
        s2 = _sigmoid(u2)
        du2 = du3_ref[...] * (s2 * (1.0 + u2 * (1.0 - s2)))
        dg_ref[...] += _colsum(du2 * xh)
        dbb_ref[...] += _colsum(du2)
        dxh = du2 * g
        du1 = rstd * (dxh - jnp.mean(dxh, axis=-1, keepdims=True) - xh * jnp.mean(dxh * xh, axis=-1, keepdims=True))
        dcb_ref[...] += _colsum(du1)
        dbuf[0:tm, :] = du1
        ubuf[HALO:HALO + tm, :] = u0_ref[...]
        ubuf[0:HALO, :] = jnp.where(r > 0, halo_ref[...], 0.0)
        _shifted_copies(dbuf, dsh, tm)
        _shifted_copies(ubuf, ush, tm)
        for j in range(D // LANES):
            sl = slice(j * LANES, (j + 1) * LANES)
            for r0 in range(0, tm, CONV_ROWS):
                d1 = dbuf[r0:r0 + CONV_ROWS, sl]
                acc = jnp.zeros((CONV_ROWS, LANES), F32)
                for kk in range(CONV_K):
                    acc = acc + w_ref[kk:kk + 1, sl] * _window(dbuf, dsh, r0 + CONV_K - 1 - kk, CONV_ROWS, sl)
                    prod = d1 * _window(ubuf, ush, r0 + HALO - (CONV_K - 1) + kk, CONV_ROWS, sl)
                    dw8[kk * SUBLANES:(kk + 1) * SUBLANES, sl] += prod.reshape(
                        CONV_ROWS // SUBLANES, SUBLANES, LANES).sum(axis=0)
                du0_buf[r0:r0 + CONV_ROWS, sl] = acc
        dbuf[tm:tm + HALO, :] = dbuf[0:HALO, :]
        du0 = du0_buf[...]
        af, bfl = a_ref[...].astype(F32), b_ref[...].astype(F32)
        sb = _sigmoid(bfl)
        dgl_ref[:, 0:D] = (du0 * sb).astype(BF)
        dgl_ref[:, D:2 * D] = (du0 * af * sb * (1.0 - sb)).astype(BF)

        @pl.when(i == n - 1)
        def _():
            for kk in range(CONV_KP):
                dw_ref[kk:kk + 1, :] = _colsum(dw8[kk * SUBLANES:(kk + 1) * SUBLANES, :])

    rs = lambda cb: _rspec(tm, D, cb, n)
    halo_spec = pl.BlockSpec((HALO, D), lambda i: (jnp.maximum((n - 1 - i) * per - 1, 0), 0))
    ins = [(du3, rs(0)), (u1, rs(0)), (u0, rs(0)), (u0, halo_spec), (proj, rs(0)), (proj, rs(1)),
           (cw, _full(cw)), (lng, _full(lng)), (lnb, _full(lnb)), (dgg, pl.BlockSpec(memory_space=pl.ANY))]
    outs = [(_sds(dgg.shape, dgg.dtype), _rspec(tm, 2 * D, 0, n)),
            _acc_out((1, D)), _acc_out((1, D)), _acc_out((1, D)), _acc_out((CONV_KP, D))]
    shifted = pltpu.VMEM((SUBLANES - 1, tm + HALO, D), F32)
    return _rows_call(body, "conv_bwd", n, ins, outs,
                      [pltpu.VMEM((tm + HALO, D), F32), pltpu.VMEM((tm + HALO, D), F32), pltpu.VMEM((tm, D), F32),
                       shifted, shifted, pltpu.VMEM((CONV_KP * SUBLANES, D), F32)], aliases={len(ins) - 1: 0})


def _merge(ba, bb, proj):
    T = ba.shape[0]
    tm = TM_ROWS

    def body(ba_ref, bb_ref, ga_ref, gb_ref, o_ref):
        sa, sb = _sigmoid(ga_ref[...].astype(F32)), _sigmoid(gb_ref[...].astype(F32))
        o_ref[...] = (sa * ba_ref[...] + sb * bb_ref[...]).astype(BF)

    ins = [(ba, _rspec(tm, D)), (bb, _rspec(tm, D)), (proj, _rspec(tm, D, 2)), (proj, _rspec(tm, D, 3))]
    return _rows_call(body, "merge", T // tm, ins, [_row_out(T, tm, D, BF)])[0]


def _post_out(x, mo, mod, n2g):
    T = x.shape[0]
    tm = TM_ROWS

    def body(x_ref, mo_ref, mod_ref, g_ref, x1_ref, h2_ref):
        g1 = mod_ref[:, 2 * D:3 * D]
        sh2, sc2 = mod_ref[:, 3 * D:4 * D], mod_ref[:, 4 * D:5 * D]
        x1 = x_ref[...] + g1 * mo_ref[...]
        x1_ref[...] = x1
        r = lax.rsqrt(jnp.mean(x1 * x1, axis=-1, keepdims=True) + EPS)
        h2_ref[...] = ((x1 * r) * g_ref[...] * (1.0 + sc2) + sh2).astype(BF)

    ins = [(x, _rspec(tm, D)), (mo, _rspec(tm, D)), (mod, _full(mod)), (n2g, _full(n2g))]
    return _rows_call(body, "post_out", T // tm, ins, [_row_out(T, tm, D, F32), _row_out(T, tm, D, BF)])


def _loss_head(x1, m2, tgt, mod):
    T = x1.shape[0]
    tm = TM_ROWS

    def body(x1_ref, m2_ref, t_ref, mod_ref, dy_ref, dm2_ref, dg2_ref, sq_ref):
        i = pl.program_id(0)

        @pl.when(i == 0)
        def _():
            dg2_ref[...] = jnp.zeros_like(dg2_ref)
            sq_ref[...] = jnp.zeros_like(sq_ref)

        g2 = mod_ref[:, 5 * D:6 * D]
        m2 = m2_ref[...]
        err = x1_ref[...] + g2 * m2 - t_ref[...]
        dy = err * (1.0 / D)
        dy_ref[...] = dy
        dm2_ref[...] = (g2 * dy).astype(BF)
        dg2_ref[...] += _colsum(dy * m2)
        sq_ref[...] += _colsum(err * err)

    ins = [(x1, _rspec(tm, D)), (m2, _rspec(tm, D)), (tgt, _rspec(tm, D)), (mod, _full(mod))]
    outs = [_row_out(T, tm, D, F32), _row_out(T, tm, D, BF), _acc_out((1, D)), _acc_out((1, D))]
    return _rows_call(body, "loss_head", T // tm, ins, outs)


def _norm2_bwd(dh2, x1, dy, mo, mod, n2g):
    T = x1.shape[0]
    tm = TM_ROWS

    def body(dh_ref, x1_ref, dy_ref, mo_ref, mod_ref, g_ref, dx1_ref, dmo_ref, dsh_ref, dsc_ref, dg_ref, dg1_ref):
        i = pl.program_id(0)

        @pl.when(i == 0)
        def _():
            for r in (dsh_ref, dsc_ref, dg_ref, dg1_ref):
                r[...] = jnp.zeros_like(r)

        g1, sc2 = mod_ref[:, 2 * D:3 * D], mod_ref[:, 4 * D:5 * D]
        g = g_ref[...]
        x1 = x1_ref[...]
        dh = dh_ref[...]
        r = lax.rsqrt(jnp.mean(x1 * x1, axis=-1, keepdims=True) + EPS)
        xn = x1 * r
        dsh_ref[...] += _colsum(dh)
        dsc_ref[...] += _colsum(dh * xn * g)
        dg_ref[...] += _colsum(dh * xn * (1.0 + sc2))
        dxn = dh * g * (1.0 + sc2)
        dx1 = dy_ref[...] + r * (dxn - xn * jnp.mean(dxn * xn, axis=-1, keepdims=True))
        dx1_ref[...] = dx1
        dg1_ref[...] += _colsum(dx1 * mo_ref[...])
        dmo_ref[...] = (g1 * dx1).astype(BF)

    ins = [(dh2, _rspec(tm, D)), (x1, _rspec(tm, D)), (dy, _rspec(tm, D)), (mo, _rspec(tm, D)),
           (mod, _full(mod)), (n2g, _full(n2g))]
    outs = [_row_out(T, tm, D, F32), _row_out(T, tm, D, BF)] + [_acc_out((1, D)) for _ in range(4)]
    return _rows_call(body, "norm2_bwd", T // tm, ins, outs)


def _gate_bwd(dmerged, ba, bb, proj):
    T = ba.shape[0]
    tm = TM_ROWS

    def body(dm_ref, ba_ref, bb_ref, ga_ref, gb_ref, dba_ref, dbb_ref, dgt_ref):
        dm = dm_ref[...]
        sa, sb = _sigmoid(ga_ref[...].astype(F32)), _sigmoid(gb_ref[...].astype(F32))
        dba_ref[...] = (dm * sa).astype(BF)
        dbb_ref[...] = (dm * sb).astype(BF)
        dgt_ref[:, 0:D] = (dm * ba_ref[...] * sa * (1.0 - sa)).astype(BF)
        dgt_ref[:, D:2 * D] = (dm * bb_ref[...] * sb * (1.0 - sb)).astype(BF)

    ins = [(dmerged, _rspec(tm, D)), (ba, _rspec(tm, D)), (bb, _rspec(tm, D)),
           (proj, _rspec(tm, D, 2)), (proj, _rspec(tm, D, 3))]
    outs = [_row_out(T, tm, D, BF), _row_out(T, tm, D, BF), (_sds((T, 4 * D), BF), _rspec(tm, 2 * D, 1))]
    return _rows_call(body, "gate_bwd", T // tm, ins, outs)


def _qkv_bwd(dq, dk, dv, proj, f, dfc, dfq, qg2, kg2, bf_pad):
    T = proj.shape[0]
    tm = TM_ROWS
    n = T // tm
    seg = _seg_mat()
    tri = _tri_mat(tm, False)

    def body(dq_ref, dk_ref, dv_ref, q_ref, k_ref, f_ref, dfc_ref, dfq_ref, qg_ref, kg_ref, bf_ref, seg_ref, tri_ref,
             dqkv_ref, dfo_ref, dqg_ref, dkg_ref, dbf_ref, carry_ref):
        i = pl.program_id(0)

        @pl.when(i == 0)
        def _():
            carry_ref[...] = jnp.zeros_like(carry_ref)
            dqg_ref[...] = jnp.zeros_like(dqg_ref)
            dkg_ref[...] = jnp.zeros_like(dkg_ref)
            dbf_ref[...] = jnp.zeros_like(dbf_ref)

        segm = seg_ref[...]
        dqg = jnp.zeros((1, LANES), F32)
        dkg = jnp.zeros((1, LANES), F32)
        for j in range(D // LANES):
            sl = slice(j * LANES, (j + 1) * LANES)
            for (raw_ref, d_ref, gn_ref, which) in ((q_ref, dq_ref, qg_ref, 0), (k_ref, dk_ref, kg_ref, 1)):
                xc = raw_ref[:, sl].astype(F32)
                rr = lax.rsqrt(_dot_exact(xc * xc, segm) + EPS)
                xn = xc * rr
                dc = d_ref[:, sl]
                if which == 0:
                    dqg = dqg + _colsum(dc * xn)
                else:
                    dkg = dkg + _colsum(dc * xn)
                dxn = dc * gn_ref[...]
                osl = slice(which * D + j * LANES, which * D + (j + 1) * LANES)
                dqkv_ref[:, osl] = (rr * (dxn - xn * _dot_exact(dxn * xn, segm))).astype(BF)
        dqg_ref[...] += dqg
        dkg_ref[...] += dkg
        dqkv_ref[:, 2 * D:3 * D] = dv_ref[...].astype(BF)
        z = f_ref[...] + bf_ref[...]
        sneg_t = _sigmoid(-z).T[0:N_HEADS, :]
        dfc = dfc_ref[...] + dfq_ref[...]
        carry = carry_ref[:, 0:1]
        dlf = _dot_exact(dfc, tri_ref[...]) + carry
        carry_ref[...] = jnp.broadcast_to(carry + jnp.sum(dfc, axis=1, keepdims=True), carry_ref.shape)
        dzt = dlf * sneg_t
        dz = jnp.concatenate([dzt, jnp.zeros((LANES - N_HEADS, tm), F32)], axis=0).T
        dbf_ref[...] += _colsum(dz)
        dfo_ref[...] = dz.astype(BF)

    rs = lambda w, cb=0: _rspec(tm, w, cb, n)
    ins = [(dq, rs(D)), (dk, rs(D)), (dv, rs(D)), (proj, rs(D, 0)), (proj, rs(D, 1)), (f, rs(LANES)),
           (dfc, pl.BlockSpec((N_HEADS, tm), lambda i: (0, n - 1 - i))),
           (dfq, pl.BlockSpec((N_HEADS, tm), lambda i: (0, n - 1 - i))),
           (qg2, _full(qg2)), (kg2, _full(kg2)), (bf_pad, _full(bf_pad)), (seg, _full(seg)), (tri, _full(tri))]
    outs = [_row_out(T, tm, 3 * D, BF, n), _row_out(T, tm, LANES, BF, n),
            _acc_out((1, LANES)), _acc_out((1, LANES)), _acc_out((1, LANES))]
    return _rows_call(body, "qkv_bwd", n, ins, outs, [pltpu.VMEM((N_HEADS, LANES), F32)])


def _norm1_bwd(dh, dhf, x, dx1, mod, n1g):
    T = x.shape[0]
    tm = TM_ROWS

    def body(dh_ref, dhf_ref, x_ref, dx1_ref, mod_ref, g_ref, dx_ref, dsh_ref, dsc_ref, dg_ref):
        i = pl.program_id(0)

        @pl.when(i == 0)
        def _():
            for r in (dsh_ref, dsc_ref, dg_ref):
                r[...] = jnp.zeros_like(r)

        sc1 = mod_ref[:, D:2 * D]
        g = g_ref[...]
        xv = x_ref[...]
        dh = dh_ref[...] + dhf_ref[...]
        r = lax.rsqrt(jnp.mean(xv * xv, axis=-1, keepdims=True) + EPS)
        xn = xv * r
        dsh_ref[...] += _colsum(dh)
        dsc_ref[...] += _colsum(dh * xn * g)
        dg_ref[...] += _colsum(dh * xn * (1.0 + sc1))
        dxn = dh * g * (1.0 + sc1)
        dx_ref[...] = dx1_ref[...] + r * (dxn - xn * jnp.mean(dxn * xn, axis=-1, keepdims=True))

    ins = [(dh, _rspec(tm, D)), (dhf, _rspec(tm, D)), (x, _rspec(tm, D)), (dx1, _rspec(tm, D)),
           (mod, _full(mod)), (n1g, _full(n1g))]
    outs = [_row_out(T, tm, D, F32)] + [_acc_out((1, D)) for _ in range(3)]
    return _rows_call(body, "norm1_bwd", T // tm, ins, outs)


def _adamw_math(w, g, m, v):
    m = ADAM_B1 * m + (1.0 - ADAM_B1) * g
    v = ADAM_B2 * v + (1.0 - ADAM_B2) * (g * g)
    m_hat = m / (1.0 - ADAM_B1 ** ADAM_STEP)
    v_hat = v / (1.0 - ADAM_B2 ** ADAM_STEP)
    delta = -ADAM_LR * (m_hat / (jnp.sqrt(v_hat) + ADAM_EPS) + ADAM_WD * w)
    return delta, m, v


def _adamw(parts, w, m, v, name):
    n, R, C = parts.shape
    tr = R if R <= 256 else 256
    assert R % tr == 0

    def body(p_ref, w_ref, m_ref, v_ref, g_ref, d_ref, mo_ref, vo_ref):
        g = p_ref[0].astype(F32)
        for s in range(1, n):
            g = g + p_ref[s].astype(F32)
        g_ref[...] = g
        d_ref[...], mo_ref[...], vo_ref[...] = _adamw_math(w_ref[...], g, m_ref[...], v_ref[...])

    spec = pl.BlockSpec((None, tr, C), lambda i: (0, i, 0))
    return pl.pallas_call(
        body, name=name, grid=(R // tr,),
        in_specs=[pl.BlockSpec((n, tr, C), lambda i: (0, i, 0)), spec, spec, spec],
        out_specs=[spec] * 4, out_shape=[_sds((1, R, C), F32)] * 4,
        compiler_params=_cp(("parallel",)),
    )(parts, w, m, v)


def _rcopy(src, dst, ssem, rsem, peer):
    return pltpu.make_async_remote_copy(src_ref=src, dst_ref=dst, send_sem=ssem, recv_sem=rsem,
                                        device_id=peer, device_id_type=MESH)


def _ada_fwd(c, w_ada, b_slice, cw_shard):
    def body(c_ref, w_ref, b_ref, cw_ref, mod_ref, ca_ref, cwf_ref, call, mp, ssem, rsem):
        x, y, cc, me = _my_pos()
        call[pl.ds(me, 1), :] = c_ref[...]
        cwf_ref[me] = cw_ref[...]
        first = []
        for d in range(1, N_DEV):
            peer, _ = _peer(x, y, cc, d)
            first.append(_rcopy(c_ref, call.at[pl.ds(me, 1), :], ssem.at[0, d - 1], rsem.at[0, d - 1], peer))
            first.append(_rcopy(cw_ref, cwf_ref.at[me], ssem.at[1, d - 1], rsem.at[1, d - 1], peer))
        for cp in first:
            cp.start()
        for d in range(1, N_DEV):
            peer, pid = _peer(x, y, cc, d)
            _rcopy(c_ref, call.at[pl.ds(pid, 1), :], ssem.at[0, d - 1], rsem.at[0, d - 1], peer).wait_recv()
            _rcopy(cw_ref, cwf_ref.at[pid], ssem.at[1, d - 1], rsem.at[1, d - 1], peer).wait_recv()
        cv = call[...]
        ca = cv * _sigmoid(cv)
        ca_ref[...] = ca
        mp[...] = _dot_f32(ca, w_ref[...]) + b_ref[...]
        mod_ref[pl.ds(me, 1), :] = mp[pl.ds(me, 1), :]
        second = []
        for d in range(1, N_DEV):
            peer, pid = _peer(x, y, cc, d)
            second.append(_rcopy(mp.at[pl.ds(pid, 1), :], mod_ref.at[pl.ds(me, 1), :], ssem.at[2, d - 1], rsem.at[2, d - 1], peer))
        for cp in second:
            cp.start()
        for d in range(1, N_DEV):
            peer, pid = _peer(x, y, cc, d)
            _rcopy(mp.at[pl.ds(pid, 1), :], mod_ref.at[pl.ds(pid, 1), :], ssem.at[2, d - 1], rsem.at[2, d - 1], peer).wait_recv()
        for cp in first + second:
            cp.wait_send()

    vm = pl.BlockSpec(memory_space=pltpu.VMEM)
    return pl.pallas_call(
        body, name="ada_fwd",
        in_specs=[vm, vm, vm, vm], out_specs=[vm, vm, vm],
        out_shape=[_sds((N_DEV, ADA_SHARD), F32), _sds((N_DEV, D), F32), _sds((N_DEV, CONV_KP, LANES), F32)],
        scratch_shapes=[pltpu.VMEM((N_DEV, D), F32), pltpu.VMEM((N_DEV, ADA_SHARD), F32),
                        pltpu.SemaphoreType.DMA((3, N_DEV - 1)), pltpu.SemaphoreType.DMA((3, N_DEV - 1))],
        compiler_params=pltpu.CompilerParams(vmem_limit_bytes=VMEM_LIMIT),
    )(c, w_ada, b_slice, cw_shard)


def _xchg_parts(arrays):
    n = len(arrays)
    anyspec = pl.BlockSpec(memory_space=pl.ANY)
    shapes = [_sds((N_DEV,) + tuple(a.shape[-2:]), a.dtype) for a in arrays]
    scratch = [pltpu.SemaphoreType.DMA((n,)), pltpu.SemaphoreType.DMA((n, N_DEV - 1)),
               pltpu.SemaphoreType.DMA((n, N_DEV - 1))]
    return [anyspec] * n, shapes, scratch


def _xchg(ins, outs, sems, gather, wait, halves=None):
    lsem, ssem, rsem = sems
    x, y, cc, me = _my_pos()
    for a in range(len(ins)):
        hx = None if halves is None else halves[a]
        if hx is None:
            local = pltpu.make_async_copy(ins[a] if gather else ins[a].at[me], outs[a].at[me], lsem.at[a])
            if not wait:
                local.start()
            for d in range(1, N_DEV):
                peer, pid = _peer(x, y, cc, d)
                src = ins[a] if gather else ins[a].at[pid]
                if not wait:
                    _rcopy(src, outs[a].at[me], ssem.at[a, d - 1], rsem.at[a, d - 1], peer).start()
                else:
                    cp = _rcopy(src, outs[a].at[pid], ssem.at[a, d - 1], rsem.at[a, d - 1], peer)
                    cp.wait_recv()
                    cp.wait_send()
            if wait:
                local.wait()
            continue
        half = N_DEV // 2
        mine = jnp.clip(me - half * hx, 0, half - 1)

        @pl.when(x == hx)
        def _():
            local = pltpu.make_async_copy(ins[a].at[mine], outs[a].at[me], lsem.at[a])
            if wait:
                local.wait()
            else:
                local.start()

        for d in range(1, N_DEV):
            peer, pid = _peer(x, y, cc, d)
            src = ins[a].at[jnp.clip(pid - half * hx, 0, half - 1)]
            sends = (x == hx) if d < half else (x != hx)
            if not wait:
                @pl.when(sends)
                def _():
                    _rcopy(src, outs[a].at[me], ssem.at[a, d - 1], rsem.at[a, d - 1], peer).start()
            else:
                cp = _rcopy(src, outs[a].at[pid], ssem.at[a, d - 1], rsem.at[a, d - 1], peer)

                @pl.when(x == hx)
                def _():
                    cp.wait_recv()

                @pl.when(sends)
                def _():
                    cp.wait_send()


def _gather_two_level(shard, name):
    def body(x_ref, out_ref, ssem, rsem, lsem):
        x, y, c, me = _my_pos()
        sibling = (x, y, 1 - c)
        chips = [(1 - x, y), (x, 1 - y), (1 - x, 1 - y)]
        slot = lambda px, py, pc: out_ref.at[4 * px + 2 * py + pc]

        def copy(kk, block, to, src=None):
            return _rcopy(slot(*block) if src is None else src, slot(*block), ssem.at[kk], rsem.at[kk], to)

        mine = pltpu.make_async_copy(x_ref, slot(x, y, c), lsem)
        mine.start()
        first = [copy(0, (x, y, c), sibling, src=x_ref)]
        first += [copy(1 + j, (x, y, c), (*chip, c), src=x_ref) for j, chip in enumerate(chips)]
        for cp in first:
            cp.start()
        passed = [copy(4 + j, (*chip, c), sibling) for j, chip in enumerate(chips)]
        for j, chip in enumerate(chips):
            copy(1 + j, (*chip, c), (x, y, c)).wait_recv()
            passed[j].start()
        copy(0, sibling, (x, y, c)).wait_recv()
        for j, chip in enumerate(chips):
            copy(4 + j, (*chip, 1 - c), (x, y, c)).wait_recv()
        for cp in first + passed:
            cp.wait_send()
        mine.wait()

    anyspec = pl.BlockSpec(memory_space=pl.ANY)
    return pl.pallas_call(
        body, name=name, in_specs=[anyspec], out_specs=anyspec,
        out_shape=_sds((N_DEV,) + tuple(shard.shape), shard.dtype),
        scratch_shapes=[pltpu.SemaphoreType.DMA((N_DEV - 1,)), pltpu.SemaphoreType.DMA((N_DEV - 1,)),
                        pltpu.SemaphoreType.DMA(())],
    )(shard)


PACK_ROWS = 16
ROW_MISC = 5
ROW_LOSS = 6
ROW_DMOD = 8


def _small_bwd(pack, dmodb, dcw, cat, wp, mp_, vp, cw_w, cw_m, cw_v):
    def body(pack_ref, dmodb_ref, dcw_ref, cat_ref, wp_ref, mp_ref, vp_ref, cww_ref, cwm_ref, cwv_ref,
             g_ref, d_ref, mo_ref, vo_ref, cg_ref, cd_ref, cm_ref, cv_ref, gwa_ref, loss_ref,
             allp, dmc, cwg, ssem, rsem):
        x, y, cc, me = _my_pos()
        allp[me] = pack_ref[...]
        dmc[pl.ds(me, 1), :] = dmodb_ref[pl.ds(me, 1), :]
        cwg[me] = dcw_ref[me]
        sends = []
        for d in range(1, N_DEV):
            peer, pid = _peer(x, y, cc, d)
            sends.append(_rcopy(pack_ref, allp.at[me], ssem.at[0, d - 1], rsem.at[0, d - 1], peer))
            sends.append(_rcopy(dmodb_ref.at[pl.ds(pid, 1), :], dmc.at[pl.ds(me, 1), :], ssem.at[1, d - 1], rsem.at[1, d - 1], peer))
            sends.append(_rcopy(dcw_ref.at[pid], cwg.at[me], ssem.at[2, d - 1], rsem.at[2, d - 1], peer))
        for cp in sends:
            cp.start()
        for d in range(1, N_DEV):
            peer, pid = _peer(x, y, cc, d)
            _rcopy(pack_ref, allp.at[pid], ssem.at[0, d - 1], rsem.at[0, d - 1], peer).wait_recv()
            _rcopy(dmodb_ref.at[pl.ds(pid, 1), :], dmc.at[pl.ds(pid, 1), :], ssem.at[1, d - 1], rsem.at[1, d - 1], peer).wait_recv()
            _rcopy(dcw_ref.at[pid], cwg.at[pid], ssem.at[2, d - 1], rsem.at[2, d - 1], peer).wait_recv()
        for cp in sends:
            cp.wait_send()

        tot = allp[0]
        cg = cwg[0]
        for s in range(1, N_DEV):
            tot = tot + allp[s]
            cg = cg + cwg[s]
        lane = lax.broadcasted_iota(jnp.int32, (PACK_ROWS, D), 1)
        row = lax.broadcasted_iota(jnp.int32, (PACK_ROWS, D), 0)
        gains = (row == ROW_MISC) & (lane >= LANES) & (lane < 3 * LANES)
        folded = tot + pltpu.roll(tot, D - HEAD_DIM, axis=1)
        keep = (lane % LANES) < HEAD_DIM
        g = jnp.where(gains, jnp.where(keep, folded, 0.0), tot)
        loss_ref[...] = jnp.broadcast_to(
            (0.5 / D) * jnp.sum(jnp.where(row == ROW_LOSS, tot, 0.0), keepdims=True).reshape(1, 1), loss_ref.shape)
        g = jnp.where(row == ROW_LOSS, 0.0, g)
        g_ref[...] = g
        d_ref[...], mo_ref[...], vo_ref[...] = _adamw_math(wp_ref[...], g, mp_ref[...], vp_ref[...])
        cg_ref[...] = cg
        cd_ref[...], cm_ref[...], cv_ref[...] = _adamw_math(cww_ref[...], cg, cwm_ref[...], cwv_ref[...])
        dm_pad = jnp.concatenate([dmc[...], jnp.zeros((LANES - N_DEV, ADA_SHARD), F32)], axis=0)
        gwa_ref[...] = _dot_f32(cat_ref[...], dm_pad)

    vm = pl.BlockSpec(memory_space=pltpu.VMEM)
    p16 = _sds((PACK_ROWS, D), F32)
    c32 = _sds((CONV_KP, LANES), F32)
    return pl.pallas_call(
        body, name="small_bwd",
        in_specs=[vm] * 10, out_specs=[vm] * 10,
        out_shape=[p16, p16, p16, p16, c32, c32, c32, c32, _sds((D, ADA_SHARD), F32), _sds((8, LANES), F32)],
        scratch_shapes=[pltpu.VMEM((N_DEV, PACK_ROWS, D), F32), pltpu.VMEM((N_DEV, ADA_SHARD), F32),
                        pltpu.VMEM((N_DEV, CONV_KP, LANES), F32),
                        pltpu.SemaphoreType.DMA((3, N_DEV - 1)), pltpu.SemaphoreType.DMA((3, N_DEV - 1))],
        compiler_params=pltpu.CompilerParams(vmem_limit_bytes=VMEM_LIMIT),
    )(pack, dmodb, dcw, cat, wp, mp_, vp, cw_w, cw_m, cw_v)


def _lanes(vec, start, total=D):
    n = vec.shape[1]
    return jnp.pad(vec, ((0, 0), (start, total - start - n)))


def _pack_small(rows5, misc, loss_row, six):
    z = jnp.zeros((1, D), F32)
    return jnp.concatenate(rows5 + [misc, loss_row, z] + [six.reshape(N_ADA, D), z, z], axis=0)


def kernel(x, c, w_ada, b_ada, norm1_g, w_in, b_forget, q_norm_g, k_norm_g, w_attn_proj, conv_w, conv_b, conv_ln_g, conv_ln_b, w_conv_proj, w_out, norm2_g, w_mlp1, w_mlp2, loss_target, m_w_ada, m_b_ada, m_norm1_g, m_w_in, m_b_forget, m_q_norm_g, m_k_norm_g, m_w_attn_proj, m_conv_w, m_conv_b, m_conv_ln_g, m_conv_ln_b, m_w_conv_proj, m_w_out, m_norm2_g, m_w_mlp1, m_w_mlp2, v_w_ada, v_b_ada, v_norm1_g, v_w_in, v_b_forget, v_q_norm_g, v_k_norm_g, v_w_attn_proj, v_conv_w, v_conv_b, v_conv_ln_g, v_conv_ln_b, v_w_conv_proj, v_w_out, v_norm2_g, v_w_mlp1, v_w_mlp2):
    me = 4 * lax.axis_index("x") + 2 * lax.axis_index("y") + lax.axis_index("c")
    xs, tgt = x[0], loss_target[0]
    T = xs.shape[0]
    sq = lambda a: a[0]
    pad_taps = lambda a: jnp.pad(a[0], ((0, CONV_KP - CONV_K), (0, 0)))

    b_slice = lax.dynamic_slice(b_ada, (0, me * ADA_SHARD), (1, ADA_SHARD))
    modb, ca_all, cwf = _ada_fwd(c, sq(w_ada), b_slice, pad_taps(conv_w))
    mod = modb.reshape(1, N_ADA * D)
    cw = jnp.transpose(cwf, (1, 0, 2)).reshape(CONV_KP, D)

    g_in = _gather_two_level(sq(w_in).astype(BF), "w_in_gather")
    d_in = g_in.shape[2] * N_DEV
    w_in_f = jnp.transpose(g_in, (1, 0, 2)).reshape(D, d_in)
    w_qkv = w_in_f[:, :3 * D]
    w_gg = w_in_f[:, 3 * D + N_HEADS:]
    w_f = jnp.pad(w_in_f[:, 3 * D:3 * D + N_HEADS], ((0, 0), (0, LANES - N_HEADS)))
    shards = [sq(w_attn_proj).astype(BF), sq(w_conv_proj).astype(BF), sq(w_out).astype(BF),
              sq(w_mlp1).astype(BF), sq(w_mlp2).astype(BF)]

    qg2 = jnp.tile(q_norm_g, (1, 2))
    kg2 = jnp.tile(k_norm_g, (1, 2))
    bf_pad = _lanes(b_forget, 0, LANES)

    h = _pre_in(xs, mod, norm1_g)
    pqkv = _matmul(h, w_qkv, "nn", BF, "mm_proj_qkv")
    pgg = _matmul(h, w_gg, "nn", BF, "mm_proj_gg")
    f = _matmul(h, w_f, "nn", F32, "mm_f")
    q, k, v, fc = _qkv_post(pqkv, f, qg2, kg2, bf_pad)
    fc3 = fc.reshape(N_HEADS // 2, 2, T)
    o, lse, g_ap, g_cp, g_out, g_1, g_2 = _flash_fwd(q, k, v, fc3, shards)
    w_ap, w_cp, w_o = g_ap.reshape(D, D), g_cp.reshape(D, D), g_out.reshape(D, D)
    w_2 = g_2.reshape(D_FF, D)
    ba = _matmul(o, w_ap, "nn", F32, "mm_ba")
    u0, u1, u3 = _conv_fwd(pgg, cw, conv_b, conv_ln_g, conv_ln_b)
    bb = _matmul(u3, w_cp, "nn", F32, "mm_bb")
    merged = _merge(ba, bb, pgg)
    mo = _matmul(merged, w_o, "nn", F32, "mm_out")
    x1, h2 = _post_out(xs, mo, mod, norm2_g)
    a, rl = _matmul(h2, g_1, "nn", BF, "mm_mlp1", relu2=True, b_slots=True)
    m2 = _matmul(rl, w_2, "nn", F32, "mm_mlp2")
    dy, dm2, dg2, sqcols = _loss_head(x1, m2, tgt, mod)

    da = _matmul(dm2, w_2, "nt", BF, "mm_drl", relu_of=a)
    dw_2 = _matmul(rl, dm2, "tn", BF, "mm_dw2")
    dh2 = _matmul(da, g_1, "nt", F32, "mm_dh2", b_slots=True)
    dw_1 = _matmul(h2, da, "tn", BF, "mm_dw1", out_slots=True)
    dx1, dmo, dsh2, dsc2, dn2g, dg1 = _norm2_bwd(dh2, x1, dy, mo, mod, norm2_g)
    dmerged = _matmul(dmo, w_o, "nt", F32, "mm_dmerged")
    dw_o = _matmul(merged, dmo, "tn", BF, "mm_dwout")
    dba, dbb, dgg = _gate_bwd(dmerged, ba, bb, pgg)
    du3 = _matmul(dbb, w_cp, "nt", F32, "mm_du3")
    dw_cp = _matmul(u3, dbb, "tn", BF, "mm_dwcp")
    do = _matmul(dba, w_ap, "nt", F32, "mm_do")
    dw_ap = _matmul(o, dba, "tn", BF, "mm_dwap")
    dgg, dlng, dlnb, dcb, dcw_full = _conv_bwd(du3, u1, u0, pgg, cw, conv_ln_g, conv_ln_b, dgg)
    delta = _attn_delta(do, o)
    dw_gg = _matmul(h, dgg, "tn", BF, "mm_dw_gg")
    shard_in = d_in // N_DEV
    n_lo = (N_DEV // 2) * shard_in - 3 * D - N_HEADS
    part_in_hi = jnp.transpose(dw_gg[:, n_lo:].reshape(D, N_DEV // 2, shard_in), (1, 0, 2))
    parts = [dw_ap.reshape(N_DEV, D // N_DEV, D), dw_cp.reshape(N_DEV, D // N_DEV, D), dw_o.reshape(N_DEV, D // N_DEV, D),
             dw_1, dw_2.reshape(N_DEV, D_FF // N_DEV, D), part_in_hi]
    dq, rs_a, rs_b, dk, dv, dfc3, r_ap, r_cp, r_out, r_1, r_2, r_in_hi = _flash_bwd(
        q, k, v, do, lse, delta, fc3, parts, [None] * 5 + [1])
    dfq = jnp.stack([rs_a, rs_b], axis=1).reshape(N_HEADS, T)
    dqkv, df, dqg, dkg, dbf = _qkv_bwd(dq, dk, dv, pqkv, f, dfc3.reshape(N_HEADS, T), dfq, qg2, kg2, bf_pad)
    dw_qkv = _matmul(h, dqkv, "tn", BF, "mm_dw_qkv")
    dw_f = _matmul(h, df, "tn", BF, "mm_dwf")
    dw_in_lo = jnp.concatenate([dw_qkv, dw_f[:, :N_HEADS], dw_gg[:, :n_lo]], axis=1)
    part_in_lo = jnp.transpose(dw_in_lo.reshape(D, N_DEV // 2, shard_in), (1, 0, 2))
    dh, r_in_lo = _matmul(dqkv, w_qkv, "nt", F32, "mm_dh", scatter=(part_in_lo,), more=((dgg, w_gg),), halves=[0])
    r_in = jnp.where(lax.axis_index("x") == 1, r_in_hi, r_in_lo)
    dhf = _matmul(df, w_f, "nt", F32, "mm_dhf")
    grad_x, dsh1, dsc1, dn1g = _norm1_bwd(dh, dhf, xs, dx1, mod, norm1_g)

    dmod = jnp.concatenate([dsh1, dsc1, dg1, dsh2, dsc2, dg2], axis=1)
    misc = jnp.concatenate([dbf, dqg, dkg, jnp.zeros((1, D - 3 * LANES), F32)], axis=1)
    pack = _pack_small([dn1g, dcb, dlng, dlnb, dn2g], misc, sqcols, dmod)
    dcw_blocks = jnp.transpose(dcw_full.reshape(CONV_KP, N_DEV, LANES), (1, 0, 2))

    def small_params(b_a, n1, bfg, qn, kn, cvb, lg, lb, n2):
        misc_p = jnp.concatenate([_lanes(bfg, 0, LANES), _lanes(qn, 0, LANES), _lanes(kn, 0, LANES),
                                  jnp.zeros((1, D - 3 * LANES), F32)], axis=1)
        return _pack_small([n1, cvb, lg, lb, n2], misc_p, jnp.zeros((1, D), F32), b_a)

    wp = small_params(b_ada, norm1_g, b_forget, q_norm_g, k_norm_g, conv_b, conv_ln_g, conv_ln_b, norm2_g)
    mp_ = small_params(m_b_ada, m_norm1_g, m_b_forget, m_q_norm_g, m_k_norm_g, m_conv_b, m_conv_ln_g, m_conv_ln_b, m_norm2_g)
    vp = small_params(v_b_ada, v_norm1_g, v_b_forget, v_q_norm_g, v_k_norm_g, v_conv_b, v_conv_ln_g, v_conv_ln_b, v_norm2_g)
    cat = jnp.pad(jnp.transpose(ca_all), ((0, 0), (0, LANES - N_DEV)))
    small = _small_bwd(pack, dmod.reshape(N_DEV, ADA_SHARD), dcw_blocks, cat,
                       wp, mp_, vp, pad_taps(conv_w), pad_taps(m_conv_w), pad_taps(v_conv_w))
    sp = small[0:4]
    scw = small[4:8]
    gw_ada, loss_t = small[8], small[9]
    loss = loss_t[0, 0]

    def unpack(p):
        misc_r = p[ROW_MISC:ROW_MISC + 1]
        return dict(
            b_ada=p[ROW_DMOD:ROW_DMOD + N_ADA].reshape(1, N_ADA * D), norm1_g=p[0:1], conv_b=p[1:2], conv_ln_g=p[2:3],
            conv_ln_b=p[3:4], norm2_g=p[4:5], b_forget=misc_r[:, 0:N_HEADS],
            q_norm_g=misc_r[:, LANES:LANES + HEAD_DIM], k_norm_g=misc_r[:, 2 * LANES:2 * LANES + HEAD_DIM])

    res = {}
    res["w_ada"] = _adamw(gw_ada[None], w_ada, m_w_ada, v_w_ada, "adamw_w_ada")
    res["w_in"] = _adamw(r_in, w_in, m_w_in, v_w_in, "adamw_w_in")
    res["w_attn_proj"] = _adamw(r_ap, w_attn_proj, m_w_attn_proj, v_w_attn_proj, "adamw_w_ap")
    res["w_conv_proj"] = _adamw(r_cp, w_conv_proj, m_w_conv_proj, v_w_conv_proj, "adamw_w_cp")
    res["w_out"] = _adamw(r_out, w_out, m_w_out, v_w_out, "adamw_w_out")
    res["w_mlp1"] = _adamw(r_1, w_mlp1, m_w_mlp1, v_w_mlp1, "adamw_w_mlp1")
    res["w_mlp2"] = _adamw(r_2, w_mlp2, m_w_mlp2, v_w_mlp2, "adamw_w_mlp2")

    names = ["w_ada", "b_ada", "norm1_g", "w_in", "b_forget", "q_norm_g", "k_norm_g", "w_attn_proj", "conv_w", "conv_b",
             "conv_ln_g", "conv_ln_b", "w_conv_proj", "w_out", "norm2_g", "w_mlp1", "w_mlp2"]
    outs = [loss, grad_x[None]]
    for kind in range(4):
        small_d = unpack(sp[kind])
        for nm in names:
            if nm in res:
                outs.append(res[nm][kind])
            elif nm == "conv_w":
                outs.append(scw[kind][:CONV_K][None])
            else:
                outs.append(small_d[nm])
    return tuple(outs)
```

```python
import functools

import jax
import jax.numpy as jnp
from jax import lax
from jax.experimental import pallas as pl
from jax.experimental.pallas import tpu as pltpu

F32 = jnp.float32
BF = jnp.bfloat16

N_DEV = 8
D = 1024
N_HEADS = 16
HEAD_DIM = 64
LANES = 128
SUBLANES = 8
CONV_K = 31
CONV_KP = 32
HALO = 32
CONV_ROWS = 64
D_FF = 4 * D
N_ADA = 6
ADA_SHARD = N_ADA * D // N_DEV
EPS = 1e-6
QK_SCALE = HEAD_DIM ** -0.5
LOG2E = 1.4426950408889634
LN2 = 0.6931471805599453
NEG = -1e30

ADAM_LR = 0.001
ADAM_B1 = 0.9
ADAM_B2 = 0.999
ADAM_EPS = 1e-08
ADAM_WD = 0.01
ADAM_STEP = 10

VMEM_LIMIT = 56 * 1024 * 1024
TM_ROWS = 512
CONV_TM = 256
TQ = 512

MESH = pl.DeviceIdType.MESH


def _cp(sem=None):
    return pltpu.CompilerParams(dimension_semantics=sem, vmem_limit_bytes=VMEM_LIMIT)


def _sds(shape, dtype):
    return jax.ShapeDtypeStruct(tuple(shape), dtype)


def _full(arr):
    nd = arr.ndim
    return pl.BlockSpec(arr.shape, lambda *_: (0,) * nd)


def _fullshape(shape):
    nd = len(shape)
    return pl.BlockSpec(tuple(shape), lambda *_: (0,) * nd)


def _split3(x):
    hi = x.astype(BF)
    r1 = x - hi.astype(F32)
    mid = r1.astype(BF)
    lo = (r1 - mid.astype(F32)).astype(BF)
    return hi, mid, lo


def _dot_exact(x, mat):
    hi, mid, lo = _split3(x)
    d = lambda t: jnp.dot(t, mat, preferred_element_type=F32)
    return d(hi) + d(mid) + d(lo)


def _dot_f32(a, b):
    a1, a2, a3 = _split3(a)
    b1, b2, b3 = _split3(b)
    d = lambda s, t: jnp.dot(s, t, preferred_element_type=F32)
    return (d(a1, b3) + d(a3, b1) + d(a2, b2)) + (d(a1, b2) + d(a2, b1)) + d(a1, b1)


def _sigmoid(x):
    return 1.0 / (1.0 + jnp.exp(-x))


def _colsum(x):
    return jnp.sum(x, axis=0, keepdims=True)


def _my_pos():
    x, y, c = lax.axis_index("x"), lax.axis_index("y"), lax.axis_index("c")
    return x, y, c, 4 * x + 2 * y + c


def _peer(x, y, c, d):
    px = (1 - x) if d & 4 else x
    py = (1 - y) if d & 2 else y
    pc = (1 - c) if d & 1 else c
    return (px, py, pc), 4 * px + 2 * py + pc


def _matmul(a, b, form, out_dtype, name, tm=1024, tn=1024, tk=1024, scatter=(), relu2=False, relu_of=None,
            b_slots=False, out_slots=False, more=()):
    width = None
    if b_slots:
        assert form in ("nn", "nt") and b.shape[0] == N_DEV
        width = b.shape[2]
        if form == "nn":
            (M, K), N, tn = a.shape, N_DEV * width, 2 * width
        else:
            (M, K), N, tk = a.shape, b.shape[1], 2 * width
    elif form == "nn":
        (M, K), N = a.shape, b.shape[1]
    elif form == "nt":
        (M, K), N = a.shape, b.shape[0]
    else:
        (K, M), N = a.shape, b.shape[1]
    if out_slots:
        width = N // N_DEV
        tn = 2 * width
    tm, tn, tk = min(tm, M), min(tn, N), min(tk, K)
    assert M % tm == 0 and N % tn == 0 and K % tk == 0, (name, M, N, K)
    nk = K // tk
    if form == "tn":
        a_spec = pl.BlockSpec((tk, tm), lambda i, j, k: (k, i))
        dn = (((0,), (0,)), ((), ()))
    else:
        a_spec = pl.BlockSpec((tm, tk), lambda i, j, k: (i, k))
        dn = (((1,), (1 if form == "nt" else 0,)), ((), ()))
    if b_slots and form == "nn":
        b_spec = pl.BlockSpec((2, tk, width), lambda i, j, k: (j, k, 0))
    elif b_slots:
        b_spec = pl.BlockSpec((2, tn, width), lambda i, j, k: (k, j, 0))
    elif form == "nt":
        b_spec = pl.BlockSpec((tn, tk), lambda i, j, k: (j, k))
    else:
        b_spec = pl.BlockSpec((tk, tn), lambda i, j, k: (k, j))

    pairs = [(a, b)] + list(more)
    seg = [0]
    for a_s, _ in pairs:
        k_s = K if len(pairs) == 1 else a_s.shape[1]
        assert k_s % tk == 0 and (len(pairs) == 1 or (form == "nt" and not b_slots))
        seg.append(seg[-1] + k_s // tk)
    nk = seg[-1]
    specs_more = []
    for s in range(1, len(pairs)):
        lo_k, n_k = seg[s], seg[s + 1] - seg[s]
        kk = lambda k, lo_k=lo_k, n_k=n_k: jnp.clip(k - lo_k, 0, n_k - 1)
        specs_more += [pl.BlockSpec((tm, tk), lambda i, j, k, kk=kk: (i, kk(k))),
                       pl.BlockSpec((tn, tk), lambda i, j, k, kk=kk: (j, kk(k)))]
    if len(pairs) > 1:
        n0 = seg[1]
        a_spec = pl.BlockSpec((tm, tk), lambda i, j, k: (i, jnp.minimum(k, n0 - 1)))
        b_spec = pl.BlockSpec((tn, tk), lambda i, j, k: (j, jnp.minimum(k, n0 - 1)))

    nx = len(scatter)
    ne = 0 if relu_of is None else 1
    no = 2 if relu2 else 1
    nm = 2 * (len(pairs) - 1)
    grid = (M // tm, N // tn, nk)

    def body(a_ref, b_ref, *rest):
        ab_refs = [(a_ref, b_ref)] + [(rest[2 * s], rest[2 * s + 1]) for s in range(len(pairs) - 1)]
        rest = rest[nm:]
        e_ref = rest[0] if ne else None
        x_in = rest[ne:ne + nx]
        o_refs = rest[ne + nx:ne + nx + no]
        x_out = rest[ne + nx + no:ne + 2 * nx + no]
        scr = rest[ne + 2 * nx + no:]
        k = pl.program_id(2)
        if nx:
            first, last = _first_last(grid)

            @pl.when(first)
            def _():
                _xchg(x_in, x_out, scr[-3:], False, wait=False)

        def finish(val):
            if out_slots:
                o_refs[0][0] = val[:, 0:width].astype(out_dtype)
                o_refs[0][1] = val[:, width:2 * width].astype(out_dtype)
            elif relu2:
                o_refs[0][...] = val.astype(out_dtype)
                r = jnp.maximum(val, 0.0)
                o_refs[1][...] = (r * r).astype(out_dtype)
            elif ne:
                o_refs[0][...] = (val * (2.0 * jnp.maximum(e_ref[...].astype(F32), 0.0))).astype(out_dtype)
            else:
                o_refs[0][...] = val.astype(out_dtype)

        def accumulate(ar, br):
            dot = lambda u, w: lax.dot_general(u.astype(BF), w.astype(BF), dn, preferred_element_type=F32)
            if b_slots and form == "nn":
                part = jnp.concatenate([dot(ar[...], br[0]), dot(ar[...], br[1])], axis=1)
            elif b_slots:
                part = dot(ar[:, 0:width], br[0]) + dot(ar[:, width:2 * width], br[1])
            else:
                part = dot(ar[...], br[...])
            if nk == 1:
                finish(part)
            else:
                acc = scr[0]

                @pl.when(k == 0)
                def _():
                    acc[...] = part

                @pl.when(k > 0)
                def _():
                    acc[...] += part

        if len(pairs) == 1:
            accumulate(a_ref, b_ref)
        else:
            for s, (ar, br) in enumerate(ab_refs):
                @pl.when((k >= seg[s]) & (k < seg[s + 1]))
                def _(ar=ar, br=br):
                    accumulate(ar, br)

        if nk > 1:
            @pl.when(k == nk - 1)
            def _():
                finish(scr[0][...])

        if nx:
            @pl.when(last)
            def _():
                _xchg(x_in, x_out, scr[-3:], False, wait=True)

    x_specs, x_shapes, x_scratch = _xchg_parts(scatter) if nx else ([], [], [])
    sem = ("arbitrary",) * 3 if nx else ("parallel", "parallel", "arbitrary")
    o_spec = pl.BlockSpec((tm, tn), lambda i, j, k: (i, j))
    o_shape = _sds((M, N), out_dtype)
    if out_slots:
        o_spec = pl.BlockSpec((2, tm, width), lambda i, j, k: (j, i, 0))
        o_shape = _sds((N_DEV, M, width), out_dtype)
    res = pl.pallas_call(
        body, name=name, grid=grid,
        in_specs=[a_spec, b_spec] + specs_more + [o_spec] * ne + x_specs,
        out_specs=[o_spec] * no + x_specs,
        out_shape=[o_shape] * no + x_shapes,
        scratch_shapes=([] if nk == 1 else [pltpu.VMEM((tm, tn), F32)]) + x_scratch,
        compiler_params=_cp(sem),
    )(a, b, *[t for p in more for t in p], *([relu_of] if ne else []), *scatter)
    return res if (nx or relu2) else res[0]


def _rows_call(body, name, n_tiles, ins, outs, scratch=(), aliases=None):
    res = pl.pallas_call(
        body, name=name, grid=(n_tiles,),
        in_specs=[s for _, s in ins],
        out_specs=[s for _, s in outs],
        out_shape=[o for o, _ in outs],
        scratch_shapes=list(scratch),
        input_output_aliases=aliases or {},
        compiler_params=_cp(("arbitrary",)),
    )(*[a for a, _ in ins])
    return res


def _rspec(tm, width, cb=0, rev_n=None):
    if rev_n is None:
        return pl.BlockSpec((tm, width), lambda i: (i, cb))
    return pl.BlockSpec((tm, width), lambda i: (rev_n - 1 - i, cb))


def _row_out(T, tm, width, dtype, rev_n=None):
    return (_sds((T, width), dtype), _rspec(tm, width, 0, rev_n))


def _acc_out(shape, dtype=F32):
    return (_sds(shape, dtype), _fullshape(shape))


def _mod_parts(mod):
    return [mod[:, i * D:(i + 1) * D] for i in range(N_ADA)]


def _pre_in(x, mod, n1g):
    T = x.shape[0]
    tm = TM_ROWS

    def body(x_ref, mod_ref, g_ref, h_ref):
        sh1, sc1 = mod_ref[:, 0:D], mod_ref[:, D:2 * D]
        xv = x_ref[...]
        r = lax.rsqrt(jnp.mean(xv * xv, axis=-1, keepdims=True) + EPS)
        h_ref[...] = ((xv * r) * g_ref[...] * (1.0 + sc1) + sh1).astype(BF)

    return _rows_call(body, "pre_in", T // tm,
                      [(x, _rspec(tm, D)), (mod, _full(mod)), (n1g, _full(n1g))],
                      [_row_out(T, tm, D, BF)])[0]


def _seg_mat():
    r = jnp.arange(LANES)[:, None] // HEAD_DIM
    c = jnp.arange(LANES)[None, :] // HEAD_DIM
    return jnp.where(r == c, 1.0 / HEAD_DIM, 0.0).astype(BF)


def _tri_mat(n, upper):
    r = jnp.arange(n)[:, None]
    c = jnp.arange(n)[None, :]
    return jnp.where((r <= c) if upper else (r >= c), 1.0, 0.0).astype(BF)


def _log_sigmoid(z):
    return jnp.minimum(z, 0.0) - jnp.log(1.0 + jnp.exp(-jnp.abs(z)))


def _qkv_post(proj, f, qg2, kg2, bf_pad):
    T = proj.shape[0]
    tm = TM_ROWS
    seg = _seg_mat()
    tri = _tri_mat(tm, True)

    def body(q_ref, k_ref, v_ref, f_ref, qg_ref, kg_ref, bf_ref, seg_ref, tri_ref,
             qo_ref, ko_ref, vo_ref, fc_ref, carry_ref):
        i = pl.program_id(0)

        @pl.when(i == 0)
        def _():
            carry_ref[...] = jnp.zeros_like(carry_ref)

        segm = seg_ref[...]
        for j in range(D // LANES):
            sl = slice(j * LANES, (j + 1) * LANES)
            qc = q_ref[:, sl].astype(F32)
            rq = lax.rsqrt(_dot_exact(qc * qc, segm) + EPS)
            qo_ref[:, sl] = ((qc * rq) * qg_ref[...] * (QK_SCALE * LOG2E)).astype(BF)
            kc = k_ref[:, sl].astype(F32)
            rk = lax.rsqrt(_dot_exact(kc * kc, segm) + EPS)
            ko_ref[:, sl] = ((kc * rk) * kg_ref[...]).astype(BF)
        vo_ref[...] = v_ref[...].astype(BF)
        lf = _log_sigmoid(f_ref[...] + bf_ref[...])
        lft = lf.T[0:N_HEADS, :]
        carry = carry_ref[:, 0:1]
        fc_ref[...] = _dot_exact(lft, tri_ref[...]) + carry
        carry_ref[...] = jnp.broadcast_to(carry + jnp.sum(lft, axis=1, keepdims=True), carry_ref.shape)

    outs = [_row_out(T, tm, D, BF), _row_out(T, tm, D, BF), _row_out(T, tm, D, BF),
            (_sds((N_HEADS, T), F32), pl.BlockSpec((N_HEADS, tm), lambda i: (0, i)))]
    ins = [(proj, _rspec(tm, D, 0)), (proj, _rspec(tm, D, 1)), (proj, _rspec(tm, D, 2)), (f, _rspec(tm, LANES)),
           (qg2, _full(qg2)), (kg2, _full(kg2)), (bf_pad, _full(bf_pad)), (seg, _full(seg)), (tri, _full(tri))]
    return _rows_call(body, "qkv_post", T // tm, ins, outs, [pltpu.VMEM((N_HEADS, LANES), F32)])


def _lane_lo():
    return lax.broadcasted_iota(jnp.int32, (1, LANES), 1) < HEAD_DIM


def _nt(a, b):
    return lax.dot_general(a, b, (((1,), (1,)), ((), ())), preferred_element_type=F32)


def _tn(a, b):
    return lax.dot_general(a, b, (((0,), (0,)), ((), ())), preferred_element_type=F32)


def _head_rep(x, lo):
    rolled = pltpu.roll(x, HEAD_DIM, axis=1)
    return jnp.where(lo, x, rolled), jnp.where(lo, rolled, x)


def _diag_mask(t):
    return lax.broadcasted_iota(jnp.int32, (t, t), 1) <= lax.broadcasted_iota(jnp.int32, (t, t), 0)


def _first_last(grid):
    ids = [pl.program_id(a) for a in range(len(grid))]
    first = functools.reduce(jnp.logical_and, [i == 0 for i in ids])
    last = functools.reduce(jnp.logical_and, [i == g - 1 for i, g in zip(ids, grid)])
    return first, last


def _flash_fwd(q, k, v, fc3, shards):
    T = q.shape[0]
    tq = TQ
    nq = T // tq
    hp_n = N_HEADS // 2
    rep = tq // LANES
    nx = len(shards)
    grid = (hp_n, nq, nq)

    def body(q_ref, k_ref, v_ref, fk_ref, fq_ref, *rest):
        x_in, (o_ref, lse_ref), x_out = rest[:nx], rest[nx:nx + 2], rest[nx + 2:2 * nx + 2]
        acc_ref, m_ref = rest[2 * nx + 2:2 * nx + 4]
        sems = rest[2 * nx + 4:]
        qi, ki = pl.program_id(1), pl.program_id(2)
        first, last = _first_last(grid)

        @pl.when(first)
        def _():
            _xchg(x_in, x_out, sems, True, wait=False)

        @pl.when(ki == 0)
        def _():
            acc_ref[...] = jnp.zeros_like(acc_ref)
            m_ref[...] = jnp.full_like(m_ref, NEG)

        lane = lax.broadcasted_iota(jnp.int32, (1, LANES), 1)
        sum_lane = (HEAD_DIM, 0)

        def step(diag):
            lo = _lane_lo()
            q2, k2, v2 = q_ref[...], k_ref[...], v_ref[...]
            zero = jnp.zeros_like(k2)
            bias = (fq_ref[:, 0:1] - fk_ref[...]) * LOG2E
            for hh in range(2):
                sel = (lambda t: jnp.where(lo, t, zero)) if hh == 0 else (lambda t: jnp.where(lo, zero, t))
                ones = jnp.where(lane == sum_lane[hh], 1.0, 0.0).astype(BF)
                v_aug = jnp.where(lo, v2, ones) if hh == 0 else jnp.where(lo, ones, v2)
                s = _nt(sel(q2), k2) + bias[hh:hh + 1, :]
                if diag:
                    s = jnp.where(_diag_mask(tq), s, NEG)
                m_old = m_ref[hh]
                m_new = jnp.maximum(m_old, jnp.max(s, axis=-1, keepdims=True))
                alpha = jnp.exp2(m_old - m_new)
                p = jnp.exp2(s - jnp.tile(m_new, (1, rep)))
                m_ref[hh] = m_new
                acc_ref[hh] = acc_ref[hh] * alpha + jnp.dot(p.astype(BF), v_aug, preferred_element_type=F32)

        @pl.when(ki < qi)
        def _():
            step(False)

        @pl.when(ki == qi)
        def _():
            step(True)
            lo = _lane_lo()
            acc_a, acc_b = acc_ref[0], acc_ref[1]
            la = jnp.broadcast_to(acc_a[:, sum_lane[0]:sum_lane[0] + 1], (tq, LANES))
            lb = jnp.broadcast_to(acc_b[:, sum_lane[1]:sum_lane[1] + 1], (tq, LANES))
            o_ref[...] = jnp.where(lo, acc_a / la, acc_b / lb)
            lse_ref[...] = jnp.where(lo, m_ref[0] + jnp.log(la) * LOG2E, m_ref[1] + jnp.log(lb) * LOG2E)

        @pl.when(last)
        def _():
            _xchg(x_in, x_out, sems, True, wait=True)

    qspec = pl.BlockSpec((tq, LANES), lambda h, i, j: (i, h))
    kspec = pl.BlockSpec((tq, LANES), lambda h, i, j: (jnp.minimum(i, j), h))
    fkspec = pl.BlockSpec((None, 2, tq), lambda h, i, j: (h, 0, jnp.minimum(i, j)))
    fqspec = pl.BlockSpec((None, 2, tq), lambda h, i, j: (h, 0, i))
    x_specs, x_shapes, x_scratch = _xchg_parts(shards)
    return pl.pallas_call(
        body, name="attn_fwd", grid=grid,
        in_specs=[qspec, kspec, kspec, fkspec, fqspec] + x_specs,
        out_specs=[qspec, qspec] + x_specs,
        out_shape=[_sds((T, D), F32), _sds((T, D), F32)] + x_shapes,
        scratch_shapes=[pltpu.VMEM((2, tq, LANES), F32), pltpu.VMEM((2, tq, LANES), F32)] + x_scratch,
        compiler_params=_cp(("arbitrary", "arbitrary", "arbitrary")),
    )(q, k, v, fc3, fc3, *shards)


def _attn_delta(do, o):
    T = o.shape[0]
    tm = TM_ROWS
    ones = (_seg_mat().astype(F32) * HEAD_DIM).astype(BF)

    def body(do_ref, o_ref, seg_ref, dl_ref):
        segm = seg_ref[...]
        for j in range(D // LANES):
            sl = slice(j * LANES, (j + 1) * LANES)
            dl_ref[:, sl] = _dot_exact(do_ref[:, sl].astype(BF).astype(F32) * o_ref[:, sl], segm)

    ins = [(do, _rspec(tm, D)), (o, _rspec(tm, D)), (ones, _full(ones))]
    return _rows_call(body, "attn_delta", T // tm, ins, [_row_out(T, tm, D, F32)])[0]


def _flash_bwd(q, k, v, do, lse, delta, fc3, parts):
    T = q.shape[0]
    tq = TQ
    nq = T // tq
    hp_n = N_HEADS // 2
    rep = tq // LANES
    nx = len(parts)
    grid = (hp_n, nq, nq)

    def body(q_ref, k_ref, v_ref, do_ref, lse_ref, dl_ref, fk_ref, fq_ref, *rest):
        x_in, x_out = rest[:nx], rest[nx + 6:2 * nx + 6]
        dq_ref, ra_ref, rb_ref, dk_ref, dv_ref, dfc_ref = rest[nx:nx + 6]
        dk_acc, dv_acc, df_acc = rest[2 * nx + 6:2 * nx + 9]
        sems = rest[2 * nx + 9:]
        ki, qi = pl.program_id(1), pl.program_id(2)
        first, last = _first_last(grid)
        qrows = pl.ds(pl.multiple_of(qi * tq, tq), tq)

        @pl.when(first)
        def _():
            _xchg(x_in, x_out, sems, False, wait=False)

        @pl.when((ki == 0) & (qi == 0))
        def _():
            dq_ref[...] = jnp.zeros_like(dq_ref)
            ra_ref[...] = jnp.zeros_like(ra_ref)
            rb_ref[...] = jnp.zeros_like(rb_ref)

        @pl.when(qi == 0)
        def _():
            dk_acc[...] = jnp.zeros_like(dk_acc)
            dv_acc[...] = jnp.zeros_like(dv_acc)
            df_acc[...] = jnp.zeros_like(df_acc)

        def step(diag):
            lo = _lane_lo()
            q2, k2, v2 = q_ref[...], k_ref[...], v_ref[...]
            do2 = do_ref[...].astype(BF)
            zero = jnp.zeros_like(q2)
            bias = (fq_ref[:, 0:1] - fk_ref[...]) * LOG2E
            lses = _head_rep(lse_ref[...], lo)
            dls = _head_rep(dl_ref[...], lo)
            dk_t = None
            dv_t = None
            dq_t = None
            for hh in range(2):
                sel = (lambda t: jnp.where(lo, t, zero)) if hh == 0 else (lambda t: jnp.where(lo, zero, t))
                s = _nt(sel(q2), k2) + bias[hh:hh + 1, :]
                if diag:
                    s = jnp.where(_diag_mask(tq), s, NEG)
                p = jnp.exp2(s - jnp.tile(lses[hh], (1, rep)))
                dp = _nt(sel(do2), v2)
                ds = p * (dp - jnp.tile(dls[hh], (1, rep)))
                ds_b = ds.astype(BF)
                dvp = _tn(p.astype(BF), sel(do2))
                dkp = _tn(ds_b, sel(q2))
                dqp = jnp.dot(ds_b, sel(k2), preferred_element_type=F32)
                dv_t = dvp if dv_t is None else dv_t + dvp
                dk_t = dkp if dk_t is None else dk_t + dkp
                dq_t = dqp if dq_t is None else dq_t + dqp
                df_acc[hh:hh + 1, :] -= _colsum(ds)
                r_ref = ra_ref if hh == 0 else rb_ref
                r_ref[qrows, :] += jnp.sum(ds, axis=-1, keepdims=True)
            dk_acc[...] += dk_t
            dv_acc[...] += dv_t
            dq_ref[qrows, :] += dq_t * QK_SCALE

        @pl.when(qi > ki)
        def _():
            step(False)

        @pl.when(qi == ki)
        def _():
            step(True)

        @pl.when(qi == nq - 1)
        def _():
            dk_ref[...] = dk_acc[...] * LN2
            dv_ref[...] = dv_acc[...]
            dfc_ref[...] = df_acc[...]

        @pl.when(last)
        def _():
            _xchg(x_in, x_out, sems, False, wait=True)

    kspec = pl.BlockSpec((tq, LANES), lambda h, j, i: (j, h))
    qspec = pl.BlockSpec((tq, LANES), lambda h, j, i: (jnp.maximum(i, j), h))
    fkspec = pl.BlockSpec((None, 2, tq), lambda h, j, i: (h, 0, j))
    fqspec = pl.BlockSpec((None, 2, tq), lambda h, j, i: (h, 0, jnp.maximum(i, j)))
    x_specs, x_shapes, x_scratch = _xchg_parts(parts)
    dqspec = pl.BlockSpec((T, LANES), lambda h, j, i: (0, h))
    rspec = pl.BlockSpec((None, T, 1), lambda h, j, i: (h, 0, 0))
    return pl.pallas_call(
        body, name="attn_bwd", grid=grid,
        in_specs=[qspec, kspec, kspec, qspec, qspec, qspec, fkspec, fqspec] + x_specs,
        out_specs=[dqspec, rspec, rspec, kspec, kspec, fkspec] + x_specs,
        out_shape=[_sds((T, D), F32), _sds((hp_n, T, 1), F32), _sds((hp_n, T, 1), F32),
                   _sds((T, D), F32), _sds((T, D), F32), _sds((hp_n, 2, T), F32)] + x_shapes,
        scratch_shapes=[pltpu.VMEM((tq, LANES), F32), pltpu.VMEM((tq, LANES), F32), pltpu.VMEM((2, tq), F32)] + x_scratch,
        compiler_params=_cp(("arbitrary", "arbitrary", "arbitrary")),
    )(q, k, v, do, lse, delta, fc3, fc3, *parts)


def _layer_norm_stats(u1):
    mu = jnp.mean(u1, axis=-1, keepdims=True)
    xc = u1 - mu
    rstd = lax.rsqrt(jnp.mean(xc * xc, axis=-1, keepdims=True) + EPS)
    return xc * rstd, rstd


def _shifted_copies(buf, sh, tm):
    rows = tm + HALO - SUBLANES
    for b in range(1, SUBLANES):
        sh[b - 1, 0:rows, :] = buf[b:b + rows, :]


def _window(buf, sh, off, rows, sl):
    a8, b = off // SUBLANES * SUBLANES, off % SUBLANES
    return buf[a8:a8 + rows, sl] if b == 0 else sh[b - 1, a8:a8 + rows, sl]


def _conv_fwd(proj, cw, cb, lng, lnb):
    T = proj.shape[0]
    tm = CONV_TM

    def body(a_ref, b_ref, w_ref, cb_ref, g_ref, bb_ref, u0_ref, u1_ref, u3_ref, buf, sh):
        i = pl.program_id(0)

        @pl.when(i == 0)
        def _():
            buf[0:HALO, :] = jnp.zeros((HALO, D), F32)

        u0 = a_ref[...].astype(F32) * _sigmoid(b_ref[...].astype(F32))
        u0_ref[...] = u0
        buf[HALO:HALO + tm, :] = u0
        _shifted_copies(buf, sh, tm)
        for j in range(D // LANES):
            sl = slice(j * LANES, (j + 1) * LANES)
            for r0 in range(0, tm, CONV_ROWS):
                acc = jnp.broadcast_to(cb_ref[:, sl], (CONV_ROWS, LANES))
                for kk in range(CONV_K):
                    acc = acc + w_ref[kk:kk + 1, sl] * _window(buf, sh, r0 + HALO - (CONV_K - 1) + kk, CONV_ROWS, sl)
                u1_ref[r0:r0 + CONV_ROWS, sl] = acc
        buf[0:HALO, :] = buf[tm:tm + HALO, :]
        xh, _ = _layer_norm_stats(u1_ref[...])
        u2 = xh * g_ref[...] + bb_ref[...]
        u3_ref[...] = (u2 * _sigmoid(u2)).astype(BF)

    ins = [(proj, _rspec(tm, D, 0)), (proj, _rspec(tm, D, 1)), (cw, _full(cw)), (cb, _full(cb)),
           (lng, _full(lng)), (lnb, _full(lnb))]
    outs = [_row_out(T, tm, D, F32), _row_out(T, tm, D, F32), _row_out(T, tm, D, BF)]
    return _rows_call(body, "conv_fwd", T // tm, ins, outs,
                      [pltpu.VMEM((tm + HALO, D), F32), pltpu.VMEM((SUBLANES - 1, tm + HALO, D), F32)])


def _conv_bwd(du3, u1, u0, proj, cw, lng, lnb, dgg):
    T = du3.shape[0]
    tm = CONV_TM
    n = T // tm
    per = tm // HALO

    def body(du3_ref, u1_ref, u0_ref, halo_ref, a_ref, b_ref, w_ref, g_ref, bb_ref, dgg_in_ref,
             dgl_ref, dg_ref, dbb_ref, dcb_ref, dw_ref, dbuf, ubuf, du0_buf, dsh, ush, dw8):
        i = pl.program_id(0)
        r = n - 1 - i

        @pl.when(i == 0)
        def _():
            dbuf[tm:tm + HALO, :] = jnp.zeros((HALO, D), F32)
            dg_ref[...] = jnp.zeros_like(dg_ref)
            dbb_ref[...] = jnp.zeros_like(dbb_ref)
            dcb_ref[...] = jnp.zeros_like(dcb_ref)
            dw8[...] = jnp.zeros_like(dw8)

        xh, rstd = _layer_norm_stats(u1_ref[...])
        g = g_ref[...]
        u2 = xh * g + bb_ref[...]
        s2 = _sigmoid(u2)
        du2 = du3_ref[...] * (s2 * (1.0 + u2 * (1.0 - s2)))
        dg_ref[...] += _colsum(du2 * xh)
        dbb_ref[...] += _colsum(du2)
        dxh = du2 * g
        du1 = rstd * (dxh - jnp.mean(dxh, axis=-1, keepdims=True) - xh * jnp.mean(dxh * xh, axis=-1, keepdims=True))
        dcb_ref[...] += _colsum(du1)
        dbuf[0:tm, :] = du1
        ubuf[HALO:HALO + tm, :] = u0_ref[...]
        ubuf[0:HALO, :] = jnp.where(r > 0, halo_ref[...], 0.0)
        _shifted_copies(dbuf, dsh, tm)
        _shifted_copies(ubuf, ush, tm)
        for j in range(D // LANES):
            sl = slice(j * LANES, (j + 1) * LANES)
            for r0 in range(0, tm, CONV_ROWS):
                d1 = dbuf[r0:r0 + CONV_ROWS, sl]
                acc = jnp.zeros((CONV_ROWS, LANES), F32)
                for kk in range(CONV_K):
                    acc = acc + w_ref[kk:kk + 1, sl] * _window(dbuf, dsh, r0 + CONV_K - 1 - kk, CONV_ROWS, sl)
                    prod = d1 * _window(ubuf, ush, r0 + HALO - (CONV_K - 1) + kk, CONV_ROWS, sl)
                    dw8[kk * SUBLANES:(kk + 1) * SUBLANES, sl] += prod.reshape(
                        CONV_ROWS // SUBLANES, SUBLANES, LANES).sum(axis=0)
                du0_buf[r0:r0 + CONV_ROWS, sl] = acc
        dbuf[tm:tm + HALO, :] = dbuf[0:HALO, :]
        du0 = du0_buf[...]
        af, bfl = a_ref[...].astype(F32), b_ref[...].astype(F32)
        sb = _sigmoid(bfl)
        dgl_ref[:, 0:D] = (du0 * sb).astype(BF)
        dgl_ref[:, D:2 * D] = (du0 * af * sb * (1.0 - sb)).astype(BF)

        @pl.when(i == n - 1)
        def _():
            for kk in range(CONV_KP):
                dw_ref[kk:kk + 1, :] = _colsum(dw8[kk * SUBLANES:(kk + 1) * SUBLANES, :])

    rs = lambda cb: _rspec(tm, D, cb, n)
    halo_spec = pl.BlockSpec((HALO, D), lambda i: (jnp.maximum((n - 1 - i) * per - 1, 0), 0))
    ins = [(du3, rs(0)), (u1, rs(0)), (u0, rs(0)), (u0, halo_spec), (proj, rs(0)), (proj, rs(1)),
           (cw, _full(cw)), (lng, _full(lng)), (lnb, _full(lnb)), (dgg, pl.BlockSpec(memory_space=pl.ANY))]
    outs = [(_sds(dgg.shape, dgg.dtype), _rspec(tm, 2 * D, 0, n)),
            _acc_out((1, D)), _acc_out((1, D)), _acc_out((1, D)), _acc_out((CONV_KP, D))]
    shifted = pltpu.VMEM((SUBLANES - 1, tm + HALO, D), F32)
    return _rows_call(body, "conv_bwd", n, ins, outs,
                      [pltpu.VMEM((tm + HALO, D), F32), pltpu.VMEM((tm + HALO, D), F32), pltpu.VMEM((tm, D), F32),
                       shifted, shifted, pltpu.VMEM((CONV_KP * SUBLANES, D), F32)], aliases={len(ins) - 1: 0})


def _merge(ba, bb, proj):
    T = ba.shape[0]
    tm = TM_ROWS

    def body(ba_ref, bb_ref, ga_ref, gb_ref, o_ref):
        sa, sb = _sigmoid(ga_ref[...].astype(F32)), _sigmoid(gb_ref[...].astype(F32))
        o_ref[...] = (sa * ba_ref[...] + sb * bb_ref[...]).astype(BF)

    ins = [(ba, _rspec(tm, D)), (bb, _rspec(tm, D)), (proj, _rspec(tm, D, 2)), (proj, _rspec(tm, D, 3))]
    return _rows_call(body, "merge", T // tm, ins, [_row_out(T, tm, D, BF)])[0]


def _post_out(x, mo, mod, n2g):
    T = x.shape[0]
    tm = TM_ROWS

    def body(x_ref, mo_ref, mod_ref, g_ref, x1_ref, h2_ref):
        g1 = mod_ref[:, 2 * D:3 * D]
        sh2, sc2 = mod_ref[:, 3 * D:4 * D], mod_ref[:, 4 * D:5 * D]
        x1 = x_ref[...] + g1 * mo_ref[...]
        x1_ref[...] = x1
        r = lax.rsqrt(jnp.mean(x1 * x1, axis=-1, keepdims=True) + EPS)
        h2_ref[...] = ((x1 * r) * g_ref[...] * (1.0 + sc2) + sh2).astype(BF)

    ins = [(x, _rspec(tm, D)), (mo, _rspec(tm, D)), (mod, _full(mod)), (n2g, _full(n2g))]
    return _rows_call(body, "post_out", T // tm, ins, [_row_out(T, tm, D, F32), _row_out(T, tm, D, BF)])


def _loss_head(x1, m2, tgt, mod):
    T = x1.shape[0]
    tm = TM_ROWS

    def body(x1_ref, m2_ref, t_ref, mod_ref, dy_ref, dm2_ref, dg2_ref, sq_ref):
        i = pl.program_id(0)

        @pl.when(i == 0)
        def _():
            dg2_ref[...] = jnp.zeros_like(dg2_ref)
            sq_ref[...] = jnp.zeros_like(sq_ref)

        g2 = mod_ref[:, 5 * D:6 * D]
        m2 = m2_ref[...]
        err = x1_ref[...] + g2 * m2 - t_ref[...]
        dy = err * (1.0 / D)
        dy_ref[...] = dy
        dm2_ref[...] = (g2 * dy).astype(BF)
        dg2_ref[...] += _colsum(dy * m2)
        sq_ref[...] += _colsum(err * err)

    ins = [(x1, _rspec(tm, D)), (m2, _rspec(tm, D)), (tgt, _rspec(tm, D)), (mod, _full(mod))]
    outs = [_row_out(T, tm, D, F32), _row_out(T, tm, D, BF), _acc_out((1, D)), _acc_out((1, D))]
    return _rows_call(body, "loss_head", T // tm, ins, outs)


def _norm2_bwd(dh2, x1, dy, mo, mod, n2g):
    T = x1.shape[0]
    tm = TM_ROWS

    def body(dh_ref, x1_ref, dy_ref, mo_ref, mod_ref, g_ref, dx1_ref, dmo_ref, dsh_ref, dsc_ref, dg_ref, dg1_ref):
        i = pl.program_id(0)

        @pl.when(i == 0)
        def _():
            for r in (dsh_ref, dsc_ref, dg_ref, dg1_ref):
                r[...] = jnp.zeros_like(r)

        g1, sc2 = mod_ref[:, 2 * D:3 * D], mod_ref[:, 4 * D:5 * D]
        g = g_ref[...]
        x1 = x1_ref[...]
        dh = dh_ref[...]
        r = lax.rsqrt(jnp.mean(x1 * x1, axis=-1, keepdims=True) + EPS)
        xn = x1 * r
        dsh_ref[...] += _colsum(dh)
        dsc_ref[...] += _colsum(dh * xn * g)
        dg_ref[...] += _colsum(dh * xn * (1.0 + sc2))
        dxn = dh * g * (1.0 + sc2)
        dx1 = dy_ref[...] + r * (dxn - xn * jnp.mean(dxn * xn, axis=-1, keepdims=True))
        dx1_ref[...] = dx1
        dg1_ref[...] += _colsum(dx1 * mo_ref[...])
        dmo_ref[...] = (g1 * dx1).astype(BF)

    ins = [(dh2, _rspec(tm, D)), (x1, _rspec(tm, D)), (dy, _rspec(tm, D)), (mo, _rspec(tm, D)),
           (mod, _full(mod)), (n2g, _full(n2g))]
    outs = [_row_out(T, tm, D, F32), _row_out(T, tm, D, BF)] + [_acc_out((1, D)) for _ in range(4)]
    return _rows_call(body, "norm2_bwd", T // tm, ins, outs)


def _gate_bwd(dmerged, ba, bb, proj):
    T = ba.shape[0]
    tm = TM_ROWS

    def body(dm_ref, ba_ref, bb_ref, ga_ref, gb_ref, dba_ref, dbb_ref, dgt_ref):
        dm = dm_ref[...]
        sa, sb = _sigmoid(ga_ref[...].astype(F32)), _sigmoid(gb_ref[...].astype(F32))
        dba_ref[...] = (dm * sa).astype(BF)
        dbb_ref[...] = (dm * sb).astype(BF)
        dgt_ref[:, 0:D] = (dm * ba_ref[...] * sa * (1.0 - sa)).astype(BF)
        dgt_ref[:, D:2 * D] = (dm * bb_ref[...] * sb * (1.0 - sb)).astype(BF)

    ins = [(dmerged, _rspec(tm, D)), (ba, _rspec(tm, D)), (bb, _rspec(tm, D)),
           (proj, _rspec(tm, D, 2)), (proj, _rspec(tm, D, 3))]
    outs = [_row_out(T, tm, D, BF), _row_out(T, tm, D, BF), (_sds((T, 4 * D), BF), _rspec(tm, 2 * D, 1))]
    return _rows_call(body, "gate_bwd", T // tm, ins, outs)


def _qkv_bwd(dq, dk, dv, proj, f, dfc, dfq, qg2, kg2, bf_pad):
    T = proj.shape[0]
    tm = TM_ROWS
    n = T // tm
    seg = _seg_mat()
    tri = _tri_mat(tm, False)

    def body(dq_ref, dk_ref, dv_ref, q_ref, k_ref, f_ref, dfc_ref, dfq_ref, qg_ref, kg_ref, bf_ref, seg_ref, tri_ref,
             dqkv_ref, dfo_ref, dqg_ref, dkg_ref, dbf_ref, carry_ref):
        i = pl.program_id(0)

        @pl.when(i == 0)
        def _():
            carry_ref[...] = jnp.zeros_like(carry_ref)
            dqg_ref[...] = jnp.zeros_like(dqg_ref)
            dkg_ref[...] = jnp.zeros_like(dkg_ref)
            dbf_ref[...] = jnp.zeros_like(dbf_ref)

        segm = seg_ref[...]
        dqg = jnp.zeros((1, LANES), F32)
        dkg = jnp.zeros((1, LANES), F32)
        for j in range(D // LANES):
            sl = slice(j * LANES, (j + 1) * LANES)
            for (raw_ref, d_ref, gn_ref, which) in ((q_ref, dq_ref, qg_ref, 0), (k_ref, dk_ref, kg_ref, 1)):
                xc = raw_ref[:, sl].astype(F32)
                rr = lax.rsqrt(_dot_exact(xc * xc, segm) + EPS)
                xn = xc * rr
                dc = d_ref[:, sl]
                if which == 0:
                    dqg = dqg + _colsum(dc * xn)
                else:
                    dkg = dkg + _colsum(dc * xn)
                dxn = dc * gn_ref[...]
                osl = slice(which * D + j * LANES, which * D + (j + 1) * LANES)
                dqkv_ref[:, osl] = (rr * (dxn - xn * _dot_exact(dxn * xn, segm))).astype(BF)
        dqg_ref[...] += dqg
        dkg_ref[...] += dkg
        dqkv_ref[:, 2 * D:3 * D] = dv_ref[...].astype(BF)
        z = f_ref[...] + bf_ref[...]
        sneg_t = _sigmoid(-z).T[0:N_HEADS, :]
        dfc = dfc_ref[...] + dfq_ref[...]
        carry = carry_ref[:, 0:1]
        dlf = _dot_exact(dfc, tri_ref[...]) + carry
        carry_ref[...] = jnp.broadcast_to(carry + jnp.sum(dfc, axis=1, keepdims=True), carry_ref.shape)
        dzt = dlf * sneg_t
        dz = jnp.concatenate([dzt, jnp.zeros((LANES - N_HEADS, tm), F32)], axis=0).T
        dbf_ref[...] += _colsum(dz)
        dfo_ref[...] = dz.astype(BF)

    rs = lambda w, cb=0: _rspec(tm, w, cb, n)
    ins = [(dq, rs(D)), (dk, rs(D)), (dv, rs(D)), (proj, rs(D, 0)), (proj, rs(D, 1)), (f, rs(LANES)),
           (dfc, pl.BlockSpec((N_HEADS, tm), lambda i: (0, n - 1 - i))),
           (dfq, pl.BlockSpec((N_HEADS, tm), lambda i: (0, n - 1 - i))),
           (qg2, _full(qg2)), (kg2, _full(kg2)), (bf_pad, _full(bf_pad)), (seg, _full(seg)), (tri, _full(tri))]
    outs = [_row_out(T, tm, 3 * D, BF, n), _row_out(T, tm, LANES, BF, n),
            _acc_out((1, LANES)), _acc_out((1, LANES)), _acc_out((1, LANES))]
    return _rows_call(body, "qkv_bwd", n, ins, outs, [pltpu.VMEM((N_HEADS, LANES), F32)])


def _norm1_bwd(dh, dhf, x, dx1, mod, n1g):
    T = x.shape[0]
    tm = TM_ROWS

    def body(dh_ref, dhf_ref, x_ref, dx1_ref, mod_ref, g_ref, dx_ref, dsh_ref, dsc_ref, dg_ref):
        i = pl.program_id(0)

        @pl.when(i == 0)
        def _():
            for r in (dsh_ref, dsc_ref, dg_ref):
                r[...] = jnp.zeros_like(r)

        sc1 = mod_ref[:, D:2 * D]
        g = g_ref[...]
        xv = x_ref[...]
        dh = dh_ref[...] + dhf_ref[...]
        r = lax.rsqrt(jnp.mean(xv * xv, axis=-1, keepdims=True) + EPS)
        xn = xv * r
        dsh_ref[...] += _colsum(dh)
        dsc_ref[...] += _colsum(dh * xn * g)
        dg_ref[...] += _colsum(dh * xn * (1.0 + sc1))
        dxn = dh * g * (1.0 + sc1)
        dx_ref[...] = dx1_ref[...] + r * (dxn - xn * jnp.mean(dxn * xn, axis=-1, keepdims=True))

    ins = [(dh, _rspec(tm, D)), (dhf, _rspec(tm, D)), (x, _rspec(tm, D)), (dx1, _rspec(tm, D)),
           (mod, _full(mod)), (n1g, _full(n1g))]
    outs = [_row_out(T, tm, D, F32)] + [_acc_out((1, D)) for _ in range(3)]
    return _rows_call(body, "norm1_bwd", T // tm, ins, outs)


def _adamw_math(w, g, m, v):
    m = ADAM_B1 * m + (1.0 - ADAM_B1) * g
    v = ADAM_B2 * v + (1.0 - ADAM_B2) * (g * g)
    m_hat = m / (1.0 - ADAM_B1 ** ADAM_STEP)
    v_hat = v / (1.0 - ADAM_B2 ** ADAM_STEP)
    delta = -ADAM_LR * (m_hat / (jnp.sqrt(v_hat) + ADAM_EPS) + ADAM_WD * w)
    return delta, m, v


def _adamw(parts, w, m, v, name):
    n, R, C = parts.shape
    tr = R if R <= 256 else 256
    assert R % tr == 0

    def body(p_ref, w_ref, m_ref, v_ref, g_ref, d_ref, mo_ref, vo_ref):
        g = p_ref[0].astype(F32)
        for s in range(1, n):
            g = g + p_ref[s].astype(F32)
        g_ref[...] = g
        d_ref[...], mo_ref[...], vo_ref[...] = _adamw_math(w_ref[...], g, m_ref[...], v_ref[...])

    spec = pl.BlockSpec((None, tr, C), lambda i: (0, i, 0))
    return pl.pallas_call(
        body, name=name, grid=(R // tr,),
        in_specs=[pl.BlockSpec((n, tr, C), lambda i: (0, i, 0)), spec, spec, spec],
        out_specs=[spec] * 4, out_shape=[_sds((1, R, C), F32)] * 4,
        compiler_params=_cp(("parallel",)),
    )(parts, w, m, v)


def _rcopy(src, dst, ssem, rsem, peer):
    return pltpu.make_async_remote_copy(src_ref=src, dst_ref=dst, send_sem=ssem, recv_sem=rsem,
                                        device_id=peer, device_id_type=MESH)


def _ada_fwd(c, w_ada, b_slice, cw_shard):
    def body(c_ref, w_ref, b_ref, cw_ref, mod_ref, ca_ref, cwf_ref, call, mp, ssem, rsem):
        x, y, cc, me = _my_pos()
        call[pl.ds(me, 1), :] = c_ref[...]
        cwf_ref[me] = cw_ref[...]
        first = []
        for d in range(1, N_DEV):
            peer, _ = _peer(x, y, cc, d)
            first.append(_rcopy(c_ref, call.at[pl.ds(me, 1), :], ssem.at[0, d - 1], rsem.at[0, d - 1], peer))
            first.append(_rcopy(cw_ref, cwf_ref.at[me], ssem.at[1, d - 1], rsem.at[1, d - 1], peer))
        for cp in first:
            cp.start()
        for d in range(1, N_DEV):
            peer, pid = _peer(x, y, cc, d)
            _rcopy(c_ref, call.at[pl.ds(pid, 1), :], ssem.at[0, d - 1], rsem.at[0, d - 1], peer).wait_recv()
            _rcopy(cw_ref, cwf_ref.at[pid], ssem.at[1, d - 1], rsem.at[1, d - 1], peer).wait_recv()
        cv = call[...]
        ca = cv * _sigmoid(cv)
        ca_ref[...] = ca
        mp[...] = _dot_f32(ca, w_ref[...]) + b_ref[...]
        mod_ref[pl.ds(me, 1), :] = mp[pl.ds(me, 1), :]
        second = []
        for d in range(1, N_DEV):
            peer, pid = _peer(x, y, cc, d)
            second.append(_rcopy(mp.at[pl.ds(pid, 1), :], mod_ref.at[pl.ds(me, 1), :], ssem.at[2, d - 1], rsem.at[2, d - 1], peer))
        for cp in second:
            cp.start()
        for d in range(1, N_DEV):
            peer, pid = _peer(x, y, cc, d)
            _rcopy(mp.at[pl.ds(pid, 1), :], mod_ref.at[pl.ds(pid, 1), :], ssem.at[2, d - 1], rsem.at[2, d - 1], peer).wait_recv()
        for cp in first + second:
            cp.wait_send()

    vm = pl.BlockSpec(memory_space=pltpu.VMEM)
    return pl.pallas_call(
        body, name="ada_fwd",
        in_specs=[vm, vm, vm, vm], out_specs=[vm, vm, vm],
        out_shape=[_sds((N_DEV, ADA_SHARD), F32), _sds((N_DEV, D), F32), _sds((N_DEV, CONV_KP, LANES), F32)],
        scratch_shapes=[pltpu.VMEM((N_DEV, D), F32), pltpu.VMEM((N_DEV, ADA_SHARD), F32),
                        pltpu.SemaphoreType.DMA((3, N_DEV - 1)), pltpu.SemaphoreType.DMA((3, N_DEV - 1))],
        compiler_params=pltpu.CompilerParams(vmem_limit_bytes=VMEM_LIMIT),
    )(c, w_ada, b_slice, cw_shard)


def _xchg_parts(arrays):
    n = len(arrays)
    anyspec = pl.BlockSpec(memory_space=pl.ANY)
    shapes = [_sds((N_DEV,) + tuple(a.shape[-2:]), a.dtype) for a in arrays]
    scratch = [pltpu.SemaphoreType.DMA((n,)), pltpu.SemaphoreType.DMA((n, N_DEV - 1)),
               pltpu.SemaphoreType.DMA((n, N_DEV - 1))]
    return [anyspec] * n, shapes, scratch


def _xchg(ins, outs, sems, gather, wait):
    lsem, ssem, rsem = sems
    x, y, cc, me = _my_pos()
    for a in range(len(ins)):
        local = pltpu.make_async_copy(ins[a] if gather else ins[a].at[me], outs[a].at[me], lsem.at[a])
        if not wait:
            local.start()
        for d in range(1, N_DEV):
            peer, pid = _peer(x, y, cc, d)
            src = ins[a] if gather else ins[a].at[pid]
            if not wait:
                _rcopy(src, outs[a].at[me], ssem.at[a, d - 1], rsem.at[a, d - 1], peer).start()
            else:
                cp = _rcopy(src, outs[a].at[pid], ssem.at[a, d - 1], rsem.at[a, d - 1], peer)
                cp.wait_recv()
                cp.wait_send()
        if wait:
            local.wait()


def _gather_two_level(shard, name):
    def body(x_ref, out_ref, ssem, rsem, lsem):
        x, y, c, me = _my_pos()
        sibling = (x, y, 1 - c)
        chips = [(1 - x, y), (x, 1 - y), (1 - x, 1 - y)]
        slot = lambda px, py, pc: out_ref.at[4 * px + 2 * py + pc]

        def copy(kk, block, to, src=None):
            return _rcopy(slot(*block) if src is None else src, slot(*block), ssem.at[kk], rsem.at[kk], to)

        mine = pltpu.make_async_copy(x_ref, slot(x, y, c), lsem)
        mine.start()
        first = [copy(0, (x, y, c), sibling, src=x_ref)]
        first += [copy(1 + j, (x, y, c), (*chip, c), src=x_ref) for j, chip in enumerate(chips)]
        for cp in first:
            cp.start()
        passed = [copy(4 + j, (*chip, c), sibling) for j, chip in enumerate(chips)]
        for j, chip in enumerate(chips):
            copy(1 + j, (*chip, c), (x, y, c)).wait_recv()
            passed[j].start()
        copy(0, sibling, (x, y, c)).wait_recv()
        for j, chip in enumerate(chips):
            copy(4 + j, (*chip, 1 - c), (x, y, c)).wait_recv()
        for cp in first + passed:
            cp.wait_send()
        mine.wait()

    anyspec = pl.BlockSpec(memory_space=pl.ANY)
    return pl.pallas_call(
        body, name=name, in_specs=[anyspec], out_specs=anyspec,
        out_shape=_sds((N_DEV,) + tuple(shard.shape), shard.dtype),
        scratch_shapes=[pltpu.SemaphoreType.DMA((N_DEV - 1,)), pltpu.SemaphoreType.DMA((N_DEV - 1,)),
                        pltpu.SemaphoreType.DMA(())],
    )(shard)


PACK_ROWS = 16
ROW_MISC = 5
ROW_LOSS = 6
ROW_DMOD = 8


def _small_bwd(pack, dmodb, dcw, cat, wp, mp_, vp, cw_w, cw_m, cw_v):
    def body(pack_ref, dmodb_ref, dcw_ref, cat_ref, wp_ref, mp_ref, vp_ref, cww_ref, cwm_ref, cwv_ref,
             g_ref, d_ref, mo_ref, vo_ref, cg_ref, cd_ref, cm_ref, cv_ref, gwa_ref, loss_ref,
             allp, dmc, cwg, ssem, rsem):
        x, y, cc, me = _my_pos()
        allp[me] = pack_ref[...]
        dmc[pl.ds(me, 1), :] = dmodb_ref[pl.ds(me, 1), :]
        cwg[me] = dcw_ref[me]
        sends = []
        for d in range(1, N_DEV):
            peer, pid = _peer(x, y, cc, d)
            sends.append(_rcopy(pack_ref, allp.at[me], ssem.at[0, d - 1], rsem.at[0, d - 1], peer))
            sends.append(_rcopy(dmodb_ref.at[pl.ds(pid, 1), :], dmc.at[pl.ds(me, 1), :], ssem.at[1, d - 1], rsem.at[1, d - 1], peer))
            sends.append(_rcopy(dcw_ref.at[pid], cwg.at[me], ssem.at[2, d - 1], rsem.at[2, d - 1], peer))
        for cp in sends:
            cp.start()
        for d in range(1, N_DEV):
            peer, pid = _peer(x, y, cc, d)
            _rcopy(pack_ref, allp.at[pid], ssem.at[0, d - 1], rsem.at[0, d - 1], peer).wait_recv()
            _rcopy(dmodb_ref.at[pl.ds(pid, 1), :], dmc.at[pl.ds(pid, 1), :], ssem.at[1, d - 1], rsem.at[1, d - 1], peer).wait_recv()
            _rcopy(dcw_ref.at[pid], cwg.at[pid], ssem.at[2, d - 1], rsem.at[2, d - 1], peer).wait_recv()
        for cp in sends:
            cp.wait_send()

        tot = allp[0]
        cg = cwg[0]
        for s in range(1, N_DEV):
            tot = tot + allp[s]
            cg = cg + cwg[s]
        lane = lax.broadcasted_iota(jnp.int32, (PACK_ROWS, D), 1)
        row = lax.broadcasted_iota(jnp.int32, (PACK_ROWS, D), 0)
        gains = (row == ROW_MISC) & (lane >= LANES) & (lane < 3 * LANES)
        folded = tot + pltpu.roll(tot, D - HEAD_DIM, axis=1)
        keep = (lane % LANES) < HEAD_DIM
        g = jnp.where(gains, jnp.where(keep, folded, 0.0), tot)
        loss_ref[...] = jnp.broadcast_to(
            (0.5 / D) * jnp.sum(jnp.where(row == ROW_LOSS, tot, 0.0), keepdims=True).reshape(1, 1), loss_ref.shape)
        g = jnp.where(row == ROW_LOSS, 0.0, g)
        g_ref[...] = g
        d_ref[...], mo_ref[...], vo_ref[...] = _adamw_math(wp_ref[...], g, mp_ref[...], vp_ref[...])
        cg_ref[...] = cg
        cd_ref[...], cm_ref[...], cv_ref[...] = _adamw_math(cww_ref[...], cg, cwm_ref[...], cwv_ref[...])
        dm_pad = jnp.concatenate([dmc[...], jnp.zeros((LANES - N_DEV, ADA_SHARD), F32)], axis=0)
        gwa_ref[...] = _dot_f32(cat_ref[...], dm_pad)

    vm = pl.BlockSpec(memory_space=pltpu.VMEM)
    p16 = _sds((PACK_ROWS, D), F32)
    c32 = _sds((CONV_KP, LANES), F32)
    return pl.pallas_call(
        body, name="small_bwd",
        in_specs=[vm] * 10, out_specs=[vm] * 10,
        out_shape=[p16, p16, p16, p16, c32, c32, c32, c32, _sds((D, ADA_SHARD), F32), _sds((8, LANES), F32)],
        scratch_shapes=[pltpu.VMEM((N_DEV, PACK_ROWS, D), F32), pltpu.VMEM((N_DEV, ADA_SHARD), F32),
                        pltpu.VMEM((N_DEV, CONV_KP, LANES), F32),
                        pltpu.SemaphoreType.DMA((3, N_DEV - 1)), pltpu.SemaphoreType.DMA((3, N_DEV - 1))],
        compiler_params=pltpu.CompilerParams(vmem_limit_bytes=VMEM_LIMIT),
    )(pack, dmodb, dcw, cat, wp, mp_, vp, cw_w, cw_m, cw_v)


def _lanes(vec, start, total=D):
    n = vec.shape[1]
    return jnp.pad(vec, ((0, 0), (start, total - start - n)))


def _pack_small(rows5, misc, loss_row, six):
    z = jnp.zeros((1, D), F32)
    return jnp.concatenate(rows5 + [misc, loss_row, z] + [six.reshape(N_ADA, D), z, z], axis=0)


def kernel(x, c, w_ada, b_ada, norm1_g, w_in, b_forget, q_norm_g, k_norm_g, w_attn_proj, conv_w, conv_b, conv_ln_g, conv_ln_b, w_conv_proj, w_out, norm2_g, w_mlp1, w_mlp2, loss_target, m_w_ada, m_b_ada, m_norm1_g, m_w_in, m_b_forget, m_q_norm_g, m_k_norm_g, m_w_attn_proj, m_conv_w, m_conv_b, m_conv_ln_g, m_conv_ln_b, m_w_conv_proj, m_w_out, m_norm2_g, m_w_mlp1, m_w_mlp2, v_w_ada, v_b_ada, v_norm1_g, v_w_in, v_b_forget, v_q_norm_g, v_k_norm_g, v_w_attn_proj, v_conv_w, v_conv_b, v_conv_ln_g, v_conv_ln_b, v_w_conv_proj, v_w_out, v_norm2_g, v_w_mlp1, v_w_mlp2):
    me = 4 * lax.axis_index("x") + 2 * lax.axis_index("y") + lax.axis_index("c")
    xs, tgt = x[0], loss_target[0]
    T = xs.shape[0]
    sq = lambda a: a[0]
    pad_taps = lambda a: jnp.pad(a[0], ((0, CONV_KP - CONV_K), (0, 0)))

    b_slice = lax.dynamic_slice(b_ada, (0, me * ADA_SHARD), (1, ADA_SHARD))
    modb, ca_all, cwf = _ada_fwd(c, sq(w_ada), b_slice, pad_taps(conv_w))
    mod = modb.reshape(1, N_ADA * D)
    cw = jnp.transpose(cwf, (1, 0, 2)).reshape(CONV_KP, D)

    g_in = _gather_two_level(sq(w_in).astype(BF), "w_in_gather")
    d_in = g_in.shape[2] * N_DEV
    w_in_f = jnp.transpose(g_in, (1, 0, 2)).reshape(D, d_in)
    w_qkv = w_in_f[:, :3 * D]
    w_gg = w_in_f[:, 3 * D + N_HEADS:]
    w_f = jnp.pad(w_in_f[:, 3 * D:3 * D + N_HEADS], ((0, 0), (0, LANES - N_HEADS)))
    shards = [sq(w_attn_proj).astype(BF), sq(w_conv_proj).astype(BF), sq(w_out).astype(BF),
              sq(w_mlp1).astype(BF), sq(w_mlp2).astype(BF)]

    qg2 = jnp.tile(q_norm_g, (1, 2))
    kg2 = jnp.tile(k_norm_g, (1, 2))
    bf_pad = _lanes(b_forget, 0, LANES)

    h = _pre_in(xs, mod, norm1_g)
    pqkv = _matmul(h, w_qkv, "nn", BF, "mm_proj_qkv")
    pgg = _matmul(h, w_gg, "nn", BF, "mm_proj_gg")
    f = _matmul(h, w_f, "nn", F32, "mm_f")
    q, k, v, fc = _qkv_post(pqkv, f, qg2, kg2, bf_pad)
    fc3 = fc.reshape(N_HEADS // 2, 2, T)
    o, lse, g_ap, g_cp, g_out, g_1, g_2 = _flash_fwd(q, k, v, fc3, shards)
    w_ap, w_cp, w_o = g_ap.reshape(D, D), g_cp.reshape(D, D), g_out.reshape(D, D)
    w_2 = g_2.reshape(D_FF, D)
    ba = _matmul(o, w_ap, "nn", F32, "mm_ba")
    u0, u1, u3 = _conv_fwd(pgg, cw, conv_b, conv_ln_g, conv_ln_b)
    bb = _matmul(u3, w_cp, "nn", F32, "mm_bb")
    merged = _merge(ba, bb, pgg)
    mo = _matmul(merged, w_o, "nn", F32, "mm_out")
    x1, h2 = _post_out(xs, mo, mod, norm2_g)
    a, rl = _matmul(h2, g_1, "nn", BF, "mm_mlp1", relu2=True, b_slots=True)
    m2 = _matmul(rl, w_2, "nn", F32, "mm_mlp2")
    dy, dm2, dg2, sqcols = _loss_head(x1, m2, tgt, mod)

    da = _matmul(dm2, w_2, "nt", BF, "mm_drl", relu_of=a)
    dw_2 = _matmul(rl, dm2, "tn", BF, "mm_dw2")
    dh2 = _matmul(da, g_1, "nt", F32, "mm_dh2", b_slots=True)
    dw_1 = _matmul(h2, da, "tn", BF, "mm_dw1", out_slots=True)
    dx1, dmo, dsh2, dsc2, dn2g, dg1 = _norm2_bwd(dh2, x1, dy, mo, mod, norm2_g)
    dmerged = _matmul(dmo, w_o, "nt", F32, "mm_dmerged")
    dw_o = _matmul(merged, dmo, "tn", BF, "mm_dwout")
    dba, dbb, dgg = _gate_bwd(dmerged, ba, bb, pgg)
    du3 = _matmul(dbb, w_cp, "nt", F32, "mm_du3")
    dw_cp = _matmul(u3, dbb, "tn", BF, "mm_dwcp")
    do = _matmul(dba, w_ap, "nt", F32, "mm_do")
    dw_ap = _matmul(o, dba, "tn", BF, "mm_dwap")
    dgg, dlng, dlnb, dcb, dcw_full = _conv_bwd(du3, u1, u0, pgg, cw, conv_ln_g, conv_ln_b, dgg)
    delta = _attn_delta(do, o)
    dw_gg = _matmul(h, dgg, "tn", BF, "mm_dw_gg")
    parts = [dw_ap.reshape(N_DEV, D // N_DEV, D), dw_cp.reshape(N_DEV, D // N_DEV, D), dw_o.reshape(N_DEV, D // N_DEV, D),
             dw_1, dw_2.reshape(N_DEV, D_FF // N_DEV, D)]
    dq, rs_a, rs_b, dk, dv, dfc3, r_ap, r_cp, r_out, r_1, r_2 = _flash_bwd(q, k, v, do, lse, delta, fc3, parts)
    dfq = jnp.stack([rs_a, rs_b], axis=1).reshape(N_HEADS, T)
    dqkv, df, dqg, dkg, dbf = _qkv_bwd(dq, dk, dv, pqkv, f, dfc3.reshape(N_HEADS, T), dfq, qg2, kg2, bf_pad)
    dw_qkv = _matmul(h, dqkv, "tn", BF, "mm_dw_qkv")
    dw_f = _matmul(h, df, "tn", BF, "mm_dwf")
    dw_in_f = jnp.concatenate([dw_qkv, dw_f[:, :N_HEADS], dw_gg], axis=1)
    part_in = jnp.transpose(dw_in_f.reshape(D, N_DEV, d_in // N_DEV), (1, 0, 2))
    dh, r_in = _matmul(dqkv, w_qkv, "nt", F32, "mm_dh", scatter=(part_in,), more=((dgg, w_gg),))
    dhf = _matmul(df, w_f, "nt", F32, "mm_dhf")
    grad_x, dsh1, dsc1, dn1g = _norm1_bwd(dh, dhf, xs, dx1, mod, norm1_g)

    dmod = jnp.concatenate([dsh1, dsc1, dg1, dsh2, dsc2, dg2], axis=1)
    misc = jnp.concatenate([dbf, dqg, dkg, jnp.zeros((1, D - 3 * LANES), F32)], axis=1)
    pack = _pack_small([dn1g, dcb, dlng, dlnb, dn2g], misc, sqcols, dmod)
    dcw_blocks = jnp.transpose(dcw_full.reshape(CONV_KP, N_DEV, LANES), (1, 0, 2))

    def small_params(b_a, n1, bfg, qn, kn, cvb, lg, lb, n2):
        misc_p = jnp.concatenate([_lanes(bfg, 0, LANES), _lanes(qn, 0, LANES), _lanes(kn, 0, LANES),
                                  jnp.zeros((1, D - 3 * LANES), F32)], axis=1)
        return _pack_small([n1, cvb, lg, lb, n2], misc_p, jnp.zeros((1, D), F32), b_a)

    wp = small_params(b_ada, norm1_g, b_forget, q_norm_g, k_norm_g, conv_b, conv_ln_g, conv_ln_b, norm2_g)
    mp_ = small_params(m_b_ada, m_norm1_g, m_b_forget, m_q_norm_g, m_k_norm_g, m_conv_b, m_conv_ln_g, m_conv_ln_b, m_norm2_g)
    vp = small_params(v_b_ada, v_norm1_g, v_b_forget, v_q_norm_g, v_k_norm_g, v_conv_b, v_conv_ln_g, v_conv_ln_b, v_norm2_g)
    cat = jnp.pad(jnp.transpose(ca_all), ((0, 0), (0, LANES - N_DEV)))
    small = _small_bwd(pack, dmod.reshape(N_DEV, ADA_SHARD), dcw_blocks, cat,
                       wp, mp_, vp, pad_taps(conv_w), pad_taps(m_conv_w), pad_taps(v_conv_w))
    sp = small[0:4]
    scw = small[4:8]
    gw_ada, loss_t = small[8], small[9]
    loss = loss_t[0, 0]

    def unpack(p):
        misc_r = p[ROW_MISC:ROW_MISC + 1]
        return dict(
            b_ada=p[ROW_DMOD:ROW_DMOD + N_ADA].reshape(1, N_ADA * D), norm1_g=p[0:1], conv_b=p[1:2], conv_ln_g=p[2:3],
            conv_ln_b=p[3:4], norm2_g=p[4:5], b_forget=misc_r[:, 0:N_HEADS],
            q_norm_g=misc_r[:, LANES:LANES + HEAD_DIM], k_norm_g=misc_r[:, 2 * LANES:2 * LANES + HEAD_DIM])

    res = {}
    res["w_ada"] = _adamw(gw_ada[None], w_ada, m_w_ada, v_w_ada, "adamw_w_ada")
    res["w_in"] = _adamw(r_in, w_in, m_w_in, v_w_in, "adamw_w_in")
    res["w_attn_proj"] = _adamw(r_ap, w_attn_proj, m_w_attn_proj, v_w_attn_proj, "adamw_w_ap")
    res["w_conv_proj"] = _adamw(r_cp, w_conv_proj, m_w_conv_proj, v_w_conv_proj, "adamw_w_cp")
    res["w_out"] = _adamw(r_out, w_out, m_w_out, v_w_out, "adamw_w_out")
    res["w_mlp1"] = _adamw(r_1, w_mlp1, m_w_mlp1, v_w_mlp1, "adamw_w_mlp1")
    res["w_mlp2"] = _adamw(r_2, w_mlp2, m_w_mlp2, v_w_mlp2, "adamw_w_mlp2")

    names = ["w_ada", "b_ada", "norm1_g", "w_in", "b_forget", "q_norm_g", "k_norm_g", "w_attn_proj", "conv_w", "conv_b",
             "conv_ln_g", "conv_ln_b", "w_conv_proj", "w_out", "norm2_g", "w_mlp1", "w_mlp2"]
    outs = [loss, grad_x[None]]
    for kind in range(4):
        small_d = unpack(sp[kind])
        for nm in names:
            if nm in res:
                outs.append(res[nm][kind])
            elif nm == "conv_w":
                outs.append(scw[kind][:CONV_K][None])
            else:
                outs.append(small_d[nm])
    return tuple(outs)
```

```python
import functools

import jax
import jax.numpy as jnp
from jax import lax
from jax.experimental import pallas as pl
from jax.experimental.pallas import tpu as pltpu

F32 = jnp.float32
BF = jnp.bfloat16

N_DEV = 8
D = 1024
N_HEADS = 16
HEAD_DIM = 64
LANES = 128
SUBLANES = 8
CONV_K = 31
CONV_KP = 32
HALO = 32
CONV_ROWS = 64
D_FF = 4 * D
N_ADA = 6
ADA_SHARD = N_ADA * D // N_DEV
EPS = 1e-6
QK_SCALE = HEAD_DIM ** -0.5
LOG2E = 1.4426950408889634
LN2 = 0.6931471805599453
NEG = -1e30

ADAM_LR = 0.001
ADAM_B1 = 0.9
ADAM_B2 = 0.999
ADAM_EPS = 1e-08
ADAM_WD = 0.01
ADAM_STEP = 10

VMEM_LIMIT = 56 * 1024 * 1024
TM_ROWS = 512
CONV_TM = 256
TQ = 512

MESH = pl.DeviceIdType.MESH


def _cp(sem=None):
    return pltpu.CompilerParams(dimension_semantics=sem, vmem_limit_bytes=VMEM_LIMIT)


def _sds(shape, dtype):
    return jax.ShapeDtypeStruct(tuple(shape), dtype)


def _full(arr):
    nd = arr.ndim
    return pl.BlockSpec(arr.shape, lambda *_: (0,) * nd)


def _fullshape(shape):
    nd = len(shape)
    return pl.BlockSpec(tuple(shape), lambda *_: (0,) * nd)


def _split3(x):
    hi = x.astype(BF)
    r1 = x - hi.astype(F32)
    mid = r1.astype(BF)
    lo = (r1 - mid.astype(F32)).astype(BF)
    return hi, mid, lo


def _dot_exact(x, mat):
    hi, mid, lo = _split3(x)
    d = lambda t: jnp.dot(t, mat, preferred_element_type=F32)
    return d(hi) + d(mid) + d(lo)


def _dot_f32(a, b):
    a1, a2, a3 = _split3(a)
    b1, b2, b3 = _split3(b)
    d = lambda s, t: jnp.dot(s, t, preferred_element_type=F32)
    return (d(a1, b3) + d(a3, b1) + d(a2, b2)) + (d(a1, b2) + d(a2, b1)) + d(a1, b1)


def _sigmoid(x):
    return 1.0 / (1.0 + jnp.exp(-x))


def _colsum(x):
    return jnp.sum(x, axis=0, keepdims=True)


def _my_pos():
    x, y, c = lax.axis_index("x"), lax.axis_index("y"), lax.axis_index("c")
    return x, y, c, 4 * x + 2 * y + c


def _peer(x, y, c, d):
    px = (1 - x) if d & 4 else x
    py = (1 - y) if d & 2 else y
    pc = (1 - c) if d & 1 else c
    return (px, py, pc), 4 * px + 2 * py + pc


def _matmul(a, b, form, out_dtype, name, tm=1024, tn=1024, tk=1024, scatter=(), relu2=False, relu_of=None,
            b_slots=False, out_slots=False, more=()):
    width = None
    if b_slots:
        assert form in ("nn", "nt") and b.shape[0] == N_DEV
        width = b.shape[2]
        if form == "nn":
            (M, K), N, tn = a.shape, N_DEV * width, 2 * width
        else:
            (M, K), N, tk = a.shape, b.shape[1], 2 * width
    elif form == "nn":
        (M, K), N = a.shape, b.shape[1]
    elif form == "nt":
        (M, K), N = a.shape, b.shape[0]
    else:
        (K, M), N = a.shape, b.shape[1]
    if out_slots:
        width = N // N_DEV
        tn = 2 * width
    tm, tn, tk = min(tm, M), min(tn, N), min(tk, K)
    assert M % tm == 0 and N % tn == 0 and K % tk == 0, (name, M, N, K)
    nk = K // tk
    if form == "tn":
        a_spec = pl.BlockSpec((tk, tm), lambda i, j, k: (k, i))
        dn = (((0,), (0,)), ((), ()))
    else:
        a_spec = pl.BlockSpec((tm, tk), lambda i, j, k: (i, k))
        dn = (((1,), (1 if form == "nt" else 0,)), ((), ()))
    if b_slots and form == "nn":
        b_spec = pl.BlockSpec((2, tk, width), lambda i, j, k: (j, k, 0))
    elif b_slots:
        b_spec = pl.BlockSpec((2, tn, width), lambda i, j, k: (k, j, 0))
    elif form == "nt":
        b_spec = pl.BlockSpec((tn, tk), lambda i, j, k: (j, k))
    else:
        b_spec = pl.BlockSpec((tk, tn), lambda i, j, k: (k, j))

    pairs = [(a, b)] + list(more)
    seg = [0]
    for a_s, _ in pairs:
        k_s = K if len(pairs) == 1 else a_s.shape[1]
        assert k_s % tk == 0 and (len(pairs) == 1 or (form == "nt" and not b_slots))
        seg.append(seg[-1] + k_s // tk)
    nk = seg[-1]
    specs_more = []
    for s in range(1, len(pairs)):
        lo_k, n_k = seg[s], seg[s + 1] - seg[s]
        kk = lambda k, lo_k=lo_k, n_k=n_k: jnp.clip(k - lo_k, 0, n_k - 1)
        specs_more += [pl.BlockSpec((tm, tk), lambda i, j, k, kk=kk: (i, kk(k))),
                       pl.BlockSpec((tn, tk), lambda i, j, k, kk=kk: (j, kk(k)))]
    if len(pairs) > 1:
        n0 = seg[1]
        a_spec = pl.BlockSpec((tm, tk), lambda i, j, k: (i, jnp.minimum(k, n0 - 1)))
        b_spec = pl.BlockSpec((tn, tk), lambda i, j, k: (j, jnp.minimum(k, n0 - 1)))

    nx = len(scatter)
    ne = 0 if relu_of is None else 1
    no = 2 if relu2 else 1
    nm = 2 * (len(pairs) - 1)
    grid = (M // tm, N // tn, nk)

    def body(a_ref, b_ref, *rest):
        ab_refs = [(a_ref, b_ref)] + [(rest[2 * s], rest[2 * s + 1]) for s in range(len(pairs) - 1)]
        rest = rest[nm:]
        e_ref = rest[0] if ne else None
        x_in = rest[ne:ne + nx]
        o_refs = rest[ne + nx:ne + nx + no]
        x_out = rest[ne + nx + no:ne + 2 * nx + no]
        scr = rest[ne + 2 * nx + no:]
        k = pl.program_id(2)
        if nx:
            first, last = _first_last(grid)

            @pl.when(first)
            def _():
                _xchg(x_in, x_out, scr[-3:], False, wait=False)

        def finish(val):
            if out_slots:
                o_refs[0][0] = val[:, 0:width].astype(out_dtype)
                o_refs[0][1] = val[:, width:2 * width].astype(out_dtype)
            elif relu2:
                o_refs[0][...] = val.astype(out_dtype)
                r = jnp.maximum(val, 0.0)
                o_refs[1][...] = (r * r).astype(out_dtype)
            elif ne:
                o_refs[0][...] = (val * (2.0 * jnp.maximum(e_ref[...].astype(F32), 0.0))).astype(out_dtype)
            else:
                o_refs[0][...] = val.astype(out_dtype)

        def accumulate(ar, br):
            dot = lambda u, w: lax.dot_general(u.astype(BF), w.astype(BF), dn, preferred_element_type=F32)
            if b_slots and form == "nn":
                part = jnp.concatenate([dot(ar[...], br[0]), dot(ar[...], br[1])], axis=1)
            elif b_slots:
                part = dot(ar[:, 0:width], br[0]) + dot(ar[:, width:2 * width], br[1])
            else:
                part = dot(ar[...], br[...])
            if nk == 1:
                finish(part)
            else:
                acc = scr[0]

                @pl.when(k == 0)
                def _():
                    acc[...] = part

                @pl.when(k > 0)
                def _():
                    acc[...] += part

        if len(pairs) == 1:
            accumulate(a_ref, b_ref)
        else:
            for s, (ar, br) in enumerate(ab_refs):
                @pl.when((k >= seg[s]) & (k < seg[s + 1]))
                def _(ar=ar, br=br):
                    accumulate(ar, br)

        if nk > 1:
            @pl.when(k == nk - 1)
            def _():
                finish(scr[0][...])

        if nx:
            @pl.when(last)
            def _():
                _xchg(x_in, x_out, scr[-3:], False, wait=True)

    x_specs, x_shapes, x_scratch = _xchg_parts(scatter) if nx else ([], [], [])
    sem = ("arbitrary",) * 3 if nx else ("parallel", "parallel", "arbitrary")
    o_spec = pl.BlockSpec((tm, tn), lambda i, j, k: (i, j))
    o_shape = _sds((M, N), out_dtype)
    if out_slots:
        o_spec = pl.BlockSpec((2, tm, width), lambda i, j, k: (j, i, 0))
        o_shape = _sds((N_DEV, M, width), out_dtype)
    res = pl.pallas_call(
        body, name=name, grid=grid,
        in_specs=[a_spec, b_spec] + specs_more + [o_spec] * ne + x_specs,
        out_specs=[o_spec] * no + x_specs,
        out_shape=[o_shape] * no + x_shapes,
        scratch_shapes=([] if nk == 1 else [pltpu.VMEM((tm, tn), F32)]) + x_scratch,
        compiler_params=_cp(sem),
    )(a, b, *[t for p in more for t in p], *([relu_of] if ne else []), *scatter)
    return res if (nx or relu2) else res[0]


def _rows_call(body, name, n_tiles, ins, outs, scratch=(), aliases=None):
    res = pl.pallas_call(
        body, name=name, grid=(n_tiles,),
        in_specs=[s for _, s in ins],
        out_specs=[s for _, s in outs],
        out_shape=[o for o, _ in outs],
        scratch_shapes=list(scratch),
        input_output_aliases=aliases or {},
        compiler_params=_cp(("arbitrary",)),
    )(*[a for a, _ in ins])
    return res


def _rspec(tm, width, cb=0, rev_n=None):
    if rev_n is None:
        return pl.BlockSpec((tm, width), lambda i: (i, cb))
    return pl.BlockSpec((tm, width), lambda i: (rev_n - 1 - i, cb))


def _row_out(T, tm, width, dtype, rev_n=None):
    return (_sds((T, width), dtype), _rspec(tm, width, 0, rev_n))


def _acc_out(shape, dtype=F32):
    return (_sds(shape, dtype), _fullshape(shape))


def _mod_parts(mod):
    return [mod[:, i * D:(i + 1) * D] for i in range(N_ADA)]


def _pre_in(x, mod, n1g):
    T = x.shape[0]
    tm = TM_ROWS

    def body(x_ref, mod_ref, g_ref, h_ref):
        sh1, sc1 = mod_ref[:, 0:D], mod_ref[:, D:2 * D]
        xv = x_ref[...]
        r = lax.rsqrt(jnp.mean(xv * xv, axis=-1, keepdims=True) + EPS)
        h_ref[...] = ((xv * r) * g_ref[...] * (1.0 + sc1) + sh1).astype(BF)

    return _rows_call(body, "pre_in", T // tm,
                      [(x, _rspec(tm, D)), (mod, _full(mod)), (n1g, _full(n1g))],
                      [_row_out(T, tm, D, BF)])[0]


def _seg_mat():
    r = jnp.arange(LANES)[:, None] // HEAD_DIM
    c = jnp.arange(LANES)[None, :] // HEAD_DIM
    return jnp.where(r == c, 1.0 / HEAD_DIM, 0.0).astype(BF)


def _tri_mat(n, upper):
    r = jnp.arange(n)[:, None]
    c = jnp.arange(n)[None, :]
    return jnp.where((r <= c) if upper else (r >= c), 1.0, 0.0).astype(BF)


def _log_sigmoid(z):
    return jnp.minimum(z, 0.0) - jnp.log(1.0 + jnp.exp(-jnp.abs(z)))


def _qkv_post(proj, f, qg2, kg2, bf_pad):
    T = proj.shape[0]
    tm = TM_ROWS
    seg = _seg_mat()
    tri = _tri_mat(tm, True)

    def body(q_ref, k_ref, v_ref, f_ref, qg_ref, kg_ref, bf_ref, seg_ref, tri_ref,
             qo_ref, ko_ref, vo_ref, fc_ref, carry_ref):
        i = pl.program_id(0)

        @pl.when(i == 0)
        def _():
            carry_ref[...] = jnp.zeros_like(carry_ref)

        segm = seg_ref[...]
        for j in range(D // LANES):
            sl = slice(j * LANES, (j + 1) * LANES)
            qc = q_ref[:, sl].astype(F32)
            rq = lax.rsqrt(_dot_exact(qc * qc, segm) + EPS)
            qo_ref[:, sl] = ((qc * rq) * qg_ref[...] * (QK_SCALE * LOG2E)).astype(BF)
            kc = k_ref[:, sl].astype(F32)
            rk = lax.rsqrt(_dot_exact(kc * kc, segm) + EPS)
            ko_ref[:, sl] = ((kc * rk) * kg_ref[...]).astype(BF)
        vo_ref[...] = v_ref[...].astype(BF)
        lf = _log_sigmoid(f_ref[...] + bf_ref[...])
        lft = lf.T[0:N_HEADS, :]
        carry = carry_ref[:, 0:1]
        fc_ref[...] = _dot_exact(lft, tri_ref[...]) + carry
        carry_ref[...] = jnp.broadcast_to(carry + jnp.sum(lft, axis=1, keepdims=True), carry_ref.shape)

    outs = [_row_out(T, tm, D, BF), _row_out(T, tm, D, BF), _row_out(T, tm, D, BF),
            (_sds((N_HEADS, T), F32), pl.BlockSpec((N_HEADS, tm), lambda i: (0, i)))]
    ins = [(proj, _rspec(tm, D, 0)), (proj, _rspec(tm, D, 1)), (proj, _rspec(tm, D, 2)), (f, _rspec(tm, LANES)),
           (qg2, _full(qg2)), (kg2, _full(kg2)), (bf_pad, _full(bf_pad)), (seg, _full(seg)), (tri, _full(tri))]
    return _rows_call(body, "qkv_post", T // tm, ins, outs, [pltpu.VMEM((N_HEADS, LANES), F32)])


def _lane_lo():
    return lax.broadcasted_iota(jnp.int32, (1, LANES), 1) < HEAD_DIM


def _nt(a, b):
    return lax.dot_general(a, b, (((1,), (1,)), ((), ())), preferred_element_type=F32)


def _tn(a, b):
    return lax.dot_general(a, b, (((0,), (0,)), ((), ())), preferred_element_type=F32)


def _head_rep(x, lo):
    rolled = pltpu.roll(x, HEAD_DIM, axis=1)
    return jnp.where(lo, x, rolled), jnp.where(lo, rolled, x)


def _diag_mask(t):
    return lax.broadcasted_iota(jnp.int32, (t, t), 1) <= lax.broadcasted_iota(jnp.int32, (t, t), 0)


def _first_last(grid):
    ids = [pl.program_id(a) for a in range(len(grid))]
    first = functools.reduce(jnp.logical_and, [i == 0 for i in ids])
    last = functools.reduce(jnp.logical_and, [i == g - 1 for i, g in zip(ids, grid)])
    return first, last


def _flash_fwd(q, k, v, fc3, shards):
    T = q.shape[0]
    tq = TQ
    nq = T // tq
    hp_n = N_HEADS // 2
    rep = tq // LANES
    nx = len(shards)
    grid = (hp_n, nq, nq)

    def body(q_ref, k_ref, v_ref, fk_ref, fq_ref, *rest):
        x_in, (o_ref, lse_ref), x_out = rest[:nx], rest[nx:nx + 2], rest[nx + 2:2 * nx + 2]
        acc_ref, m_ref = rest[2 * nx + 2:2 * nx + 4]
        sems = rest[2 * nx + 4:]
        qi, ki = pl.program_id(1), pl.program_id(2)
        first, last = _first_last(grid)

        @pl.when(first)
        def _():
            _xchg(x_in, x_out, sems, True, wait=False)

        @pl.when(ki == 0)
        def _():
            acc_ref[...] = jnp.zeros_like(acc_ref)
            m_ref[...] = jnp.full_like(m_ref, NEG)

        lane = lax.broadcasted_iota(jnp.int32, (1, LANES), 1)
        sum_lane = (HEAD_DIM, 0)

        def step(diag):
            lo = _lane_lo()
            q2, k2, v2 = q_ref[...], k_ref[...], v_ref[...]
            zero = jnp.zeros_like(k2)
            bias = (fq_ref[:, 0:1] - fk_ref[...]) * LOG2E
            for hh in range(2):
                sel = (lambda t: jnp.where(lo, t, zero)) if hh == 0 else (lambda t: jnp.where(lo, zero, t))
                ones = jnp.where(lane == sum_lane[hh], 1.0, 0.0).astype(BF)
                v_aug = jnp.where(lo, v2, ones) if hh == 0 else jnp.where(lo, ones, v2)
                s = _nt(sel(q2), k2) + bias[hh:hh + 1, :]
                if diag:
                    s = jnp.where(_diag_mask(tq), s, NEG)
                m_old = m_ref[hh]
                m_new = jnp.maximum(m_old, jnp.max(s, axis=-1, keepdims=True))
                alpha = jnp.exp2(m_old - m_new)
                p = jnp.exp2(s - jnp.tile(m_new, (1, rep)))
                m_ref[hh] = m_new
                acc_ref[hh] = acc_ref[hh] * alpha + jnp.dot(p.astype(BF), v_aug, preferred_element_type=F32)

        @pl.when(ki < qi)
        def _():
            step(False)

        @pl.when(ki == qi)
        def _():
            step(True)
            lo = _lane_lo()
            acc_a, acc_b = acc_ref[0], acc_ref[1]
            la = jnp.broadcast_to(acc_a[:, sum_lane[0]:sum_lane[0] + 1], (tq, LANES))
            lb = jnp.broadcast_to(acc_b[:, sum_lane[1]:sum_lane[1] + 1], (tq, LANES))
            o_ref[...] = jnp.where(lo, acc_a / la, acc_b / lb)
            lse_ref[...] = jnp.where(lo, m_ref[0] + jnp.log(la) * LOG2E, m_ref[1] + jnp.log(lb) * LOG2E)

        @pl.when(last)
        def _():
            _xchg(x_in, x_out, sems, True, wait=True)

    qspec = pl.BlockSpec((tq, LANES), lambda h, i, j: (i, h))
    kspec = pl.BlockSpec((tq, LANES), lambda h, i, j: (jnp.minimum(i, j), h))
    fkspec = pl.BlockSpec((None, 2, tq), lambda h, i, j: (h, 0, jnp.minimum(i, j)))
    fqspec = pl.BlockSpec((None, 2, tq), lambda h, i, j: (h, 0, i))
    x_specs, x_shapes, x_scratch = _xchg_parts(shards)
    return pl.pallas_call(
        body, name="attn_fwd", grid=grid,
        in_specs=[qspec, kspec, kspec, fkspec, fqspec] + x_specs,
        out_specs=[qspec, qspec] + x_specs,
        out_shape=[_sds((T, D), F32), _sds((T, D), F32)] + x_shapes,
        scratch_shapes=[pltpu.VMEM((2, tq, LANES), F32), pltpu.VMEM((2, tq, LANES), F32)] + x_scratch,
        compiler_params=_cp(("arbitrary", "arbitrary", "arbitrary")),
    )(q, k, v, fc3, fc3, *shards)


def _attn_delta(do, o):
    T = o.shape[0]
    tm = TM_ROWS
    ones = (_seg_mat().astype(F32) * HEAD_DIM).astype(BF)

    def body(do_ref, o_ref, seg_ref, dl_ref):
        segm = seg_ref[...]
        for j in range(D // LANES):
            sl = slice(j * LANES, (j + 1) * LANES)
            dl_ref[:, sl] = _dot_exact(do_ref[:, sl].astype(BF).astype(F32) * o_ref[:, sl], segm)

    ins = [(do, _rspec(tm, D)), (o, _rspec(tm, D)), (ones, _full(ones))]
    return _rows_call(body, "attn_delta", T // tm, ins, [_row_out(T, tm, D, F32)])[0]


def _flash_bwd(q, k, v, do, lse, delta, fc3, parts):
    T = q.shape[0]
    tq = TQ
    nq = T // tq
    hp_n = N_HEADS // 2
    rep = tq // LANES
    nx = len(parts)
    grid = (hp_n, nq, nq)

    def body(q_ref, k_ref, v_ref, do_ref, lse_ref, dl_ref, fk_ref, fq_ref, *rest):
        x_in, x_out = rest[:nx], rest[nx + 6:2 * nx + 6]
        dq_ref, ra_ref, rb_ref, dk_ref, dv_ref, dfc_ref = rest[nx:nx + 6]
        dk_acc, dv_acc, df_acc = rest[2 * nx + 6:2 * nx + 9]
        sems = rest[2 * nx + 9:]
        ki, qi = pl.program_id(1), pl.program_id(2)
        first, last = _first_last(grid)
        qrows = pl.ds(pl.multiple_of(qi * tq, tq), tq)

        @pl.when(first)
        def _():
            _xchg(x_in, x_out, sems, False, wait=False)

        @pl.when((ki == 0) & (qi == 0))
        def _():
            dq_ref[...] = jnp.zeros_like(dq_ref)
            ra_ref[...] = jnp.zeros_like(ra_ref)
            rb_ref[...] = jnp.zeros_like(rb_ref)

        @pl.when(qi == 0)
        def _():
            dk_acc[...] = jnp.zeros_like(dk_acc)
            dv_acc[...] = jnp.zeros_like(dv_acc)
            df_acc[...] = jnp.zeros_like(df_acc)

        def step(diag):
            lo = _lane_lo()
            q2, k2, v2 = q_ref[...], k_ref[...], v_ref[...]
            do2 = do_ref[...].astype(BF)
            zero = jnp.zeros_like(q2)
            bias = (fq_ref[:, 0:1] - fk_ref[...]) * LOG2E
            lses = _head_rep(lse_ref[...], lo)
            dls = _head_rep(dl_ref[...], lo)
            dk_t = None
            dv_t = None
            dq_t = None
            for hh in range(2):
                sel = (lambda t: jnp.where(lo, t, zero)) if hh == 0 else (lambda t: jnp.where(lo, zero, t))
                s = _nt(sel(q2), k2) + bias[hh:hh + 1, :]
                if diag:
                    s = jnp.where(_diag_mask(tq), s, NEG)
                p = jnp.exp2(s - jnp.tile(lses[hh], (1, rep)))
                dp = _nt(sel(do2), v2)
                ds = p * (dp - jnp.tile(dls[hh], (1, rep)))
                ds_b = ds.astype(BF)
                dvp = _tn(p.astype(BF), sel(do2))
                dkp = _tn(ds_b, sel(q2))
                dqp = jnp.dot(ds_b, sel(k2), preferred_element_type=F32)
                dv_t = dvp if dv_t is None else dv_t + dvp
                dk_t = dkp if dk_t is None else dk_t + dkp
                dq_t = dqp if dq_t is None else dq_t + dqp
                df_acc[hh:hh + 1, :] -= _colsum(ds)
                r_ref = ra_ref if hh == 0 else rb_ref
                r_ref[qrows, :] += jnp.sum(ds, axis=-1, keepdims=True)
            dk_acc[...] += dk_t
            dv_acc[...] += dv_t
            dq_ref[qrows, :] += dq_t * QK_SCALE

        @pl.when(qi > ki)
        def _():
            step(False)

        @pl.when(qi == ki)
        def _():
            step(True)

        @pl.when(qi == nq - 1)
        def _():
            dk_ref[...] = dk_acc[...] * LN2
            dv_ref[...] = dv_acc[...]
            dfc_ref[...] = df_acc[...]

        @pl.when(last)
        def _():
            _xchg(x_in, x_out, sems, False, wait=True)

    kspec = pl.BlockSpec((tq, LANES), lambda h, j, i: (j, h))
    qspec = pl.BlockSpec((tq, LANES), lambda h, j, i: (jnp.maximum(i, j), h))
    fkspec = pl.BlockSpec((None, 2, tq), lambda h, j, i: (h, 0, j))
    fqspec = pl.BlockSpec((None, 2, tq), lambda h, j, i: (h, 0, jnp.maximum(i, j)))
    x_specs, x_shapes, x_scratch = _xchg_parts(parts)
    dqspec = pl.BlockSpec((T, LANES), lambda h, j, i: (0, h))
    rspec = pl.BlockSpec((None, T, 1), lambda h, j, i: (h, 0, 0))
    return pl.pallas_call(
        body, name="attn_bwd", grid=grid,
        in_specs=[qspec, kspec, kspec, qspec, qspec, qspec, fkspec, fqspec] + x_specs,
        out_specs=[dqspec, rspec, rspec, kspec, kspec, fkspec] + x_specs,
        out_shape=[_sds((T, D), F32), _sds((hp_n, T, 1), F32), _sds((hp_n, T, 1), F32),
                   _sds((T, D), F32), _sds((T, D), F32), _sds((hp_n, 2, T), F32)] + x_shapes,
        scratch_shapes=[pltpu.VMEM((tq, LANES), F32), pltpu.VMEM((tq, LANES), F32), pltpu.VMEM((2, tq), F32)] + x_scratch,
        compiler_params=_cp(("arbitrary", "arbitrary", "arbitrary")),
    )(q, k, v, do, lse, delta, fc3, fc3, *parts)


def _layer_norm_stats(u1):
    mu = jnp.mean(u1, axis=-1, keepdims=True)
    xc = u1 - mu
    rstd = lax.rsqrt(jnp.mean(xc * xc, axis=-1, keepdims=True) + EPS)
    return xc * rstd, rstd


def _shifted_copies(buf, sh, tm):
    rows = tm + HALO - SUBLANES
    for b in range(1, SUBLANES):
        sh[b - 1, 0:rows, :] = buf[b:b + rows, :]


def _window(buf, sh, off, rows, sl):
    a8, b = off // SUBLANES * SUBLANES, off % SUBLANES
    return buf[a8:a8 + rows, sl] if b == 0 else sh[b - 1, a8:a8 + rows, sl]


def _conv_fwd(proj, cw, cb, lng, lnb):
    T = proj.shape[0]
    tm = CONV_TM

    def body(a_ref, b_ref, w_ref, cb_ref, g_ref, bb_ref, u0_ref, u1_ref, u3_ref, buf, sh):
        i = pl.program_id(0)

        @pl.when(i == 0)
        def _():
            buf[0:HALO, :] = jnp.zeros((HALO, D), F32)

        u0 = a_ref[...].astype(F32) * _sigmoid(b_ref[...].astype(F32))
        u0_ref[...] = u0
        buf[HALO:HALO + tm, :] = u0
        _shifted_copies(buf, sh, tm)
        for j in range(D // LANES):
            sl = slice(j * LANES, (j + 1) * LANES)
            for r0 in range(0, tm, CONV_ROWS):
                acc = jnp.broadcast_to(cb_ref[:, sl], (CONV_ROWS, LANES))
                for kk in range(CONV_K):
                    acc = acc + w_ref[kk:kk + 1, sl] * _window(buf, sh, r0 + HALO - (CONV_K - 1) + kk, CONV_ROWS, sl)
                u1_ref[r0:r0 + CONV_ROWS, sl] = acc
        buf[0:HALO, :] = buf[tm:tm + HALO, :]
        xh, _ = _layer_norm_stats(u1_ref[...])
        u2 = xh * g_ref[...] + bb_ref[...]
        u3_ref[...] = (u2 * _sigmoid(u2)).astype(BF)

    ins = [(proj, _rspec(tm, D, 0)), (proj, _rspec(tm, D, 1)), (cw, _full(cw)), (cb, _full(cb)),
           (lng, _full(lng)), (lnb, _full(lnb))]
    outs = [_row_out(T, tm, D, F32), _row_out(T, tm, D, F32), _row_out(T, tm, D, BF)]
    return _rows_call(body, "conv_fwd", T // tm, ins, outs,
                      [pltpu.VMEM((tm + HALO, D), F32), pltpu.VMEM((SUBLANES - 1, tm + HALO, D), F32)])


def _conv_bwd(du3, u1, u0, proj, cw, lng, lnb, dgg):
    T = du3.shape[0]
    tm = CONV_TM
    n = T // tm
    per = tm // HALO

    def body(du3_ref, u1_ref, u0_ref, halo_ref, a_ref, b_ref, w_ref, g_ref, bb_ref, dgg_in_ref,
             dgl_ref, dg_ref, dbb_ref, dcb_ref, dw_ref, dbuf, ubuf, du0_buf, dsh, ush, dw8):
        i = pl.program_id(0)
        r = n - 1 - i

        @pl.when(i == 0)
        def _():
            dbuf[tm:tm + HALO, :] = jnp.zeros((HALO, D), F32)
            dg_ref[...] = jnp.zeros_like(dg_ref)
            dbb_ref[...] = jnp.zeros_like(dbb_ref)
            dcb_ref[...] = jnp.zeros_like(dcb_ref)
            dw8[...] = jnp.zeros_like(dw8)

        xh, rstd = _layer_norm_stats(u1_ref[...])
        g = g_ref[...]
        u2 = xh * g + bb_ref[...]
        s2 = _sigmoid(u2)
        du2 = du3_ref[...] * (s2 * (1.0 + u2 * (1.0 - s2)))
        dg_ref[...] += _colsum(du2 * xh)
        dbb_ref[...] += _colsum(du2)
        dxh = du2 * g
        du1 = rstd * (dxh - jnp.mean(dxh, axis=-1, keepdims=True) - xh * jnp.mean(dxh * xh, axis=-1, keepdims=True))
        dcb_ref[...] += _colsum(du1)
        dbuf[0:tm, :] = du1
        ubuf[HALO:HALO + tm, :] = u0_ref[...]
        ubuf[0:HALO, :] = jnp.where(r > 0, halo_ref[...], 0.0)
        _shifted_copies(dbuf, dsh, tm)
        _shifted_copies(ubuf, ush, tm)
        for j in range(D // LANES):
            sl = slice(j * LANES, (j + 1) * LANES)
            for r0 in range(0, tm, CONV_ROWS):
                d1 = dbuf[r0:r0 + CONV_ROWS, sl]
                acc = jnp.zeros((CONV_ROWS, LANES), F32)
                for kk in range(CONV_K):
                    acc = acc + w_ref[kk:kk + 1, sl] * _window(dbuf, dsh, r0 + CONV_K - 1 - kk, CONV_ROWS, sl)
                    prod = d1 * _window(ubuf, ush, r0 + HALO - (CONV_K - 1) + kk, CONV_ROWS, sl)
                    dw8[kk * SUBLANES:(kk + 1) * SUBLANES, sl] += prod.reshape(
                        CONV_ROWS // SUBLANES, SUBLANES, LANES).sum(axis=0)
                du0_buf[r0:r0 + CONV_ROWS, sl] = acc
        dbuf[tm:tm + HALO, :] = dbuf[0:HALO, :]
        du0 = du0_buf[...]
        af, bfl = a_ref[...].astype(F32), b_ref[...].astype(F32)
        sb = _sigmoid(bfl)
        dgl_ref[:, 0:D] = (du0 * sb).astype(BF)
        dgl_ref[:, D:2 * D] = (du0 * af * sb * (1.0 - sb)).astype(BF)

        @pl.when(i == n - 1)
        def _():
            for kk in range(CONV_KP):
                dw_ref[kk:kk + 1, :] = _colsum(dw8[kk * SUBLANES:(kk + 1) * SUBLANES, :])

    rs = lambda cb: _rspec(tm, D, cb, n)
    halo_spec = pl.BlockSpec((HALO, D), lambda i: (jnp.maximum((n - 1 - i) * per - 1, 0), 0))
    ins = [(du3, rs(0)), (u1, rs(0)), (u0, rs(0)), (u0, halo_spec), (proj, rs(0)), (proj, rs(1)),
           (cw, _full(cw)), (lng, _full(lng)), (lnb, _full(lnb)), (dgg, pl.BlockSpec(memory_space=pl.ANY))]
    outs = [(_sds(dgg.shape, dgg.dtype), _rspec(tm, 2 * D, 0, n)),
            _acc_out((1, D)), _acc_out((1, D)), _acc_out((1, D)), _acc_out((CONV_KP, D))]
    shifted = pltpu.VMEM((SUBLANES - 1, tm + HALO, D), F32)
    return _rows_call(body, "conv_bwd", n, ins, outs,
                      [pltpu.VMEM((tm + HALO, D), F32), pltpu.VMEM((tm + HALO, D), F32), pltpu.VMEM((tm, D), F32),
                       shifted, shifted, pltpu.VMEM((CONV_KP * SUBLANES, D), F32)], aliases={len(ins) - 1: 0})


def _merge(ba, bb, proj):
    T = ba.shape[0]
    tm = TM_ROWS

    def body(ba_ref, bb_ref, ga_ref, gb_ref, o_ref):
        sa, sb = _sigmoid(ga_ref[...].astype(F32)), _sigmoid(gb_ref[...].astype(F32))
        o_ref[...] = (sa * ba_ref[...] + sb * bb_ref[...]).astype(BF)

    ins = [(ba, _rspec(tm, D)), (bb, _rspec(tm, D)), (proj, _rspec(tm, D, 2)), (proj, _rspec(tm, D, 3))]
    return _rows_call(body, "merge", T // tm, ins, [_row_out(T, tm, D, BF)])[0]


def _post_out(x, mo, mod, n2g):
    T = x.shape[0]
    tm = TM_ROWS

    def body(x_ref, mo_ref, mod_ref, g_ref, x1_ref, h2_ref):
        g1 = mod_ref[:, 2 * D:3 * D]
        sh2, sc2 = mod_ref[:, 3 * D:4 * D], mod_ref[:, 4 * D:5 * D]
        x1 = x_ref[...] + g1 * mo_ref[...]
        x1_ref[...] = x1
        r = lax.rsqrt(jnp.mean(x1 * x1, axis=-1, keepdims=True) + EPS)
        h2_ref[...] = ((x1 * r) * g_ref[...] * (1.0 + sc2) + sh2).astype(BF)

    ins = [(x, _rspec(tm, D)), (mo, _rspec(tm, D)), (mod, _full(mod)), (n2g, _full(n2g))]
    return _rows_call(body, "post_out", T // tm, ins, [_row_out(T, tm, D, F32), _row_out(T, tm, D, BF)])


def _loss_head(x1, m2, tgt, mod):
    T = x1.shape[0]
    tm = TM_ROWS

    def body(x1_ref, m2_ref, t_ref, mod_ref, dy_ref, dm2_ref, dg2_ref, sq_ref):
        i = pl.program_id(0)

        @pl.when(i == 0)
        def _():
            dg2_ref[...] = jnp.zeros_like(dg2_ref)
            sq_ref[...] = jnp.zeros_like(sq_ref)

        g2 = mod_ref[:, 5 * D:6 * D]
        m2 = m2_ref[...]
        err = x1_ref[...] + g2 * m2 - t_ref[...]
        dy = err * (1.0 / D)
        dy_ref[...] = dy
        dm2_ref[...] = (g2 * dy).astype(BF)
        dg2_ref[...] += _colsum(dy * m2)
        sq_ref[...] += _colsum(err * err)

    ins = [(x1, _rspec(tm, D)), (m2, _rspec(tm, D)), (tgt, _rspec(tm, D)), (mod, _full(mod))]
    outs = [_row_out(T, tm, D, F32), _row_out(T, tm, D, BF), _acc_out((1, D)), _acc_out((1, D))]
    return _rows_call(body, "loss_head", T // tm, ins, outs)


def _norm2_bwd(dh2, x1, dy, mo, mod, n2g):
    T = x1.shape[0]
    tm = TM_ROWS

    def body(dh_ref, x1_ref, dy_ref, mo_ref, mod_ref, g_ref, dx1_ref, dmo_ref, dsh_ref, dsc_ref, dg_ref, dg1_ref):
        i = pl.program_id(0)

        @pl.when(i == 0)
        def _():
            for r in (dsh_ref, dsc_ref, dg_ref, dg1_ref):
                r[...] = jnp.zeros_like(r)

        g1, sc2 = mod_ref[:, 2 * D:3 * D], mod_ref[:, 4 * D:5 * D]
        g = g_ref[...]
        x1 = x1_ref[...]
        dh = dh_ref[...]
        r = lax.rsqrt(jnp.mean(x1 * x1, axis=-1, keepdims=True) + EPS)
        xn = x1 * r
        dsh_ref[...] += _colsum(dh)
        dsc_ref[...] += _colsum(dh * xn * g)
        dg_ref[...] += _colsum(dh * xn * (1.0 + sc2))
        dxn = dh * g * (1.0 + sc2)
        dx1 = dy_ref[...] + r * (dxn - xn * jnp.mean(dxn * xn, axis=-1, keepdims=True))
        dx1_ref[...] = dx1
        dg1_ref[...] += _colsum(dx1 * mo_ref[...])
        dmo_ref[...] = (g1 * dx1).astype(BF)

    ins = [(dh2, _rspec(tm, D)), (x1, _rspec(tm, D)), (dy, _rspec(tm, D)), (mo, _rspec(tm, D)),
           (mod, _full(mod)), (n2g, _full(n2g))]
    outs = [_row_out(T, tm, D, F32), _row_out(T, tm, D, BF)] + [_acc_out((1, D)) for _ in range(4)]
    return _rows_call(body, "norm2_bwd", T // tm, ins, outs)


def _gate_bwd(dmerged, ba, bb, proj):
    T = ba.shape[0]
    tm = TM_ROWS

    def body(dm_ref, ba_ref, bb_ref, ga_ref, gb_ref, dba_ref, dbb_ref, dgt_ref):
        dm = dm_ref[...]
        sa, sb = _sigmoid(ga_ref[...].astype(F32)), _sigmoid(gb_ref[...].astype(F32))
        dba_ref[...] = (dm * sa).astype(BF)
        dbb_ref[...] = (dm * sb).astype(BF)
        dgt_ref[:, 0:D] = (dm * ba_ref[...] * sa * (1.0 - sa)).astype(BF)
        dgt_ref[:, D:2 * D] = (dm * bb_ref[...] * sb * (1.0 - sb)).astype(BF)

    ins = [(dmerged, _rspec(tm, D)), (ba, _rspec(tm, D)), (bb, _rspec(tm, D)),
           (proj, _rspec(tm, D, 2)), (proj, _rspec(tm, D, 3))]
    outs = [_row_out(T, tm, D, BF), _row_out(T, tm, D, BF), (_sds((T, 4 * D), BF), _rspec(tm, 2 * D, 1))]
    return _rows_call(body, "gate_bwd", T // tm, ins, outs)


def _qkv_bwd(dq, dk, dv, proj, f, dfc, dfq, qg2, kg2, bf_pad):
    T = proj.shape[0]
    tm = TM_ROWS
    n = T // tm
    seg = _seg_mat()
    tri = _tri_mat(tm, False)

    def body(dq_ref, dk_ref, dv_ref, q_ref, k_ref, f_ref, dfc_ref, dfq_ref, qg_ref, kg_ref, bf_ref, seg_ref, tri_ref,
             dqkv_ref, dfo_ref, dqg_ref, dkg_ref, dbf_ref, carry_ref):
        i = pl.program_id(0)

        @pl.when(i == 0)
        def _():
            carry_ref[...] = jnp.zeros_like(carry_ref)
            dqg_ref[...] = jnp.zeros_like(dqg_ref)
            dkg_ref[...] = jnp.zeros_like(dkg_ref)
            dbf_ref[...] = jnp.zeros_like(dbf_ref)

        segm = seg_ref[...]
        dqg = jnp.zeros((1, LANES), F32)
        dkg = jnp.zeros((1, LANES), F32)
        for j in range(D // LANES):
            sl = slice(j * LANES, (j + 1) * LANES)
            for (raw_ref, d_ref, gn_ref, which) in ((q_ref, dq_ref, qg_ref, 0), (k_ref, dk_ref, kg_ref, 1)):
                xc = raw_ref[:, sl].astype(F32)
                rr = lax.rsqrt(_dot_exact(xc * xc, segm) + EPS)
                xn = xc * rr
                dc = d_ref[:, sl]
                if which == 0:
                    dqg = dqg + _colsum(dc * xn)
                else:
                    dkg = dkg + _colsum(dc * xn)
                dxn = dc * gn_ref[...]
                osl = slice(which * D + j * LANES, which * D + (j + 1) * LANES)
                dqkv_ref[:, osl] = (rr * (dxn - xn * _dot_exact(dxn * xn, segm))).astype(BF)
        dqg_ref[...] += dqg
        dkg_ref[...] += dkg
        dqkv_ref[:, 2 * D:3 * D] = dv_ref[...].astype(BF)
        z = f_ref[...] + bf_ref[...]
        sneg_t = _sigmoid(-z).T[0:N_HEADS, :]
        dfc = dfc_ref[...] + dfq_ref[...]
        carry = carry_ref[:, 0:1]
        dlf = _dot_exact(dfc, tri_ref[...]) + carry
        carry_ref[...] = jnp.broadcast_to(carry + jnp.sum(dfc, axis=1, keepdims=True), carry_ref.shape)
        dzt = dlf * sneg_t
        dz = jnp.concatenate([dzt, jnp.zeros((LANES - N_HEADS, tm), F32)], axis=0).T
        dbf_ref[...] += _colsum(dz)
        dfo_ref[...] = dz.astype(BF)

    rs = lambda w, cb=0: _rspec(tm, w, cb, n)
    ins = [(dq, rs(D)), (dk, rs(D)), (dv, rs(D)), (proj, rs(D, 0)), (proj, rs(D, 1)), (f, rs(LANES)),
           (dfc, pl.BlockSpec((N_HEADS, tm), lambda i: (0, n - 1 - i))),
           (dfq, pl.BlockSpec((N_HEADS, tm), lambda i: (0, n - 1 - i))),
           (qg2, _full(qg2)), (kg2, _full(kg2)), (bf_pad, _full(bf_pad)), (seg, _full(seg)), (tri, _full(tri))]
    outs = [_row_out(T, tm, 3 * D, BF, n), _row_out(T, tm, LANES, BF, n),
            _acc_out((1, LANES)), _acc_out((1, LANES)), _acc_out((1, LANES))]
    return _rows_call(body, "qkv_bwd", n, ins, outs, [pltpu.VMEM((N_HEADS, LANES), F32)])


def _norm1_bwd(dh, dhf, x, dx1, mod, n1g):
    T = x.shape[0]
    tm = TM_ROWS

    def body(dh_ref, dhf_ref, x_ref, dx1_ref, mod_ref, g_ref, dx_ref, dsh_ref, dsc_ref, dg_ref):
        i = pl.program_id(0)

        @pl.when(i == 0)
        def _():
            for r in (dsh_ref, dsc_ref, dg_ref):
                r[...] = jnp.zeros_like(r)

        sc1 = mod_ref[:, D:2 * D]
        g = g_ref[...]
        xv = x_ref[...]
        dh = dh_ref[...] + dhf_ref[...]
        r = lax.rsqrt(jnp.mean(xv * xv, axis=-1, keepdims=True) + EPS)
        xn = xv * r
        dsh_ref[...] += _colsum(dh)
        dsc_ref[...] += _colsum(dh * xn * g)
        dg_ref[...] += _colsum(dh * xn * (1.0 + sc1))
        dxn = dh * g * (1.0 + sc1)
        dx_ref[...] = dx1_ref[...] + r * (dxn - xn * jnp.mean(dxn * xn, axis=-1, keepdims=True))

    ins = [(dh, _rspec(tm, D)), (dhf, _rspec(tm, D)), (x, _rspec(tm, D)), (dx1, _rspec(tm, D)),
           (mod, _full(mod)), (n1g, _full(n1g))]
    outs = [_row_out(T, tm, D, F32)] + [_acc_out((1, D)) for _ in range(3)]
    return _rows_call(body, "norm1_bwd", T // tm, ins, outs)


def _adamw_math(w, g, m, v):
    m = ADAM_B1 * m + (1.0 - ADAM_B1) * g
    v = ADAM_B2 * v + (1.0 - ADAM_B2) * (g * g)
    m_hat = m / (1.0 - ADAM_B1 ** ADAM_STEP)
    v_hat = v / (1.0 - ADAM_B2 ** ADAM_STEP)
    delta = -ADAM_LR * (m_hat / (jnp.sqrt(v_hat) + ADAM_EPS) + ADAM_WD * w)
    return delta, m, v


def _adamw(parts, w, m, v, name):
    n, R, C = parts.shape
    tr = R if R <= 256 else 256
    assert R % tr == 0

    def body(p_ref, w_ref, m_ref, v_ref, g_ref, d_ref, mo_ref, vo_ref):
        g = p_ref[0].astype(F32)
        for s in range(1, n):
            g = g + p_ref[s].astype(F32)
        g_ref[...] = g
        d_ref[...], mo_ref[...], vo_ref[...] = _adamw_math(w_ref[...], g, m_ref[...], v_ref[...])

    spec = pl.BlockSpec((None, tr, C), lambda i: (0, i, 0))
    return pl.pallas_call(
        body, name=name, grid=(R // tr,),
        in_specs=[pl.BlockSpec((n, tr, C), lambda i: (0, i, 0)), spec, spec, spec],
        out_specs=[spec] * 4, out_shape=[_sds((1, R, C), F32)] * 4,
        compiler_params=_cp(("parallel",)),
    )(parts, w, m, v)


def _rcopy(src, dst, ssem, rsem, peer):
    return pltpu.make_async_remote_copy(src_ref=src, dst_ref=dst, send_sem=ssem, recv_sem=rsem,
                                        device_id=peer, device_id_type=MESH)


def _ada_fwd(c, w_ada, b_slice, cw_shard):
    def body(c_ref, w_ref, b_ref, cw_ref, mod_ref, ca_ref, cwf_ref, call, mp, ssem, rsem):
        x, y, cc, me = _my_pos()
        call[pl.ds(me, 1), :] = c_ref[...]
        cwf_ref[me] = cw_ref[...]
        first = []
        for d in range(1, N_DEV):
            peer, _ = _peer(x, y, cc, d)
            first.append(_rcopy(c_ref, call.at[pl.ds(me, 1), :], ssem.at[0, d - 1], rsem.at[0, d - 1], peer))
            first.append(_rcopy(cw_ref, cwf_ref.at[me], ssem.at[1, d - 1], rsem.at[1, d - 1], peer))
        for cp in first:
            cp.start()
        for d in range(1, N_DEV):
            peer, pid = _peer(x, y, cc, d)
            _rcopy(c_ref, call.at[pl.ds(pid, 1), :], ssem.at[0, d - 1], rsem.at[0, d - 1], peer).wait_recv()
            _rcopy(cw_ref, cwf_ref.at[pid], ssem.at[1, d - 1], rsem.at[1, d - 1], peer).wait_recv()
        cv = call[...]
        ca = cv * _sigmoid(cv)
        ca_ref[...] = ca
        mp[...] = _dot_f32(ca, w_ref[...]) + b_ref[...]
        mod_ref[pl.ds(me, 1), :] = mp[pl.ds(me, 1), :]
        second = []
        for d in range(1, N_DEV):
            peer, pid = _peer(x, y, cc, d)
            second.append(_rcopy(mp.at[pl.ds(pid, 1), :], mod_ref.at[pl.ds(me, 1), :], ssem.at[2, d - 1], rsem.at[2, d - 1], peer))
        for cp in second:
            cp.start()
        for d in range(1, N_DEV):
            peer, pid = _peer(x, y, cc, d)
            _rcopy(mp.at[pl.ds(pid, 1), :], mod_ref.at[pl.ds(pid, 1), :], ssem.at[2, d - 1], rsem.at[2, d - 1], peer).wait_recv()
        for cp in first + second:
            cp.wait_send()

    vm = pl.BlockSpec(memory_space=pltpu.VMEM)
    return pl.pallas_call(
        body, name="ada_fwd",
        in_specs=[vm, vm, vm, vm], out_specs=[vm, vm, vm],
        out_shape=[_sds((N_DEV, ADA_SHARD), F32), _sds((N_DEV, D), F32), _sds((N_DEV, CONV_KP, LANES), F32)],
        scratch_shapes=[pltpu.VMEM((N_DEV, D), F32), pltpu.VMEM((N_DEV, ADA_SHARD), F32),
                        pltpu.SemaphoreType.DMA((3, N_DEV - 1)), pltpu.SemaphoreType.DMA((3, N_DEV - 1))],
        compiler_params=pltpu.CompilerParams(vmem_limit_bytes=VMEM_LIMIT),
    )(c, w_ada, b_slice, cw_shard)


def _xchg_parts(arrays):
    n = len(arrays)
    anyspec = pl.BlockSpec(memory_space=pl.ANY)
    slots = lambda a: a.shape[0] if (a.ndim == 3 and a.shape[0] == N_DEV // 2) else N_DEV
    shapes = [_sds((slots(a),) + tuple(a.shape[-2:]), a.dtype) for a in arrays]
    scratch = [pltpu.SemaphoreType.DMA((n,)), pltpu.SemaphoreType.DMA((n, N_DEV - 1)),
               pltpu.SemaphoreType.DMA((n, N_DEV - 1))]
    return [anyspec] * n, shapes, scratch


def _xchg(ins, outs, sems, gather, wait):
    lsem, ssem, rsem = sems
    x, y, cc, me = _my_pos()
    for a in range(len(ins)):
        if not gather and ins[a].shape[0] == N_DEV // 2:
            chip = 2 * x + y
            local = pltpu.make_async_copy(ins[a].at[chip], outs[a].at[chip], lsem.at[a])
            if not wait:
                local.start()
            for d in range(1, N_DEV // 2):
                px, py = ((1 - x) if d & 2 else x), ((1 - y) if d & 1 else y)
                pchip = 2 * px + py
                if not wait:
                    _rcopy(ins[a].at[pchip], outs[a].at[chip], ssem.at[a, d - 1], rsem.at[a, d - 1], (px, py, cc)).start()
                else:
                    cp = _rcopy(ins[a].at[pchip], outs[a].at[pchip], ssem.at[a, d - 1], rsem.at[a, d - 1], (px, py, cc))
                    cp.wait_recv()
                    cp.wait_send()
            if wait:
                local.wait()
            continue
        local = pltpu.make_async_copy(ins[a] if gather else ins[a].at[me], outs[a].at[me], lsem.at[a])
        if not wait:
            local.start()
        for d in range(1, N_DEV):
            peer, pid = _peer(x, y, cc, d)
            src = ins[a] if gather else ins[a].at[pid]
            if not wait:
                _rcopy(src, outs[a].at[me], ssem.at[a, d - 1], rsem.at[a, d - 1], peer).start()
            else:
                cp = _rcopy(src, outs[a].at[pid], ssem.at[a, d - 1], rsem.at[a, d - 1], peer)
                cp.wait_recv()
                cp.wait_send()
        if wait:
            local.wait()


def _pair_reduce(part):
    n, R, C = part.shape
    half = n // 2
    tr = R if R <= 256 else 256
    assert n == N_DEV and R % tr == 0

    def swap(p_ref, got_ref, ssem, rsem):
        x, y, c, _ = _my_pos()
        cps = [_rcopy(p_ref.at[2 * i + (1 - c)], got_ref.at[i], ssem.at[i], rsem.at[i], (x, y, 1 - c)) for i in range(half)]
        for cp in cps:
            cp.start()
        for cp in cps:
            cp.wait_recv()
            cp.wait_send()

    anyspec = pl.BlockSpec(memory_space=pl.ANY)
    got = pl.pallas_call(
        swap, name="dwin_pair_swap", in_specs=[anyspec], out_specs=anyspec, out_shape=_sds((half, R, C), part.dtype),
        scratch_shapes=[pltpu.SemaphoreType.DMA((half,)), pltpu.SemaphoreType.DMA((half,))],
    )(part)

    def add(p_ref, g_ref, o_ref):
        c = lax.axis_index("c")
        o_ref[...] = (p_ref[c].astype(F32) + g_ref[...].astype(F32)).astype(o_ref.dtype)

    spec = pl.BlockSpec((None, tr, C), lambda i, r: (i, r, 0))
    return pl.pallas_call(
        add, name="dwin_pair_add", grid=(half, R // tr),
        in_specs=[pl.BlockSpec((2, tr, C), lambda i, r: (i, r, 0)), spec],
        out_specs=spec, out_shape=_sds((half, R, C), part.dtype),
        compiler_params=_cp(("parallel", "parallel")),
    )(part, got)


def _gather_two_level(shard, name):
    def body(x_ref, out_ref, ssem, rsem, lsem):
        x, y, c, me = _my_pos()
        sibling = (x, y, 1 - c)
        chips = [(1 - x, y), (x, 1 - y), (1 - x, 1 - y)]
        slot = lambda px, py, pc: out_ref.at[4 * px + 2 * py + pc]

        def copy(kk, block, to, src=None):
            return _rcopy(slot(*block) if src is None else src, slot(*block), ssem.at[kk], rsem.at[kk], to)

        mine = pltpu.make_async_copy(x_ref, slot(x, y, c), lsem)
        mine.start()
        first = [copy(0, (x, y, c), sibling, src=x_ref)]
        first += [copy(1 + j, (x, y, c), (*chip, c), src=x_ref) for j, chip in enumerate(chips)]
        for cp in first:
            cp.start()
        passed = [copy(4 + j, (*chip, c), sibling) for j, chip in enumerate(chips)]
        for j, chip in enumerate(chips):
            copy(1 + j, (*chip, c), (x, y, c)).wait_recv()
            passed[j].start()
        copy(0, sibling, (x, y, c)).wait_recv()
        for j, chip in enumerate(chips):
            copy(4 + j, (*chip, 1 - c), (x, y, c)).wait_recv()
        for cp in first + passed:
            cp.wait_send()
        mine.wait()

    anyspec = pl.BlockSpec(memory_space=pl.ANY)
    return pl.pallas_call(
        body, name=name, in_specs=[anyspec], out_specs=anyspec,
        out_shape=_sds((N_DEV,) + tuple(shard.shape), shard.dtype),
        scratch_shapes=[pltpu.SemaphoreType.DMA((N_DEV - 1,)), pltpu.SemaphoreType.DMA((N_DEV - 1,)),
                        pltpu.SemaphoreType.DMA(())],
    )(shard)


PACK_ROWS = 16
ROW_MISC = 5
ROW_LOSS = 6
ROW_DMOD = 8


def _small_bwd(pack, dmodb, dcw, cat, wp, mp_, vp, cw_w, cw_m, cw_v):
    def body(pack_ref, dmodb_ref, dcw_ref, cat_ref, wp_ref, mp_ref, vp_ref, cww_ref, cwm_ref, cwv_ref,
             g_ref, d_ref, mo_ref, vo_ref, cg_ref, cd_ref, cm_ref, cv_ref, gwa_ref, loss_ref,
             allp, dmc, cwg, ssem, rsem):
        x, y, cc, me = _my_pos()
        allp[me] = pack_ref[...]
        dmc[pl.ds(me, 1), :] = dmodb_ref[pl.ds(me, 1), :]
        cwg[me] = dcw_ref[me]
        sends = []
        for d in range(1, N_DEV):
            peer, pid = _peer(x, y, cc, d)
            sends.append(_rcopy(pack_ref, allp.at[me], ssem.at[0, d - 1], rsem.at[0, d - 1], peer))
            sends.append(_rcopy(dmodb_ref.at[pl.ds(pid, 1), :], dmc.at[pl.ds(me, 1), :], ssem.at[1, d - 1], rsem.at[1, d - 1], peer))
            sends.append(_rcopy(dcw_ref.at[pid], cwg.at[me], ssem.at[2, d - 1], rsem.at[2, d - 1], peer))
        for cp in sends:
            cp.start()
        for d in range(1, N_DEV):
            peer, pid = _peer(x, y, cc, d)
            _rcopy(pack_ref, allp.at[pid], ssem.at[0, d - 1], rsem.at[0, d - 1], peer).wait_recv()
            _rcopy(dmodb_ref.at[pl.ds(pid, 1), :], dmc.at[pl.ds(pid, 1), :], ssem.at[1, d - 1], rsem.at[1, d - 1], peer).wait_recv()
            _rcopy(dcw_ref.at[pid], cwg.at[pid], ssem.at[2, d - 1], rsem.at[2, d - 1], peer).wait_recv()
        for cp in sends:
            cp.wait_send()

        tot = allp[0]
        cg = cwg[0]
        for s in range(1, N_DEV):
            tot = tot + allp[s]
            cg = cg + cwg[s]
        lane = lax.broadcasted_iota(jnp.int32, (PACK_ROWS, D), 1)
        row = lax.broadcasted_iota(jnp.int32, (PACK_ROWS, D), 0)
        gains = (row == ROW_MISC) & (lane >= LANES) & (lane < 3 * LANES)
        folded = tot + pltpu.roll(tot, D - HEAD_DIM, axis=1)
        keep = (lane % LANES) < HEAD_DIM
        g = jnp.where(gains, jnp.where(keep, folded, 0.0), tot)
        loss_ref[...] = jnp.broadcast_to(
            (0.5 / D) * jnp.sum(jnp.where(row == ROW_LOSS, tot, 0.0), keepdims=True).reshape(1, 1), loss_ref.shape)
        g = jnp.where(row == ROW_LOSS, 0.0, g)
        g_ref[...] = g
        d_ref[...], mo_ref[...], vo_ref[...] = _adamw_math(wp_ref[...], g, mp_ref[...], vp_ref[...])
        cg_ref[...] = cg
        cd_ref[...], cm_ref[...], cv_ref[...] = _adamw_math(cww_ref[...], cg, cwm_ref[...], cwv_ref[...])
        dm_pad = jnp.concatenate([dmc[...], jnp.zeros((LANES - N_DEV, ADA_SHARD), F32)], axis=0)
        gwa_ref[...] = _dot_f32(cat_ref[...], dm_pad)

    vm = pl.BlockSpec(memory_space=pltpu.VMEM)
    p16 = _sds((PACK_ROWS, D), F32)
    c32 = _sds((CONV_KP, LANES), F32)
    return pl.pallas_call(
        body, name="small_bwd",
        in_specs=[vm] * 10, out_specs=[vm] * 10,
        out_shape=[p16, p16, p16, p16, c32, c32, c32, c32, _sds((D, ADA_SHARD), F32), _sds((8, LANES), F32)],
        scratch_shapes=[pltpu.VMEM((N_DEV, PACK_ROWS, D), F32), pltpu.VMEM((N_DEV, ADA_SHARD), F32),
                        pltpu.VMEM((N_DEV, CONV_KP, LANES), F32),
                        pltpu.SemaphoreType.DMA((3, N_DEV - 1)), pltpu.SemaphoreType.DMA((3, N_DEV - 1))],
        compiler_params=pltpu.CompilerParams(vmem_limit_bytes=VMEM_LIMIT),
    )(pack, dmodb, dcw, cat, wp, mp_, vp, cw_w, cw_m, cw_v)


def _lanes(vec, start, total=D):
    n = vec.shape[1]
    return jnp.pad(vec, ((0, 0), (start, total - start - n)))


def _pack_small(rows5, misc, loss_row, six):
    z = jnp.zeros((1, D), F32)
    return jnp.concatenate(rows5 + [misc, loss_row, z] + [six.reshape(N_ADA, D), z, z], axis=0)


def kernel(x, c, w_ada, b_ada, norm1_g, w_in, b_forget, q_norm_g, k_norm_g, w_attn_proj, conv_w, conv_b, conv_ln_g, conv_ln_b, w_conv_proj, w_out, norm2_g, w_mlp1, w_mlp2, loss_target, m_w_ada, m_b_ada, m_norm1_g, m_w_in, m_b_forget, m_q_norm_g, m_k_norm_g, m_w_attn_proj, m_conv_w, m_conv_b, m_conv_ln_g, m_conv_ln_b, m_w_conv_proj, m_w_out, m_norm2_g, m_w_mlp1, m_w_mlp2, v_w_ada, v_b_ada, v_norm1_g, v_w_in, v_b_forget, v_q_norm_g, v_k_norm_g, v_w_attn_proj, v_conv_w, v_conv_b, v_conv_ln_g, v_conv_ln_b, v_w_conv_proj, v_w_out, v_norm2_g, v_w_mlp1, v_w_mlp2):
    me = 4 * lax.axis_index("x") + 2 * lax.axis_index("y") + lax.axis_index("c")
    xs, tgt = x[0], loss_target[0]
    T = xs.shape[0]
    sq = lambda a: a[0]
    pad_taps = lambda a: jnp.pad(a[0], ((0, CONV_KP - CONV_K), (0, 0)))

    b_slice = lax.dynamic_slice(b_ada, (0, me * ADA_SHARD), (1, ADA_SHARD))
    modb, ca_all, cwf = _ada_fwd(c, sq(w_ada), b_slice, pad_taps(conv_w))
    mod = modb.reshape(1, N_ADA * D)
    cw = jnp.transpose(cwf, (1, 0, 2)).reshape(CONV_KP, D)

    g_in = _gather_two_level(sq(w_in).astype(BF), "w_in_gather")
    d_in = g_in.shape[2] * N_DEV
    w_in_f = jnp.transpose(g_in, (1, 0, 2)).reshape(D, d_in)
    w_qkv = w_in_f[:, :3 * D]
    w_gg = w_in_f[:, 3 * D + N_HEADS:]
    w_f = jnp.pad(w_in_f[:, 3 * D:3 * D + N_HEADS], ((0, 0), (0, LANES - N_HEADS)))
    shards = [sq(w_attn_proj).astype(BF), sq(w_conv_proj).astype(BF), sq(w_out).astype(BF),
              sq(w_mlp1).astype(BF), sq(w_mlp2).astype(BF)]

    qg2 = jnp.tile(q_norm_g, (1, 2))
    kg2 = jnp.tile(k_norm_g, (1, 2))
    bf_pad = _lanes(b_forget, 0, LANES)

    h = _pre_in(xs, mod, norm1_g)
    pqkv = _matmul(h, w_qkv, "nn", BF, "mm_proj_qkv")
    pgg = _matmul(h, w_gg, "nn", BF, "mm_proj_gg")
    f = _matmul(h, w_f, "nn", F32, "mm_f")
    q, k, v, fc = _qkv_post(pqkv, f, qg2, kg2, bf_pad)
    fc3 = fc.reshape(N_HEADS // 2, 2, T)
    o, lse, g_ap, g_cp, g_out, g_1, g_2 = _flash_fwd(q, k, v, fc3, shards)
    w_ap, w_cp, w_o = g_ap.reshape(D, D), g_cp.reshape(D, D), g_out.reshape(D, D)
    w_2 = g_2.reshape(D_FF, D)
    ba = _matmul(o, w_ap, "nn", F32, "mm_ba")
    u0, u1, u3 = _conv_fwd(pgg, cw, conv_b, conv_ln_g, conv_ln_b)
    bb = _matmul(u3, w_cp, "nn", F32, "mm_bb")
    merged = _merge(ba, bb, pgg)
    mo = _matmul(merged, w_o, "nn", F32, "mm_out")
    x1, h2 = _post_out(xs, mo, mod, norm2_g)
    a, rl = _matmul(h2, g_1, "nn", BF, "mm_mlp1", relu2=True, b_slots=True)
    m2 = _matmul(rl, w_2, "nn", F32, "mm_mlp2")
    dy, dm2, dg2, sqcols = _loss_head(x1, m2, tgt, mod)

    da = _matmul(dm2, w_2, "nt", BF, "mm_drl", relu_of=a)
    dw_2 = _matmul(rl, dm2, "tn", BF, "mm_dw2")
    dh2 = _matmul(da, g_1, "nt", F32, "mm_dh2", b_slots=True)
    dw_1 = _matmul(h2, da, "tn", BF, "mm_dw1", out_slots=True)
    dx1, dmo, dsh2, dsc2, dn2g, dg1 = _norm2_bwd(dh2, x1, dy, mo, mod, norm2_g)
    dmerged = _matmul(dmo, w_o, "nt", F32, "mm_dmerged")
    dw_o = _matmul(merged, dmo, "tn", BF, "mm_dwout")
    dba, dbb, dgg = _gate_bwd(dmerged, ba, bb, pgg)
    du3 = _matmul(dbb, w_cp, "nt", F32, "mm_du3")
    dw_cp = _matmul(u3, dbb, "tn", BF, "mm_dwcp")
    do = _matmul(dba, w_ap, "nt", F32, "mm_do")
    dw_ap = _matmul(o, dba, "tn", BF, "mm_dwap")
    dgg, dlng, dlnb, dcb, dcw_full = _conv_bwd(du3, u1, u0, pgg, cw, conv_ln_g, conv_ln_b, dgg)
    delta = _attn_delta(do, o)
    dw_gg = _matmul(h, dgg, "tn", BF, "mm_dw_gg")
    parts = [dw_ap.reshape(N_DEV, D // N_DEV, D), dw_cp.reshape(N_DEV, D // N_DEV, D), dw_o.reshape(N_DEV, D // N_DEV, D),
             dw_1, dw_2.reshape(N_DEV, D_FF // N_DEV, D)]
    dq, rs_a, rs_b, dk, dv, dfc3, r_ap, r_cp, r_out, r_1, r_2 = _flash_bwd(q, k, v, do, lse, delta, fc3, parts)
    dfq = jnp.stack([rs_a, rs_b], axis=1).reshape(N_HEADS, T)
    dqkv, df, dqg, dkg, dbf = _qkv_bwd(dq, dk, dv, pqkv, f, dfc3.reshape(N_HEADS, T), dfq, qg2, kg2, bf_pad)
    dw_qkv = _matmul(h, dqkv, "tn", BF, "mm_dw_qkv")
    dw_f = _matmul(h, df, "tn", BF, "mm_dwf")
    dw_in_f = jnp.concatenate([dw_qkv, dw_f[:, :N_HEADS], dw_gg], axis=1)
    part_in = jnp.transpose(dw_in_f.reshape(D, N_DEV, d_in // N_DEV), (1, 0, 2))
    dh, r_in = _matmul(dqkv, w_qkv, "nt", F32, "mm_dh", scatter=(_pair_reduce(part_in),), more=((dgg, w_gg),))
    dhf = _matmul(df, w_f, "nt", F32, "mm_dhf")
    grad_x, dsh1, dsc1, dn1g = _norm1_bwd(dh, dhf, xs, dx1, mod, norm1_g)

    dmod = jnp.concatenate([dsh1, dsc1, dg1, dsh2, dsc2, dg2], axis=1)
    misc = jnp.concatenate([dbf, dqg, dkg, jnp.zeros((1, D - 3 * LANES), F32)], axis=1)
    pack = _pack_small([dn1g, dcb, dlng, dlnb, dn2g], misc, sqcols, dmod)
    dcw_blocks = jnp.transpose(dcw_full.reshape(CONV_KP, N_DEV, LANES), (1, 0, 2))

    def small_params(b_a, n1, bfg, qn, kn, cvb, lg, lb, n2):
        misc_p = jnp.concatenate([_lanes(bfg, 0, LANES), _lanes(qn, 0, LANES), _lanes(kn, 0, LANES),
                                  jnp.zeros((1, D - 3 * LANES), F32)], axis=1)
        return _pack_small([n1, cvb, lg, lb, n2], misc_p, jnp.zeros((1, D), F32), b_a)

    wp = small_params(b_ada, norm1_g, b_forget, q_norm_g, k_norm_g, conv_b, conv_ln_g, conv_ln_b, norm2_g)
    mp_ = small_params(m_b_ada, m_norm1_g, m_b_forget, m_q_norm_g, m_k_norm_g, m_conv_b, m_conv_ln_g, m_conv_ln_b, m_norm2_g)
    vp = small_params(v_b_ada, v_norm1_g, v_b_forget, v_q_norm_g, v_k_norm_g, v_conv_b, v_conv_ln_g, v_conv_ln_b, v_norm2_g)
    cat = jnp.pad(jnp.transpose(ca_all), ((0, 0), (0, LANES - N_DEV)))
    small = _small_bwd(pack, dmod.reshape(N_DEV, ADA_SHARD), dcw_blocks, cat,
                       wp, mp_, vp, pad_taps(conv_w), pad_taps(m_conv_w), pad_taps(v_conv_w))
    sp = small[0:4]
    scw = small[4:8]
    gw_ada, loss_t = small[8], small[9]
    loss = loss_t[0, 0]

    def unpack(p):
        misc_r = p[ROW_MISC:ROW_MISC + 1]
        return dict(
            b_ada=p[ROW_DMOD:ROW_DMOD + N_ADA].reshape(1, N_ADA * D), norm1_g=p[0:1], conv_b=p[1:2], conv_ln_g=p[2:3],
            conv_ln_b=p[3:4], norm2_g=p[4:5], b_forget=misc_r[:, 0:N_HEADS],
            q_norm_g=misc_r[:, LANES:LANES + HEAD_DIM], k_norm_g=misc_r[:, 2 * LANES:2 * LANES + HEAD_DIM])

    res = {}
    res["w_ada"] = _adamw(gw_ada[None], w_ada, m_w_ada, v_w_ada, "adamw_w_ada")
    res["w_in"] = _adamw(r_in, w_in, m_w_in, v_w_in, "adamw_w_in")
    res["w_attn_proj"] = _adamw(r_ap, w_attn_proj, m_w_attn_proj, v_w_attn_proj, "adamw_w_ap")
    res["w_conv_proj"] = _adamw(r_cp, w_conv_proj, m_w_conv_proj, v_w_conv_proj, "adamw_w_cp")
    res["w_out"] = _adamw(r_out, w_out, m_w_out, v_w_out, "adamw_w_out")
    res["w_mlp1"] = _adamw(r_1, w_mlp1, m_w_mlp1, v_w_mlp1, "adamw_w_mlp1")
    res["w_mlp2"] = _adamw(r_2, w_mlp2, m_w_mlp2, v_w_mlp2, "adamw_w_mlp2")

    names = ["w_ada", "b_ada", "norm1_g", "w_in", "b_forget", "q_norm_g", "k_norm_g", "w_attn_proj", "conv_w", "conv_b",
             "conv_ln_g", "conv_ln_b", "w_conv_proj", "w_out", "norm2_g", "w_mlp1", "w_mlp2"]
    outs = [loss, grad_x[None]]
    for kind in range(4):
        small_d = unpack(sp[kind])
        for nm in names:
            if nm in res:
                outs.append(res[nm][kind])
            elif nm == "conv_w":
                outs.append(scw[kind][:CONV_K][None])
            else:
                outs.append(small_d[nm])
    return tuple(outs)
```

```python
import functools

import jax
import jax.numpy as jnp
from jax import lax
from jax.experimental import pallas as pl
from jax.experimental.pallas import tpu as pltpu

F32 = jnp.float32
BF = jnp.bfloat16

N_DEV = 8
D = 1024
N_HEADS = 16
HEAD_DIM = 64
LANES = 128
SUBLANES = 8
CONV_K = 31
CONV_KP = 32
HALO = 32
CONV_ROWS = 64
D_FF = 4 * D
N_ADA = 6
ADA_SHARD = N_ADA * D // N_DEV
EPS = 1e-6
QK_SCALE = HEAD_DIM ** -0.5
LOG2E = 1.4426950408889634
LN2 = 0.6931471805599453
NEG = -1e30

ADAM_LR = 0.001
ADAM_B1 = 0.9
ADAM_B2 = 0.999
ADAM_EPS = 1e-08
ADAM_WD = 0.01
ADAM_STEP = 10

VMEM_LIMIT = 56 * 1024 * 1024
TM_ROWS = 512
CONV_TM = 256
TQ = 512

MESH = pl.DeviceIdType.MESH


def _cp(sem=None):
    return pltpu.CompilerParams(dimension_semantics=sem, vmem_limit_bytes=VMEM_LIMIT)


def _sds(shape, dtype):
    return jax.ShapeDtypeStruct(tuple(shape), dtype)


def _full(arr):
    nd = arr.ndim
    return pl.BlockSpec(arr.shape, lambda *_: (0,) * nd)


def _fullshape(shape):
    nd = len(shape)
    return pl.BlockSpec(tuple(shape), lambda *_: (0,) * nd)


def _split3(x):
    hi = x.astype(BF)
    r1 = x - hi.astype(F32)
    mid = r1.astype(BF)
    lo = (r1 - mid.astype(F32)).astype(BF)
    return hi, mid, lo


def _dot_exact(x, mat):
    hi, mid, lo = _split3(x)
    d = lambda t: jnp.dot(t, mat, preferred_element_type=F32)
    return d(hi) + d(mid) + d(lo)


def _dot_f32(a, b):
    a1, a2, a3 = _split3(a)
    b1, b2, b3 = _split3(b)
    d = lambda s, t: jnp.dot(s, t, preferred_element_type=F32)
    return (d(a1, b3) + d(a3, b1) + d(a2, b2)) + (d(a1, b2) + d(a2, b1)) + d(a1, b1)


def _sigmoid(x):
    return 1.0 / (1.0 + jnp.exp(-x))


def _colsum(x):
    return jnp.sum(x, axis=0, keepdims=True)


def _my_pos():
    x, y, c = lax.axis_index("x"), lax.axis_index("y"), lax.axis_index("c")
    return x, y, c, 4 * x + 2 * y + c


def _peer(x, y, c, d):
    px = (1 - x) if d & 4 else x
    py = (1 - y) if d & 2 else y
    pc = (1 - c) if d & 1 else c
    return (px, py, pc), 4 * px + 2 * py + pc


def _matmul(a, b, form, out_dtype, name, tm=1024, tn=1024, tk=1024, scatter=(), relu2=False, relu_of=None,
            b_slots=False, out_slots=False, more=()):
    width = None
    if b_slots:
        assert form in ("nn", "nt") and b.shape[0] == N_DEV
        width = b.shape[2]
        if form == "nn":
            (M, K), N, tn = a.shape, N_DEV * width, 2 * width
        else:
            (M, K), N, tk = a.shape, b.shape[1], 2 * width
    elif form == "nn":
        (M, K), N = a.shape, b.shape[1]
    elif form == "nt":
        (M, K), N = a.shape, b.shape[0]
    else:
        (K, M), N = a.shape, b.shape[1]
    if out_slots:
        width = N // N_DEV
        tn = 2 * width
    tm, tn, tk = min(tm, M), min(tn, N), min(tk, K)
    assert M % tm == 0 and N % tn == 0 and K % tk == 0, (name, M, N, K)
    nk = K // tk
    if form == "tn":
        a_spec = pl.BlockSpec((tk, tm), lambda i, j, k: (k, i))
        dn = (((0,), (0,)), ((), ()))
    else:
        a_spec = pl.BlockSpec((tm, tk), lambda i, j, k: (i, k))
        dn = (((1,), (1 if form == "nt" else 0,)), ((), ()))
    if b_slots and form == "nn":
        b_spec = pl.BlockSpec((2, tk, width), lambda i, j, k: (j, k, 0))
    elif b_slots:
        b_spec = pl.BlockSpec((2, tn, width), lambda i, j, k: (k, j, 0))
    elif form == "nt":
        b_spec = pl.BlockSpec((tn, tk), lambda i, j, k: (j, k))
    else:
        b_spec = pl.BlockSpec((tk, tn), lambda i, j, k: (k, j))

    pairs = [(a, b)] + list(more)
    seg = [0]
    for a_s, _ in pairs:
        k_s = K if len(pairs) == 1 else a_s.shape[1]
        assert k_s % tk == 0 and (len(pairs) == 1 or (form == "nt" and not b_slots))
        seg.append(seg[-1] + k_s // tk)
    nk = seg[-1]
    specs_more = []
    for s in range(1, len(pairs)):
        lo_k, n_k = seg[s], seg[s + 1] - seg[s]
        kk = lambda k, lo_k=lo_k, n_k=n_k: jnp.clip(k - lo_k, 0, n_k - 1)
        specs_more += [pl.BlockSpec((tm, tk), lambda i, j, k, kk=kk: (i, kk(k))),
                       pl.BlockSpec((tn, tk), lambda i, j, k, kk=kk: (j, kk(k)))]
    if len(pairs) > 1:
        n0 = seg[1]
        a_spec = pl.BlockSpec((tm, tk), lambda i, j, k: (i, jnp.minimum(k, n0 - 1)))
        b_spec = pl.BlockSpec((tn, tk), lambda i, j, k: (j, jnp.minimum(k, n0 - 1)))

    nx = len(scatter)
    ne = 0 if relu_of is None else 1
    no = 2 if relu2 else 1
    nm = 2 * (len(pairs) - 1)
    grid = (M // tm, N // tn, nk)

    def body(a_ref, b_ref, *rest):
        ab_refs = [(a_ref, b_ref)] + [(rest[2 * s], rest[2 * s + 1]) for s in range(len(pairs) - 1)]
        rest = rest[nm:]
        e_ref = rest[0] if ne else None
        x_in = rest[ne:ne + nx]
        o_refs = rest[ne + nx:ne + nx + no]
        x_out = rest[ne + nx + no:ne + 2 * nx + no]
        scr = rest[ne + 2 * nx + no:]
        k = pl.program_id(2)
        if nx:
            first, last = _first_last(grid)

            @pl.when(first)
            def _():
                _xchg(x_in, x_out, scr[-3:], False, wait=False)

        def finish(val):
            if out_slots:
                o_refs[0][0] = val[:, 0:width].astype(out_dtype)
                o_refs[0][1] = val[:, width:2 * width].astype(out_dtype)
            elif relu2:
                o_refs[0][...] = val.astype(out_dtype)
                r = jnp.maximum(val, 0.0)
                o_refs[1][...] = (r * r).astype(out_dtype)
            elif ne:
                o_refs[0][...] = (val * (2.0 * jnp.maximum(e_ref[...].astype(F32), 0.0))).astype(out_dtype)
            else:
                o_refs[0][...] = val.astype(out_dtype)

        def accumulate(ar, br):
            dot = lambda u, w: lax.dot_general(u.astype(BF), w.astype(BF), dn, preferred_element_type=F32)
            if b_slots and form == "nn":
                part = jnp.concatenate([dot(ar[...], br[0]), dot(ar[...], br[1])], axis=1)
            elif b_slots:
                part = dot(ar[:, 0:width], br[0]) + dot(ar[:, width:2 * width], br[1])
            else:
                part = dot(ar[...], br[...])
            if nk == 1:
                finish(part)
            else:
                acc = scr[0]

                @pl.when(k == 0)
                def _():
                    acc[...] = part

                @pl.when(k > 0)
                def _():
                    acc[...] += part

        if len(pairs) == 1:
            accumulate(a_ref, b_ref)
        else:
            for s, (ar, br) in enumerate(ab_refs):
                @pl.when((k >= seg[s]) & (k < seg[s + 1]))
                def _(ar=ar, br=br):
                    accumulate(ar, br)

        if nk > 1:
            @pl.when(k == nk - 1)
            def _():
                finish(scr[0][...])

        if nx:
            @pl.when(last)
            def _():
                _xchg(x_in, x_out, scr[-3:], False, wait=True)

    x_specs, x_shapes, x_scratch = _xchg_parts(scatter) if nx else ([], [], [])
    sem = ("arbitrary",) * 3 if nx else ("parallel", "parallel", "arbitrary")
    o_spec = pl.BlockSpec((tm, tn), lambda i, j, k: (i, j))
    o_shape = _sds((M, N), out_dtype)
    if out_slots:
        o_spec = pl.BlockSpec((2, tm, width), lambda i, j, k: (j, i, 0))
        o_shape = _sds((N_DEV, M, width), out_dtype)
    res = pl.pallas_call(
        body, name=name, grid=grid,
        in_specs=[a_spec, b_spec] + specs_more + [o_spec] * ne + x_specs,
        out_specs=[o_spec] * no + x_specs,
        out_shape=[o_shape] * no + x_shapes,
        scratch_shapes=([] if nk == 1 else [pltpu.VMEM((tm, tn), F32)]) + x_scratch,
        compiler_params=_cp(sem),
    )(a, b, *[t for p in more for t in p], *([relu_of] if ne else []), *scatter)
    return res if (nx or relu2) else res[0]


def _rows_call(body, name, n_tiles, ins, outs, scratch=(), aliases=None):
    res = pl.pallas_call(
        body, name=name, grid=(n_tiles,),
        in_specs=[s for _, s in ins],
        out_specs=[s for _, s in outs],
        out_shape=[o for o, _ in outs],
        scratch_shapes=list(scratch),
        input_output_aliases=aliases or {},
        compiler_params=_cp(("arbitrary",)),
    )(*[a for a, _ in ins])
    return res


def _rspec(tm, width, cb=0, rev_n=None):
    if rev_n is None:
        return pl.BlockSpec((tm, width), lambda i: (i, cb))
    return pl.BlockSpec((tm, width), lambda i: (rev_n - 1 - i, cb))


def _row_out(T, tm, width, dtype, rev_n=None):
    return (_sds((T, width), dtype), _rspec(tm, width, 0, rev_n))


def _acc_out(shape, dtype=F32):
    return (_sds(shape, dtype), _fullshape(shape))


def _mod_parts(mod):
    return [mod[:, i * D:(i + 1) * D] for i in range(N_ADA)]


def _pre_in(x, mod, n1g):
    T = x.shape[0]
    tm = TM_ROWS

    def body(x_ref, mod_ref, g_ref, h_ref):
        sh1, sc1 = mod_ref[:, 0:D], mod_ref[:, D:2 * D]
        xv = x_ref[...]
        r = lax.rsqrt(jnp.mean(xv * xv, axis=-1, keepdims=True) + EPS)
        h_ref[...] = ((xv * r) * g_ref[...] * (1.0 + sc1) + sh1).astype(BF)

    return _rows_call(body, "pre_in", T // tm,
                      [(x, _rspec(tm, D)), (mod, _full(mod)), (n1g, _full(n1g))],
                      [_row_out(T, tm, D, BF)])[0]


def _seg_mat():
    r = jnp.arange(LANES)[:, None] // HEAD_DIM
    c = jnp.arange(LANES)[None, :] // HEAD_DIM
    return jnp.where(r == c, 1.0 / HEAD_DIM, 0.0).astype(BF)


def _tri_mat(n, upper):
    r = jnp.arange(n)[:, None]
    c = jnp.arange(n)[None, :]
    return jnp.where((r <= c) if upper else (r >= c), 1.0, 0.0).astype(BF)


def _log_sigmoid(z):
    return jnp.minimum(z, 0.0) - jnp.log(1.0 + jnp.exp(-jnp.abs(z)))


def _qkv_post(proj, f, qg2, kg2, bf_pad):
    T = proj.shape[0]
    tm = TM_ROWS
    seg = _seg_mat()
    tri = _tri_mat(tm, True)

    def body(q_ref, k_ref, v_ref, f_ref, qg_ref, kg_ref, bf_ref, seg_ref, tri_ref,
             qo_ref, ko_ref, vo_ref, fc_ref, carry_ref):
        i = pl.program_id(0)

        @pl.when(i == 0)
        def _():
            carry_ref[...] = jnp.zeros_like(carry_ref)

        segm = seg_ref[...]
        for j in range(D // LANES):
            sl = slice(j * LANES, (j + 1) * LANES)
            qc = q_ref[:, sl].astype(F32)
            rq = lax.rsqrt(_dot_exact(qc * qc, segm) + EPS)
            qo_ref[:, sl] = ((qc * rq) * qg_ref[...] * (QK_SCALE * LOG2E)).astype(BF)
            kc = k_ref[:, sl].astype(F32)
            rk = lax.rsqrt(_dot_exact(kc * kc, segm) + EPS)
            ko_ref[:, sl] = ((kc * rk) * kg_ref[...]).astype(BF)
        vo_ref[...] = v_ref[...].astype(BF)
        lf = _log_sigmoid(f_ref[...] + bf_ref[...])
        lft = lf.T[0:N_HEADS, :]
        carry = carry_ref[:, 0:1]
        fc_ref[...] = _dot_exact(lft, tri_ref[...]) + carry
        carry_ref[...] = jnp.broadcast_to(carry + jnp.sum(lft, axis=1, keepdims=True), carry_ref.shape)

    outs = [_row_out(T, tm, D, BF), _row_out(T, tm, D, BF), _row_out(T, tm, D, BF),
            (_sds((N_HEADS, T), F32), pl.BlockSpec((N_HEADS, tm), lambda i: (0, i)))]
    ins = [(proj, _rspec(tm, D, 0)), (proj, _rspec(tm, D, 1)), (proj, _rspec(tm, D, 2)), (f, _rspec(tm, LANES)),
           (qg2, _full(qg2)), (kg2, _full(kg2)), (bf_pad, _full(bf_pad)), (seg, _full(seg)), (tri, _full(tri))]
    return _rows_call(body, "qkv_post", T // tm, ins, outs, [pltpu.VMEM((N_HEADS, LANES), F32)])


def _lane_lo():
    return lax.broadcasted_iota(jnp.int32, (1, LANES), 1) < HEAD_DIM


def _nt(a, b):
    return lax.dot_general(a, b, (((1,), (1,)), ((), ())), preferred_element_type=F32)


def _tn(a, b):
    return lax.dot_general(a, b, (((0,), (0,)), ((), ())), preferred_element_type=F32)


def _head_rep(x, lo):
    rolled = pltpu.roll(x, HEAD_DIM, axis=1)
    return jnp.where(lo, x, rolled), jnp.where(lo, rolled, x)


def _diag_mask(t):
    return lax.broadcasted_iota(jnp.int32, (t, t), 1) <= lax.broadcasted_iota(jnp.int32, (t, t), 0)


def _first_last(grid):
    ids = [pl.program_id(a) for a in range(len(grid))]
    first = functools.reduce(jnp.logical_and, [i == 0 for i in ids])
    last = functools.reduce(jnp.logical_and, [i == g - 1 for i, g in zip(ids, grid)])
    return first, last


def _flash_fwd(q, k, v, fc3, shards):
    T = q.shape[0]
    tq = TQ
    nq = T // tq
    hp_n = N_HEADS // 2
    rep = tq // LANES
    nx = len(shards)
    grid = (hp_n, nq, nq)

    def body(q_ref, k_ref, v_ref, fk_ref, fq_ref, *rest):
        x_in, (o_ref, lse_ref), x_out = rest[:nx], rest[nx:nx + 2], rest[nx + 2:2 * nx + 2]
        acc_ref, m_ref = rest[2 * nx + 2:2 * nx + 4]
        sems = rest[2 * nx + 4:]
        qi, ki = pl.program_id(1), pl.program_id(2)
        first, last = _first_last(grid)

        @pl.when(first)
        def _():
            _xchg(x_in, x_out, sems, True, wait=False)

        @pl.when(ki == 0)
        def _():
            acc_ref[...] = jnp.zeros_like(acc_ref)
            m_ref[...] = jnp.full_like(m_ref, NEG)

        lane = lax.broadcasted_iota(jnp.int32, (1, LANES), 1)
        sum_lane = (HEAD_DIM, 0)

        def step(diag):
            lo = _lane_lo()
            q2, k2, v2 = q_ref[...], k_ref[...], v_ref[...]
            zero = jnp.zeros_like(k2)
            bias = (fq_ref[:, 0:1] - fk_ref[...]) * LOG2E
            for hh in range(2):
                sel = (lambda t: jnp.where(lo, t, zero)) if hh == 0 else (lambda t: jnp.where(lo, zero, t))
                ones = jnp.where(lane == sum_lane[hh], 1.0, 0.0).astype(BF)
                v_aug = jnp.where(lo, v2, ones) if hh == 0 else jnp.where(lo, ones, v2)
                s = _nt(sel(q2), k2) + bias[hh:hh + 1, :]
                if diag:
                    s = jnp.where(_diag_mask(tq), s, NEG)
                m_old = m_ref[hh]
                m_new = jnp.maximum(m_old, jnp.max(s, axis=-1, keepdims=True))
                alpha = jnp.exp2(m_old - m_new)
                p = jnp.exp2(s - jnp.tile(m_new, (1, rep)))
                m_ref[hh] = m_new
                acc_ref[hh] = acc_ref[hh] * alpha + jnp.dot(p.astype(BF), v_aug, preferred_element_type=F32)

        @pl.when(ki < qi)
        def _():
            step(False)

        @pl.when(ki == qi)
        def _():
            step(True)
            lo = _lane_lo()
            acc_a, acc_b = acc_ref[0], acc_ref[1]
            la = jnp.broadcast_to(acc_a[:, sum_lane[0]:sum_lane[0] + 1], (tq, LANES))
            lb = jnp.broadcast_to(acc_b[:, sum_lane[1]:sum_lane[1] + 1], (tq, LANES))
            o_ref[...] = jnp.where(lo, acc_a / la, acc_b / lb)
            lse_ref[...] = jnp.where(lo, m_ref[0] + jnp.log(la) * LOG2E, m_ref[1] + jnp.log(lb) * LOG2E)

        @pl.when(last)
        def _():
            _xchg(x_in, x_out, sems, True, wait=True)

    qspec = pl.BlockSpec((tq, LANES), lambda h, i, j: (i, h))
    kspec = pl.BlockSpec((tq, LANES), lambda h, i, j: (jnp.minimum(i, j), h))
    fkspec = pl.BlockSpec((None, 2, tq), lambda h, i, j: (h, 0, jnp.minimum(i, j)))
    fqspec = pl.BlockSpec((None, 2, tq), lambda h, i, j: (h, 0, i))
    x_specs, x_shapes, x_scratch = _xchg_parts(shards)
    return pl.pallas_call(
        body, name="attn_fwd", grid=grid,
        in_specs=[qspec, kspec, kspec, fkspec, fqspec] + x_specs,
        out_specs=[qspec, qspec] + x_specs,
        out_shape=[_sds((T, D), F32), _sds((T, D), F32)] + x_shapes,
        scratch_shapes=[pltpu.VMEM((2, tq, LANES), F32), pltpu.VMEM((2, tq, LANES), F32)] + x_scratch,
        compiler_params=_cp(("arbitrary", "arbitrary", "arbitrary")),
    )(q, k, v, fc3, fc3, *shards)


def _attn_delta(do, o):
    T = o.shape[0]
    tm = TM_ROWS
    ones = (_seg_mat().astype(F32) * HEAD_DIM).astype(BF)

    def body(do_ref, o_ref, seg_ref, dl_ref):
        segm = seg_ref[...]
        for j in range(D // LANES):
            sl = slice(j * LANES, (j + 1) * LANES)
            dl_ref[:, sl] = _dot_exact(do_ref[:, sl].astype(BF).astype(F32) * o_ref[:, sl], segm)

    ins = [(do, _rspec(tm, D)), (o, _rspec(tm, D)), (ones, _full(ones))]
    return _rows_call(body, "attn_delta", T // tm, ins, [_row_out(T, tm, D, F32)])[0]


def _flash_bwd(q, k, v, do, lse, delta, fc3, parts):
    T = q.shape[0]
    tq = TQ
    nq = T // tq
    hp_n = N_HEADS // 2
    rep = tq // LANES
    nx = len(parts)
    grid = (hp_n, nq, nq)

    def body(q_ref, k_ref, v_ref, do_ref, lse_ref, dl_ref, fk_ref, fq_ref, *rest):
        x_in, x_out = rest[:nx], rest[nx + 6:2 * nx + 6]
        dq_ref, ra_ref, rb_ref, dk_ref, dv_ref, dfc_ref = rest[nx:nx + 6]
        dk_acc, dv_acc, df_acc = rest[2 * nx + 6:2 * nx + 9]
        sems = rest[2 * nx + 9:]
        ki, qi = pl.program_id(1), pl.program_id(2)
        first, last = _first_last(grid)
        qrows = pl.ds(pl.multiple_of(qi * tq, tq), tq)

        @pl.when(first)
        def _():
            _xchg(x_in, x_out, sems, False, wait=False)

        @pl.when((ki == 0) & (qi == 0))
        def _():
            dq_ref[...] = jnp.zeros_like(dq_ref)
            ra_ref[...] = jnp.zeros_like(ra_ref)
            rb_ref[...] = jnp.zeros_like(rb_ref)

        @pl.when(qi == 0)
        def _():
            dk_acc[...] = jnp.zeros_like(dk_acc)
            dv_acc[...] = jnp.zeros_like(dv_acc)
            df_acc[...] = jnp.zeros_like(df_acc)

        def step(diag):
            lo = _lane_lo()
            q2, k2, v2 = q_ref[...], k_ref[...], v_ref[...]
            do2 = do_ref[...].astype(BF)
            zero = jnp.zeros_like(q2)
            bias = (fq_ref[:, 0:1] - fk_ref[...]) * LOG2E
            lses = _head_rep(lse_ref[...], lo)
            dls = _head_rep(dl_ref[...], lo)
            dk_t = None
            dv_t = None
            dq_t = None
            for hh in range(2):
                sel = (lambda t: jnp.where(lo, t, zero)) if hh == 0 else (lambda t: jnp.where(lo, zero, t))
                s = _nt(sel(q2), k2) + bias[hh:hh + 1, :]
                if diag:
                    s = jnp.where(_diag_mask(tq), s, NEG)
                p = jnp.exp2(s - jnp.tile(lses[hh], (1, rep)))
                dp = _nt(sel(do2), v2)
                ds = p * (dp - jnp.tile(dls[hh], (1, rep)))
                ds_b = ds.astype(BF)
                dvp = _tn(p.astype(BF), sel(do2))
                dkp = _tn(ds_b, sel(q2))
                dqp = jnp.dot(ds_b, sel(k2), preferred_element_type=F32)
                dv_t = dvp if dv_t is None else dv_t + dvp
                dk_t = dkp if dk_t is None else dk_t + dkp
                dq_t = dqp if dq_t is None else dq_t + dqp
                df_acc[hh:hh + 1, :] -= _colsum(ds)
                r_ref = ra_ref if hh == 0 else rb_ref
                r_ref[qrows, :] += jnp.sum(ds, axis=-1, keepdims=True)
            dk_acc[...] += dk_t
            dv_acc[...] += dv_t
            dq_ref[qrows, :] += dq_t * QK_SCALE

        @pl.when(qi > ki)
        def _():
            step(False)

        @pl.when(qi == ki)
        def _():
            step(True)

        @pl.when(qi == nq - 1)
        def _():
            dk_ref[...] = dk_acc[...] * LN2
            dv_ref[...] = dv_acc[...]
            dfc_ref[...] = df_acc[...]

        @pl.when(last)
        def _():
            _xchg(x_in, x_out, sems, False, wait=True)

    kspec = pl.BlockSpec((tq, LANES), lambda h, j, i: (j, h))
    qspec = pl.BlockSpec((tq, LANES), lambda h, j, i: (jnp.maximum(i, j), h))
    fkspec = pl.BlockSpec((None, 2, tq), lambda h, j, i: (h, 0, j))
    fqspec = pl.BlockSpec((None, 2, tq), lambda h, j, i: (h, 0, jnp.maximum(i, j)))
    x_specs, x_shapes, x_scratch = _xchg_parts(parts)
    dqspec = pl.BlockSpec((T, LANES), lambda h, j, i: (0, h))
    rspec = pl.BlockSpec((None, T, 1), lambda h, j, i: (h, 0, 0))
    return pl.pallas_call(
        body, name="attn_bwd", grid=grid,
        in_specs=[qspec, kspec, kspec, qspec, qspec, qspec, fkspec, fqspec] + x_specs,
        out_specs=[dqspec, rspec, rspec, kspec, kspec, fkspec] + x_specs,
        out_shape=[_sds((T, D), F32), _sds((hp_n, T, 1), F32), _sds((hp_n, T, 1), F32),
                   _sds((T, D), F32), _sds((T, D), F32), _sds((hp_n, 2, T), F32)] + x_shapes,
        scratch_shapes=[pltpu.VMEM((tq, LANES), F32), pltpu.VMEM((tq, LANES), F32), pltpu.VMEM((2, tq), F32)] + x_scratch,
        compiler_params=_cp(("arbitrary", "arbitrary", "arbitrary")),
    )(q, k, v, do, lse, delta, fc3, fc3, *parts)


def _layer_norm_stats(u1):
    mu = jnp.mean(u1, axis=-1, keepdims=True)
    xc = u1 - mu
    rstd = lax.rsqrt(jnp.mean(xc * xc, axis=-1, keepdims=True) + EPS)
    return xc * rstd, rstd


def _shifted_copies(buf, sh, tm):
    rows = tm + HALO - SUBLANES
    for b in range(1, SUBLANES):
        sh[b - 1, 0:rows, :] = buf[b:b + rows, :]


def _window(buf, sh, off, rows, sl):
    a8, b = off // SUBLANES * SUBLANES, off % SUBLANES
    return buf[a8:a8 + rows, sl] if b == 0 else sh[b - 1, a8:a8 + rows, sl]


def _conv_fwd(proj, cw, cb, lng, lnb):
    T = proj.shape[0]
    tm = CONV_TM

    def body(a_ref, b_ref, w_ref, cb_ref, g_ref, bb_ref, u0_ref, u1_ref, u3_ref, buf, sh):
        i = pl.program_id(0)

        @pl.when(i == 0)
        def _():
            buf[0:HALO, :] = jnp.zeros((HALO, D), F32)

        u0 = a_ref[...].astype(F32) * _sigmoid(b_ref[...].astype(F32))
        u0_ref[...] = u0
        buf[HALO:HALO + tm, :] = u0
        _shifted_copies(buf, sh, tm)
        for j in range(D // LANES):
            sl = slice(j * LANES, (j + 1) * LANES)
            for r0 in range(0, tm, CONV_ROWS):
                acc = jnp.broadcast_to(cb_ref[:, sl], (CONV_ROWS, LANES))
                for kk in range(CONV_K):
                    acc = acc + w_ref[kk:kk + 1, sl] * _window(buf, sh, r0 + HALO - (CONV_K - 1) + kk, CONV_ROWS, sl)
                u1_ref[r0:r0 + CONV_ROWS, sl] = acc
        buf[0:HALO, :] = buf[tm:tm + HALO, :]
        xh, _ = _layer_norm_stats(u1_ref[...])
        u2 = xh * g_ref[...] + bb_ref[...]
        u3_ref[...] = (u2 * _sigmoid(u2)).astype(BF)

    ins = [(proj, _rspec(tm, D, 0)), (proj, _rspec(tm, D, 1)), (cw, _full(cw)), (cb, _full(cb)),
           (lng, _full(lng)), (lnb, _full(lnb))]
    outs = [_row_out(T, tm, D, F32), _row_out(T, tm, D, F32), _row_out(T, tm, D, BF)]
    return _rows_call(body, "conv_fwd", T // tm, ins, outs,
                      [pltpu.VMEM((tm + HALO, D), F32), pltpu.VMEM((SUBLANES - 1, tm + HALO, D), F32)])


def _conv_bwd(du3, u1, u0, proj, cw, lng, lnb, dgg):
    T = du3.shape[0]
    tm = CONV_TM
    n = T // tm
    per = tm // HALO

    def body(du3_ref, u1_ref, u0_ref, halo_ref, a_ref, b_ref, w_ref, g_ref, bb_ref, dgg_in_ref,
             dgl_ref, dg_ref, dbb_ref, dcb_ref, dw_ref, dbuf, ubuf, du0_buf, dsh, ush, dw8):
        i = pl.program_id(0)
        r = n - 1 - i

        @pl.when(i == 0)
        def _():
            dbuf[tm:tm + HALO, :] = jnp.zeros((HALO, D), F32)
            dg_ref[...] = jnp.zeros_like(dg_ref)
            dbb_ref[...] = jnp.zeros_like(dbb_ref)
            dcb_ref[...] = jnp.zeros_like(dcb_ref)
            dw8[...] = jnp.zeros_like(dw8)

        xh, rstd = _layer_norm_stats(u1_ref[...])
        g = g_ref[...]
        u2 = xh * g + bb_ref[...]
        s2 = _sigmoid(u2)
        du2 = du3_ref[...] * (s2 * (1.0 + u2 * (1.0 - s2)))
        dg_ref[...] += _colsum(du2 * xh)
        dbb_ref[...] += _colsum(du2)
        dxh = du2 * g
        du1 = rstd * (dxh - jnp.mean(dxh, axis=-1, keepdims=True) - xh * jnp.mean(dxh * xh, axis=-1, keepdims=True))
        dcb_ref[...] += _colsum(du1)
        dbuf[0:tm, :] = du1
        ubuf[HALO:HALO + tm, :] = u0_ref[...]
        ubuf[0:HALO, :] = jnp.where(r > 0, halo_ref[...], 0.0)
        _shifted_copies(dbuf, dsh, tm)
        _shifted_copies(ubuf, ush, tm)
        for j in range(D // LANES):
            sl = slice(j * LANES, (j + 1) * LANES)
            for r0 in range(0, tm, CONV_ROWS):
                d1 = dbuf[r0:r0 + CONV_ROWS, sl]
                acc = jnp.zeros((CONV_ROWS, LANES), F32)
                for kk in range(CONV_K):
                    acc = acc + w_ref[kk:kk + 1, sl] * _window(dbuf, dsh, r0 + CONV_K - 1 - kk, CONV_ROWS, sl)
                    prod = d1 * _window(ubuf, ush, r0 + HALO - (CONV_K - 1) + kk, CONV_ROWS, sl)
                    dw8[kk * SUBLANES:(kk + 1) * SUBLANES, sl] += prod.reshape(
                        CONV_ROWS // SUBLANES, SUBLANES, LANES).sum(axis=0)
                du0_buf[r0:r0 + CONV_ROWS, sl] = acc
        dbuf[tm:tm + HALO, :] = dbuf[0:HALO, :]
        du0 = du0_buf[...]
        af, bfl = a_ref[...].astype(F32), b_ref[...].astype(F32)
        sb = _sigmoid(bfl)
        dgl_ref[:, 0:D] = (du0 * sb).astype(BF)
        dgl_ref[:, D:2 * D] = (du0 * af * sb * (1.0 - sb)).astype(BF)

        @pl.when(i == n - 1)
        def _():
            for kk in range(CONV_KP):
                dw_ref[kk:kk + 1, :] = _colsum(dw8[kk * SUBLANES:(kk + 1) * SUBLANES, :])

    rs = lambda cb: _rspec(tm, D, cb, n)
    halo_spec = pl.BlockSpec((HALO, D), lambda i: (jnp.maximum((n - 1 - i) * per - 1, 0), 0))
    ins = [(du3, rs(0)), (u1, rs(0)), (u0, rs(0)), (u0, halo_spec), (proj, rs(0)), (proj, rs(1)),
           (cw, _full(cw)), (lng, _full(lng)), (lnb, _full(lnb)), (dgg, pl.BlockSpec(memory_space=pl.ANY))]
    outs = [(_sds(dgg.shape, dgg.dtype), _rspec(tm, 2 * D, 0, n)),
            _acc_out((1, D)), _acc_out((1, D)), _acc_out((1, D)), _acc_out((CONV_KP, D))]
    shifted = pltpu.VMEM((SUBLANES - 1, tm + HALO, D), F32)
    return _rows_call(body, "conv_bwd", n, ins, outs,
                      [pltpu.VMEM((tm + HALO, D), F32), pltpu.VMEM((tm + HALO, D), F32), pltpu.VMEM((tm, D), F32),
                       shifted, shifted, pltpu.VMEM((CONV_KP * SUBLANES, D), F32)], aliases={len(ins) - 1: 0})


def _merge(ba, bb, proj):
    T = ba.shape[0]
    tm = TM_ROWS

    def body(ba_ref, bb_ref, ga_ref, gb_ref, o_ref):
        sa, sb = _sigmoid(ga_ref[...].astype(F32)), _sigmoid(gb_ref[...].astype(F32))
        o_ref[...] = (sa * ba_ref[...].astype(F32) + sb * bb_ref[...].astype(F32)).astype(BF)

    ins = [(ba, _rspec(tm, D)), (bb, _rspec(tm, D)), (proj, _rspec(tm, D, 2)), (proj, _rspec(tm, D, 3))]
    return _rows_call(body, "merge", T // tm, ins, [_row_out(T, tm, D, BF)])[0]


def _post_out(x, mo, mod, n2g):
    T = x.shape[0]
    tm = TM_ROWS

    def body(x_ref, mo_ref, mod_ref, g_ref, x1_ref, h2_ref):
        g1 = mod_ref[:, 2 * D:3 * D]
        sh2, sc2 = mod_ref[:, 3 * D:4 * D], mod_ref[:, 4 * D:5 * D]
        x1 = x_ref[...] + g1 * mo_ref[...]
        x1_ref[...] = x1
        r = lax.rsqrt(jnp.mean(x1 * x1, axis=-1, keepdims=True) + EPS)
        h2_ref[...] = ((x1 * r) * g_ref[...] * (1.0 + sc2) + sh2).astype(BF)

    ins = [(x, _rspec(tm, D)), (mo, _rspec(tm, D)), (mod, _full(mod)), (n2g, _full(n2g))]
    return _rows_call(body, "post_out", T // tm, ins, [_row_out(T, tm, D, F32), _row_out(T, tm, D, BF)])


def _loss_head(x1, m2, tgt, mod):
    T = x1.shape[0]
    tm = TM_ROWS

    def body(x1_ref, m2_ref, t_ref, mod_ref, dy_ref, dm2_ref, dg2_ref, sq_ref):
        i = pl.program_id(0)

        @pl.when(i == 0)
        def _():
            dg2_ref[...] = jnp.zeros_like(dg2_ref)
            sq_ref[...] = jnp.zeros_like(sq_ref)

        g2 = mod_ref[:, 5 * D:6 * D]
        m2 = m2_ref[...]
        err = x1_ref[...] + g2 * m2 - t_ref[...]
        dy = err * (1.0 / D)
        dy_ref[...] = dy
        dm2_ref[...] = (g2 * dy).astype(BF)
        dg2_ref[...] += _colsum(dy * m2)
        sq_ref[...] += _colsum(err * err)

    ins = [(x1, _rspec(tm, D)), (m2, _rspec(tm, D)), (tgt, _rspec(tm, D)), (mod, _full(mod))]
    outs = [_row_out(T, tm, D, F32), _row_out(T, tm, D, BF), _acc_out((1, D)), _acc_out((1, D))]
    return _rows_call(body, "loss_head", T // tm, ins, outs)


def _norm2_bwd(dh2, x1, dy, mo, mod, n2g):
    T = x1.shape[0]
    tm = TM_ROWS

    def body(dh_ref, x1_ref, dy_ref, mo_ref, mod_ref, g_ref, dx1_ref, dmo_ref, dsh_ref, dsc_ref, dg_ref, dg1_ref):
        i = pl.program_id(0)

        @pl.when(i == 0)
        def _():
            for r in (dsh_ref, dsc_ref, dg_ref, dg1_ref):
                r[...] = jnp.zeros_like(r)

        g1, sc2 = mod_ref[:, 2 * D:3 * D], mod_ref[:, 4 * D:5 * D]
        g = g_ref[...]
        x1 = x1_ref[...]
        dh = dh_ref[...]
        r = lax.rsqrt(jnp.mean(x1 * x1, axis=-1, keepdims=True) + EPS)
        xn = x1 * r
        dsh_ref[...] += _colsum(dh)
        dsc_ref[...] += _colsum(dh * xn * g)
        dg_ref[...] += _colsum(dh * xn * (1.0 + sc2))
        dxn = dh * g * (1.0 + sc2)
        dx1 = dy_ref[...] + r * (dxn - xn * jnp.mean(dxn * xn, axis=-1, keepdims=True))
        dx1_ref[...] = dx1
        dg1_ref[...] += _colsum(dx1 * mo_ref[...])
        dmo_ref[...] = (g1 * dx1).astype(BF)

    ins = [(dh2, _rspec(tm, D)), (x1, _rspec(tm, D)), (dy, _rspec(tm, D)), (mo, _rspec(tm, D)),
           (mod, _full(mod)), (n2g, _full(n2g))]
    outs = [_row_out(T, tm, D, F32), _row_out(T, tm, D, BF)] + [_acc_out((1, D)) for _ in range(4)]
    return _rows_call(body, "norm2_bwd", T // tm, ins, outs)


def _gate_bwd(dmerged, ba, bb, proj):
    T = ba.shape[0]
    tm = TM_ROWS

    def body(dm_ref, ba_ref, bb_ref, ga_ref, gb_ref, dba_ref, dbb_ref, dgt_ref):
        dm = dm_ref[...].astype(F32)
        sa, sb = _sigmoid(ga_ref[...].astype(F32)), _sigmoid(gb_ref[...].astype(F32))
        dba_ref[...] = (dm * sa).astype(BF)
        dbb_ref[...] = (dm * sb).astype(BF)
        dgt_ref[:, 0:D] = (dm * ba_ref[...].astype(F32) * sa * (1.0 - sa)).astype(BF)
        dgt_ref[:, D:2 * D] = (dm * bb_ref[...].astype(F32) * sb * (1.0 - sb)).astype(BF)

    ins = [(dmerged, _rspec(tm, D)), (ba, _rspec(tm, D)), (bb, _rspec(tm, D)),
           (proj, _rspec(tm, D, 2)), (proj, _rspec(tm, D, 3))]
    outs = [_row_out(T, tm, D, BF), _row_out(T, tm, D, BF), (_sds((T, 4 * D), BF), _rspec(tm, 2 * D, 1))]
    return _rows_call(body, "gate_bwd", T // tm, ins, outs)


def _qkv_bwd(dq, dk, dv, proj, f, dfc, dfq, qg2, kg2, bf_pad):
    T = proj.shape[0]
    tm = TM_ROWS
    n = T // tm
    seg = _seg_mat()
    tri = _tri_mat(tm, False)

    def body(dq_ref, dk_ref, dv_ref, q_ref, k_ref, f_ref, dfc_ref, dfq_ref, qg_ref, kg_ref, bf_ref, seg_ref, tri_ref,
             dqkv_ref, dfo_ref, dqg_ref, dkg_ref, dbf_ref, carry_ref):
        i = pl.program_id(0)

        @pl.when(i == 0)
        def _():
            carry_ref[...] = jnp.zeros_like(carry_ref)
            dqg_ref[...] = jnp.zeros_like(dqg_ref)
            dkg_ref[...] = jnp.zeros_like(dkg_ref)
            dbf_ref[...] = jnp.zeros_like(dbf_ref)

        segm = seg_ref[...]
        dqg = jnp.zeros((1, LANES), F32)
        dkg = jnp.zeros((1, LANES), F32)
        for j in range(D // LANES):
            sl = slice(j * LANES, (j + 1) * LANES)
            for (raw_ref, d_ref, gn_ref, which) in ((q_ref, dq_ref, qg_ref, 0), (k_ref, dk_ref, kg_ref, 1)):
                xc = raw_ref[:, sl].astype(F32)
                rr = lax.rsqrt(_dot_exact(xc * xc, segm) + EPS)
                xn = xc * rr
                dc = d_ref[:, sl]
                if which == 0:
                    dqg = dqg + _colsum(dc * xn)
                else:
                    dkg = dkg + _colsum(dc * xn)
                dxn = dc * gn_ref[...]
                osl = slice(which * D + j * LANES, which * D + (j + 1) * LANES)
                dqkv_ref[:, osl] = (rr * (dxn - xn * _dot_exact(dxn * xn, segm))).astype(BF)
        dqg_ref[...] += dqg
        dkg_ref[...] += dkg
        dqkv_ref[:, 2 * D:3 * D] = dv_ref[...].astype(BF)
        z = f_ref[...] + bf_ref[...]
        sneg_t = _sigmoid(-z).T[0:N_HEADS, :]
        dfc = dfc_ref[...] + dfq_ref[...]
        carry = carry_ref[:, 0:1]
        dlf = _dot_exact(dfc, tri_ref[...]) + carry
        carry_ref[...] = jnp.broadcast_to(carry + jnp.sum(dfc, axis=1, keepdims=True), carry_ref.shape)
        dzt = dlf * sneg_t
        dz = jnp.concatenate([dzt, jnp.zeros((LANES - N_HEADS, tm), F32)], axis=0).T
        dbf_ref[...] += _colsum(dz)
        dfo_ref[...] = dz.astype(BF)

    rs = lambda w, cb=0: _rspec(tm, w, cb, n)
    ins = [(dq, rs(D)), (dk, rs(D)), (dv, rs(D)), (proj, rs(D, 0)), (proj, rs(D, 1)), (f, rs(LANES)),
           (dfc, pl.BlockSpec((N_HEADS, tm), lambda i: (0, n - 1 - i))),
           (dfq, pl.BlockSpec((N_HEADS, tm), lambda i: (0, n - 1 - i))),
           (qg2, _full(qg2)), (kg2, _full(kg2)), (bf_pad, _full(bf_pad)), (seg, _full(seg)), (tri, _full(tri))]
    outs = [_row_out(T, tm, 3 * D, BF, n), _row_out(T, tm, LANES, BF, n),
            _acc_out((1, LANES)), _acc_out((1, LANES)), _acc_out((1, LANES))]
    return _rows_call(body, "qkv_bwd", n, ins, outs, [pltpu.VMEM((N_HEADS, LANES), F32)])


def _norm1_bwd(dh, dhf, x, dx1, mod, n1g):
    T = x.shape[0]
    tm = TM_ROWS

    def body(dh_ref, dhf_ref, x_ref, dx1_ref, mod_ref, g_ref, dx_ref, dsh_ref, dsc_ref, dg_ref):
        i = pl.program_id(0)

        @pl.when(i == 0)
        def _():
            for r in (dsh_ref, dsc_ref, dg_ref):
                r[...] = jnp.zeros_like(r)

        sc1 = mod_ref[:, D:2 * D]
        g = g_ref[...]
        xv = x_ref[...]
        dh = dh_ref[...] + dhf_ref[...]
        r = lax.rsqrt(jnp.mean(xv * xv, axis=-1, keepdims=True) + EPS)
        xn = xv * r
        dsh_ref[...] += _colsum(dh)
        dsc_ref[...] += _colsum(dh * xn * g)
        dg_ref[...] += _colsum(dh * xn * (1.0 + sc1))
        dxn = dh * g * (1.0 + sc1)
        dx_ref[...] = dx1_ref[...] + r * (dxn - xn * jnp.mean(dxn * xn, axis=-1, keepdims=True))

    ins = [(dh, _rspec(tm, D)), (dhf, _rspec(tm, D)), (x, _rspec(tm, D)), (dx1, _rspec(tm, D)),
           (mod, _full(mod)), (n1g, _full(n1g))]
    outs = [_row_out(T, tm, D, F32)] + [_acc_out((1, D)) for _ in range(3)]
    return _rows_call(body, "norm1_bwd", T // tm, ins, outs)


def _adamw_math(w, g, m, v):
    m = ADAM_B1 * m + (1.0 - ADAM_B1) * g
    v = ADAM_B2 * v + (1.0 - ADAM_B2) * (g * g)
    m_hat = m / (1.0 - ADAM_B1 ** ADAM_STEP)
    v_hat = v / (1.0 - ADAM_B2 ** ADAM_STEP)
    delta = -ADAM_LR * (m_hat / (jnp.sqrt(v_hat) + ADAM_EPS) + ADAM_WD * w)
    return delta, m, v


def _adamw(parts, w, m, v, name):
    n, R, C = parts.shape
    tr = R if R <= 256 else 256
    assert R % tr == 0

    def body(p_ref, w_ref, m_ref, v_ref, g_ref, d_ref, mo_ref, vo_ref):
        g = p_ref[0].astype(F32)
        for s in range(1, n):
            g = g + p_ref[s].astype(F32)
        g_ref[...] = g
        d_ref[...], mo_ref[...], vo_ref[...] = _adamw_math(w_ref[...], g, m_ref[...], v_ref[...])

    spec = pl.BlockSpec((None, tr, C), lambda i: (0, i, 0))
    return pl.pallas_call(
        body, name=name, grid=(R // tr,),
        in_specs=[pl.BlockSpec((n, tr, C), lambda i: (0, i, 0)), spec, spec, spec],
        out_specs=[spec] * 4, out_shape=[_sds((1, R, C), F32)] * 4,
        compiler_params=_cp(("parallel",)),
    )(parts, w, m, v)


def _rcopy(src, dst, ssem, rsem, peer):
    return pltpu.make_async_remote_copy(src_ref=src, dst_ref=dst, send_sem=ssem, recv_sem=rsem,
                                        device_id=peer, device_id_type=MESH)


def _ada_fwd(c, w_ada, b_slice, cw_shard):
    def body(c_ref, w_ref, b_ref, cw_ref, mod_ref, ca_ref, cwf_ref, call, mp, ssem, rsem):
        x, y, cc, me = _my_pos()
        call[pl.ds(me, 1), :] = c_ref[...]
        cwf_ref[me] = cw_ref[...]
        first = []
        for d in range(1, N_DEV):
            peer, _ = _peer(x, y, cc, d)
            first.append(_rcopy(c_ref, call.at[pl.ds(me, 1), :], ssem.at[0, d - 1], rsem.at[0, d - 1], peer))
            first.append(_rcopy(cw_ref, cwf_ref.at[me], ssem.at[1, d - 1], rsem.at[1, d - 1], peer))
        for cp in first:
            cp.start()
        for d in range(1, N_DEV):
            peer, pid = _peer(x, y, cc, d)
            _rcopy(c_ref, call.at[pl.ds(pid, 1), :], ssem.at[0, d - 1], rsem.at[0, d - 1], peer).wait_recv()
            _rcopy(cw_ref, cwf_ref.at[pid], ssem.at[1, d - 1], rsem.at[1, d - 1], peer).wait_recv()
        cv = call[...]
        ca = cv * _sigmoid(cv)
        ca_ref[...] = ca
        mp[...] = _dot_f32(ca, w_ref[...]) + b_ref[...]
        mod_ref[pl.ds(me, 1), :] = mp[pl.ds(me, 1), :]
        second = []
        for d in range(1, N_DEV):
            peer, pid = _peer(x, y, cc, d)
            second.append(_rcopy(mp.at[pl.ds(pid, 1), :], mod_ref.at[pl.ds(me, 1), :], ssem.at[2, d - 1], rsem.at[2, d - 1], peer))
        for cp in second:
            cp.start()
        for d in range(1, N_DEV):
            peer, pid = _peer(x, y, cc, d)
            _rcopy(mp.at[pl.ds(pid, 1), :], mod_ref.at[pl.ds(pid, 1), :], ssem.at[2, d - 1], rsem.at[2, d - 1], peer).wait_recv()
        for cp in first + second:
            cp.wait_send()

    vm = pl.BlockSpec(memory_space=pltpu.VMEM)
    return pl.pallas_call(
        body, name="ada_fwd",
        in_specs=[vm, vm, vm, vm], out_specs=[vm, vm, vm],
        out_shape=[_sds((N_DEV, ADA_SHARD), F32), _sds((N_DEV, D), F32), _sds((N_DEV, CONV_KP, LANES), F32)],
        scratch_shapes=[pltpu.VMEM((N_DEV, D), F32), pltpu.VMEM((N_DEV, ADA_SHARD), F32),
                        pltpu.SemaphoreType.DMA((3, N_DEV - 1)), pltpu.SemaphoreType.DMA((3, N_DEV - 1))],
        compiler_params=pltpu.CompilerParams(vmem_limit_bytes=VMEM_LIMIT),
    )(c, w_ada, b_slice, cw_shard)


def _xchg_parts(arrays):
    n = len(arrays)
    anyspec = pl.BlockSpec(memory_space=pl.ANY)
    slots = lambda a: a.shape[0] if (a.ndim == 3 and a.shape[0] == N_DEV // 2) else N_DEV
    shapes = [_sds((slots(a),) + tuple(a.shape[-2:]), a.dtype) for a in arrays]
    scratch = [pltpu.SemaphoreType.DMA((n,)), pltpu.SemaphoreType.DMA((n, N_DEV - 1)),
               pltpu.SemaphoreType.DMA((n, N_DEV - 1))]
    return [anyspec] * n, shapes, scratch


def _xchg(ins, outs, sems, gather, wait):
    lsem, ssem, rsem = sems
    x, y, cc, me = _my_pos()
    for a in range(len(ins)):
        if not gather and ins[a].shape[0] == N_DEV // 2:
            chip = 2 * x + y
            local = pltpu.make_async_copy(ins[a].at[chip], outs[a].at[chip], lsem.at[a])
            if not wait:
                local.start()
            for d in range(1, N_DEV // 2):
                px, py = ((1 - x) if d & 2 else x), ((1 - y) if d & 1 else y)
                pchip = 2 * px + py
                if not wait:
                    _rcopy(ins[a].at[pchip], outs[a].at[chip], ssem.at[a, d - 1], rsem.at[a, d - 1], (px, py, cc)).start()
                else:
                    cp = _rcopy(ins[a].at[pchip], outs[a].at[pchip], ssem.at[a, d - 1], rsem.at[a, d - 1], (px, py, cc))
                    cp.wait_recv()
                    cp.wait_send()
            if wait:
                local.wait()
            continue
        local = pltpu.make_async_copy(ins[a] if gather else ins[a].at[me], outs[a].at[me], lsem.at[a])
        if not wait:
            local.start()
        for d in range(1, N_DEV):
            peer, pid = _peer(x, y, cc, d)
            src = ins[a] if gather else ins[a].at[pid]
            if not wait:
                _rcopy(src, outs[a].at[me], ssem.at[a, d - 1], rsem.at[a, d - 1], peer).start()
            else:
                cp = _rcopy(src, outs[a].at[pid], ssem.at[a, d - 1], rsem.at[a, d - 1], peer)
                cp.wait_recv()
                cp.wait_send()
        if wait:
            local.wait()


def _pair_reduce(part):
    n, R, C = part.shape
    half = n // 2
    tr = R if R <= 256 else 256
    assert n == N_DEV and R % tr == 0

    def swap(p_ref, got_ref, ssem, rsem):
        x, y, c, _ = _my_pos()
        cps = [_rcopy(p_ref.at[2 * i + (1 - c)], got_ref.at[i], ssem.at[i], rsem.at[i], (x, y, 1 - c)) for i in range(half)]
        for cp in cps:
            cp.start()
        for cp in cps:
            cp.wait_recv()
            cp.wait_send()

    anyspec = pl.BlockSpec(memory_space=pl.ANY)
    got = pl.pallas_call(
        swap, name="dwin_pair_swap", in_specs=[anyspec], out_specs=anyspec, out_shape=_sds((half, R, C), part.dtype),
        scratch_shapes=[pltpu.SemaphoreType.DMA((half,)), pltpu.SemaphoreType.DMA((half,))],
    )(part)

    def add(p_ref, g_ref, o_ref):
        c = lax.axis_index("c")
        o_ref[...] = (p_ref[c].astype(F32) + g_ref[...].astype(F32)).astype(o_ref.dtype)

    spec = pl.BlockSpec((None, tr, C), lambda i, r: (i, r, 0))
    return pl.pallas_call(
        add, name="dwin_pair_add", grid=(half, R // tr),
        in_specs=[pl.BlockSpec((2, tr, C), lambda i, r: (i, r, 0)), spec],
        out_specs=spec, out_shape=_sds((half, R, C), part.dtype),
        compiler_params=_cp(("parallel", "parallel")),
    )(part, got)


def _gather_two_level(shard, name):
    def body(x_ref, out_ref, ssem, rsem, lsem):
        x, y, c, me = _my_pos()
        sibling = (x, y, 1 - c)
        chips = [(1 - x, y), (x, 1 - y), (1 - x, 1 - y)]
        slot = lambda px, py, pc: out_ref.at[4 * px + 2 * py + pc]

        def copy(kk, block, to, src=None):
            return _rcopy(slot(*block) if src is None else src, slot(*block), ssem.at[kk], rsem.at[kk], to)

        mine = pltpu.make_async_copy(x_ref, slot(x, y, c), lsem)
        mine.start()
        first = [copy(0, (x, y, c), sibling, src=x_ref)]
        first += [copy(1 + j, (x, y, c), (*chip, c), src=x_ref) for j, chip in enumerate(chips)]
        for cp in first:
            cp.start()
        passed = [copy(4 + j, (*chip, c), sibling) for j, chip in enumerate(chips)]
        for j, chip in enumerate(chips):
            copy(1 + j, (*chip, c), (x, y, c)).wait_recv()
            passed[j].start()
        copy(0, sibling, (x, y, c)).wait_recv()
        for j, chip in enumerate(chips):
            copy(4 + j, (*chip, 1 - c), (x, y, c)).wait_recv()
        for cp in first + passed:
            cp.wait_send()
        mine.wait()

    anyspec = pl.BlockSpec(memory_space=pl.ANY)
    return pl.pallas_call(
        body, name=name, in_specs=[anyspec], out_specs=anyspec,
        out_shape=_sds((N_DEV,) + tuple(shard.shape), shard.dtype),
        scratch_shapes=[pltpu.SemaphoreType.DMA((N_DEV - 1,)), pltpu.SemaphoreType.DMA((N_DEV - 1,)),
                        pltpu.SemaphoreType.DMA(())],
    )(shard)


PACK_ROWS = 16
ROW_MISC = 5
ROW_LOSS = 6
ROW_DMOD = 8


def _small_bwd(pack, dmodb, dcw, cat, wp, mp_, vp, cw_w, cw_m, cw_v):
    def body(pack_ref, dmodb_ref, dcw_ref, cat_ref, wp_ref, mp_ref, vp_ref, cww_ref, cwm_ref, cwv_ref,
             g_ref, d_ref, mo_ref, vo_ref, cg_ref, cd_ref, cm_ref, cv_ref, gwa_ref, loss_ref,
             allp, dmc, cwg, ssem, rsem):
        x, y, cc, me = _my_pos()
        allp[me] = pack_ref[...]
        dmc[pl.ds(me, 1), :] = dmodb_ref[pl.ds(me, 1), :]
        cwg[me] = dcw_ref[me]
        sends = []
        for d in range(1, N_DEV):
            peer, pid = _peer(x, y, cc, d)
            sends.append(_rcopy(pack_ref, allp.at[me], ssem.at[0, d - 1], rsem.at[0, d - 1], peer))
            sends.append(_rcopy(dmodb_ref.at[pl.ds(pid, 1), :], dmc.at[pl.ds(me, 1), :], ssem.at[1, d - 1], rsem.at[1, d - 1], peer))
            sends.append(_rcopy(dcw_ref.at[pid], cwg.at[me], ssem.at[2, d - 1], rsem.at[2, d - 1], peer))
        for cp in sends:
            cp.start()
        for d in range(1, N_DEV):
            peer, pid = _peer(x, y, cc, d)
            _rcopy(pack_ref, allp.at[pid], ssem.at[0, d - 1], rsem.at[0, d - 1], peer).wait_recv()
            _rcopy(dmodb_ref.at[pl.ds(pid, 1), :], dmc.at[pl.ds(pid, 1), :], ssem.at[1, d - 1], rsem.at[1, d - 1], peer).wait_recv()
            _rcopy(dcw_ref.at[pid], cwg.at[pid], ssem.at[2, d - 1], rsem.at[2, d - 1], peer).wait_recv()
        for cp in sends:
            cp.wait_send()

        tot = allp[0]
        cg = cwg[0]
        for s in range(1, N_DEV):
            tot = tot + allp[s]
            cg = cg + cwg[s]
        lane = lax.broadcasted_iota(jnp.int32, (PACK_ROWS, D), 1)
        row = lax.broadcasted_iota(jnp.int32, (PACK_ROWS, D), 0)
        gains = (row == ROW_MISC) & (lane >= LANES) & (lane < 3 * LANES)
        folded = tot + pltpu.roll(tot, D - HEAD_DIM, axis=1)
        keep = (lane % LANES) < HEAD_DIM
        g = jnp.where(gains, jnp.where(keep, folded, 0.0), tot)
        loss_ref[...] = jnp.broadcast_to(
            (0.5 / D) * jnp.sum(jnp.where(row == ROW_LOSS, tot, 0.0), keepdims=True).reshape(1, 1), loss_ref.shape)
        g = jnp.where(row == ROW_LOSS, 0.0, g)
        g_ref[...] = g
        d_ref[...], mo_ref[...], vo_ref[...] = _adamw_math(wp_ref[...], g, mp_ref[...], vp_ref[...])
        cg_ref[...] = cg
        cd_ref[...], cm_ref[...], cv_ref[...] = _adamw_math(cww_ref[...], cg, cwm_ref[...], cwv_ref[...])
        dm_pad = jnp.concatenate([dmc[...], jnp.zeros((LANES - N_DEV, ADA_SHARD), F32)], axis=0)
        gwa_ref[...] = _dot_f32(cat_ref[...], dm_pad)

    vm = pl.BlockSpec(memory_space=pltpu.VMEM)
    p16 = _sds((PACK_ROWS, D), F32)
    c32 = _sds((CONV_KP, LANES), F32)
    return pl.pallas_call(
        body, name="small_bwd",
        in_specs=[vm] * 10, out_specs=[vm] * 10,
        out_shape=[p16, p16, p16, p16, c32, c32, c32, c32, _sds((D, ADA_SHARD), F32), _sds((8, LANES), F32)],
        scratch_shapes=[pltpu.VMEM((N_DEV, PACK_ROWS, D), F32), pltpu.VMEM((N_DEV, ADA_SHARD), F32),
                        pltpu.VMEM((N_DEV, CONV_KP, LANES), F32),
                        pltpu.SemaphoreType.DMA((3, N_DEV - 1)), pltpu.SemaphoreType.DMA((3, N_DEV - 1))],
        compiler_params=pltpu.CompilerParams(vmem_limit_bytes=VMEM_LIMIT),
    )(pack, dmodb, dcw, cat, wp, mp_, vp, cw_w, cw_m, cw_v)


def _lanes(vec, start, total=D):
    n = vec.shape[1]
    return jnp.pad(vec, ((0, 0), (start, total - start - n)))


def _pack_small(rows5, misc, loss_row, six):
    z = jnp.zeros((1, D), F32)
    return jnp.concatenate(rows5 + [misc, loss_row, z] + [six.reshape(N_ADA, D), z, z], axis=0)


def kernel(x, c, w_ada, b_ada, norm1_g, w_in, b_forget, q_norm_g, k_norm_g, w_attn_proj, conv_w, conv_b, conv_ln_g, conv_ln_b, w_conv_proj, w_out, norm2_g, w_mlp1, w_mlp2, loss_target, m_w_ada, m_b_ada, m_norm1_g, m_w_in, m_b_forget, m_q_norm_g, m_k_norm_g, m_w_attn_proj, m_conv_w, m_conv_b, m_conv_ln_g, m_conv_ln_b, m_w_conv_proj, m_w_out, m_norm2_g, m_w_mlp1, m_w_mlp2, v_w_ada, v_b_ada, v_norm1_g, v_w_in, v_b_forget, v_q_norm_g, v_k_norm_g, v_w_attn_proj, v_conv_w, v_conv_b, v_conv_ln_g, v_conv_ln_b, v_w_conv_proj, v_w_out, v_norm2_g, v_w_mlp1, v_w_mlp2):
    me = 4 * lax.axis_index("x") + 2 * lax.axis_index("y") + lax.axis_index("c")
    xs, tgt = x[0], loss_target[0]
    T = xs.shape[0]
    sq = lambda a: a[0]
    pad_taps = lambda a: jnp.pad(a[0], ((0, CONV_KP - CONV_K), (0, 0)))

    b_slice = lax.dynamic_slice(b_ada, (0, me * ADA_SHARD), (1, ADA_SHARD))
    modb, ca_all, cwf = _ada_fwd(c, sq(w_ada), b_slice, pad_taps(conv_w))
    mod = modb.reshape(1, N_ADA * D)
    cw = jnp.transpose(cwf, (1, 0, 2)).reshape(CONV_KP, D)

    g_in = _gather_two_level(sq(w_in).astype(BF), "w_in_gather")
    d_in = g_in.shape[2] * N_DEV
    w_in_f = jnp.transpose(g_in, (1, 0, 2)).reshape(D, d_in)
    w_qkv = w_in_f[:, :3 * D]
    w_gg = w_in_f[:, 3 * D + N_HEADS:]
    w_f = jnp.pad(w_in_f[:, 3 * D:3 * D + N_HEADS], ((0, 0), (0, LANES - N_HEADS)))
    shards = [sq(w_attn_proj).astype(BF), sq(w_conv_proj).astype(BF), sq(w_out).astype(BF),
              sq(w_mlp1).astype(BF), sq(w_mlp2).astype(BF)]

    qg2 = jnp.tile(q_norm_g, (1, 2))
    kg2 = jnp.tile(k_norm_g, (1, 2))
    bf_pad = _lanes(b_forget, 0, LANES)

    h = _pre_in(xs, mod, norm1_g)
    pqkv = _matmul(h, w_qkv, "nn", BF, "mm_proj_qkv")
    pgg = _matmul(h, w_gg, "nn", BF, "mm_proj_gg")
    f = _matmul(h, w_f, "nn", F32, "mm_f")
    q, k, v, fc = _qkv_post(pqkv, f, qg2, kg2, bf_pad)
    fc3 = fc.reshape(N_HEADS // 2, 2, T)
    o, lse, g_ap, g_cp, g_out, g_1, g_2 = _flash_fwd(q, k, v, fc3, shards)
    w_ap, w_cp, w_o = g_ap.reshape(D, D), g_cp.reshape(D, D), g_out.reshape(D, D)
    w_2 = g_2.reshape(D_FF, D)
    ba = _matmul(o, w_ap, "nn", BF, "mm_ba")
    u0, u1, u3 = _conv_fwd(pgg, cw, conv_b, conv_ln_g, conv_ln_b)
    bb = _matmul(u3, w_cp, "nn", BF, "mm_bb")
    merged = _merge(ba, bb, pgg)
    mo = _matmul(merged, w_o, "nn", F32, "mm_out")
    x1, h2 = _post_out(xs, mo, mod, norm2_g)
    a, rl = _matmul(h2, g_1, "nn", BF, "mm_mlp1", relu2=True, b_slots=True)
    m2 = _matmul(rl, w_2, "nn", F32, "mm_mlp2")
    dy, dm2, dg2, sqcols = _loss_head(x1, m2, tgt, mod)

    da = _matmul(dm2, w_2, "nt", BF, "mm_drl", relu_of=a)
    dw_2 = _matmul(rl, dm2, "tn", BF, "mm_dw2")
    dh2 = _matmul(da, g_1, "nt", F32, "mm_dh2", b_slots=True)
    dw_1 = _matmul(h2, da, "tn", BF, "mm_dw1", out_slots=True)
    dx1, dmo, dsh2, dsc2, dn2g, dg1 = _norm2_bwd(dh2, x1, dy, mo, mod, norm2_g)
    dmerged = _matmul(dmo, w_o, "nt", BF, "mm_dmerged")
    dw_o = _matmul(merged, dmo, "tn", BF, "mm_dwout")
    dba, dbb, dgg = _gate_bwd(dmerged, ba, bb, pgg)
    du3 = _matmul(dbb, w_cp, "nt", F32, "mm_du3")
    dw_cp = _matmul(u3, dbb, "tn", BF, "mm_dwcp")
    do = _matmul(dba, w_ap, "nt", BF, "mm_do")
    dw_ap = _matmul(o, dba, "tn", BF, "mm_dwap")
    dgg, dlng, dlnb, dcb, dcw_full = _conv_bwd(du3, u1, u0, pgg, cw, conv_ln_g, conv_ln_b, dgg)
    delta = _attn_delta(do, o)
    dw_gg = _matmul(h, dgg, "tn", BF, "mm_dw_gg")
    parts = [dw_ap.reshape(N_DEV, D // N_DEV, D), dw_cp.reshape(N_DEV, D // N_DEV, D), dw_o.reshape(N_DEV, D // N_DEV, D),
             dw_1, dw_2.reshape(N_DEV, D_FF // N_DEV, D)]
    dq, rs_a, rs_b, dk, dv, dfc3, r_ap, r_cp, r_out, r_1, r_2 = _flash_bwd(q, k, v, do, lse, delta, fc3, parts)
    dfq = jnp.stack([rs_a, rs_b], axis=1).reshape(N_HEADS, T)
    dqkv, df, dqg, dkg, dbf = _qkv_bwd(dq, dk, dv, pqkv, f, dfc3.reshape(N_HEADS, T), dfq, qg2, kg2, bf_pad)
    dw_qkv = _matmul(h, dqkv, "tn", BF, "mm_dw_qkv")
    dw_f = _matmul(h, df, "tn", BF, "mm_dwf")
    dw_in_f = jnp.concatenate([dw_qkv, dw_f[:, :N_HEADS], dw_gg], axis=1)
    part_in = jnp.transpose(dw_in_f.reshape(D, N_DEV, d_in // N_DEV), (1, 0, 2))
    dh, r_in = _matmul(dqkv, w_qkv, "nt", F32, "mm_dh", scatter=(_pair_reduce(part_in),), more=((dgg, w_gg),))
    dhf = _matmul(df, w_f, "nt", F32, "mm_dhf")
    grad_x, dsh1, dsc1, dn1g = _norm1_bwd(dh, dhf, xs, dx1, mod, norm1_g)

    dmod = jnp.concatenate([dsh1, dsc1, dg1, dsh2, dsc2, dg2], axis=1)
    misc = jnp.concatenate([dbf, dqg, dkg, jnp.zeros((1, D - 3 * LANES), F32)], axis=1)
    pack = _pack_small([dn1g, dcb, dlng, dlnb, dn2g], misc, sqcols, dmod)
    dcw_blocks = jnp.transpose(dcw_full.reshape(CONV_KP, N_DEV, LANES), (1, 0, 2))

    def small_params(b_a, n1, bfg, qn, kn, cvb, lg, lb, n2):
        misc_p = jnp.concatenate([_lanes(bfg, 0, LANES), _lanes(qn, 0, LANES), _lanes(kn, 0, LANES),
                                  jnp.zeros((1, D - 3 * LANES), F32)], axis=1)
        return _pack_small([n1, cvb, lg, lb, n2], misc_p, jnp.zeros((1, D), F32), b_a)

    wp = small_params(b_ada, norm1_g, b_forget, q_norm_g, k_norm_g, conv_b, conv_ln_g, conv_ln_b, norm2_g)
    mp_ = small_params(m_b_ada, m_norm1_g, m_b_forget, m_q_norm_g, m_k_norm_g, m_conv_b, m_conv_ln_g, m_conv_ln_b, m_norm2_g)
    vp = small_params(v_b_ada, v_norm1_g, v_b_forget, v_q_norm_g, v_k_norm_g, v_conv_b, v_conv_ln_g, v_conv_ln_b, v_norm2_g)
    cat = jnp.pad(jnp.transpose(ca_all), ((0, 0), (0, LANES - N_DEV)))
    small = _small_bwd(pack, dmod.reshape(N_DEV, ADA_SHARD), dcw_blocks, cat,
                       wp, mp_, vp, pad_taps(conv_w), pad_taps(m_conv_w), pad_taps(v_conv_w))
    sp = small[0:4]
    scw = small[4:8]
    gw_ada, loss_t = small[8], small[9]
    loss = loss_t[0, 0]

    def unpack(p):
        misc_r = p[ROW_MISC:ROW_MISC + 1]
        return dict(
            b_ada=p[ROW_DMOD:ROW_DMOD + N_ADA].reshape(1, N_ADA * D), norm1_g=p[0:1], conv_b=p[1:2], conv_ln_g=p[2:3],
            conv_ln_b=p[3:4], norm2_g=p[4:5], b_forget=misc_r[:, 0:N_HEADS],
            q_norm_g=misc_r[:, LANES:LANES + HEAD_DIM], k_norm_g=misc_r[:, 2 * LANES:2 * LANES + HEAD_DIM])

    res = {}
    res["w_ada"] = _adamw(gw_ada[None], w_ada, m_w_ada, v_w_ada, "adamw_w_ada")
    res["w_in"] = _adamw(r_in, w_in, m_w_in, v_w_in, "adamw_w_in")
    res["w_attn_proj"] = _adamw(r_ap, w_attn_proj, m_w_attn_proj, v_w_attn_proj, "adamw_w_ap")
    res["w_conv_proj"] = _adamw(r_cp, w_conv_proj, m_w_conv_proj, v_w_conv_proj, "adamw_w_cp")
    res["w_out"] = _adamw(r_out, w_out, m_w_out, v_w_out, "adamw_w_out")
    res["w_mlp1"] = _adamw(r_1, w_mlp1, m_w_mlp1, v_w_mlp1, "adamw_w_mlp1")
    res["w_mlp2"] = _adamw(r_2, w_mlp2, m_w_mlp2, v_w_mlp2, "adamw_w_mlp2")

    names = ["w_ada", "b_ada", "norm1_g", "w_in", "b_forget", "q_norm_g", "k_norm_g", "w_attn_proj", "conv_w", "conv_b",
             "conv_ln_g", "conv_ln_b", "w_conv_proj", "w_out", "norm2_g", "w_mlp1", "w_mlp2"]
    outs = [loss, grad_x[None]]
    for kind in range(4):
        small_d = unpack(sp[kind])
        for nm in names:
            if nm in res:
                outs.append(res[nm][kind])
            elif nm == "conv_w":
                outs.append(scw[kind][:CONV_K][None])
            else:
                outs.append(small_d[nm])
    return tuple(outs)
```

```python
import functools

import jax
import jax.numpy as jnp
from jax import lax
from jax.experimental import pallas as pl
from jax.experimental.pallas import tpu as pltpu

F32 = jnp.float32
BF = jnp.bfloat16

N_DEV = 8
D = 1024
N_HEADS = 16
HEAD_DIM = 64
LANES = 128
SUBLANES = 8
CONV_K = 31
CONV_KP = 32
HALO = 32
CONV_ROWS = 64
D_FF = 4 * D
N_ADA = 6
ADA_SHARD = N_ADA * D // N_DEV
EPS = 1e-6
QK_SCALE = HEAD_DIM ** -0.5
LOG2E = 1.4426950408889634
LN2 = 0.6931471805599453
NEG = -1e30

ADAM_LR = 0.001
ADAM_B1 = 0.9
ADAM_B2 = 0.999
ADAM_EPS = 1e-08
ADAM_WD = 0.01
ADAM_STEP = 10

VMEM_LIMIT = 56 * 1024 * 1024
TM_ROWS = 1024
CONV_TM = 256
TQ = 512

MESH = pl.DeviceIdType.MESH


def _cp(sem=None):
    return pltpu.CompilerParams(dimension_semantics=sem, vmem_limit_bytes=VMEM_LIMIT)


def _sds(shape, dtype):
    return jax.ShapeDtypeStruct(tuple(shape), dtype)


def _full(arr):
    nd = arr.ndim
    return pl.BlockSpec(arr.shape, lambda *_: (0,) * nd)


def _fullshape(shape):
    nd = len(shape)
    return pl.BlockSpec(tuple(shape), lambda *_: (0,) * nd)


def _split3(x):
    hi = x.astype(BF)
    r1 = x - hi.astype(F32)
    mid = r1.astype(BF)
    lo = (r1 - mid.astype(F32)).astype(BF)
    return hi, mid, lo


def _dot_exact(x, mat):
    hi, mid, lo = _split3(x)
    d = lambda t: jnp.dot(t, mat, preferred_element_type=F32)
    return d(hi) + d(mid) + d(lo)


def _dot_f32(a, b):
    a1, a2, a3 = _split3(a)
    b1, b2, b3 = _split3(b)
    d = lambda s, t: jnp.dot(s, t, preferred_element_type=F32)
    return (d(a1, b3) + d(a3, b1) + d(a2, b2)) + (d(a1, b2) + d(a2, b1)) + d(a1, b1)


def _sigmoid(x):
    return 1.0 / (1.0 + jnp.exp(-x))


def _colsum(x):
    return jnp.sum(x, axis=0, keepdims=True)


def _my_pos():
    x, y, c = lax.axis_index("x"), lax.axis_index("y"), lax.axis_index("c")
    return x, y, c, 4 * x + 2 * y + c


def _peer(x, y, c, d):
    px = (1 - x) if d & 4 else x
    py = (1 - y) if d & 2 else y
    pc = (1 - c) if d & 1 else c
    return (px, py, pc), 4 * px + 2 * py + pc


def _matmul(a, b, form, out_dtype, name, tm=1024, tn=1024, tk=1024, scatter=(), relu2=False, relu_of=None,
            b_slots=False, out_slots=False, more=()):
    width = None
    if b_slots:
        assert form in ("nn", "nt") and b.shape[0] == N_DEV
        width = b.shape[2]
        if form == "nn":
            (M, K), N, tn = a.shape, N_DEV * width, 2 * width
        else:
            (M, K), N, tk = a.shape, b.shape[1], 2 * width
    elif form == "nn":
        (M, K), N = a.shape, b.shape[1]
    elif form == "nt":
        (M, K), N = a.shape, b.shape[0]
    else:
        (K, M), N = a.shape, b.shape[1]
    if out_slots:
        width = N // N_DEV
        tn = 2 * width
    tm, tn, tk = min(tm, M), min(tn, N), min(tk, K)
    assert M % tm == 0 and N % tn == 0 and K % tk == 0, (name, M, N, K)
    nk = K // tk
    if form == "tn":
        a_spec = pl.BlockSpec((tk, tm), lambda i, j, k: (k, i))
        dn = (((0,), (0,)), ((), ()))
    else:
        a_spec = pl.BlockSpec((tm, tk), lambda i, j, k: (i, k))
        dn = (((1,), (1 if form == "nt" else 0,)), ((), ()))
    if b_slots and form == "nn":
        b_spec = pl.BlockSpec((2, tk, width), lambda i, j, k: (j, k, 0))
    elif b_slots:
        b_spec = pl.BlockSpec((2, tn, width), lambda i, j, k: (k, j, 0))
    elif form == "nt":
        b_spec = pl.BlockSpec((tn, tk), lambda i, j, k: (j, k))
    else:
        b_spec = pl.BlockSpec((tk, tn), lambda i, j, k: (k, j))

    pairs = [(a, b)] + list(more)
    seg = [0]
    for a_s, _ in pairs:
        k_s = K if len(pairs) == 1 else a_s.shape[1]
        assert k_s % tk == 0 and (len(pairs) == 1 or (form == "nt" and not b_slots))
        seg.append(seg[-1] + k_s // tk)
    nk = seg[-1]
    specs_more = []
    for s in range(1, len(pairs)):
        lo_k, n_k = seg[s], seg[s + 1] - seg[s]
        kk = lambda k, lo_k=lo_k, n_k=n_k: jnp.clip(k - lo_k, 0, n_k - 1)
        specs_more += [pl.BlockSpec((tm, tk), lambda i, j, k, kk=kk: (i, kk(k))),
                       pl.BlockSpec((tn, tk), lambda i, j, k, kk=kk: (j, kk(k)))]
    if len(pairs) > 1:
        n0 = seg[1]
        a_spec = pl.BlockSpec((tm, tk), lambda i, j, k: (i, jnp.minimum(k, n0 - 1)))
        b_spec = pl.BlockSpec((tn, tk), lambda i, j, k: (j, jnp.minimum(k, n0 - 1)))

    nx = len(scatter)
    ne = 0 if relu_of is None else 1
    no = 2 if relu2 else 1
    nm = 2 * (len(pairs) - 1)
    grid = (M // tm, N // tn, nk)

    def body(a_ref, b_ref, *rest):
        ab_refs = [(a_ref, b_ref)] + [(rest[2 * s], rest[2 * s + 1]) for s in range(len(pairs) - 1)]
        rest = rest[nm:]
        e_ref = rest[0] if ne else None
        x_in = rest[ne:ne + nx]
        o_refs = rest[ne + nx:ne + nx + no]
        x_out = rest[ne + nx + no:ne + 2 * nx + no]
        scr = rest[ne + 2 * nx + no:]
        k = pl.program_id(2)
        if nx:
            first, last = _first_last(grid)

            @pl.when(first)
            def _():
                _xchg(x_in, x_out, scr[-3:], False, wait=False)

        def finish(val):
            if out_slots:
                o_refs[0][0] = val[:, 0:width].astype(out_dtype)
                o_refs[0][1] = val[:, width:2 * width].astype(out_dtype)
            elif relu2:
                o_refs[0][...] = val.astype(out_dtype)
                r = jnp.maximum(val, 0.0)
                o_refs[1][...] = (r * r).astype(out_dtype)
            elif ne:
                o_refs[0][...] = (val * (2.0 * jnp.maximum(e_ref[...].astype(F32), 0.0))).astype(out_dtype)
            else:
                o_refs[0][...] = val.astype(out_dtype)

        def accumulate(ar, br):
            dot = lambda u, w: lax.dot_general(u.astype(BF), w.astype(BF), dn, preferred_element_type=F32)
            if b_slots and form == "nn":
                part = jnp.concatenate([dot(ar[...], br[0]), dot(ar[...], br[1])], axis=1)
            elif b_slots:
                part = dot(ar[:, 0:width], br[0]) + dot(ar[:, width:2 * width], br[1])
            else:
                part = dot(ar[...], br[...])
            if nk == 1:
                finish(part)
            else:
                acc = scr[0]

                @pl.when(k == 0)
                def _():
                    acc[...] = part

                @pl.when(k > 0)
                def _():
                    acc[...] += part

        if len(pairs) == 1:
            accumulate(a_ref, b_ref)
        else:
            for s, (ar, br) in enumerate(ab_refs):
                @pl.when((k >= seg[s]) & (k < seg[s + 1]))
                def _(ar=ar, br=br):
                    accumulate(ar, br)

        if nk > 1:
            @pl.when(k == nk - 1)
            def _():
                finish(scr[0][...])

        if nx:
            @pl.when(last)
            def _():
                _xchg(x_in, x_out, scr[-3:], False, wait=True)

    x_specs, x_shapes, x_scratch = _xchg_parts(scatter) if nx else ([], [], [])
    sem = ("arbitrary",) * 3 if nx else ("parallel", "parallel", "arbitrary")
    o_spec = pl.BlockSpec((tm, tn), lambda i, j, k: (i, j))
    o_shape = _sds((M, N), out_dtype)
    if out_slots:
        o_spec = pl.BlockSpec((2, tm, width), lambda i, j, k: (j, i, 0))
        o_shape = _sds((N_DEV, M, width), out_dtype)
    res = pl.pallas_call(
        body, name=name, grid=grid,
        in_specs=[a_spec, b_spec] + specs_more + [o_spec] * ne + x_specs,
        out_specs=[o_spec] * no + x_specs,
        out_shape=[o_shape] * no + x_shapes,
        scratch_shapes=([] if nk == 1 else [pltpu.VMEM((tm, tn), F32)]) + x_scratch,
        compiler_params=_cp(sem),
    )(a, b, *[t for p in more for t in p], *([relu_of] if ne else []), *scatter)
    return res if (nx or relu2) else res[0]


def _rows_call(body, name, n_tiles, ins, outs, scratch=(), aliases=None):
    res = pl.pallas_call(
        body, name=name, grid=(n_tiles,),
        in_specs=[s for _, s in ins],
        out_specs=[s for _, s in outs],
        out_shape=[o for o, _ in outs],
        scratch_shapes=list(scratch),
        input_output_aliases=aliases or {},
        compiler_params=_cp(("arbitrary",)),
    )(*[a for a, _ in ins])
    return res


def _rspec(tm, width, cb=0, rev_n=None):
    if rev_n is None:
        return pl.BlockSpec((tm, width), lambda i: (i, cb))
    return pl.BlockSpec((tm, width), lambda i: (rev_n - 1 - i, cb))


def _row_out(T, tm, width, dtype, rev_n=None):
    return (_sds((T, width), dtype), _rspec(tm, width, 0, rev_n))


def _acc_out(shape, dtype=F32):
    return (_sds(shape, dtype), _fullshape(shape))


def _mod_parts(mod):
    return [mod[:, i * D:(i + 1) * D] for i in range(N_ADA)]


def _pre_in(x, mod, n1g):
    T = x.shape[0]
    tm = TM_ROWS

    def body(x_ref, mod_ref, g_ref, h_ref):
        sh1, sc1 = mod_ref[:, 0:D], mod_ref[:, D:2 * D]
        xv = x_ref[...]
        r = lax.rsqrt(jnp.mean(xv * xv, axis=-1, keepdims=True) + EPS)
        h_ref[...] = ((xv * r) * g_ref[...] * (1.0 + sc1) + sh1).astype(BF)

    return _rows_call(body, "pre_in", T // tm,
                      [(x, _rspec(tm, D)), (mod, _full(mod)), (n1g, _full(n1g))],
                      [_row_out(T, tm, D, BF)])[0]


def _seg_mat():
    r = jnp.arange(LANES)[:, None] // HEAD_DIM
    c = jnp.arange(LANES)[None, :] // HEAD_DIM
    return jnp.where(r == c, 1.0 / HEAD_DIM, 0.0).astype(BF)


def _tri_mat(n, upper):
    r = jnp.arange(n)[:, None]
    c = jnp.arange(n)[None, :]
    return jnp.where((r <= c) if upper else (r >= c), 1.0, 0.0).astype(BF)


def _log_sigmoid(z):
    return jnp.minimum(z, 0.0) - jnp.log(1.0 + jnp.exp(-jnp.abs(z)))


def _qkv_post(proj, f, qg2, kg2, bf_pad):
    T = proj.shape[0]
    tm = TM_ROWS
    seg = _seg_mat()
    tri = _tri_mat(tm, True)

    def body(q_ref, k_ref, v_ref, f_ref, qg_ref, kg_ref, bf_ref, seg_ref, tri_ref,
             qo_ref, ko_ref, vo_ref, fc_ref, carry_ref):
        i = pl.program_id(0)

        @pl.when(i == 0)
        def _():
            carry_ref[...] = jnp.zeros_like(carry_ref)

        segm = seg_ref[...]
        for j in range(D // LANES):
            sl = slice(j * LANES, (j + 1) * LANES)
            qc = q_ref[:, sl].astype(F32)
            rq = lax.rsqrt(_dot_exact(qc * qc, segm) + EPS)
            qo_ref[:, sl] = ((qc * rq) * qg_ref[...] * (QK_SCALE * LOG2E)).astype(BF)
            kc = k_ref[:, sl].astype(F32)
            rk = lax.rsqrt(_dot_exact(kc * kc, segm) + EPS)
            ko_ref[:, sl] = ((kc * rk) * kg_ref[...]).astype(BF)
        vo_ref[...] = v_ref[...].astype(BF)
        lf = _log_sigmoid(f_ref[...] + bf_ref[...])
        lft = lf.T[0:N_HEADS, :]
        carry = carry_ref[:, 0:1]
        fc_ref[...] = _dot_exact(lft, tri_ref[...]) + carry
        carry_ref[...] = jnp.broadcast_to(carry + jnp.sum(lft, axis=1, keepdims=True), carry_ref.shape)

    outs = [_row_out(T, tm, D, BF), _row_out(T, tm, D, BF), _row_out(T, tm, D, BF),
            (_sds((N_HEADS, T), F32), pl.BlockSpec((N_HEADS, tm), lambda i: (0, i)))]
    ins = [(proj, _rspec(tm, D, 0)), (proj, _rspec(tm, D, 1)), (proj, _rspec(tm, D, 2)), (f, _rspec(tm, LANES)),
           (qg2, _full(qg2)), (kg2, _full(kg2)), (bf_pad, _full(bf_pad)), (seg, _full(seg)), (tri, _full(tri))]
    return _rows_call(body, "qkv_post", T // tm, ins, outs, [pltpu.VMEM((N_HEADS, LANES), F32)])


def _lane_lo():
    return lax.broadcasted_iota(jnp.int32, (1, LANES), 1) < HEAD_DIM


def _nt(a, b):
    return lax.dot_general(a, b, (((1,), (1,)), ((), ())), preferred_element_type=F32)


def _tn(a, b):
    return lax.dot_general(a, b, (((0,), (0,)), ((), ())), preferred_element_type=F32)


def _head_rep(x, lo):
    rolled = pltpu.roll(x, HEAD_DIM, axis=1)
    return jnp.where(lo, x, rolled), jnp.where(lo, rolled, x)


def _diag_mask(t):
    return lax.broadcasted_iota(jnp.int32, (t, t), 1) <= lax.broadcasted_iota(jnp.int32, (t, t), 0)


def _first_last(grid):
    ids = [pl.program_id(a) for a in range(len(grid))]
    first = functools.reduce(jnp.logical_and, [i == 0 for i in ids])
    last = functools.reduce(jnp.logical_and, [i == g - 1 for i, g in zip(ids, grid)])
    return first, last


def _flash_fwd(q, k, v, fc3, shards):
    T = q.shape[0]
    tq = TQ
    nq = T // tq
    hp_n = N_HEADS // 2
    rep = tq // LANES
    nx = len(shards)
    grid = (hp_n, nq, nq)

    def body(q_ref, k_ref, v_ref, fk_ref, fq_ref, *rest):
        x_in, (o_ref, lse_ref), x_out = rest[:nx], rest[nx:nx + 2], rest[nx + 2:2 * nx + 2]
        acc_ref, m_ref = rest[2 * nx + 2:2 * nx + 4]
        sems = rest[2 * nx + 4:]
        qi, ki = pl.program_id(1), pl.program_id(2)
        first, last = _first_last(grid)

        @pl.when(first)
        def _():
            _xchg(x_in, x_out, sems, True, wait=False)

        @pl.when(ki == 0)
        def _():
            acc_ref[...] = jnp.zeros_like(acc_ref)
            m_ref[...] = jnp.full_like(m_ref, NEG)

        lane = lax.broadcasted_iota(jnp.int32, (1, LANES), 1)
        sum_lane = (HEAD_DIM, 0)

        def step(diag):
            lo = _lane_lo()
            q2, k2, v2 = q_ref[...], k_ref[...], v_ref[...]
            zero = jnp.zeros_like(k2)
            bias = (fq_ref[:, 0:1] - fk_ref[...]) * LOG2E
            for hh in range(2):
                sel = (lambda t: jnp.where(lo, t, zero)) if hh == 0 else (lambda t: jnp.where(lo, zero, t))
                ones = jnp.where(lane == sum_lane[hh], 1.0, 0.0).astype(BF)
                v_aug = jnp.where(lo, v2, ones) if hh == 0 else jnp.where(lo, ones, v2)
                s = _nt(sel(q2), k2) + bias[hh:hh + 1, :]
                if diag:
                    s = jnp.where(_diag_mask(tq), s, NEG)
                m_old = m_ref[hh]
                m_new = jnp.maximum(m_old, jnp.max(s, axis=-1, keepdims=True))
                alpha = jnp.exp2(m_old - m_new)
                p = jnp.exp2(s - jnp.tile(m_new, (1, rep)))
                m_ref[hh] = m_new
                acc_ref[hh] = acc_ref[hh] * alpha + jnp.dot(p.astype(BF), v_aug, preferred_element_type=F32)

        @pl.when(ki < qi)
        def _():
            step(False)

        @pl.when(ki == qi)
        def _():
            step(True)
            lo = _lane_lo()
            acc_a, acc_b = acc_ref[0], acc_ref[1]
            la = jnp.broadcast_to(acc_a[:, sum_lane[0]:sum_lane[0] + 1], (tq, LANES))
            lb = jnp.broadcast_to(acc_b[:, sum_lane[1]:sum_lane[1] + 1], (tq, LANES))
            o_ref[...] = jnp.where(lo, acc_a / la, acc_b / lb)
            lse_ref[...] = jnp.where(lo, m_ref[0] + jnp.log(la) * LOG2E, m_ref[1] + jnp.log(lb) * LOG2E)

        @pl.when(last)
        def _():
            _xchg(x_in, x_out, sems, True, wait=True)

    qspec = pl.BlockSpec((tq, LANES), lambda h, i, j: (i, h))
    kspec = pl.BlockSpec((tq, LANES), lambda h, i, j: (jnp.minimum(i, j), h))
    fkspec = pl.BlockSpec((None, 2, tq), lambda h, i, j: (h, 0, jnp.minimum(i, j)))
    fqspec = pl.BlockSpec((None, 2, tq), lambda h, i, j: (h, 0, i))
    x_specs, x_shapes, x_scratch = _xchg_parts(shards)
    return pl.pallas_call(
        body, name="attn_fwd", grid=grid,
        in_specs=[qspec, kspec, kspec, fkspec, fqspec] + x_specs,
        out_specs=[qspec, qspec] + x_specs,
        out_shape=[_sds((T, D), F32), _sds((T, D), F32)] + x_shapes,
        scratch_shapes=[pltpu.VMEM((2, tq, LANES), F32), pltpu.VMEM((2, tq, LANES), F32)] + x_scratch,
        compiler_params=_cp(("arbitrary", "arbitrary", "arbitrary")),
    )(q, k, v, fc3, fc3, *shards)


def _attn_delta(do, o):
    T = o.shape[0]
    tm = TM_ROWS
    ones = (_seg_mat().astype(F32) * HEAD_DIM).astype(BF)

    def body(do_ref, o_ref, seg_ref, dl_ref):
        segm = seg_ref[...]
        for j in range(D // LANES):
            sl = slice(j * LANES, (j + 1) * LANES)
            dl_ref[:, sl] = _dot_exact(do_ref[:, sl].astype(BF).astype(F32) * o_ref[:, sl], segm)

    ins = [(do, _rspec(tm, D)), (o, _rspec(tm, D)), (ones, _full(ones))]
    return _rows_call(body, "attn_delta", T // tm, ins, [_row_out(T, tm, D, F32)])[0]


def _flash_bwd(q, k, v, do, lse, delta, fc3, parts):
    T = q.shape[0]
    tq = TQ
    nq = T // tq
    hp_n = N_HEADS // 2
    rep = tq // LANES
    nx = len(parts)
    grid = (hp_n, nq, nq)

    def body(q_ref, k_ref, v_ref, do_ref, lse_ref, dl_ref, fk_ref, fq_ref, *rest):
        x_in, x_out = rest[:nx], rest[nx + 6:2 * nx + 6]
        dq_ref, ra_ref, rb_ref, dk_ref, dv_ref, dfc_ref = rest[nx:nx + 6]
        dk_acc, dv_acc, df_acc = rest[2 * nx + 6:2 * nx + 9]
        sems = rest[2 * nx + 9:]
        ki, qi = pl.program_id(1), pl.program_id(2)
        first, last = _first_last(grid)
        qrows = pl.ds(pl.multiple_of(qi * tq, tq), tq)

        @pl.when(first)
        def _():
            _xchg(x_in, x_out, sems, False, wait=False)

        @pl.when((ki == 0) & (qi == 0))
        def _():
            dq_ref[...] = jnp.zeros_like(dq_ref)
            ra_ref[...] = jnp.zeros_like(ra_ref)
            rb_ref[...] = jnp.zeros_like(rb_ref)

        @pl.when(qi == 0)
        def _():
            dk_acc[...] = jnp.zeros_like(dk_acc)
            dv_acc[...] = jnp.zeros_like(dv_acc)
            df_acc[...] = jnp.zeros_like(df_acc)

        def step(diag):
            lo = _lane_lo()
            q2, k2, v2 = q_ref[...], k_ref[...], v_ref[...]
            do2 = do_ref[...].astype(BF)
            zero = jnp.zeros_like(q2)
            bias = (fq_ref[:, 0:1] - fk_ref[...]) * LOG2E
            lses = _head_rep(lse_ref[...], lo)
            dls = _head_rep(dl_ref[...], lo)
            dk_t = None
            dv_t = None
            dq_t = None
            for hh in range(2):
                sel = (lambda t: jnp.where(lo, t, zero)) if hh == 0 else (lambda t: jnp.where(lo, zero, t))
                s = _nt(sel(q2), k2) + bias[hh:hh + 1, :]
                if diag:
                    s = jnp.where(_diag_mask(tq), s, NEG)
                p = jnp.exp2(s - jnp.tile(lses[hh], (1, rep)))
                dp = _nt(sel(do2), v2)
                ds = p * (dp - jnp.tile(dls[hh], (1, rep)))
                ds_b = ds.astype(BF)
                dvp = _tn(p.astype(BF), sel(do2))
                dkp = _tn(ds_b, sel(q2))
                dqp = jnp.dot(ds_b, sel(k2), preferred_element_type=F32)
                dv_t = dvp if dv_t is None else dv_t + dvp
                dk_t = dkp if dk_t is None else dk_t + dkp
                dq_t = dqp if dq_t is None else dq_t + dqp
                df_acc[hh:hh + 1, :] -= _colsum(ds)
                r_ref = ra_ref if hh == 0 else rb_ref
                r_ref[qrows, :] += jnp.sum(ds, axis=-1, keepdims=True)
            dk_acc[...] += dk_t
            dv_acc[...] += dv_t
            dq_ref[qrows, :] += dq_t * QK_SCALE

        @pl.when(qi > ki)
        def _():
            step(False)

        @pl.when(qi == ki)
        def _():
            step(True)

        @pl.when(qi == nq - 1)
        def _():
            dk_ref[...] = dk_acc[...] * LN2
            dv_ref[...] = dv_acc[...]
            dfc_ref[...] = df_acc[...]

        @pl.when(last)
        def _():
            _xchg(x_in, x_out, sems, False, wait=True)

    kspec = pl.BlockSpec((tq, LANES), lambda h, j, i: (j, h))
    qspec = pl.BlockSpec((tq, LANES), lambda h, j, i: (jnp.maximum(i, j), h))
    fkspec = pl.BlockSpec((None, 2, tq), lambda h, j, i: (h, 0, j))
    fqspec = pl.BlockSpec((None, 2, tq), lambda h, j, i: (h, 0, jnp.maximum(i, j)))
    x_specs, x_shapes, x_scratch = _xchg_parts(parts)
    dqspec = pl.BlockSpec((T, LANES), lambda h, j, i: (0, h))
    rspec = pl.BlockSpec((None, T, 1), lambda h, j, i: (h, 0, 0))
    return pl.pallas_call(
        body, name="attn_bwd", grid=grid,
        in_specs=[qspec, kspec, kspec, qspec, qspec, qspec, fkspec, fqspec] + x_specs,
        out_specs=[dqspec, rspec, rspec, kspec, kspec, fkspec] + x_specs,
        out_shape=[_sds((T, D), F32), _sds((hp_n, T, 1), F32), _sds((hp_n, T, 1), F32),
                   _sds((T, D), F32), _sds((T, D), F32), _sds((hp_n, 2, T), F32)] + x_shapes,
        scratch_shapes=[pltpu.VMEM((tq, LANES), F32), pltpu.VMEM((tq, LANES), F32), pltpu.VMEM((2, tq), F32)] + x_scratch,
        compiler_params=_cp(("arbitrary", "arbitrary", "arbitrary")),
    )(q, k, v, do, lse, delta, fc3, fc3, *parts)


def _layer_norm_stats(u1):
    mu = jnp.mean(u1, axis=-1, keepdims=True)
    xc = u1 - mu
    rstd = lax.rsqrt(jnp.mean(xc * xc, axis=-1, keepdims=True) + EPS)
    return xc * rstd, rstd


def _shifted_copies(buf, sh, tm):
    rows = tm + HALO - SUBLANES
    for b in range(1, SUBLANES):
        sh[b - 1, 0:rows, :] = buf[b:b + rows, :]


def _window(buf, sh, off, rows, sl):
    a8, b = off // SUBLANES * SUBLANES, off % SUBLANES
    return buf[a8:a8 + rows, sl] if b == 0 else sh[b - 1, a8:a8 + rows, sl]


def _conv_fwd(proj, cw, cb, lng, lnb):
    T = proj.shape[0]
    tm = CONV_TM

    def body(a_ref, b_ref, w_ref, cb_ref, g_ref, bb_ref, u0_ref, u1_ref, u3_ref, buf, sh):
        i = pl.program_id(0)

        @pl.when(i == 0)
        def _():
            buf[0:HALO, :] = jnp.zeros((HALO, D), F32)

        u0 = a_ref[...].astype(F32) * _sigmoid(b_ref[...].astype(F32))
        u0_ref[...] = u0
        buf[HALO:HALO + tm, :] = u0
        _shifted_copies(buf, sh, tm)
        for j in range(D // LANES):
            sl = slice(j * LANES, (j + 1) * LANES)
            for r0 in range(0, tm, CONV_ROWS):
                acc = jnp.broadcast_to(cb_ref[:, sl], (CONV_ROWS, LANES))
                for kk in range(CONV_K):
                    acc = acc + w_ref[kk:kk + 1, sl] * _window(buf, sh, r0 + HALO - (CONV_K - 1) + kk, CONV_ROWS, sl)
                u1_ref[r0:r0 + CONV_ROWS, sl] = acc
        buf[0:HALO, :] = buf[tm:tm + HALO, :]
        xh, _ = _layer_norm_stats(u1_ref[...])
        u2 = xh * g_ref[...] + bb_ref[...]
        u3_ref[...] = (u2 * _sigmoid(u2)).astype(BF)

    ins = [(proj, _rspec(tm, D, 0)), (proj, _rspec(tm, D, 1)), (cw, _full(cw)), (cb, _full(cb)),
           (lng, _full(lng)), (lnb, _full(lnb))]
    outs = [_row_out(T, tm, D, F32), _row_out(T, tm, D, F32), _row_out(T, tm, D, BF)]
    return _rows_call(body, "conv_fwd", T // tm, ins, outs,
                      [pltpu.VMEM((tm + HALO, D), F32), pltpu.VMEM((SUBLANES - 1, tm + HALO, D), F32)])


def _conv_bwd(du3, u1, u0, proj, cw, lng, lnb, dgg):
    T = du3.shape[0]
    tm = CONV_TM
    n = T // tm
    per = tm // HALO

    def body(du3_ref, u1_ref, u0_ref, halo_ref, a_ref, b_ref, w_ref, g_ref, bb_ref, dgg_in_ref,
             dgl_ref, dg_ref, dbb_ref, dcb_ref, dw_ref, dbuf, ubuf, du0_buf, dsh, ush, dw8):
        i = pl.program_id(0)
        r = n - 1 - i

        @pl.when(i == 0)
        def _():
            dbuf[tm:tm + HALO, :] = jnp.zeros((HALO, D), F32)
            dg_ref[...] = jnp.zeros_like(dg_ref)
            dbb_ref[...] = jnp.zeros_like(dbb_ref)
            dcb_ref[...] = jnp.zeros_like(dcb_ref)
            dw8[...] = jnp.zeros_like(dw8)

        xh, rstd = _layer_norm_stats(u1_ref[...])
        g = g_ref[...]
        u2 = xh * g + bb_ref[...]
        s2 = _sigmoid(u2)
        du2 = du3_ref[...] * (s2 * (1.0 + u2 * (1.0 - s2)))
        dg_ref[...] += _colsum(du2 * xh)
        dbb_ref[...] += _colsum(du2)
        dxh = du2 * g
        du1 = rstd * (dxh - jnp.mean(dxh, axis=-1, keepdims=True) - xh * jnp.mean(dxh * xh, axis=-1, keepdims=True))
        dcb_ref[...] += _colsum(du1)
        dbuf[0:tm, :] = du1
        ubuf[HALO:HALO + tm, :] = u0_ref[...]
        ubuf[0:HALO, :] = jnp.where(r > 0, halo_ref[...], 0.0)
        _shifted_copies(dbuf, dsh, tm)
        _shifted_copies(ubuf, ush, tm)
        for j in range(D // LANES):
            sl = slice(j * LANES, (j + 1) * LANES)
            for r0 in range(0, tm, CONV_ROWS):
                d1 = dbuf[r0:r0 + CONV_ROWS, sl]
                acc = jnp.zeros((CONV_ROWS, LANES), F32)
                for kk in range(CONV_K):
                    acc = acc + w_ref[kk:kk + 1, sl] * _window(dbuf, dsh, r0 + CONV_K - 1 - kk, CONV_ROWS, sl)
                    prod = d1 * _window(ubuf, ush, r0 + HALO - (CONV_K - 1) + kk, CONV_ROWS, sl)
                    dw8[kk * SUBLANES:(kk + 1) * SUBLANES, sl] += prod.reshape(
                        CONV_ROWS // SUBLANES, SUBLANES, LANES).sum(axis=0)
                du0_buf[r0:r0 + CONV_ROWS, sl] = acc
        dbuf[tm:tm + HALO, :] = dbuf[0:HALO, :]
        du0 = du0_buf[...]
        af, bfl = a_ref[...].astype(F32), b_ref[...].astype(F32)
        sb = _sigmoid(bfl)
        dgl_ref[:, 0:D] = (du0 * sb).astype(BF)
        dgl_ref[:, D:2 * D] = (du0 * af * sb * (1.0 - sb)).astype(BF)

        @pl.when(i == n - 1)
        def _():
            for kk in range(CONV_KP):
                dw_ref[kk:kk + 1, :] = _colsum(dw8[kk * SUBLANES:(kk + 1) * SUBLANES, :])

    rs = lambda cb: _rspec(tm, D, cb, n)
    halo_spec = pl.BlockSpec((HALO, D), lambda i: (jnp.maximum((n - 1 - i) * per - 1, 0), 0))
    ins = [(du3, rs(0)), (u1, rs(0)), (u0, rs(0)), (u0, halo_spec), (proj, rs(0)), (proj, rs(1)),
           (cw, _full(cw)), (lng, _full(lng)), (lnb, _full(lnb)), (dgg, pl.BlockSpec(memory_space=pl.ANY))]
    outs = [(_sds(dgg.shape, dgg.dtype), _rspec(tm, 2 * D, 0, n)),
            _acc_out((1, D)), _acc_out((1, D)), _acc_out((1, D)), _acc_out((CONV_KP, D))]
    shifted = pltpu.VMEM((SUBLANES - 1, tm + HALO, D), F32)
    return _rows_call(body, "conv_bwd", n, ins, outs,
                      [pltpu.VMEM((tm + HALO, D), F32), pltpu.VMEM((tm + HALO, D), F32), pltpu.VMEM((tm, D), F32),
                       shifted, shifted, pltpu.VMEM((CONV_KP * SUBLANES, D), F32)], aliases={len(ins) - 1: 0})


def _merge(ba, bb, proj):
    T = ba.shape[0]
    tm = TM_ROWS

    def body(ba_ref, bb_ref, ga_ref, gb_ref, o_ref):
        sa, sb = _sigmoid(ga_ref[...].astype(F32)), _sigmoid(gb_ref[...].astype(F32))
        o_ref[...] = (sa * ba_ref[...].astype(F32) + sb * bb_ref[...].astype(F32)).astype(BF)

    ins = [(ba, _rspec(tm, D)), (bb, _rspec(tm, D)), (proj, _rspec(tm, D, 2)), (proj, _rspec(tm, D, 3))]
    return _rows_call(body, "merge", T // tm, ins, [_row_out(T, tm, D, BF)])[0]


def _post_out(x, mo, mod, n2g):
    T = x.shape[0]
    tm = TM_ROWS

    def body(x_ref, mo_ref, mod_ref, g_ref, x1_ref, h2_ref):
        g1 = mod_ref[:, 2 * D:3 * D]
        sh2, sc2 = mod_ref[:, 3 * D:4 * D], mod_ref[:, 4 * D:5 * D]
        x1 = x_ref[...] + g1 * mo_ref[...]
        x1_ref[...] = x1
        r = lax.rsqrt(jnp.mean(x1 * x1, axis=-1, keepdims=True) + EPS)
        h2_ref[...] = ((x1 * r) * g_ref[...] * (1.0 + sc2) + sh2).astype(BF)

    ins = [(x, _rspec(tm, D)), (mo, _rspec(tm, D)), (mod, _full(mod)), (n2g, _full(n2g))]
    return _rows_call(body, "post_out", T // tm, ins, [_row_out(T, tm, D, F32), _row_out(T, tm, D, BF)])


def _loss_head(x1, m2, tgt, mod):
    T = x1.shape[0]
    tm = TM_ROWS

    def body(x1_ref, m2_ref, t_ref, mod_ref, dy_ref, dm2_ref, dg2_ref, sq_ref):
        i = pl.program_id(0)

        @pl.when(i == 0)
        def _():
            dg2_ref[...] = jnp.zeros_like(dg2_ref)
            sq_ref[...] = jnp.zeros_like(sq_ref)

        g2 = mod_ref[:, 5 * D:6 * D]
        m2 = m2_ref[...]
        err = x1_ref[...] + g2 * m2 - t_ref[...]
        dy = err * (1.0 / D)
        dy_ref[...] = dy
        dm2_ref[...] = (g2 * dy).astype(BF)
        dg2_ref[...] += _colsum(dy * m2)
        sq_ref[...] += _colsum(err * err)

    ins = [(x1, _rspec(tm, D)), (m2, _rspec(tm, D)), (tgt, _rspec(tm, D)), (mod, _full(mod))]
    outs = [_row_out(T, tm, D, F32), _row_out(T, tm, D, BF), _acc_out((1, D)), _acc_out((1, D))]
    return _rows_call(body, "loss_head", T // tm, ins, outs)


def _norm2_bwd(dh2, x1, dy, mo, mod, n2g):
    T = x1.shape[0]
    tm = TM_ROWS

    def body(dh_ref, x1_ref, dy_ref, mo_ref, mod_ref, g_ref, dx1_ref, dmo_ref, dsh_ref, dsc_ref, dg_ref, dg1_ref):
        i = pl.program_id(0)

        @pl.when(i == 0)
        def _():
            for r in (dsh_ref, dsc_ref, dg_ref, dg1_ref):
                r[...] = jnp.zeros_like(r)

        g1, sc2 = mod_ref[:, 2 * D:3 * D], mod_ref[:, 4 * D:5 * D]
        g = g_ref[...]
        x1 = x1_ref[...]
        dh = dh_ref[...]
        r = lax.rsqrt(jnp.mean(x1 * x1, axis=-1, keepdims=True) + EPS)
        xn = x1 * r
        dsh_ref[...] += _colsum(dh)
        dsc_ref[...] += _colsum(dh * xn * g)
        dg_ref[...] += _colsum(dh * xn * (1.0 + sc2))
        dxn = dh * g * (1.0 + sc2)
        dx1 = dy_ref[...] + r * (dxn - xn * jnp.mean(dxn * xn, axis=-1, keepdims=True))
        dx1_ref[...] = dx1
        dg1_ref[...] += _colsum(dx1 * mo_ref[...])
        dmo_ref[...] = (g1 * dx1).astype(BF)

    ins = [(dh2, _rspec(tm, D)), (x1, _rspec(tm, D)), (dy, _rspec(tm, D)), (mo, _rspec(tm, D)),
           (mod, _full(mod)), (n2g, _full(n2g))]
    outs = [_row_out(T, tm, D, F32), _row_out(T, tm, D, BF)] + [_acc_out((1, D)) for _ in range(4)]
    return _rows_call(body, "norm2_bwd", T // tm, ins, outs)


def _gate_bwd(dmerged, ba, bb, proj):
    T = ba.shape[0]
    tm = TM_ROWS

    def body(dm_ref, ba_ref, bb_ref, ga_ref, gb_ref, dba_ref, dbb_ref, dgt_ref):
        dm = dm_ref[...].astype(F32)
        sa, sb = _sigmoid(ga_ref[...].astype(F32)), _sigmoid(gb_ref[...].astype(F32))
        dba_ref[...] = (dm * sa).astype(BF)
        dbb_ref[...] = (dm * sb).astype(BF)
        dgt_ref[:, 0:D] = (dm * ba_ref[...].astype(F32) * sa * (1.0 - sa)).astype(BF)
        dgt_ref[:, D:2 * D] = (dm * bb_ref[...].astype(F32) * sb * (1.0 - sb)).astype(BF)

    ins = [(dmerged, _rspec(tm, D)), (ba, _rspec(tm, D)), (bb, _rspec(tm, D)),
           (proj, _rspec(tm, D, 2)), (proj, _rspec(tm, D, 3))]
    outs = [_row_out(T, tm, D, BF), _row_out(T, tm, D, BF), (_sds((T, 4 * D), BF), _rspec(tm, 2 * D, 1))]
    return _rows_call(body, "gate_bwd", T // tm, ins, outs)


def _qkv_bwd(dq, dk, dv, proj, f, dfc, dfq, qg2, kg2, bf_pad):
    T = proj.shape[0]
    tm = TM_ROWS
    n = T // tm
    seg = _seg_mat()
    tri = _tri_mat(tm, False)

    def body(dq_ref, dk_ref, dv_ref, q_ref, k_ref, f_ref, dfc_ref, dfq_ref, qg_ref, kg_ref, bf_ref, seg_ref, tri_ref,
             dqkv_ref, dfo_ref, dqg_ref, dkg_ref, dbf_ref, carry_ref):
        i = pl.program_id(0)

        @pl.when(i == 0)
        def _():
            carry_ref[...] = jnp.zeros_like(carry_ref)
            dqg_ref[...] = jnp.zeros_like(dqg_ref)
            dkg_ref[...] = jnp.zeros_like(dkg_ref)
            dbf_ref[...] = jnp.zeros_like(dbf_ref)

        segm = seg_ref[...]
        dqg = jnp.zeros((1, LANES), F32)
        dkg = jnp.zeros((1, LANES), F32)
        for j in range(D // LANES):
            sl = slice(j * LANES, (j + 1) * LANES)
            for (raw_ref, d_ref, gn_ref, which) in ((q_ref, dq_ref, qg_ref, 0), (k_ref, dk_ref, kg_ref, 1)):
                xc = raw_ref[:, sl].astype(F32)
                rr = lax.rsqrt(_dot_exact(xc * xc, segm) + EPS)
                xn = xc * rr
                dc = d_ref[:, sl]
                if which == 0:
                    dqg = dqg + _colsum(dc * xn)
                else:
                    dkg = dkg + _colsum(dc * xn)
                dxn = dc * gn_ref[...]
                osl = slice(which * D + j * LANES, which * D + (j + 1) * LANES)
                dqkv_ref[:, osl] = (rr * (dxn - xn * _dot_exact(dxn * xn, segm))).astype(BF)
        dqg_ref[...] += dqg
        dkg_ref[...] += dkg
        dqkv_ref[:, 2 * D:3 * D] = dv_ref[...].astype(BF)
        z = f_ref[...] + bf_ref[...]
        sneg_t = _sigmoid(-z).T[0:N_HEADS, :]
        dfc = dfc_ref[...] + dfq_ref[...]
        carry = carry_ref[:, 0:1]
        dlf = _dot_exact(dfc, tri_ref[...]) + carry
        carry_ref[...] = jnp.broadcast_to(carry + jnp.sum(dfc, axis=1, keepdims=True), carry_ref.shape)
        dzt = dlf * sneg_t
        dz = jnp.concatenate([dzt, jnp.zeros((LANES - N_HEADS, tm), F32)], axis=0).T
        dbf_ref[...] += _colsum(dz)
        dfo_ref[...] = dz.astype(BF)

    rs = lambda w, cb=0: _rspec(tm, w, cb, n)
    ins = [(dq, rs(D)), (dk, rs(D)), (dv, rs(D)), (proj, rs(D, 0)), (proj, rs(D, 1)), (f, rs(LANES)),
           (dfc, pl.BlockSpec((N_HEADS, tm), lambda i: (0, n - 1 - i))),
           (dfq, pl.BlockSpec((N_HEADS, tm), lambda i: (0, n - 1 - i))),
           (qg2, _full(qg2)), (kg2, _full(kg2)), (bf_pad, _full(bf_pad)), (seg, _full(seg)), (tri, _full(tri))]
    outs = [_row_out(T, tm, 3 * D, BF, n), _row_out(T, tm, LANES, BF, n),
            _acc_out((1, LANES)), _acc_out((1, LANES)), _acc_out((1, LANES))]
    return _rows_call(body, "qkv_bwd", n, ins, outs, [pltpu.VMEM((N_HEADS, LANES), F32)])


def _norm1_bwd(dh, dhf, x, dx1, mod, n1g):
    T = x.shape[0]
    tm = TM_ROWS

    def body(dh_ref, dhf_ref, x_ref, dx1_ref, mod_ref, g_ref, dx_ref, dsh_ref, dsc_ref, dg_ref):
        i = pl.program_id(0)

        @pl.when(i == 0)
        def _():
            for r in (dsh_ref, dsc_ref, dg_ref):
                r[...] = jnp.zeros_like(r)

        sc1 = mod_ref[:, D:2 * D]
        g = g_ref[...]
        xv = x_ref[...]
        dh = dh_ref[...] + dhf_ref[...]
        r = lax.rsqrt(jnp.mean(xv * xv, axis=-1, keepdims=True) + EPS)
        xn = xv * r
        dsh_ref[...] += _colsum(dh)
        dsc_ref[...] += _colsum(dh * xn * g)
        dg_ref[...] += _colsum(dh * xn * (1.0 + sc1))
        dxn = dh * g * (1.0 + sc1)
        dx_ref[...] = dx1_ref[...] + r * (dxn - xn * jnp.mean(dxn * xn, axis=-1, keepdims=True))

    ins = [(dh, _rspec(tm, D)), (dhf, _rspec(tm, D)), (x, _rspec(tm, D)), (dx1, _rspec(tm, D)),
           (mod, _full(mod)), (n1g, _full(n1g))]
    outs = [_row_out(T, tm, D, F32)] + [_acc_out((1, D)) for _ in range(3)]
    return _rows_call(body, "norm1_bwd", T // tm, ins, outs)


def _adamw_math(w, g, m, v):
    m = ADAM_B1 * m + (1.0 - ADAM_B1) * g
    v = ADAM_B2 * v + (1.0 - ADAM_B2) * (g * g)
    m_hat = m / (1.0 - ADAM_B1 ** ADAM_STEP)
    v_hat = v / (1.0 - ADAM_B2 ** ADAM_STEP)
    delta = -ADAM_LR * (m_hat / (jnp.sqrt(v_hat) + ADAM_EPS) + ADAM_WD * w)
    return delta, m, v


def _adamw(parts, w, m, v, name):
    n, R, C = parts.shape
    tr = R if R <= 256 else 256
    assert R % tr == 0

    def body(p_ref, w_ref, m_ref, v_ref, g_ref, d_ref, mo_ref, vo_ref):
        g = p_ref[0].astype(F32)
        for s in range(1, n):
            g = g + p_ref[s].astype(F32)
        g_ref[...] = g
        d_ref[...], mo_ref[...], vo_ref[...] = _adamw_math(w_ref[...], g, m_ref[...], v_ref[...])

    spec = pl.BlockSpec((None, tr, C), lambda i: (0, i, 0))
    return pl.pallas_call(
        body, name=name, grid=(R // tr,),
        in_specs=[pl.BlockSpec((n, tr, C), lambda i: (0, i, 0)), spec, spec, spec],
        out_specs=[spec] * 4, out_shape=[_sds((1, R, C), F32)] * 4,
        compiler_params=_cp(("parallel",)),
    )(parts, w, m, v)


def _rcopy(src, dst, ssem, rsem, peer):
    return pltpu.make_async_remote_copy(src_ref=src, dst_ref=dst, send_sem=ssem, recv_sem=rsem,
                                        device_id=peer, device_id_type=MESH)


def _ada_fwd(c, w_ada, b_slice, cw_shard):
    def body(c_ref, w_ref, b_ref, cw_ref, mod_ref, ca_ref, cwf_ref, call, mp, ssem, rsem):
        x, y, cc, me = _my_pos()
        call[pl.ds(me, 1), :] = c_ref[...]
        cwf_ref[me] = cw_ref[...]
        first = []
        for d in range(1, N_DEV):
            peer, _ = _peer(x, y, cc, d)
            first.append(_rcopy(c_ref, call.at[pl.ds(me, 1), :], ssem.at[0, d - 1], rsem.at[0, d - 1], peer))
            first.append(_rcopy(cw_ref, cwf_ref.at[me], ssem.at[1, d - 1], rsem.at[1, d - 1], peer))
        for cp in first:
            cp.start()
        for d in range(1, N_DEV):
            peer, pid = _peer(x, y, cc, d)
            _rcopy(c_ref, call.at[pl.ds(pid, 1), :], ssem.at[0, d - 1], rsem.at[0, d - 1], peer).wait_recv()
            _rcopy(cw_ref, cwf_ref.at[pid], ssem.at[1, d - 1], rsem.at[1, d - 1], peer).wait_recv()
        cv = call[...]
        ca = cv * _sigmoid(cv)
        ca_ref[...] = ca
        mp[...] = _dot_f32(ca, w_ref[...]) + b_ref[...]
        mod_ref[pl.ds(me, 1), :] = mp[pl.ds(me, 1), :]
        second = []
        for d in range(1, N_DEV):
            peer, pid = _peer(x, y, cc, d)
            second.append(_rcopy(mp.at[pl.ds(pid, 1), :], mod_ref.at[pl.ds(me, 1), :], ssem.at[2, d - 1], rsem.at[2, d - 1], peer))
        for cp in second:
            cp.start()
        for d in range(1, N_DEV):
            peer, pid = _peer(x, y, cc, d)
            _rcopy(mp.at[pl.ds(pid, 1), :], mod_ref.at[pl.ds(pid, 1), :], ssem.at[2, d - 1], rsem.at[2, d - 1], peer).wait_recv()
        for cp in first + second:
            cp.wait_send()

    vm = pl.BlockSpec(memory_space=pltpu.VMEM)
    return pl.pallas_call(
        body, name="ada_fwd",
        in_specs=[vm, vm, vm, vm], out_specs=[vm, vm, vm],
        out_shape=[_sds((N_DEV, ADA_SHARD), F32), _sds((N_DEV, D), F32), _sds((N_DEV, CONV_KP, LANES), F32)],
        scratch_shapes=[pltpu.VMEM((N_DEV, D), F32), pltpu.VMEM((N_DEV, ADA_SHARD), F32),
                        pltpu.SemaphoreType.DMA((3, N_DEV - 1)), pltpu.SemaphoreType.DMA((3, N_DEV - 1))],
        compiler_params=pltpu.CompilerParams(vmem_limit_bytes=VMEM_LIMIT),
    )(c, w_ada, b_slice, cw_shard)


def _xchg_parts(arrays):
    n = len(arrays)
    anyspec = pl.BlockSpec(memory_space=pl.ANY)
    slots = lambda a: a.shape[0] if (a.ndim == 3 and a.shape[0] == N_DEV // 2) else N_DEV
    shapes = [_sds((slots(a),) + tuple(a.shape[-2:]), a.dtype) for a in arrays]
    scratch = [pltpu.SemaphoreType.DMA((n,)), pltpu.SemaphoreType.DMA((n, N_DEV - 1)),
               pltpu.SemaphoreType.DMA((n, N_DEV - 1))]
    return [anyspec] * n, shapes, scratch


def _xchg(ins, outs, sems, gather, wait):
    lsem, ssem, rsem = sems
    x, y, cc, me = _my_pos()
    for a in range(len(ins)):
        if not gather and ins[a].shape[0] == N_DEV // 2:
            chip = 2 * x + y
            local = pltpu.make_async_copy(ins[a].at[chip], outs[a].at[chip], lsem.at[a])
            if not wait:
                local.start()
            for d in range(1, N_DEV // 2):
                px, py = ((1 - x) if d & 2 else x), ((1 - y) if d & 1 else y)
                pchip = 2 * px + py
                if not wait:
                    _rcopy(ins[a].at[pchip], outs[a].at[chip], ssem.at[a, d - 1], rsem.at[a, d - 1], (px, py, cc)).start()
                else:
                    cp = _rcopy(ins[a].at[pchip], outs[a].at[pchip], ssem.at[a, d - 1], rsem.at[a, d - 1], (px, py, cc))
                    cp.wait_recv()
                    cp.wait_send()
            if wait:
                local.wait()
            continue
        local = pltpu.make_async_copy(ins[a] if gather else ins[a].at[me], outs[a].at[me], lsem.at[a])
        if not wait:
            local.start()
        for d in range(1, N_DEV):
            peer, pid = _peer(x, y, cc, d)
            src = ins[a] if gather else ins[a].at[pid]
            if not wait:
                _rcopy(src, outs[a].at[me], ssem.at[a, d - 1], rsem.at[a, d - 1], peer).start()
            else:
                cp = _rcopy(src, outs[a].at[pid], ssem.at[a, d - 1], rsem.at[a, d - 1], peer)
                cp.wait_recv()
                cp.wait_send()
        if wait:
            local.wait()


def _pair_reduce(part):
    n, R, C = part.shape
    half = n // 2
    tr = R if R <= 256 else 256
    assert n == N_DEV and R % tr == 0

    def swap(p_ref, got_ref, ssem, rsem):
        x, y, c, _ = _my_pos()
        cps = [_rcopy(p_ref.at[2 * i + (1 - c)], got_ref.at[i], ssem.at[i], rsem.at[i], (x, y, 1 - c)) for i in range(half)]
        for cp in cps:
            cp.start()
        for cp in cps:
            cp.wait_recv()
            cp.wait_send()

    anyspec = pl.BlockSpec(memory_space=pl.ANY)
    got = pl.pallas_call(
        swap, name="dwin_pair_swap", in_specs=[anyspec], out_specs=anyspec, out_shape=_sds((half, R, C), part.dtype),
        scratch_shapes=[pltpu.SemaphoreType.DMA((half,)), pltpu.SemaphoreType.DMA((half,))],
    )(part)

    def add(p_ref, g_ref, o_ref):
        c = lax.axis_index("c")
        o_ref[...] = (p_ref[c].astype(F32) + g_ref[...].astype(F32)).astype(o_ref.dtype)

    spec = pl.BlockSpec((None, tr, C), lambda i, r: (i, r, 0))
    return pl.pallas_call(
        add, name="dwin_pair_add", grid=(half, R // tr),
        in_specs=[pl.BlockSpec((2, tr, C), lambda i, r: (i, r, 0)), spec],
        out_specs=spec, out_shape=_sds((half, R, C), part.dtype),
        compiler_params=_cp(("parallel", "parallel")),
    )(part, got)


def _gather_two_level(shard, name):
    def body(x_ref, out_ref, ssem, rsem, lsem):
        x, y, c, me = _my_pos()
        sibling = (x, y, 1 - c)
        chips = [(1 - x, y), (x, 1 - y), (1 - x, 1 - y)]
        slot = lambda px, py, pc: out_ref.at[4 * px + 2 * py + pc]

        def copy(kk, block, to, src=None):
            return _rcopy(slot(*block) if src is None else src, slot(*block), ssem.at[kk], rsem.at[kk], to)

        mine = pltpu.make_async_copy(x_ref, slot(x, y, c), lsem)
        mine.start()
        first = [copy(0, (x, y, c), sibling, src=x_ref)]
        first += [copy(1 + j, (x, y, c), (*chip, c), src=x_ref) for j, chip in enumerate(chips)]
        for cp in first:
            cp.start()
        passed = [copy(4 + j, (*chip, c), sibling) for j, chip in enumerate(chips)]
        for j, chip in enumerate(chips):
            copy(1 + j, (*chip, c), (x, y, c)).wait_recv()
            passed[j].start()
        copy(0, sibling, (x, y, c)).wait_recv()
        for j, chip in enumerate(chips):
            copy(4 + j, (*chip, 1 - c), (x, y, c)).wait_recv()
        for cp in first + passed:
            cp.wait_send()
        mine.wait()

    anyspec = pl.BlockSpec(memory_space=pl.ANY)
    return pl.pallas_call(
        body, name=name, in_specs=[anyspec], out_specs=anyspec,
        out_shape=_sds((N_DEV,) + tuple(shard.shape), shard.dtype),
        scratch_shapes=[pltpu.SemaphoreType.DMA((N_DEV - 1,)), pltpu.SemaphoreType.DMA((N_DEV - 1,)),
                        pltpu.SemaphoreType.DMA(())],
    )(shard)


PACK_ROWS = 16
ROW_MISC = 5
ROW_LOSS = 6
ROW_DMOD = 8


def _small_bwd(pack, dmodb, dcw, cat, wp, mp_, vp, cw_w, cw_m, cw_v):
    def body(pack_ref, dmodb_ref, dcw_ref, cat_ref, wp_ref, mp_ref, vp_ref, cww_ref, cwm_ref, cwv_ref,
             g_ref, d_ref, mo_ref, vo_ref, cg_ref, cd_ref, cm_ref, cv_ref, gwa_ref, loss_ref,
             allp, dmc, cwg, ssem, rsem):
        x, y, cc, me = _my_pos()
        allp[me] = pack_ref[...]
        dmc[pl.ds(me, 1), :] = dmodb_ref[pl.ds(me, 1), :]
        cwg[me] = dcw_ref[me]
        sends = []
        for d in range(1, N_DEV):
            peer, pid = _peer(x, y, cc, d)
            sends.append(_rcopy(pack_ref, allp.at[me], ssem.at[0, d - 1], rsem.at[0, d - 1], peer))
            sends.append(_rcopy(dmodb_ref.at[pl.ds(pid, 1), :], dmc.at[pl.ds(me, 1), :], ssem.at[1, d - 1], rsem.at[1, d - 1], peer))
            sends.append(_rcopy(dcw_ref.at[pid], cwg.at[me], ssem.at[2, d - 1], rsem.at[2, d - 1], peer))
        for cp in sends:
            cp.start()
        for d in range(1, N_DEV):
            peer, pid = _peer(x, y, cc, d)
            _rcopy(pack_ref, allp.at[pid], ssem.at[0, d - 1], rsem.at[0, d - 1], peer).wait_recv()
            _rcopy(dmodb_ref.at[pl.ds(pid, 1), :], dmc.at[pl.ds(pid, 1), :], ssem.at[1, d - 1], rsem.at[1, d - 1], peer).wait_recv()
            _rcopy(dcw_ref.at[pid], cwg.at[pid], ssem.at[2, d - 1], rsem.at[2, d - 1], peer).wait_recv()
        for cp in sends:
            cp.wait_send()

        tot = allp[0]
        cg = cwg[0]
        for s in range(1, N_DEV):
            tot = tot + allp[s]
            cg = cg + cwg[s]
        lane = lax.broadcasted_iota(jnp.int32, (PACK_ROWS, D), 1)
        row = lax.broadcasted_iota(jnp.int32, (PACK_ROWS, D), 0)
        gains = (row == ROW_MISC) & (lane >= LANES) & (lane < 3 * LANES)
        folded = tot + pltpu.roll(tot, D - HEAD_DIM, axis=1)
        keep = (lane % LANES) < HEAD_DIM
        g = jnp.where(gains, jnp.where(keep, folded, 0.0), tot)
        loss_ref[...] = jnp.broadcast_to(
            (0.5 / D) * jnp.sum(jnp.where(row == ROW_LOSS, tot, 0.0), keepdims=True).reshape(1, 1), loss_ref.shape)
        g = jnp.where(row == ROW_LOSS, 0.0, g)
        g_ref[...] = g
        d_ref[...], mo_ref[...], vo_ref[...] = _adamw_math(wp_ref[...], g, mp_ref[...], vp_ref[...])
        cg_ref[...] = cg
        cd_ref[...], cm_ref[...], cv_ref[...] = _adamw_math(cww_ref[...], cg, cwm_ref[...], cwv_ref[...])
        dm_pad = jnp.concatenate([dmc[...], jnp.zeros((LANES - N_DEV, ADA_SHARD), F32)], axis=0)
        gwa_ref[...] = _dot_f32(cat_ref[...], dm_pad)

    vm = pl.BlockSpec(memory_space=pltpu.VMEM)
    p16 = _sds((PACK_ROWS, D), F32)
    c32 = _sds((CONV_KP, LANES), F32)
    return pl.pallas_call(
        body, name="small_bwd",
        in_specs=[vm] * 10, out_specs=[vm] * 10,
        out_shape=[p16, p16, p16, p16, c32, c32, c32, c32, _sds((D, ADA_SHARD), F32), _sds((8, LANES), F32)],
        scratch_shapes=[pltpu.VMEM((N_DEV, PACK_ROWS, D), F32), pltpu.VMEM((N_DEV, ADA_SHARD), F32),
                        pltpu.VMEM((N_DEV, CONV_KP, LANES), F32),
                        pltpu.SemaphoreType.DMA((3, N_DEV - 1)), pltpu.SemaphoreType.DMA((3, N_DEV - 1))],
        compiler_params=pltpu.CompilerParams(vmem_limit_bytes=VMEM_LIMIT),
    )(pack, dmodb, dcw, cat, wp, mp_, vp, cw_w, cw_m, cw_v)


def _lanes(vec, start, total=D):
    n = vec.shape[1]
    return jnp.pad(vec, ((0, 0), (start, total - start - n)))


def _pack_small(rows5, misc, loss_row, six):
    z = jnp.zeros((1, D), F32)
    return jnp.concatenate(rows5 + [misc, loss_row, z] + [six.reshape(N_ADA, D), z, z], axis=0)


def kernel(x, c, w_ada, b_ada, norm1_g, w_in, b_forget, q_norm_g, k_norm_g, w_attn_proj, conv_w, conv_b, conv_ln_g, conv_ln_b, w_conv_proj, w_out, norm2_g, w_mlp1, w_mlp2, loss_target, m_w_ada, m_b_ada, m_norm1_g, m_w_in, m_b_forget, m_q_norm_g, m_k_norm_g, m_w_attn_proj, m_conv_w, m_conv_b, m_conv_ln_g, m_conv_ln_b, m_w_conv_proj, m_w_out, m_norm2_g, m_w_mlp1, m_w_mlp2, v_w_ada, v_b_ada, v_norm1_g, v_w_in, v_b_forget, v_q_norm_g, v_k_norm_g, v_w_attn_proj, v_conv_w, v_conv_b, v_conv_ln_g, v_conv_ln_b, v_w_conv_proj, v_w_out, v_norm2_g, v_w_mlp1, v_w_mlp2):
    me = 4 * lax.axis_index("x") + 2 * lax.axis_index("y") + lax.axis_index("c")
    xs, tgt = x[0], loss_target[0]
    T = xs.shape[0]
    sq = lambda a: a[0]
    pad_taps = lambda a: jnp.pad(a[0], ((0, CONV_KP - CONV_K), (0, 0)))

    b_slice = lax.dynamic_slice(b_ada, (0, me * ADA_SHARD), (1, ADA_SHARD))
    modb, ca_all, cwf = _ada_fwd(c, sq(w_ada), b_slice, pad_taps(conv_w))
    mod = modb.reshape(1, N_ADA * D)
    cw = jnp.transpose(cwf, (1, 0, 2)).reshape(CONV_KP, D)

    g_in = _gather_two_level(sq(w_in).astype(BF), "w_in_gather")
    d_in = g_in.shape[2] * N_DEV
    w_in_f = jnp.transpose(g_in, (1, 0, 2)).reshape(D, d_in)
    w_qkv = w_in_f[:, :3 * D]
    w_gg = w_in_f[:, 3 * D + N_HEADS:]
    w_f = jnp.pad(w_in_f[:, 3 * D:3 * D + N_HEADS], ((0, 0), (0, LANES - N_HEADS)))
    shards = [sq(w_attn_proj).astype(BF), sq(w_conv_proj).astype(BF), sq(w_out).astype(BF),
              sq(w_mlp1).astype(BF), sq(w_mlp2).astype(BF)]

    qg2 = jnp.tile(q_norm_g, (1, 2))
    kg2 = jnp.tile(k_norm_g, (1, 2))
    bf_pad = _lanes(b_forget, 0, LANES)

    h = _pre_in(xs, mod, norm1_g)
    pqkv = _matmul(h, w_qkv, "nn", BF, "mm_proj_qkv")
    pgg = _matmul(h, w_gg, "nn", BF, "mm_proj_gg")
    f = _matmul(h, w_f, "nn", F32, "mm_f")
    q, k, v, fc = _qkv_post(pqkv, f, qg2, kg2, bf_pad)
    fc3 = fc.reshape(N_HEADS // 2, 2, T)
    o, lse, g_ap, g_cp, g_out, g_1, g_2 = _flash_fwd(q, k, v, fc3, shards)
    w_ap, w_cp, w_o = g_ap.reshape(D, D), g_cp.reshape(D, D), g_out.reshape(D, D)
    w_2 = g_2.reshape(D_FF, D)
    ba = _matmul(o, w_ap, "nn", BF, "mm_ba")
    u0, u1, u3 = _conv_fwd(pgg, cw, conv_b, conv_ln_g, conv_ln_b)
    bb = _matmul(u3, w_cp, "nn", BF, "mm_bb")
    merged = _merge(ba, bb, pgg)
    mo = _matmul(merged, w_o, "nn", F32, "mm_out")
    x1, h2 = _post_out(xs, mo, mod, norm2_g)
    a, rl = _matmul(h2, g_1, "nn", BF, "mm_mlp1", relu2=True, b_slots=True)
    m2 = _matmul(rl, w_2, "nn", F32, "mm_mlp2")
    dy, dm2, dg2, sqcols = _loss_head(x1, m2, tgt, mod)

    da = _matmul(dm2, w_2, "nt", BF, "mm_drl", relu_of=a)
    dw_2 = _matmul(rl, dm2, "tn", BF, "mm_dw2")
    dh2 = _matmul(da, g_1, "nt", F32, "mm_dh2", b_slots=True)
    dw_1 = _matmul(h2, da, "tn", BF, "mm_dw1", out_slots=True)
    dx1, dmo, dsh2, dsc2, dn2g, dg1 = _norm2_bwd(dh2, x1, dy, mo, mod, norm2_g)
    dmerged = _matmul(dmo, w_o, "nt", BF, "mm_dmerged")
    dw_o = _matmul(merged, dmo, "tn", BF, "mm_dwout")
    dba, dbb, dgg = _gate_bwd(dmerged, ba, bb, pgg)
    du3 = _matmul(dbb, w_cp, "nt", F32, "mm_du3")
    dw_cp = _matmul(u3, dbb, "tn", BF, "mm_dwcp")
    do = _matmul(dba, w_ap, "nt", BF, "mm_do")
    dw_ap = _matmul(o, dba, "tn", BF, "mm_dwap")
    dgg, dlng, dlnb, dcb, dcw_full = _conv_bwd(du3, u1, u0, pgg, cw, conv_ln_g, conv_ln_b, dgg)
    delta = _attn_delta(do, o)
    dw_gg = _matmul(h, dgg, "tn", BF, "mm_dw_gg")
    parts = [dw_ap.reshape(N_DEV, D // N_DEV, D), dw_cp.reshape(N_DEV, D // N_DEV, D), dw_o.reshape(N_DEV, D // N_DEV, D),
             dw_1, dw_2.reshape(N_DEV, D_FF // N_DEV, D)]
    dq, rs_a, rs_b, dk, dv, dfc3, r_ap, r_cp, r_out, r_1, r_2 = _flash_bwd(q, k, v, do, lse, delta, fc3, parts)
    dfq = jnp.stack([rs_a, rs_b], axis=1).reshape(N_HEADS, T)
    dqkv, df, dqg, dkg, dbf = _qkv_bwd(dq, dk, dv, pqkv, f, dfc3.reshape(N_HEADS, T), dfq, qg2, kg2, bf_pad)
    dw_qkv = _matmul(h, dqkv, "tn", BF, "mm_dw_qkv")
    dw_f = _matmul(h, df, "tn", BF, "mm_dwf")
    dw_in_f = jnp.concatenate([dw_qkv, dw_f[:, :N_HEADS], dw_gg], axis=1)
    part_in = jnp.transpose(dw_in_f.reshape(D, N_DEV, d_in // N_DEV), (1, 0, 2))
    dh, r_in = _matmul(dqkv, w_qkv, "nt", F32, "mm_dh", scatter=(_pair_reduce(part_in),), more=((dgg, w_gg),))
    dhf = _matmul(df, w_f, "nt", F32, "mm_dhf")
    grad_x, dsh1, dsc1, dn1g = _norm1_bwd(dh, dhf, xs, dx1, mod, norm1_g)

    dmod = jnp.concatenate([dsh1, dsc1, dg1, dsh2, dsc2, dg2], axis=1)
    misc = jnp.concatenate([dbf, dqg, dkg, jnp.zeros((1, D - 3 * LANES), F32)], axis=1)
    pack = _pack_small([dn1g, dcb, dlng, dlnb, dn2g], misc, sqcols, dmod)
    dcw_blocks = jnp.transpose(dcw_full.reshape(CONV_KP, N_DEV, LANES), (1, 0, 2))

    def small_params(b_a, n1, bfg, qn, kn, cvb, lg, lb, n2):
        misc_p = jnp.concatenate([_lanes(bfg, 0, LANES), _lanes(qn, 0, LANES), _lanes(kn, 0, LANES),
                                  jnp.zeros((1, D - 3 * LANES), F32)], axis=1)
        return _pack_small([n1, cvb, lg, lb, n2], misc_p, jnp.zeros((1, D), F32), b_a)

    wp = small_params(b_ada, norm1_g, b_forget, q_norm_g, k_norm_g, conv_b, conv_ln_g, conv_ln_b, norm2_g)
    mp_ = small_params(m_b_ada, m_norm1_g, m_b_forget, m_q_norm_g, m_k_norm_g, m_conv_b, m_conv_ln_g, m_conv_ln_b, m_norm2_g)
    vp = small_params(v_b_ada, v_norm1_g, v_b_forget, v_q_norm_g, v_k_norm_g, v_conv_b, v_conv_ln_g, v_conv_ln_b, v_norm2_g)
    cat = jnp.pad(jnp.transpose(ca_all), ((0, 0), (0, LANES - N_DEV)))
    small = _small_bwd(pack, dmod.reshape(N_DEV, ADA_SHARD), dcw_blocks, cat,
                       wp, mp_, vp, pad_taps(conv_w), pad_taps(m_conv_w), pad_taps(v_conv_w))
    sp = small[0:4]
    scw = small[4:8]
    gw_ada, loss_t = small[8], small[9]
    loss = loss_t[0, 0]

    def unpack(p):
        misc_r = p[ROW_MISC:ROW_MISC + 1]
        return dict(
            b_ada=p[ROW_DMOD:ROW_DMOD + N_ADA].reshape(1, N_ADA * D), norm1_g=p[0:1], conv_b=p[1:2], conv_ln_g=p[2:3],
            conv_ln_b=p[3:4], norm2_g=p[4:5], b_forget=misc_r[:, 0:N_HEADS],
            q_norm_g=misc_r[:, LANES:LANES + HEAD_DIM], k_norm_g=misc_r[:, 2 * LANES:2 * LANES + HEAD_DIM])

    res = {}
    res["w_ada"] = _adamw(gw_ada[None], w_ada, m_w_ada, v_w_ada, "adamw_w_ada")
    res["w_in"] = _adamw(r_in, w_in, m_w_in, v_w_in, "adamw_w_in")
    res["w_attn_proj"] = _adamw(r_ap, w_attn_proj, m_w_attn_proj, v_w_attn_proj, "adamw_w_ap")
    res["w_conv_proj"] = _adamw(r_cp, w_conv_proj, m_w_conv_proj, v_w_conv_proj, "adamw_w_cp")
    res["w_out"] = _adamw(r_out, w_out, m_w_out, v_w_out, "adamw_w_out")
    res["w_mlp1"] = _adamw(r_1, w_mlp1, m_w_mlp1, v_w_mlp1, "adamw_w_mlp1")
    res["w_mlp2"] = _adamw(r_2, w_mlp2, m_w_mlp2, v_w_mlp2, "adamw_w_mlp2")

    names = ["w_ada", "b_ada", "norm1_g", "w_in", "b_forget", "q_norm_g", "k_norm_g", "w_attn_proj", "conv_w", "conv_b",
             "conv_ln_g", "conv_ln_b", "w_conv_proj", "w_out", "norm2_g", "w_mlp1", "w_mlp2"]
    outs = [loss, grad_x[None]]
    for kind in range(4):
        small_d = unpack(sp[kind])
        for nm in names:
            if nm in res:
                outs.append(res[nm][kind])
            elif nm == "conv_w":
                outs.append(scw[kind][:CONV_K][None])
            else:
                outs.append(small_d[nm])
    return tuple(outs)
```

```python
import functools

import jax
import jax.numpy as jnp
from jax import lax
from jax.experimental import pallas as pl
from jax.experimental.pallas import tpu as pltpu

F32 = jnp.float32
BF = jnp.bfloat16

N_DEV = 8
D = 1024
N_HEADS = 16
HEAD_DIM = 64
LANES = 128
SUBLANES = 8
CONV_K = 31
CONV_KP = 32
HALO = 32
CONV_ROWS = 64
D_FF = 4 * D
N_ADA = 6
ADA_SHARD = N_ADA * D // N_DEV
EPS = 1e-6
QK_SCALE = HEAD_DIM ** -0.5
LOG2E = 1.4426950408889634
LN2 = 0.6931471805599453
NEG = -1e30

ADAM_LR = 0.001
ADAM_B1 = 0.9
ADAM_B2 = 0.999
ADAM_EPS = 1e-08
ADAM_WD = 0.01
ADAM_STEP = 10

VMEM_LIMIT = 56 * 1024 * 1024
TM_ROWS = 512
CONV_TM = 256
TQ = 512

MESH = pl.DeviceIdType.MESH


def _cp(sem=None):
    return pltpu.CompilerParams(dimension_semantics=sem, vmem_limit_bytes=VMEM_LIMIT)


def _sds(shape, dtype):
    return jax.ShapeDtypeStruct(tuple(shape), dtype)


def _full(arr):
    nd = arr.ndim
    return pl.BlockSpec(arr.shape, lambda *_: (0,) * nd)


def _fullshape(shape):
    nd = len(shape)
    return pl.BlockSpec(tuple(shape), lambda *_: (0,) * nd)


def _split3(x):
    hi = x.astype(BF)
    r1 = x - hi.astype(F32)
    mid = r1.astype(BF)
    lo = (r1 - mid.astype(F32)).astype(BF)
    return hi, mid, lo


def _dot_exact(x, mat):
    hi, mid, lo = _split3(x)
    d = lambda t: jnp.dot(t, mat, preferred_element_type=F32)
    return d(hi) + d(mid) + d(lo)


def _dot_f32(a, b):
    a1, a2, a3 = _split3(a)
    b1, b2, b3 = _split3(b)
    d = lambda s, t: jnp.dot(s, t, preferred_element_type=F32)
    return (d(a1, b3) + d(a3, b1) + d(a2, b2)) + (d(a1, b2) + d(a2, b1)) + d(a1, b1)


def _sigmoid(x):
    return 1.0 / (1.0 + jnp.exp(-x))


def _colsum(x):
    return jnp.sum(x, axis=0, keepdims=True)


def _my_pos():
    x, y, c = lax.axis_index("x"), lax.axis_index("y"), lax.axis_index("c")
    return x, y, c, 4 * x + 2 * y + c


def _peer(x, y, c, d):
    px = (1 - x) if d & 4 else x
    py = (1 - y) if d & 2 else y
    pc = (1 - c) if d & 1 else c
    return (px, py, pc), 4 * px + 2 * py + pc


def _matmul(a, b, form, out_dtype, name, tm=1024, tn=1024, tk=2048, scatter=(), relu2=False, relu_of=None,
            b_slots=False, out_slots=False, more=()):
    width = None
    if b_slots:
        assert form in ("nn", "nt") and b.shape[0] == N_DEV
        width = b.shape[2]
        if form == "nn":
            (M, K), N, tn = a.shape, N_DEV * width, 2 * width
        else:
            (M, K), N, tk = a.shape, b.shape[1], 2 * width
    elif form == "nn":
        (M, K), N = a.shape, b.shape[1]
    elif form == "nt":
        (M, K), N = a.shape, b.shape[0]
    else:
        (K, M), N = a.shape, b.shape[1]
    if out_slots:
        width = N // N_DEV
        tn = 2 * width
    tm, tn, tk = min(tm, M), min(tn, N), min(tk, K)
    assert M % tm == 0 and N % tn == 0 and K % tk == 0, (name, M, N, K)
    nk = K // tk
    if form == "tn":
        a_spec = pl.BlockSpec((tk, tm), lambda i, j, k: (k, i))
        dn = (((0,), (0,)), ((), ()))
    else:
        a_spec = pl.BlockSpec((tm, tk), lambda i, j, k: (i, k))
        dn = (((1,), (1 if form == "nt" else 0,)), ((), ()))
    if b_slots and form == "nn":
        b_spec = pl.BlockSpec((2, tk, width), lambda i, j, k: (j, k, 0))
    elif b_slots:
        b_spec = pl.BlockSpec((2, tn, width), lambda i, j, k: (k, j, 0))
    elif form == "nt":
        b_spec = pl.BlockSpec((tn, tk), lambda i, j, k: (j, k))
    else:
        b_spec = pl.BlockSpec((tk, tn), lambda i, j, k: (k, j))

    pairs = [(a, b)] + list(more)
    seg = [0]
    for a_s, _ in pairs:
        k_s = K if len(pairs) == 1 else a_s.shape[1]
        assert k_s % tk == 0 and (len(pairs) == 1 or (form == "nt" and not b_slots))
        seg.append(seg[-1] + k_s // tk)
    nk = seg[-1]
    specs_more = []
    for s in range(1, len(pairs)):
        lo_k, n_k = seg[s], seg[s + 1] - seg[s]
        kk = lambda k, lo_k=lo_k, n_k=n_k: jnp.clip(k - lo_k, 0, n_k - 1)
        specs_more += [pl.BlockSpec((tm, tk), lambda i, j, k, kk=kk: (i, kk(k))),
                       pl.BlockSpec((tn, tk), lambda i, j, k, kk=kk: (j, kk(k)))]
    if len(pairs) > 1:
        n0 = seg[1]
        a_spec = pl.BlockSpec((tm, tk), lambda i, j, k: (i, jnp.minimum(k, n0 - 1)))
        b_spec = pl.BlockSpec((tn, tk), lambda i, j, k: (j, jnp.minimum(k, n0 - 1)))

    nx = len(scatter)
    ne = 0 if relu_of is None else 1
    no = 2 if relu2 else 1
    nm = 2 * (len(pairs) - 1)
    grid = (M // tm, N // tn, nk)

    def body(a_ref, b_ref, *rest):
        ab_refs = [(a_ref, b_ref)] + [(rest[2 * s], rest[2 * s + 1]) for s in range(len(pairs) - 1)]
        rest = rest[nm:]
        e_ref = rest[0] if ne else None
        x_in = rest[ne:ne + nx]
        o_refs = rest[ne + nx:ne + nx + no]
        x_out = rest[ne + nx + no:ne + 2 * nx + no]
        scr = rest[ne + 2 * nx + no:]
        k = pl.program_id(2)
        if nx:
            first, last = _first_last(grid)

            @pl.when(first)
            def _():
                _xchg(x_in, x_out, scr[-3:], False, wait=False)

        def finish(val):
            if out_slots:
                o_refs[0][0] = val[:, 0:width].astype(out_dtype)
                o_refs[0][1] = val[:, width:2 * width].astype(out_dtype)
            elif relu2:
                o_refs[0][...] = val.astype(out_dtype)
                r = jnp.maximum(val, 0.0)
                o_refs[1][...] = (r * r).astype(out_dtype)
            elif ne:
                o_refs[0][...] = (val * (2.0 * jnp.maximum(e_ref[...].astype(F32), 0.0))).astype(out_dtype)
            else:
                o_refs[0][...] = val.astype(out_dtype)

        def accumulate(ar, br):
            dot = lambda u, w: lax.dot_general(u.astype(BF), w.astype(BF), dn, preferred_element_type=F32)
            if b_slots and form == "nn":
                part = jnp.concatenate([dot(ar[...], br[0]), dot(ar[...], br[1])], axis=1)
            elif b_slots:
                part = dot(ar[:, 0:width], br[0]) + dot(ar[:, width:2 * width], br[1])
            else:
                part = dot(ar[...], br[...])
            if nk == 1:
                finish(part)
            else:
                acc = scr[0]

                @pl.when(k == 0)
                def _():
                    acc[...] = part

                @pl.when(k > 0)
                def _():
                    acc[...] += part

        if len(pairs) == 1:
            accumulate(a_ref, b_ref)
        else:
            for s, (ar, br) in enumerate(ab_refs):
                @pl.when((k >= seg[s]) & (k < seg[s + 1]))
                def _(ar=ar, br=br):
                    accumulate(ar, br)

        if nk > 1:
            @pl.when(k == nk - 1)
            def _():
                finish(scr[0][...])

        if nx:
            @pl.when(last)
            def _():
                _xchg(x_in, x_out, scr[-3:], False, wait=True)

    x_specs, x_shapes, x_scratch = _xchg_parts(scatter) if nx else ([], [], [])
    sem = ("arbitrary",) * 3 if nx else ("parallel", "parallel", "arbitrary")
    o_spec = pl.BlockSpec((tm, tn), lambda i, j, k: (i, j))
    o_shape = _sds((M, N), out_dtype)
    if out_slots:
        o_spec = pl.BlockSpec((2, tm, width), lambda i, j, k: (j, i, 0))
        o_shape = _sds((N_DEV, M, width), out_dtype)
    res = pl.pallas_call(
        body, name=name, grid=grid,
        in_specs=[a_spec, b_spec] + specs_more + [o_spec] * ne + x_specs,
        out_specs=[o_spec] * no + x_specs,
        out_shape=[o_shape] * no + x_shapes,
        scratch_shapes=([] if nk == 1 else [pltpu.VMEM((tm, tn), F32)]) + x_scratch,
        compiler_params=_cp(sem),
    )(a, b, *[t for p in more for t in p], *([relu_of] if ne else []), *scatter)
    return res if (nx or relu2) else res[0]


def _rows_call(body, name, n_tiles, ins, outs, scratch=(), aliases=None):
    res = pl.pallas_call(
        body, name=name, grid=(n_tiles,),
        in_specs=[s for _, s in ins],
        out_specs=[s for _, s in outs],
        out_shape=[o for o, _ in outs],
        scratch_shapes=list(scratch),
        input_output_aliases=aliases or {},
        compiler_params=_cp(("arbitrary",)),
    )(*[a for a, _ in ins])
    return res


def _rspec(tm, width, cb=0, rev_n=None):
    if rev_n is None:
        return pl.BlockSpec((tm, width), lambda i: (i, cb))
    return pl.BlockSpec((tm, width), lambda i: (rev_n - 1 - i, cb))


def _row_out(T, tm, width, dtype, rev_n=None):
    return (_sds((T, width), dtype), _rspec(tm, width, 0, rev_n))


def _acc_out(shape, dtype=F32):
    return (_sds(shape, dtype), _fullshape(shape))


def _mod_parts(mod):
    return [mod[:, i * D:(i + 1) * D] for i in range(N_ADA)]


def _pre_in(x, mod, n1g):
    T = x.shape[0]
    tm = TM_ROWS

    def body(x_ref, mod_ref, g_ref, h_ref):
        sh1, sc1 = mod_ref[:, 0:D], mod_ref[:, D:2 * D]
        xv = x_ref[...]
        r = lax.rsqrt(jnp.mean(xv * xv, axis=-1, keepdims=True) + EPS)
        h_ref[...] = ((xv * r) * g_ref[...] * (1.0 + sc1) + sh1).astype(BF)

    return _rows_call(body, "pre_in", T // tm,
                      [(x, _rspec(tm, D)), (mod, _full(mod)), (n1g, _full(n1g))],
                      [_row_out(T, tm, D, BF)])[0]


def _seg_mat():
    r = jnp.arange(LANES)[:, None] // HEAD_DIM
    c = jnp.arange(LANES)[None, :] // HEAD_DIM
    return jnp.where(r == c, 1.0 / HEAD_DIM, 0.0).astype(BF)


def _tri_mat(n, upper):
    r = jnp.arange(n)[:, None]
    c = jnp.arange(n)[None, :]
    return jnp.where((r <= c) if upper else (r >= c), 1.0, 0.0).astype(BF)


def _log_sigmoid(z):
    return jnp.minimum(z, 0.0) - jnp.log(1.0 + jnp.exp(-jnp.abs(z)))


def _qkv_post(proj, f, qg2, kg2, bf_pad):
    T = proj.shape[0]
    tm = TM_ROWS
    seg = _seg_mat()
    tri = _tri_mat(tm, True)

    def body(q_ref, k_ref, v_ref, f_ref, qg_ref, kg_ref, bf_ref, seg_ref, tri_ref,
             qo_ref, ko_ref, vo_ref, fc_ref, carry_ref):
        i = pl.program_id(0)

        @pl.when(i == 0)
        def _():
            carry_ref[...] = jnp.zeros_like(carry_ref)

        segm = seg_ref[...]
        for j in range(D // LANES):
            sl = slice(j * LANES, (j + 1) * LANES)
            qc = q_ref[:, sl].astype(F32)
            rq = lax.rsqrt(_dot_exact(qc * qc, segm) + EPS)
            qo_ref[:, sl] = ((qc * rq) * qg_ref[...] * (QK_SCALE * LOG2E)).astype(BF)
            kc = k_ref[:, sl].astype(F32)
            rk = lax.rsqrt(_dot_exact(kc * kc, segm) + EPS)
            ko_ref[:, sl] = ((kc * rk) * kg_ref[...]).astype(BF)
        vo_ref[...] = v_ref[...].astype(BF)
        lf = _log_sigmoid(f_ref[...] + bf_ref[...])
        lft = lf.T[0:N_HEADS, :]
        carry = carry_ref[:, 0:1]
        fc_ref[...] = _dot_exact(lft, tri_ref[...]) + carry
        carry_ref[...] = jnp.broadcast_to(carry + jnp.sum(lft, axis=1, keepdims=True), carry_ref.shape)

    outs = [_row_out(T, tm, D, BF), _row_out(T, tm, D, BF), _row_out(T, tm, D, BF),
            (_sds((N_HEADS, T), F32), pl.BlockSpec((N_HEADS, tm), lambda i: (0, i)))]
    ins = [(proj, _rspec(tm, D, 0)), (proj, _rspec(tm, D, 1)), (proj, _rspec(tm, D, 2)), (f, _rspec(tm, LANES)),
           (qg2, _full(qg2)), (kg2, _full(kg2)), (bf_pad, _full(bf_pad)), (seg, _full(seg)), (tri, _full(tri))]
    return _rows_call(body, "qkv_post", T // tm, ins, outs, [pltpu.VMEM((N_HEADS, LANES), F32)])


def _lane_lo():
    return lax.broadcasted_iota(jnp.int32, (1, LANES), 1) < HEAD_DIM


def _nt(a, b):
    return lax.dot_general(a, b, (((1,), (1,)), ((), ())), preferred_element_type=F32)


def _tn(a, b):
    return lax.dot_general(a, b, (((0,), (0,)), ((), ())), preferred_element_type=F32)


def _head_rep(x, lo):
    rolled = pltpu.roll(x, HEAD_DIM, axis=1)
    return jnp.where(lo, x, rolled), jnp.where(lo, rolled, x)


def _diag_mask(t):
    return lax.broadcasted_iota(jnp.int32, (t, t), 1) <= lax.broadcasted_iota(jnp.int32, (t, t), 0)


def _first_last(grid):
    ids = [pl.program_id(a) for a in range(len(grid))]
    first = functools.reduce(jnp.logical_and, [i == 0 for i in ids])
    last = functools.reduce(jnp.logical_and, [i == g - 1 for i, g in zip(ids, grid)])
    return first, last


def _flash_fwd(q, k, v, fc3, shards):
    T = q.shape[0]
    tq = TQ
    nq = T // tq
    hp_n = N_HEADS // 2
    rep = tq // LANES
    nx = len(shards)
    grid = (hp_n, nq, nq)

    def body(q_ref, k_ref, v_ref, fk_ref, fq_ref, *rest):
        x_in, (o_ref, lse_ref), x_out = rest[:nx], rest[nx:nx + 2], rest[nx + 2:2 * nx + 2]
        acc_ref, m_ref = rest[2 * nx + 2:2 * nx + 4]
        sems = rest[2 * nx + 4:]
        qi, ki = pl.program_id(1), pl.program_id(2)
        first, last = _first_last(grid)

        @pl.when(first)
        def _():
            _xchg(x_in, x_out, sems, True, wait=False)

        @pl.when(ki == 0)
        def _():
            acc_ref[...] = jnp.zeros_like(acc_ref)
            m_ref[...] = jnp.full_like(m_ref, NEG)

        lane = lax.broadcasted_iota(jnp.int32, (1, LANES), 1)
        sum_lane = (HEAD_DIM, 0)

        def step(diag):
            lo = _lane_lo()
            q2, k2, v2 = q_ref[...], k_ref[...], v_ref[...]
            zero = jnp.zeros_like(k2)
            bias = (fq_ref[:, 0:1] - fk_ref[...]) * LOG2E
            for hh in range(2):
                sel = (lambda t: jnp.where(lo, t, zero)) if hh == 0 else (lambda t: jnp.where(lo, zero, t))
                ones = jnp.where(lane == sum_lane[hh], 1.0, 0.0).astype(BF)
                v_aug = jnp.where(lo, v2, ones) if hh == 0 else jnp.where(lo, ones, v2)
                s = _nt(sel(q2), k2) + bias[hh:hh + 1, :]
                if diag:
                    s = jnp.where(_diag_mask(tq), s, NEG)
                m_old = m_ref[hh]
                m_new = jnp.maximum(m_old, jnp.max(s, axis=-1, keepdims=True))
                alpha = jnp.exp2(m_old - m_new)
                p = jnp.exp2(s - jnp.tile(m_new, (1, rep)))
                m_ref[hh] = m_new
                acc_ref[hh] = acc_ref[hh] * alpha + jnp.dot(p.astype(BF), v_aug, preferred_element_type=F32)

        @pl.when(ki < qi)
        def _():
            step(False)

        @pl.when(ki == qi)
        def _():
            step(True)
            lo = _lane_lo()
            acc_a, acc_b = acc_ref[0], acc_ref[1]
            la = jnp.broadcast_to(acc_a[:, sum_lane[0]:sum_lane[0] + 1], (tq, LANES))
            lb = jnp.broadcast_to(acc_b[:, sum_lane[1]:sum_lane[1] + 1], (tq, LANES))
            o_ref[...] = jnp.where(lo, acc_a / la, acc_b / lb)
            lse_ref[...] = jnp.where(lo, m_ref[0] + jnp.log(la) * LOG2E, m_ref[1] + jnp.log(lb) * LOG2E)

        @pl.when(last)
        def _():
            _xchg(x_in, x_out, sems, True, wait=True)

    qspec = pl.BlockSpec((tq, LANES), lambda h, i, j: (i, h))
    kspec = pl.BlockSpec((tq, LANES), lambda h, i, j: (jnp.minimum(i, j), h))
    fkspec = pl.BlockSpec((None, 2, tq), lambda h, i, j: (h, 0, jnp.minimum(i, j)))
    fqspec = pl.BlockSpec((None, 2, tq), lambda h, i, j: (h, 0, i))
    x_specs, x_shapes, x_scratch = _xchg_parts(shards)
    return pl.pallas_call(
        body, name="attn_fwd", grid=grid,
        in_specs=[qspec, kspec, kspec, fkspec, fqspec] + x_specs,
        out_specs=[qspec, qspec] + x_specs,
        out_shape=[_sds((T, D), F32), _sds((T, D), F32)] + x_shapes,
        scratch_shapes=[pltpu.VMEM((2, tq, LANES), F32), pltpu.VMEM((2, tq, LANES), F32)] + x_scratch,
        compiler_params=_cp(("arbitrary", "arbitrary", "arbitrary")),
    )(q, k, v, fc3, fc3, *shards)


def _attn_delta(do, o):
    T = o.shape[0]
    tm = TM_ROWS
    ones = (_seg_mat().astype(F32) * HEAD_DIM).astype(BF)

    def body(do_ref, o_ref, seg_ref, dl_ref):
        segm = seg_ref[...]
        for j in range(D // LANES):
            sl = slice(j * LANES, (j + 1) * LANES)
            dl_ref[:, sl] = _dot_exact(do_ref[:, sl].astype(BF).astype(F32) * o_ref[:, sl], segm)

    ins = [(do, _rspec(tm, D)), (o, _rspec(tm, D)), (ones, _full(ones))]
    return _rows_call(body, "attn_delta", T // tm, ins, [_row_out(T, tm, D, F32)])[0]


def _flash_bwd(q, k, v, do, lse, delta, fc3, parts):
    T = q.shape[0]
    tq = TQ
    nq = T // tq
    hp_n = N_HEADS // 2
    rep = tq // LANES
    nx = len(parts)
    grid = (hp_n, nq, nq)

    def body(q_ref, k_ref, v_ref, do_ref, lse_ref, dl_ref, fk_ref, fq_ref, *rest):
        x_in, x_out = rest[:nx], rest[nx + 6:2 * nx + 6]
        dq_ref, ra_ref, rb_ref, dk_ref, dv_ref, dfc_ref = rest[nx:nx + 6]
        dk_acc, dv_acc, df_acc = rest[2 * nx + 6:2 * nx + 9]
        sems = rest[2 * nx + 9:]
        ki, qi = pl.program_id(1), pl.program_id(2)
        first, last = _first_last(grid)
        qrows = pl.ds(pl.multiple_of(qi * tq, tq), tq)

        @pl.when(first)
        def _():
            _xchg(x_in, x_out, sems, False, wait=False)

        @pl.when((ki == 0) & (qi == 0))
        def _():
            dq_ref[...] = jnp.zeros_like(dq_ref)
            ra_ref[...] = jnp.zeros_like(ra_ref)
            rb_ref[...] = jnp.zeros_like(rb_ref)

        @pl.when(qi == 0)
        def _():
            dk_acc[...] = jnp.zeros_like(dk_acc)
            dv_acc[...] = jnp.zeros_like(dv_acc)
            df_acc[...] = jnp.zeros_like(df_acc)

        def step(diag):
            lo = _lane_lo()
            q2, k2, v2 = q_ref[...], k_ref[...], v_ref[...]
            do2 = do_ref[...].astype(BF)
            zero = jnp.zeros_like(q2)
            bias = (fq_ref[:, 0:1] - fk_ref[...]) * LOG2E
            lses = _head_rep(lse_ref[...], lo)
            dls = _head_rep(dl_ref[...], lo)
            dk_t = None
            dv_t = None
            dq_t = None
            for hh in range(2):
                sel = (lambda t: jnp.where(lo, t, zero)) if hh == 0 else (lambda t: jnp.where(lo, zero, t))
                s = _nt(sel(q2), k2) + bias[hh:hh + 1, :]
                if diag:
                    s = jnp.where(_diag_mask(tq), s, NEG)
                p = jnp.exp2(s - jnp.tile(lses[hh], (1, rep)))
                dp = _nt(sel(do2), v2)
                ds = p * (dp - jnp.tile(dls[hh], (1, rep)))
                ds_b = ds.astype(BF)
                dvp = _tn(p.astype(BF), sel(do2))
                dkp = _tn(ds_b, sel(q2))
                dqp = jnp.dot(ds_b, sel(k2), preferred_element_type=F32)
                dv_t = dvp if dv_t is None else dv_t + dvp
                dk_t = dkp if dk_t is None else dk_t + dkp
                dq_t = dqp if dq_t is None else dq_t + dqp
                df_acc[hh:hh + 1, :] -= _colsum(ds)
                r_ref = ra_ref if hh == 0 else rb_ref
                r_ref[qrows, :] += jnp.sum(ds, axis=-1, keepdims=True)
            dk_acc[...] += dk_t
            dv_acc[...] += dv_t
            dq_ref[qrows, :] += dq_t * QK_SCALE

        @pl.when(qi > ki)
        def _():
            step(False)

        @pl.when(qi == ki)
        def _():
            step(True)

        @pl.when(qi == nq - 1)
        def _():
            dk_ref[...] = dk_acc[...] * LN2
            dv_ref[...] = dv_acc[...]
            dfc_ref[...] = df_acc[...]

        @pl.when(last)
        def _():
            _xchg(x_in, x_out, sems, False, wait=True)

    kspec = pl.BlockSpec((tq, LANES), lambda h, j, i: (j, h))
    qspec = pl.BlockSpec((tq, LANES), lambda h, j, i: (jnp.maximum(i, j), h))
    fkspec = pl.BlockSpec((None, 2, tq), lambda h, j, i: (h, 0, j))
    fqspec = pl.BlockSpec((None, 2, tq), lambda h, j, i: (h, 0, jnp.maximum(i, j)))
    x_specs, x_shapes, x_scratch = _xchg_parts(parts)
    dqspec = pl.BlockSpec((T, LANES), lambda h, j, i: (0, h))
    rspec = pl.BlockSpec((None, T, 1), lambda h, j, i: (h, 0, 0))
    return pl.pallas_call(
        body, name="attn_bwd", grid=grid,
        in_specs=[qspec, kspec, kspec, qspec, qspec, qspec, fkspec, fqspec] + x_specs,
        out_specs=[dqspec, rspec, rspec, kspec, kspec, fkspec] + x_specs,
        out_shape=[_sds((T, D), F32), _sds((hp_n, T, 1), F32), _sds((hp_n, T, 1), F32),
                   _sds((T, D), F32), _sds((T, D), F32), _sds((hp_n, 2, T), F32)] + x_shapes,
        scratch_shapes=[pltpu.VMEM((tq, LANES), F32), pltpu.VMEM((tq, LANES), F32), pltpu.VMEM((2, tq), F32)] + x_scratch,
        compiler_params=_cp(("arbitrary", "arbitrary", "arbitrary")),
    )(q, k, v, do, lse, delta, fc3, fc3, *parts)


def _layer_norm_stats(u1):
    mu = jnp.mean(u1, axis=-1, keepdims=True)
    xc = u1 - mu
    rstd = lax.rsqrt(jnp.mean(xc * xc, axis=-1, keepdims=True) + EPS)
    return xc * rstd, rstd


def _shifted_copies(buf, sh, tm):
    rows = tm + HALO - SUBLANES
    for b in range(1, SUBLANES):
        sh[b - 1, 0:rows, :] = buf[b:b + rows, :]


def _window(buf, sh, off, rows, sl):
    a8, b = off // SUBLANES * SUBLANES, off % SUBLANES
    return buf[a8:a8 + rows, sl] if b == 0 else sh[b - 1, a8:a8 + rows, sl]


def _conv_fwd(proj, cw, cb, lng, lnb):
    T = proj.shape[0]
    tm = CONV_TM

    def body(a_ref, b_ref, w_ref, cb_ref, g_ref, bb_ref, u0_ref, u1_ref, u3_ref, buf, sh):
        i = pl.program_id(0)

        @pl.when(i == 0)
        def _():
            buf[0:HALO, :] = jnp.zeros((HALO, D), F32)

        u0 = a_ref[...].astype(F32) * _sigmoid(b_ref[...].astype(F32))
        u0_ref[...] = u0
        buf[HALO:HALO + tm, :] = u0
        _shifted_copies(buf, sh, tm)
        for j in range(D // LANES):
            sl = slice(j * LANES, (j + 1) * LANES)
            for r0 in range(0, tm, CONV_ROWS):
                acc = jnp.broadcast_to(cb_ref[:, sl], (CONV_ROWS, LANES))
                for kk in range(CONV_K):
                    acc = acc + w_ref[kk:kk + 1, sl] * _window(buf, sh, r0 + HALO - (CONV_K - 1) + kk, CONV_ROWS, sl)
                u1_ref[r0:r0 + CONV_ROWS, sl] = acc
        buf[0:HALO, :] = buf[tm:tm + HALO, :]
        xh, _ = _layer_norm_stats(u1_ref[...])
        u2 = xh * g_ref[...] + bb_ref[...]
        u3_ref[...] = (u2 * _sigmoid(u2)).astype(BF)

    ins = [(proj, _rspec(tm, D, 0)), (proj, _rspec(tm, D, 1)), (cw, _full(cw)), (cb, _full(cb)),
           (lng, _full(lng)), (lnb, _full(lnb))]
    outs = [_row_out(T, tm, D, F32), _row_out(T, tm, D, F32), _row_out(T, tm, D, BF)]
    return _rows_call(body, "conv_fwd", T // tm, ins, outs,
                      [pltpu.VMEM((tm + HALO, D), F32), pltpu.VMEM((SUBLANES - 1, tm + HALO, D), F32)])


def _conv_bwd(du3, u1, u0, proj, cw, lng, lnb, dgg):
    T = du3.shape[0]
    tm = CONV_TM
    n = T // tm
    per = tm // HALO

    def body(du3_ref, u1_ref, u0_ref, halo_ref, a_ref, b_ref, w_ref, g_ref, bb_ref, dgg_in_ref,
             dgl_ref, dg_ref, dbb_ref, dcb_ref, dw_ref, dbuf, ubuf, du0_buf, dsh, ush, dw8):
        i = pl.program_id(0)
        r = n - 1 - i

        @pl.when(i == 0)
        def _():
            dbuf[tm:tm + HALO, :] = jnp.zeros((HALO, D), F32)
            dg_ref[...] = jnp.zeros_like(dg_ref)
            dbb_ref[...] = jnp.zeros_like(dbb_ref)
            dcb_ref[...] = jnp.zeros_like(dcb_ref)
            dw8[...] = jnp.zeros_like(dw8)

        xh, rstd = _layer_norm_stats(u1_ref[...])
        g = g_ref[...]
        u2 = xh * g + bb_ref[...]
        s2 = _sigmoid(u2)
        du2 = du3_ref[...] * (s2 * (1.0 + u2 * (1.0 - s2)))
        dg_ref[...] += _colsum(du2 * xh)
        dbb_ref[...] += _colsum(du2)
        dxh = du2 * g
        du1 = rstd * (dxh - jnp.mean(dxh, axis=-1, keepdims=True) - xh * jnp.mean(dxh * xh, axis=-1, keepdims=True))
        dcb_ref[...] += _colsum(du1)
        dbuf[0:tm, :] = du1
        ubuf[HALO:HALO + tm, :] = u0_ref[...]
        ubuf[0:HALO, :] = jnp.where(r > 0, halo_ref[...], 0.0)
        _shifted_copies(dbuf, dsh, tm)
        _shifted_copies(ubuf, ush, tm)
        for j in range(D // LANES):
            sl = slice(j * LANES, (j + 1) * LANES)
            for r0 in range(0, tm, CONV_ROWS):
                d1 = dbuf[r0:r0 + CONV_ROWS, sl]
                acc = jnp.zeros((CONV_ROWS, LANES), F32)
                for kk in range(CONV_K):
                    acc = acc + w_ref[kk:kk + 1, sl] * _window(dbuf, dsh, r0 + CONV_K - 1 - kk, CONV_ROWS, sl)
                    prod = d1 * _window(ubuf, ush, r0 + HALO - (CONV_K - 1) + kk, CONV_ROWS, sl)
                    dw8[kk * SUBLANES:(kk + 1) * SUBLANES, sl] += prod.reshape(
                        CONV_ROWS // SUBLANES, SUBLANES, LANES).sum(axis=0)
                du0_buf[r0:r0 + CONV_ROWS, sl] = acc
        dbuf[tm:tm + HALO, :] = dbuf[0:HALO, :]
        du0 = du0_buf[...]
        af, bfl = a_ref[...].astype(F32), b_ref[...].astype(F32)
        sb = _sigmoid(bfl)
        dgl_ref[:, 0:D] = (du0 * sb).astype(BF)
        dgl_ref[:, D:2 * D] = (du0 * af * sb * (1.0 - sb)).astype(BF)

        @pl.when(i == n - 1)
        def _():
            for kk in range(CONV_KP):
                dw_ref[kk:kk + 1, :] = _colsum(dw8[kk * SUBLANES:(kk + 1) * SUBLANES, :])

    rs = lambda cb: _rspec(tm, D, cb, n)
    halo_spec = pl.BlockSpec((HALO, D), lambda i: (jnp.maximum((n - 1 - i) * per - 1, 0), 0))
    ins = [(du3, rs(0)), (u1, rs(0)), (u0, rs(0)), (u0, halo_spec), (proj, rs(0)), (proj, rs(1)),
           (cw, _full(cw)), (lng, _full(lng)), (lnb, _full(lnb)), (dgg, pl.BlockSpec(memory_space=pl.ANY))]
    outs = [(_sds(dgg.shape, dgg.dtype), _rspec(tm, 2 * D, 0, n)),
            _acc_out((1, D)), _acc_out((1, D)), _acc_out((1, D)), _acc_out((CONV_KP, D))]
    shifted = pltpu.VMEM((SUBLANES - 1, tm + HALO, D), F32)
    return _rows_call(body, "conv_bwd", n, ins, outs,
                      [pltpu.VMEM((tm + HALO, D), F32), pltpu.VMEM((tm + HALO, D), F32), pltpu.VMEM((tm, D), F32),
                       shifted, shifted, pltpu.VMEM((CONV_KP * SUBLANES, D), F32)], aliases={len(ins) - 1: 0})


def _merge(ba, bb, proj):
    T = ba.shape[0]
    tm = TM_ROWS

    def body(ba_ref, bb_ref, ga_ref, gb_ref, o_ref):
        sa, sb = _sigmoid(ga_ref[...].astype(F32)), _sigmoid(gb_ref[...].astype(F32))
        o_ref[...] = (sa * ba_ref[...].astype(F32) + sb * bb_ref[...].astype(F32)).astype(BF)

    ins = [(ba, _rspec(tm, D)), (bb, _rspec(tm, D)), (proj, _rspec(tm, D, 2)), (proj, _rspec(tm, D, 3))]
    return _rows_call(body, "merge", T // tm, ins, [_row_out(T, tm, D, BF)])[0]


def _post_out(x, mo, mod, n2g):
    T = x.shape[0]
    tm = TM_ROWS

    def body(x_ref, mo_ref, mod_ref, g_ref, x1_ref, h2_ref):
        g1 = mod_ref[:, 2 * D:3 * D]
        sh2, sc2 = mod_ref[:, 3 * D:4 * D], mod_ref[:, 4 * D:5 * D]
        x1 = x_ref[...] + g1 * mo_ref[...]
        x1_ref[...] = x1
        r = lax.rsqrt(jnp.mean(x1 * x1, axis=-1, keepdims=True) + EPS)
        h2_ref[...] = ((x1 * r) * g_ref[...] * (1.0 + sc2) + sh2).astype(BF)

    ins = [(x, _rspec(tm, D)), (mo, _rspec(tm, D)), (mod, _full(mod)), (n2g, _full(n2g))]
    return _rows_call(body, "post_out", T // tm, ins, [_row_out(T, tm, D, F32), _row_out(T, tm, D, BF)])


def _loss_head(x1, m2, tgt, mod):
    T = x1.shape[0]
    tm = TM_ROWS

    def body(x1_ref, m2_ref, t_ref, mod_ref, dy_ref, dm2_ref, dg2_ref, sq_ref):
        i = pl.program_id(0)

        @pl.when(i == 0)
        def _():
            dg2_ref[...] = jnp.zeros_like(dg2_ref)
            sq_ref[...] = jnp.zeros_like(sq_ref)

        g2 = mod_ref[:, 5 * D:6 * D]
        m2 = m2_ref[...]
        err = x1_ref[...] + g2 * m2 - t_ref[...]
        dy = err * (1.0 / D)
        dy_ref[...] = dy
        dm2_ref[...] = (g2 * dy).astype(BF)
        dg2_ref[...] += _colsum(dy * m2)
        sq_ref[...] += _colsum(err * err)

    ins = [(x1, _rspec(tm, D)), (m2, _rspec(tm, D)), (tgt, _rspec(tm, D)), (mod, _full(mod))]
    outs = [_row_out(T, tm, D, F32), _row_out(T, tm, D, BF), _acc_out((1, D)), _acc_out((1, D))]
    return _rows_call(body, "loss_head", T // tm, ins, outs)


def _norm2_bwd(dh2, x1, dy, mo, mod, n2g):
    T = x1.shape[0]
    tm = TM_ROWS

    def body(dh_ref, x1_ref, dy_ref, mo_ref, mod_ref, g_ref, dx1_ref, dmo_ref, dsh_ref, dsc_ref, dg_ref, dg1_ref):
        i = pl.program_id(0)

        @pl.when(i == 0)
        def _():
            for r in (dsh_ref, dsc_ref, dg_ref, dg1_ref):
                r[...] = jnp.zeros_like(r)

        g1, sc2 = mod_ref[:, 2 * D:3 * D], mod_ref[:, 4 * D:5 * D]
        g = g_ref[...]
        x1 = x1_ref[...]
        dh = dh_ref[...]
        r = lax.rsqrt(jnp.mean(x1 * x1, axis=-1, keepdims=True) + EPS)
        xn = x1 * r
        dsh_ref[...] += _colsum(dh)
        dsc_ref[...] += _colsum(dh * xn * g)
        dg_ref[...] += _colsum(dh * xn * (1.0 + sc2))
        dxn = dh * g * (1.0 + sc2)
        dx1 = dy_ref[...] + r * (dxn - xn * jnp.mean(dxn * xn, axis=-1, keepdims=True))
        dx1_ref[...] = dx1
        dg1_ref[...] += _colsum(dx1 * mo_ref[...])
        dmo_ref[...] = (g1 * dx1).astype(BF)

    ins = [(dh2, _rspec(tm, D)), (x1, _rspec(tm, D)), (dy, _rspec(tm, D)), (mo, _rspec(tm, D)),
           (mod, _full(mod)), (n2g, _full(n2g))]
    outs = [_row_out(T, tm, D, F32), _row_out(T, tm, D, BF)] + [_acc_out((1, D)) for _ in range(4)]
    return _rows_call(body, "norm2_bwd", T // tm, ins, outs)


def _gate_bwd(dmerged, ba, bb, proj):
    T = ba.shape[0]
    tm = TM_ROWS

    def body(dm_ref, ba_ref, bb_ref, ga_ref, gb_ref, dba_ref, dbb_ref, dgt_ref):
        dm = dm_ref[...].astype(F32)
        sa, sb = _sigmoid(ga_ref[...].astype(F32)), _sigmoid(gb_ref[...].astype(F32))
        dba_ref[...] = (dm * sa).astype(BF)
        dbb_ref[...] = (dm * sb).astype(BF)
        dgt_ref[:, 0:D] = (dm * ba_ref[...].astype(F32) * sa * (1.0 - sa)).astype(BF)
        dgt_ref[:, D:2 * D] = (dm * bb_ref[...].astype(F32) * sb * (1.0 - sb)).astype(BF)

    ins = [(dmerged, _rspec(tm, D)), (ba, _rspec(tm, D)), (bb, _rspec(tm, D)),
           (proj, _rspec(tm, D, 2)), (proj, _rspec(tm, D, 3))]
    outs = [_row_out(T, tm, D, BF), _row_out(T, tm, D, BF), (_sds((T, 4 * D), BF), _rspec(tm, 2 * D, 1))]
    return _rows_call(body, "gate_bwd", T // tm, ins, outs)


def _qkv_bwd(dq, dk, dv, proj, f, dfc, dfq, qg2, kg2, bf_pad):
    T = proj.shape[0]
    tm = TM_ROWS
    n = T // tm
    seg = _seg_mat()
    tri = _tri_mat(tm, False)

    def body(dq_ref, dk_ref, dv_ref, q_ref, k_ref, f_ref, dfc_ref, dfq_ref, qg_ref, kg_ref, bf_ref, seg_ref, tri_ref,
             dqkv_ref, dfo_ref, dqg_ref, dkg_ref, dbf_ref, carry_ref):
        i = pl.program_id(0)

        @pl.when(i == 0)
        def _():
            carry_ref[...] = jnp.zeros_like(carry_ref)
            dqg_ref[...] = jnp.zeros_like(dqg_ref)
            dkg_ref[...] = jnp.zeros_like(dkg_ref)
            dbf_ref[...] = jnp.zeros_like(dbf_ref)

        segm = seg_ref[...]
        dqg = jnp.zeros((1, LANES), F32)
        dkg = jnp.zeros((1, LANES), F32)
        for j in range(D // LANES):
            sl = slice(j * LANES, (j + 1) * LANES)
            for (raw_ref, d_ref, gn_ref, which) in ((q_ref, dq_ref, qg_ref, 0), (k_ref, dk_ref, kg_ref, 1)):
                xc = raw_ref[:, sl].astype(F32)
                rr = lax.rsqrt(_dot_exact(xc * xc, segm) + EPS)
                xn = xc * rr
                dc = d_ref[:, sl]
                if which == 0:
                    dqg = dqg + _colsum(dc * xn)
                else:
                    dkg = dkg + _colsum(dc * xn)
                dxn = dc * gn_ref[...]
                osl = slice(which * D + j * LANES, which * D + (j + 1) * LANES)
                dqkv_ref[:, osl] = (rr * (dxn - xn * _dot_exact(dxn * xn, segm))).astype(BF)
        dqg_ref[...] += dqg
        dkg_ref[...] += dkg
        dqkv_ref[:, 2 * D:3 * D] = dv_ref[...].astype(BF)
        z = f_ref[...] + bf_ref[...]
        sneg_t = _sigmoid(-z).T[0:N_HEADS, :]
        dfc = dfc_ref[...] + dfq_ref[...]
        carry = carry_ref[:, 0:1]
        dlf = _dot_exact(dfc, tri_ref[...]) + carry
        carry_ref[...] = jnp.broadcast_to(carry + jnp.sum(dfc, axis=1, keepdims=True), carry_ref.shape)
        dzt = dlf * sneg_t
        dz = jnp.concatenate([dzt, jnp.zeros((LANES - N_HEADS, tm), F32)], axis=0).T
        dbf_ref[...] += _colsum(dz)
        dfo_ref[...] = dz.astype(BF)

    rs = lambda w, cb=0: _rspec(tm, w, cb, n)
    ins = [(dq, rs(D)), (dk, rs(D)), (dv, rs(D)), (proj, rs(D, 0)), (proj, rs(D, 1)), (f, rs(LANES)),
           (dfc, pl.BlockSpec((N_HEADS, tm), lambda i: (0, n - 1 - i))),
           (dfq, pl.BlockSpec((N_HEADS, tm), lambda i: (0, n - 1 - i))),
           (qg2, _full(qg2)), (kg2, _full(kg2)), (bf_pad, _full(bf_pad)), (seg, _full(seg)), (tri, _full(tri))]
    outs = [_row_out(T, tm, 3 * D, BF, n), _row_out(T, tm, LANES, BF, n),
            _acc_out((1, LANES)), _acc_out((1, LANES)), _acc_out((1, LANES))]
    return _rows_call(body, "qkv_bwd", n, ins, outs, [pltpu.VMEM((N_HEADS, LANES), F32)])


def _norm1_bwd(dh, dhf, x, dx1, mod, n1g):
    T = x.shape[0]
    tm = TM_ROWS

    def body(dh_ref, dhf_ref, x_ref, dx1_ref, mod_ref, g_ref, dx_ref, dsh_ref, dsc_ref, dg_ref):
        i = pl.program_id(0)

        @pl.when(i == 0)
        def _():
            for r in (dsh_ref, dsc_ref, dg_ref):
                r[...] = jnp.zeros_like(r)

        sc1 = mod_ref[:, D:2 * D]
        g = g_ref[...]
        xv = x_ref[...]
        dh = dh_ref[...] + dhf_ref[...]
        r = lax.rsqrt(jnp.mean(xv * xv, axis=-1, keepdims=True) + EPS)
        xn = xv * r
        dsh_ref[...] += _colsum(dh)
        dsc_ref[...] += _colsum(dh * xn * g)
        dg_ref[...] += _colsum(dh * xn * (1.0 + sc1))
        dxn = dh * g * (1.0 + sc1)
        dx_ref[...] = dx1_ref[...] + r * (dxn - xn * jnp.mean(dxn * xn, axis=-1, keepdims=True))

    ins = [(dh, _rspec(tm, D)), (dhf, _rspec(tm, D)), (x, _rspec(tm, D)), (dx1, _rspec(tm, D)),
           (mod, _full(mod)), (n1g, _full(n1g))]
    outs = [_row_out(T, tm, D, F32)] + [_acc_out((1, D)) for _ in range(3)]
    return _rows_call(body, "norm1_bwd", T // tm, ins, outs)


def _adamw_math(w, g, m, v):
    m = ADAM_B1 * m + (1.0 - ADAM_B1) * g
    v = ADAM_B2 * v + (1.0 - ADAM_B2) * (g * g)
    m_hat = m / (1.0 - ADAM_B1 ** ADAM_STEP)
    v_hat = v / (1.0 - ADAM_B2 ** ADAM_STEP)
    delta = -ADAM_LR * (m_hat / (jnp.sqrt(v_hat) + ADAM_EPS) + ADAM_WD * w)
    return delta, m, v


def _adamw(parts, w, m, v, name):
    n, R, C = parts.shape
    tr = R if R <= 256 else 256
    assert R % tr == 0

    def body(p_ref, w_ref, m_ref, v_ref, g_ref, d_ref, mo_ref, vo_ref):
        g = p_ref[0].astype(F32)
        for s in range(1, n):
            g = g + p_ref[s].astype(F32)
        g_ref[...] = g
        d_ref[...], mo_ref[...], vo_ref[...] = _adamw_math(w_ref[...], g, m_ref[...], v_ref[...])

    spec = pl.BlockSpec((None, tr, C), lambda i: (0, i, 0))
    return pl.pallas_call(
        body, name=name, grid=(R // tr,),
        in_specs=[pl.BlockSpec((n, tr, C), lambda i: (0, i, 0)), spec, spec, spec],
        out_specs=[spec] * 4, out_shape=[_sds((1, R, C), F32)] * 4,
        compiler_params=_cp(("parallel",)),
    )(parts, w, m, v)


def _rcopy(src, dst, ssem, rsem, peer):
    return pltpu.make_async_remote_copy(src_ref=src, dst_ref=dst, send_sem=ssem, recv_sem=rsem,
                                        device_id=peer, device_id_type=MESH)


def _ada_fwd(c, w_ada, b_slice, cw_shard):
    def body(c_ref, w_ref, b_ref, cw_ref, mod_ref, ca_ref, cwf_ref, call, mp, ssem, rsem):
        x, y, cc, me = _my_pos()
        call[pl.ds(me, 1), :] = c_ref[...]
        cwf_ref[me] = cw_ref[...]
        first = []
        for d in range(1, N_DEV):
            peer, _ = _peer(x, y, cc, d)
            first.append(_rcopy(c_ref, call.at[pl.ds(me, 1), :], ssem.at[0, d - 1], rsem.at[0, d - 1], peer))
            first.append(_rcopy(cw_ref, cwf_ref.at[me], ssem.at[1, d - 1], rsem.at[1, d - 1], peer))
        for cp in first:
            cp.start()
        for d in range(1, N_DEV):
            peer, pid = _peer(x, y, cc, d)
            _rcopy(c_ref, call.at[pl.ds(pid, 1), :], ssem.at[0, d - 1], rsem.at[0, d - 1], peer).wait_recv()
            _rcopy(cw_ref, cwf_ref.at[pid], ssem.at[1, d - 1], rsem.at[1, d - 1], peer).wait_recv()
        cv = call[...]
        ca = cv * _sigmoid(cv)
        ca_ref[...] = ca
        mp[...] = _dot_f32(ca, w_ref[...]) + b_ref[...]
        mod_ref[pl.ds(me, 1), :] = mp[pl.ds(me, 1), :]
        second = []
        for d in range(1, N_DEV):
            peer, pid = _peer(x, y, cc, d)
            second.append(_rcopy(mp.at[pl.ds(pid, 1), :], mod_ref.at[pl.ds(me, 1), :], ssem.at[2, d - 1], rsem.at[2, d - 1], peer))
        for cp in second:
            cp.start()
        for d in range(1, N_DEV):
            peer, pid = _peer(x, y, cc, d)
            _rcopy(mp.at[pl.ds(pid, 1), :], mod_ref.at[pl.ds(pid, 1), :], ssem.at[2, d - 1], rsem.at[2, d - 1], peer).wait_recv()
        for cp in first + second:
            cp.wait_send()

    vm = pl.BlockSpec(memory_space=pltpu.VMEM)
    return pl.pallas_call(
        body, name="ada_fwd",
        in_specs=[vm, vm, vm, vm], out_specs=[vm, vm, vm],
        out_shape=[_sds((N_DEV, ADA_SHARD), F32), _sds((N_DEV, D), F32), _sds((N_DEV, CONV_KP, LANES), F32)],
        scratch_shapes=[pltpu.VMEM((N_DEV, D), F32), pltpu.VMEM((N_DEV, ADA_SHARD), F32),
                        pltpu.SemaphoreType.DMA((3, N_DEV - 1)), pltpu.SemaphoreType.DMA((3, N_DEV - 1))],
        compiler_params=pltpu.CompilerParams(vmem_limit_bytes=VMEM_LIMIT),
    )(c, w_ada, b_slice, cw_shard)


def _xchg_parts(arrays):
    n = len(arrays)
    anyspec = pl.BlockSpec(memory_space=pl.ANY)
    slots = lambda a: a.shape[0] if (a.ndim == 3 and a.shape[0] == N_DEV // 2) else N_DEV
    shapes = [_sds((slots(a),) + tuple(a.shape[-2:]), a.dtype) for a in arrays]
    scratch = [pltpu.SemaphoreType.DMA((n,)), pltpu.SemaphoreType.DMA((n, N_DEV - 1)),
               pltpu.SemaphoreType.DMA((n, N_DEV - 1))]
    return [anyspec] * n, shapes, scratch


def _xchg(ins, outs, sems, gather, wait):
    lsem, ssem, rsem = sems
    x, y, cc, me = _my_pos()
    for a in range(len(ins)):
        if not gather and ins[a].shape[0] == N_DEV // 2:
            chip = 2 * x + y
            local = pltpu.make_async_copy(ins[a].at[chip], outs[a].at[chip], lsem.at[a])
            if not wait:
                local.start()
            for d in range(1, N_DEV // 2):
                px, py = ((1 - x) if d & 2 else x), ((1 - y) if d & 1 else y)
                pchip = 2 * px + py
                if not wait:
                    _rcopy(ins[a].at[pchip], outs[a].at[chip], ssem.at[a, d - 1], rsem.at[a, d - 1], (px, py, cc)).start()
                else:
                    cp = _rcopy(ins[a].at[pchip], outs[a].at[pchip], ssem.at[a, d - 1], rsem.at[a, d - 1], (px, py, cc))
                    cp.wait_recv()
                    cp.wait_send()
            if wait:
                local.wait()
            continue
        local = pltpu.make_async_copy(ins[a] if gather else ins[a].at[me], outs[a].at[me], lsem.at[a])
        if not wait:
            local.start()
        for d in range(1, N_DEV):
            peer, pid = _peer(x, y, cc, d)
            src = ins[a] if gather else ins[a].at[pid]
            if not wait:
                _rcopy(src, outs[a].at[me], ssem.at[a, d - 1], rsem.at[a, d - 1], peer).start()
            else:
                cp = _rcopy(src, outs[a].at[pid], ssem.at[a, d - 1], rsem.at[a, d - 1], peer)
                cp.wait_recv()
                cp.wait_send()
        if wait:
            local.wait()


def _pair_reduce(part):
    n, R, C = part.shape
    half = n // 2
    tr = R if R <= 256 else 256
    assert n == N_DEV and R % tr == 0

    def swap(p_ref, got_ref, ssem, rsem):
        x, y, c, _ = _my_pos()
        cps = [_rcopy(p_ref.at[2 * i + (1 - c)], got_ref.at[i], ssem.at[i], rsem.at[i], (x, y, 1 - c)) for i in range(half)]
        for cp in cps:
            cp.start()
        for cp in cps:
            cp.wait_recv()
            cp.wait_send()

    anyspec = pl.BlockSpec(memory_space=pl.ANY)
    got = pl.pallas_call(
        swap, name="dwin_pair_swap", in_specs=[anyspec], out_specs=anyspec, out_shape=_sds((half, R, C), part.dtype),
        scratch_shapes=[pltpu.SemaphoreType.DMA((half,)), pltpu.SemaphoreType.DMA((half,))],
    )(part)

    def add(p_ref, g_ref, o_ref):
        c = lax.axis_index("c")
        o_ref[...] = (p_ref[c].astype(F32) + g_ref[...].astype(F32)).astype(o_ref.dtype)

    spec = pl.BlockSpec((None, tr, C), lambda i, r: (i, r, 0))
    return pl.pallas_call(
        add, name="dwin_pair_add", grid=(half, R // tr),
        in_specs=[pl.BlockSpec((2, tr, C), lambda i, r: (i, r, 0)), spec],
        out_specs=spec, out_shape=_sds((half, R, C), part.dtype),
        compiler_params=_cp(("parallel", "parallel")),
    )(part, got)


def _gather_two_level(shard, name):
    def body(x_ref, out_ref, ssem, rsem, lsem):
        x, y, c, me = _my_pos()
        sibling = (x, y, 1 - c)
        chips = [(1 - x, y), (x, 1 - y), (1 - x, 1 - y)]
        slot = lambda px, py, pc: out_ref.at[4 * px + 2 * py + pc]

        def copy(kk, block, to, src=None):
            return _rcopy(slot(*block) if src is None else src, slot(*block), ssem.at[kk], rsem.at[kk], to)

        mine = pltpu.make_async_copy(x_ref, slot(x, y, c), lsem)
        mine.start()
        first = [copy(0, (x, y, c), sibling, src=x_ref)]
        first += [copy(1 + j, (x, y, c), (*chip, c), src=x_ref) for j, chip in enumerate(chips)]
        for cp in first:
            cp.start()
        passed = [copy(4 + j, (*chip, c), sibling) for j, chip in enumerate(chips)]
        for j, chip in enumerate(chips):
            copy(1 + j, (*chip, c), (x, y, c)).wait_recv()
            passed[j].start()
        copy(0, sibling, (x, y, c)).wait_recv()
        for j, chip in enumerate(chips):
            copy(4 + j, (*chip, 1 - c), (x, y, c)).wait_recv()
        for cp in first + passed:
            cp.wait_send()
        mine.wait()

    anyspec = pl.BlockSpec(memory_space=pl.ANY)
    return pl.pallas_call(
        body, name=name, in_specs=[anyspec], out_specs=anyspec,
        out_shape=_sds((N_DEV,) + tuple(shard.shape), shard.dtype),
        scratch_shapes=[pltpu.SemaphoreType.DMA((N_DEV - 1,)), pltpu.SemaphoreType.DMA((N_DEV - 1,)),
                        pltpu.SemaphoreType.DMA(())],
    )(shard)


PACK_ROWS = 16
ROW_MISC = 5
ROW_LOSS = 6
ROW_DMOD = 8


def _small_bwd(pack, dmodb, dcw, cat, wp, mp_, vp, cw_w, cw_m, cw_v):
    def body(pack_ref, dmodb_ref, dcw_ref, cat_ref, wp_ref, mp_ref, vp_ref, cww_ref, cwm_ref, cwv_ref,
             g_ref, d_ref, mo_ref, vo_ref, cg_ref, cd_ref, cm_ref, cv_ref, gwa_ref, loss_ref,
             allp, dmc, cwg, ssem, rsem):
        x, y, cc, me = _my_pos()
        allp[me] = pack_ref[...]
        dmc[pl.ds(me, 1), :] = dmodb_ref[pl.ds(me, 1), :]
        cwg[me] = dcw_ref[me]
        sends = []
        for d in range(1, N_DEV):
            peer, pid = _peer(x, y, cc, d)
            sends.append(_rcopy(pack_ref, allp.at[me], ssem.at[0, d - 1], rsem.at[0, d - 1], peer))
            sends.append(_rcopy(dmodb_ref.at[pl.ds(pid, 1), :], dmc.at[pl.ds(me, 1), :], ssem.at[1, d - 1], rsem.at[1, d - 1], peer))
            sends.append(_rcopy(dcw_ref.at[pid], cwg.at[me], ssem.at[2, d - 1], rsem.at[2, d - 1], peer))
        for cp in sends:
            cp.start()
        for d in range(1, N_DEV):
            peer, pid = _peer(x, y, cc, d)
            _rcopy(pack_ref, allp.at[pid], ssem.at[0, d - 1], rsem.at[0, d - 1], peer).wait_recv()
            _rcopy(dmodb_ref.at[pl.ds(pid, 1), :], dmc.at[pl.ds(pid, 1), :], ssem.at[1, d - 1], rsem.at[1, d - 1], peer).wait_recv()
            _rcopy(dcw_ref.at[pid], cwg.at[pid], ssem.at[2, d - 1], rsem.at[2, d - 1], peer).wait_recv()
        for cp in sends:
            cp.wait_send()

        tot = allp[0]
        cg = cwg[0]
        for s in range(1, N_DEV):
            tot = tot + allp[s]
            cg = cg + cwg[s]
        lane = lax.broadcasted_iota(jnp.int32, (PACK_ROWS, D), 1)
        row = lax.broadcasted_iota(jnp.int32, (PACK_ROWS, D), 0)
        gains = (row == ROW_MISC) & (lane >= LANES) & (lane < 3 * LANES)
        folded = tot + pltpu.roll(tot, D - HEAD_DIM, axis=1)
        keep = (lane % LANES) < HEAD_DIM
        g = jnp.where(gains, jnp.where(keep, folded, 0.0), tot)
        loss_ref[...] = jnp.broadcast_to(
            (0.5 / D) * jnp.sum(jnp.where(row == ROW_LOSS, tot, 0.0), keepdims=True).reshape(1, 1), loss_ref.shape)
        g = jnp.where(row == ROW_LOSS, 0.0, g)
        g_ref[...] = g
        d_ref[...], mo_ref[...], vo_ref[...] = _adamw_math(wp_ref[...], g, mp_ref[...], vp_ref[...])
        cg_ref[...] = cg
        cd_ref[...], cm_ref[...], cv_ref[...] = _adamw_math(cww_ref[...], cg, cwm_ref[...], cwv_ref[...])
        dm_pad = jnp.concatenate([dmc[...], jnp.zeros((LANES - N_DEV, ADA_SHARD), F32)], axis=0)
        gwa_ref[...] = _dot_f32(cat_ref[...], dm_pad)

    vm = pl.BlockSpec(memory_space=pltpu.VMEM)
    p16 = _sds((PACK_ROWS, D), F32)
    c32 = _sds((CONV_KP, LANES), F32)
    return pl.pallas_call(
        body, name="small_bwd",
        in_specs=[vm] * 10, out_specs=[vm] * 10,
        out_shape=[p16, p16, p16, p16, c32, c32, c32, c32, _sds((D, ADA_SHARD), F32), _sds((8, LANES), F32)],
        scratch_shapes=[pltpu.VMEM((N_DEV, PACK_ROWS, D), F32), pltpu.VMEM((N_DEV, ADA_SHARD), F32),
                        pltpu.VMEM((N_DEV, CONV_KP, LANES), F32),
                        pltpu.SemaphoreType.DMA((3, N_DEV - 1)), pltpu.SemaphoreType.DMA((3, N_DEV - 1))],
        compiler_params=pltpu.CompilerParams(vmem_limit_bytes=VMEM_LIMIT),
    )(pack, dmodb, dcw, cat, wp, mp_, vp, cw_w, cw_m, cw_v)


def _lanes(vec, start, total=D):
    n = vec.shape[1]
    return jnp.pad(vec, ((0, 0), (start, total - start - n)))


def _pack_small(rows5, misc, loss_row, six):
    z = jnp.zeros((1, D), F32)
    return jnp.concatenate(rows5 + [misc, loss_row, z] + [six.reshape(N_ADA, D), z, z], axis=0)


def kernel(x, c, w_ada, b_ada, norm1_g, w_in, b_forget, q_norm_g, k_norm_g, w_attn_proj, conv_w, conv_b, conv_ln_g, conv_ln_b, w_conv_proj, w_out, norm2_g, w_mlp1, w_mlp2, loss_target, m_w_ada, m_b_ada, m_norm1_g, m_w_in, m_b_forget, m_q_norm_g, m_k_norm_g, m_w_attn_proj, m_conv_w, m_conv_b, m_conv_ln_g, m_conv_ln_b, m_w_conv_proj, m_w_out, m_norm2_g, m_w_mlp1, m_w_mlp2, v_w_ada, v_b_ada, v_norm1_g, v_w_in, v_b_forget, v_q_norm_g, v_k_norm_g, v_w_attn_proj, v_conv_w, v_conv_b, v_conv_ln_g, v_conv_ln_b, v_w_conv_proj, v_w_out, v_norm2_g, v_w_mlp1, v_w_mlp2):
    me = 4 * lax.axis_index("x") + 2 * lax.axis_index("y") + lax.axis_index("c")
    xs, tgt = x[0], loss_target[0]
    T = xs.shape[0]
    sq = lambda a: a[0]
    pad_taps = lambda a: jnp.pad(a[0], ((0, CONV_KP - CONV_K), (0, 0)))

    b_slice = lax.dynamic_slice(b_ada, (0, me * ADA_SHARD), (1, ADA_SHARD))
    modb, ca_all, cwf = _ada_fwd(c, sq(w_ada), b_slice, pad_taps(conv_w))
    mod = modb.reshape(1, N_ADA * D)
    cw = jnp.transpose(cwf, (1, 0, 2)).reshape(CONV_KP, D)

    g_in = _gather_two_level(sq(w_in).astype(BF), "w_in_gather")
    d_in = g_in.shape[2] * N_DEV
    w_in_f = jnp.transpose(g_in, (1, 0, 2)).reshape(D, d_in)
    w_qkv = w_in_f[:, :3 * D]
    w_gg = w_in_f[:, 3 * D + N_HEADS:]
    w_f = jnp.pad(w_in_f[:, 3 * D:3 * D + N_HEADS], ((0, 0), (0, LANES - N_HEADS)))
    shards = [sq(w_attn_proj).astype(BF), sq(w_conv_proj).astype(BF), sq(w_out).astype(BF),
              sq(w_mlp1).astype(BF), sq(w_mlp2).astype(BF)]

    qg2 = jnp.tile(q_norm_g, (1, 2))
    kg2 = jnp.tile(k_norm_g, (1, 2))
    bf_pad = _lanes(b_forget, 0, LANES)

    h = _pre_in(xs, mod, norm1_g)
    pqkv = _matmul(h, w_qkv, "nn", BF, "mm_proj_qkv")
    pgg = _matmul(h, w_gg, "nn", BF, "mm_proj_gg")
    f = _matmul(h, w_f, "nn", F32, "mm_f")
    q, k, v, fc = _qkv_post(pqkv, f, qg2, kg2, bf_pad)
    fc3 = fc.reshape(N_HEADS // 2, 2, T)
    o, lse, g_ap, g_cp, g_out, g_1, g_2 = _flash_fwd(q, k, v, fc3, shards)
    w_ap, w_cp, w_o = g_ap.reshape(D, D), g_cp.reshape(D, D), g_out.reshape(D, D)
    w_2 = g_2.reshape(D_FF, D)
    ba = _matmul(o, w_ap, "nn", BF, "mm_ba")
    u0, u1, u3 = _conv_fwd(pgg, cw, conv_b, conv_ln_g, conv_ln_b)
    bb = _matmul(u3, w_cp, "nn", BF, "mm_bb")
    merged = _merge(ba, bb, pgg)
    mo = _matmul(merged, w_o, "nn", F32, "mm_out")
    x1, h2 = _post_out(xs, mo, mod, norm2_g)
    a, rl = _matmul(h2, g_1, "nn", BF, "mm_mlp1", relu2=True, b_slots=True)
    m2 = _matmul(rl, w_2, "nn", F32, "mm_mlp2")
    dy, dm2, dg2, sqcols = _loss_head(x1, m2, tgt, mod)

    da = _matmul(dm2, w_2, "nt", BF, "mm_drl", relu_of=a)
    dw_2 = _matmul(rl, dm2, "tn", BF, "mm_dw2")
    dh2 = _matmul(da, g_1, "nt", F32, "mm_dh2", b_slots=True)
    dw_1 = _matmul(h2, da, "tn", BF, "mm_dw1", out_slots=True)
    dx1, dmo, dsh2, dsc2, dn2g, dg1 = _norm2_bwd(dh2, x1, dy, mo, mod, norm2_g)
    dmerged = _matmul(dmo, w_o, "nt", BF, "mm_dmerged")
    dw_o = _matmul(merged, dmo, "tn", BF, "mm_dwout")
    dba, dbb, dgg = _gate_bwd(dmerged, ba, bb, pgg)
    du3 = _matmul(dbb, w_cp, "nt", F32, "mm_du3")
    dw_cp = _matmul(u3, dbb, "tn", BF, "mm_dwcp")
    do = _matmul(dba, w_ap, "nt", BF, "mm_do")
    dw_ap = _matmul(o, dba, "tn", BF, "mm_dwap")
    dgg, dlng, dlnb, dcb, dcw_full = _conv_bwd(du3, u1, u0, pgg, cw, conv_ln_g, conv_ln_b, dgg)
    delta = _attn_delta(do, o)
    dw_gg = _matmul(h, dgg, "tn", BF, "mm_dw_gg")
    parts = [dw_ap.reshape(N_DEV, D // N_DEV, D), dw_cp.reshape(N_DEV, D // N_DEV, D), dw_o.reshape(N_DEV, D // N_DEV, D),
             dw_1, dw_2.reshape(N_DEV, D_FF // N_DEV, D)]
    dq, rs_a, rs_b, dk, dv, dfc3, r_ap, r_cp, r_out, r_1, r_2 = _flash_bwd(q, k, v, do, lse, delta, fc3, parts)
    dfq = jnp.stack([rs_a, rs_b], axis=1).reshape(N_HEADS, T)
    dqkv, df, dqg, dkg, dbf = _qkv_bwd(dq, dk, dv, pqkv, f, dfc3.reshape(N_HEADS, T), dfq, qg2, kg2, bf_pad)
    dw_qkv = _matmul(h, dqkv, "tn", BF, "mm_dw_qkv")
    dw_f = _matmul(h, df, "tn", BF, "mm_dwf")
    dw_in_f = jnp.concatenate([dw_qkv, dw_f[:, :N_HEADS], dw_gg], axis=1)
    part_in = jnp.transpose(dw_in_f.reshape(D, N_DEV, d_in // N_DEV), (1, 0, 2))
    dh, r_in = _matmul(dqkv, w_qkv, "nt", F32, "mm_dh", tk=1024, scatter=(_pair_reduce(part_in),), more=((dgg, w_gg),))
    dhf = _matmul(df, w_f, "nt", F32, "mm_dhf")
    grad_x, dsh1, dsc1, dn1g = _norm1_bwd(dh, dhf, xs, dx1, mod, norm1_g)

    dmod = jnp.concatenate([dsh1, dsc1, dg1, dsh2, dsc2, dg2], axis=1)
    misc = jnp.concatenate([dbf, dqg, dkg, jnp.zeros((1, D - 3 * LANES), F32)], axis=1)
    pack = _pack_small([dn1g, dcb, dlng, dlnb, dn2g], misc, sqcols, dmod)
    dcw_blocks = jnp.transpose(dcw_full.reshape(CONV_KP, N_DEV, LANES), (1, 0, 2))

    def small_params(b_a, n1, bfg, qn, kn, cvb, lg, lb, n2):
        misc_p = jnp.concatenate([_lanes(bfg, 0, LANES), _lanes(qn, 0, LANES), _lanes(kn, 0, LANES),
                                  jnp.zeros((1, D - 3 * LANES), F32)], axis=1)
        return _pack_small([n1, cvb, lg, lb, n2], misc_p, jnp.zeros((1, D), F32), b_a)

    wp = small_params(b_ada, norm1_g, b_forget, q_norm_g, k_norm_g, conv_b, conv_ln_g, conv_ln_b, norm2_g)
    mp_ = small_params(m_b_ada, m_norm1_g, m_b_forget, m_q_norm_g, m_k_norm_g, m_conv_b, m_conv_ln_g, m_conv_ln_b, m_norm2_g)
    vp = small_params(v_b_ada, v_norm1_g, v_b_forget, v_q_norm_g, v_k_norm_g, v_conv_b, v_conv_ln_g, v_conv_ln_b, v_norm2_g)
    cat = jnp.pad(jnp.transpose(ca_all), ((0, 0), (0, LANES - N_DEV)))
    small = _small_bwd(pack, dmod.reshape(N_DEV, ADA_SHARD), dcw_blocks, cat,
                       wp, mp_, vp, pad_taps(conv_w), pad_taps(m_conv_w), pad_taps(v_conv_w))
    sp = small[0:4]
    scw = small[4:8]
    gw_ada, loss_t = small[8], small[9]
    loss = loss_t[0, 0]

    def unpack(p):
        misc_r = p[ROW_MISC:ROW_MISC + 1]
        return dict(
            b_ada=p[ROW_DMOD:ROW_DMOD + N_ADA].reshape(1, N_ADA * D), norm1_g=p[0:1], conv_b=p[1:2], conv_ln_g=p[2:3],
            conv_ln_b=p[3:4], norm2_g=p[4:5], b_forget=misc_r[:, 0:N_HEADS],
            q_norm_g=misc_r[:, LANES:LANES + HEAD_DIM], k_norm_g=misc_r[:, 2 * LANES:2 * LANES + HEAD_DIM])

    res = {}
    res["w_ada"] = _adamw(gw_ada[None], w_ada, m_w_ada, v_w_ada, "adamw_w_ada")
    res["w_in"] = _adamw(r_in, w_in, m_w_in, v_w_in, "adamw_w_in")
    res["w_attn_proj"] = _adamw(r_ap, w_attn_proj, m_w_attn_proj, v_w_attn_proj, "adamw_w_ap")
    res["w_conv_proj"] = _adamw(r_cp, w_conv_proj, m_w_conv_proj, v_w_conv_proj, "adamw_w_cp")
    res["w_out"] = _adamw(r_out, w_out, m_w_out, v_w_out, "adamw_w_out")
    res["w_mlp1"] = _adamw(r_1, w_mlp1, m_w_mlp1, v_w_mlp1, "adamw_w_mlp1")
    res["w_mlp2"] = _adamw(r_2, w_mlp2, m_w_mlp2, v_w_mlp2, "adamw_w_mlp2")

    names = ["w_ada", "b_ada", "norm1_g", "w_in", "b_forget", "q_norm_g", "k_norm_g", "w_attn_proj", "conv_w", "conv_b",
             "conv_ln_g", "conv_ln_b", "w_conv_proj", "w_out", "norm2_g", "w_mlp1", "w_mlp2"]
    outs = [loss, grad_x[None]]
    for kind in range(4):
        small_d = unpack(sp[kind])
        for nm in names:
            if nm in res:
                outs.append(res[nm][kind])
            elif nm == "conv_w":
                outs.append(scw[kind][:CONV_K][None])
            else:
                outs.append(small_d[nm])
    return tuple(outs)
```
